```python
import jax, jax.numpy as jnp
from jax import lax
import numpy as np

D_MODEL = 1024
BATCH = 8
SEQ = 4096
DEPTH = 1

D_MIX = D_MODEL
D_RET = D_MIX // 2
D_POOL = D_MIX - D_RET
RET_HEADS = 8
RET_HEAD_DIM = D_RET // RET_HEADS
RET_CHUNK = 128
ROPE_BASE = 10000.0
POOL_WINDOWS = (2, 4, 8, 16)
POOL_GROUPS = len(POOL_WINDOWS)
POOL_GROUP_DIM = D_POOL // POOL_GROUPS
D_IN_PROJ = 4 * D_RET + D_POOL
N_EXPERTS = 32
TOP_K = 4
D_EXPERT = D_MODEL
SWIGLU_LIMIT = 7.0
SWIGLU_ALPHA = 1.702
PLE_DIM = 256
NORM_EPS = 1e-5
GN_EPS = 1e-5

kernel_name = "hybrid_retention_pool_moe_block"


def rms_norm(x, g):
    xf = x.astype(jnp.float32)
    y = xf * lax.rsqrt(jnp.mean(xf * xf, axis=-1, keepdims=True) + NORM_EPS)
    return (y * g.astype(jnp.float32)).astype(x.dtype)


def rope(x, positions):
    half = x.shape[-1] // 2
    inv_freq = ROPE_BASE ** (-jnp.arange(half, dtype=jnp.float32) / half)
    ang = positions.astype(jnp.float32)[..., None] * inv_freq
    cos = jnp.cos(ang)[:, :, None, :]
    sin = jnp.sin(ang)[:, :, None, :]
    xf = x.astype(jnp.float32)
    x1, x2 = xf[..., :half], xf[..., half:]
    out = jnp.concatenate([x1 * cos - x2 * sin, x2 * cos + x1 * sin], axis=-1)
    return out.astype(x.dtype)


def retention(q, k, v, gn_w):
    B, S, H, d = q.shape
    C = RET_CHUNK
    N = S // C
    log_gamma = jnp.log1p(-jnp.power(2.0, -5.0 - jnp.arange(H, dtype=jnp.float32)))
    qc = q.reshape(B, N, C, H, d)
    kc = (k * (d ** -0.5)).reshape(B, N, C, H, d)
    vc = v.reshape(B, N, C, H, d)
    idx = jnp.arange(C, dtype=jnp.float32)
    rel = idx[:, None] - idx[None, :]
    causal = rel >= 0
    decay = jnp.where(causal[None], jnp.exp(jnp.where(causal, rel, 0.0)[None] * log_gamma[:, None, None]), 0.0)
    scores = jnp.einsum('bnchd,bnmhd->bnhcm', qc, kc) * decay
    intra = jnp.einsum('bnhcm,bnmhd->bnchd', scores, vc)
    zeta = jnp.exp((C - 1 - idx)[None, :] * log_gamma[:, None])
    kv = jnp.einsum('bnmhd,hm,bnmhe->bnhde', kc, zeta, vc)
    chunk_decay = jnp.exp(C * log_gamma)[:, None, None]

    def step(state, kv_i):
        return chunk_decay * state + kv_i, state

    init = jnp.zeros((B, H, d, d), dtype=kv.dtype)
    _, states = lax.scan(step, init, jnp.moveaxis(kv, 1, 0))
    states = jnp.moveaxis(states, 0, 1)
    xi = jnp.exp((idx + 1.0)[None, :] * log_gamma[:, None])
    cross = jnp.einsum('bnchd,bnhde,hc->bnche', qc, states, xi)
    y = (intra + cross).reshape(B, S, H, d).astype(jnp.float32)
    mu = jnp.mean(y, axis=-1, keepdims=True)
    var = jnp.mean(jnp.square(y - mu), axis=-1, keepdims=True)
    y = (y - mu) * lax.rsqrt(var + GN_EPS)
    y = y.reshape(B, S, H * d) * gn_w.astype(jnp.float32)
    return y.astype(q.dtype)


def pool_mixer(u, w_pool, pool_scale):
    B, S, _ = u.shape
    ug = u.reshape(B, S, POOL_GROUPS, POOL_GROUP_DIM).astype(jnp.float32)
    P = jnp.pad(jnp.cumsum(ug, axis=1), ((0, 0), (1, 0), (0, 0), (0, 0)))
    t = jnp.arange(S, dtype=jnp.float32)
    outs = []
    for gi, w in enumerate(POOL_WINDOWS):
        Pg = P[:, :, gi]
        lag = jnp.pad(Pg, ((0, 0), (w, 0), (0, 0)))[:, :S + 1]
        window_sum = Pg[:, 1:] - lag[:, 1:]
        count = jnp.minimum(t + 1.0, float(w))
        outs.append(window_sum / count[None, :, None] - ug[:, :, gi])
    pooled = jnp.stack(outs, axis=2).astype(u.dtype)
    mixed = jnp.einsum('bsgc,gcd->bsgd', pooled, w_pool)
    return mixed.reshape(B, S, D_POOL) * pool_scale


def moe(hn, router_w, router_b, w_gu, b_gu, w_down, b_down):
    B, S, D = hn.shape
    xt = hn.reshape(-1, D)
    logits = xt.astype(jnp.float32) @ router_w.astype(jnp.float32) + router_b.astype(jnp.float32)
    top_vals, top_idx = lax.top_k(logits, TOP_K)
    gates = jax.nn.softmax(top_vals, axis=-1)
    flat_e = top_idx.reshape(-1)
    order = jnp.argsort(flat_e)
    sorted_e = flat_e[order]
    tok = order // TOP_K
    xs = xt[tok]
    group_sizes = jnp.bincount(flat_e, length=N_EXPERTS).astype(jnp.int32)
    gu = lax.ragged_dot(xs, w_gu, group_sizes) + b_gu[sorted_e]
    gate = jnp.minimum(gu[:, ::2], SWIGLU_LIMIT)
    up = jnp.clip(gu[:, 1::2], -SWIGLU_LIMIT, SWIGLU_LIMIT)
    act = (up + 1.0) * (gate * jax.nn.sigmoid(SWIGLU_ALPHA * gate))
    y = lax.ragged_dot(act.astype(w_down.dtype), w_down, group_sizes) + b_down[sorted_e]
    y = y * gates.reshape(-1)[order][:, None].astype(y.dtype)
    out = jnp.zeros_like(xt).at[tok].add(y.astype(xt.dtype))
    return out.reshape(B, S, D)


def setup_inputs(seed: int = 0) -> dict:
    key = jax.random.key(seed)
    ks = jax.random.split(key, 24)
    f32 = jnp.float32
    L = DEPTH

    def nrm(k, shape, scale):
        return jax.random.normal(k, shape, dtype=f32) * scale

    offsets = jax.random.randint(ks[2], (BATCH, 1), 0, 1024, dtype=jnp.int32)
    positions = (jnp.arange(SEQ, dtype=jnp.int32)[None, :] + offsets).astype(jnp.int32)
    return {
        "x": nrm(ks[0], (BATCH, SEQ, D_MODEL), 1.0),
        "p": nrm(ks[1], (DEPTH, BATCH, SEQ, PLE_DIM), 1.0),
        "positions": positions,
        "w_in": nrm(ks[3], (L, D_MODEL, D_IN_PROJ), D_MODEL ** -0.5),
        "w_out": nrm(ks[4], (L, D_MIX, D_MODEL), D_MIX ** -0.5),
        "ret_gn_w": 1.0 + nrm(ks[5], (L, D_RET), 0.02),
        "pool_w": nrm(ks[6], (L, POOL_GROUPS, POOL_GROUP_DIM, POOL_GROUP_DIM), POOL_GROUP_DIM ** -0.5),
        "pool_scale": 1.0 + nrm(ks[7], (L, D_POOL), 0.02),
        "norm_mix_w": 1.0 + nrm(ks[8], (L, D_MODEL), 0.02),
        "norm_moe_w": 1.0 + nrm(ks[9], (L, D_MODEL), 0.02),
        "router_w": nrm(ks[10], (L, D_MODEL, N_EXPERTS), D_MODEL ** -0.5),
        "router_b": nrm(ks[11], (L, N_EXPERTS), 0.01),
        "expert_w_gate_up": nrm(ks[12], (L, N_EXPERTS, D_MODEL, 2 * D_EXPERT), D_MODEL ** -0.5),
        "expert_b_gate_up": nrm(ks[13], (L, N_EXPERTS, 2 * D_EXPERT), 0.02),
        "expert_w_down": nrm(ks[14], (L, N_EXPERTS, D_EXPERT, D_MODEL), D_EXPERT ** -0.5),
        "expert_b_down": nrm(ks[15], (L, N_EXPERTS, D_MODEL), 0.02),
        "norm_ple_w": 1.0 + nrm(ks[16], (L, D_MODEL), 0.02),
        "ple_gate_w": nrm(ks[17], (L, D_MODEL, D_MODEL), D_MODEL ** -0.5),
        "ple_proj_w": nrm(ks[18], (L, PLE_DIM, D_MODEL), PLE_DIM ** -0.5),
        "final_norm_w": 1.0 + nrm(ks[19], (D_MODEL,), 0.02),
    }


def reference(x, p, positions, w_in, w_out, ret_gn_w, pool_w, pool_scale, norm_mix_w,
              norm_moe_w, router_w, router_b, expert_w_gate_up, expert_b_gate_up,
              expert_w_down, expert_b_down, norm_ple_w, ple_gate_w, ple_proj_w, final_norm_w):
    B, S, _ = x.shape
    h = x
    for l in range(DEPTH):
        hn = rms_norm(h, norm_mix_w[l])
        proj = hn @ w_in[l]
        q = proj[..., 0 * D_RET:1 * D_RET].reshape(B, S, RET_HEADS, RET_HEAD_DIM)
        k = proj[..., 1 * D_RET:2 * D_RET].reshape(B, S, RET_HEADS, RET_HEAD_DIM)
        v = proj[..., 2 * D_RET:3 * D_RET].reshape(B, S, RET_HEADS, RET_HEAD_DIM)
        g = proj[..., 3 * D_RET:4 * D_RET]
        u = proj[..., 4 * D_RET:]
        q = rope(q, positions)
        k = rope(k, positions)
        ret = jax.nn.silu(g) * retention(q, k, v, ret_gn_w[l])
        pool = pool_mixer(u, pool_w[l], pool_scale[l])
        h = h + jnp.concatenate([ret, pool], axis=-1) @ w_out[l]
        h = h + moe(rms_norm(h, norm_moe_w[l]), router_w[l], router_b[l],
                    expert_w_gate_up[l], expert_b_gate_up[l], expert_w_down[l], expert_b_down[l])
        gate = jax.nn.sigmoid(rms_norm(h, norm_ple_w[l]) @ ple_gate_w[l])
        h = h + gate * (p[l] @ ple_proj_w[l])
    return rms_norm(h, final_norm_w)
```

```python
import functools
import math

import numpy as np
import jax
import jax.numpy as jnp
from jax import lax
from jax.experimental import pallas as pl
from jax.experimental.pallas import tpu as pltpu

D_MODEL = 1024
D_RET = 512
D_POOL = 512
RET_HEADS = 8
RET_HEAD_DIM = 64
HEAD_PAIRS = RET_HEADS // 2
ROPE_BASE = 10000.0
POOL_WINDOWS = (2, 4, 8, 16)
POOL_GROUP_DIM = 128
POOL_HISTORY = 16
D_IN_PROJ = 4 * D_RET + D_POOL
N_EXPERTS = 32
TOP_K = 4
SWIGLU_LIMIT = 7.0
SWIGLU_ALPHA = 1.702
PLE_DIM = 256
NORM_EPS = 1e-5
GN_EPS = 1e-5

LANES = 128
NEG_BIG = -1e30

TOKEN_TILE = 512
RET_CHUNK = 256
EXPERT_TILE = 512
EXPERT_COL_CHUNK = 512
COPY_ROWS = 1024
VMEM_LIMIT = 56 * 1024 * 1024

F32 = jnp.float32
BF16 = jnp.bfloat16
U32 = jnp.uint32


def _params(semantics):
    return pltpu.CompilerParams(dimension_semantics=semantics, vmem_limit_bytes=VMEM_LIMIT)


def _dot(a, b):
    return jnp.dot(a, b, preferred_element_type=F32)


def _dot_nt(a, b):
    return lax.dot_general(a, b, (((1,), (1,)), ((), ())), preferred_element_type=F32)


def _dot_tn(a, b):
    return lax.dot_general(a, b, (((0,), (0,)), ((), ())), preferred_element_type=F32)


def _rms(x, w):
    ms = jnp.mean(x * x, axis=-1, keepdims=True)
    return x * lax.rsqrt(ms + NORM_EPS) * w


def _sigmoid(z):
    return 1.0 / (1.0 + jnp.exp(-z))


def _pack_bf16_pair(lo, hi):
    lo_bits = pltpu.bitcast(lo.astype(BF16).astype(F32), U32) >> 16
    hi_bits = pltpu.bitcast(hi.astype(BF16).astype(F32), U32) & jnp.uint32(0xFFFF0000)
    return lo_bits | hi_bits


def _unpack_bf16_pair(packed):
    lo = pltpu.bitcast(packed << 16, F32)
    hi = pltpu.bitcast(packed & jnp.uint32(0xFFFF0000), F32)
    return lo, hi


def _in_proj_kernel(x_ref, pos_ref, nw_ref, w_ref, freq_ref, out_ref):
    hn = _rms(x_ref[...], nw_ref[...]).astype(BF16)
    ang = pos_ref[...].astype(F32) * freq_ref[0:1, :]
    cos = jnp.cos(ang)
    sin = jnp.sin(ang)
    reps = D_RET // LANES
    cos_t = jnp.concatenate([cos] * reps, axis=1)
    sin_up = jnp.concatenate([sin * freq_ref[1:2, :]] * reps, axis=1)
    sin_dn = jnp.concatenate([sin * freq_ref[2:3, :]] * reps, axis=1)
    half = RET_HEAD_DIM // 2

    def rope(v):
        return (v * cos_t + pltpu.roll(v, D_RET - half, 1) * sin_up
                + pltpu.roll(v, half, 1) * sin_dn)

    q = rope(_dot(hn, w_ref[:, 0:D_RET]))
    out_ref[:, 0:D_RET] = q.astype(BF16)
    k = rope(_dot(hn, w_ref[:, D_RET:2 * D_RET])) * (RET_HEAD_DIM ** -0.5)
    out_ref[:, D_RET:2 * D_RET] = k.astype(BF16)
    out_ref[:, 2 * D_RET:] = _dot(hn, w_ref[:, 2 * D_RET:]).astype(BF16)


def _in_proj(x2, pos2, norm_w, w_in_bf):
    t = x2.shape[0]
    tm = TOKEN_TILE
    j = np.arange(LANES)
    half = RET_HEAD_DIM // 2
    inv_freq = ROPE_BASE ** (-(np.arange(half, dtype=np.float32)) / half)
    freq = np.zeros((8, LANES), np.float32)
    freq[0] = inv_freq[j % half]
    freq[1] = np.where(j % RET_HEAD_DIM < half, -1.0, 0.0)
    freq[2] = np.where(j % RET_HEAD_DIM >= half, 1.0, 0.0)
    return pl.pallas_call(
        _in_proj_kernel,
        grid=(t // tm,),
        in_specs=[
            pl.BlockSpec((tm, D_MODEL), lambda i: (i, 0)),
            pl.BlockSpec((tm, 1), lambda i: (i, 0)),
            pl.BlockSpec((1, D_MODEL), lambda i: (0, 0)),
            pl.BlockSpec((D_MODEL, D_IN_PROJ), lambda i: (0, 0)),
            pl.BlockSpec((8, LANES), lambda i: (0, 0)),
        ],
        out_specs=pl.BlockSpec((tm, D_IN_PROJ), lambda i: (i, 0)),
        out_shape=jax.ShapeDtypeStruct((t, D_IN_PROJ), BF16),
        compiler_params=_params(("parallel",)),
        name="in_proj",
    )(x2, pos2, norm_w, w_in_bf, jnp.asarray(freq))


def _retention_kernel(q_ref, k_ref, v_ref, g_ref, dec_ref, xi_ref, zeta_ref, cd_ref, bd_ref,
                      m64_ref, gnw_ref, out_ref, state_ref):
    @pl.when(pl.program_id(1) == 0)
    def _():
        state_ref[...] = jnp.zeros_like(state_ref)

    lane = lax.broadcasted_iota(jnp.int32, (1, LANES), 1)
    m64 = m64_ref[...]
    bd = bd_ref[...]

    def group_mean(v):
        hi = v.astype(BF16)
        lo = (v - hi.astype(F32)).astype(BF16)
        return _dot(hi, m64) + _dot(lo, m64)

    for p in range(HEAD_PAIRS):
        sl = slice(p * LANES, (p + 1) * LANES)
        qp = q_ref[:, sl]
        kp = k_ref[:, sl]
        vp = v_ref[:, sl]
        y = None
        for hh in range(2):
            in_head = (lane >= RET_HEAD_DIM) == bool(hh)
            qm = jnp.where(in_head, qp, jnp.zeros_like(qp))
            vm = jnp.where(in_head, vp, jnp.zeros_like(vp))
            scores = _dot_nt(qm, kp) * dec_ref[2 * p + hh]
            part = _dot(scores.astype(BF16), vm)
            y = part if y is None else y + part
        st = state_ref[p]
        y = y + _dot((qp.astype(F32) * xi_ref[p]).astype(BF16), st.astype(BF16))
        kz = (kp.astype(F32) * zeta_ref[p]).astype(BF16)
        state_ref[p] = cd_ref[p] * st + _dot_tn(kz, vp) * bd
        mu = group_mean(y)
        d = y - mu
        var = group_mean(d * d)
        yn = d * lax.rsqrt(var + GN_EPS) * gnw_ref[:, sl]
        g = g_ref[:, sl].astype(F32)
        out_ref[:, sl] = (yn * g * _sigmoid(g)).astype(BF16)


def _retention_tables(c):
    h = np.arange(RET_HEADS, dtype=np.float64)
    log_gamma = np.log1p(-np.power(2.0, -5.0 - h))
    idx = np.arange(c, dtype=np.float64)
    rel = idx[:, None] - idx[None, :]
    dec = np.where(rel >= 0, np.exp(np.where(rel >= 0, rel, 0.0)[None] * log_gamma[:, None, None]), 0.0)
    lane_head = np.arange(LANES) // RET_HEAD_DIM
    xi = np.zeros((HEAD_PAIRS, c, LANES))
    zeta = np.zeros((HEAD_PAIRS, c, LANES))
    cd = np.zeros((HEAD_PAIRS, LANES, LANES))
    same = lane_head[:, None] == lane_head[None, :]
    for p in range(HEAD_PAIRS):
        lg = log_gamma[2 * p + lane_head]
        xi[p] = np.exp((idx + 1.0)[:, None] * lg[None, :])
        zeta[p] = np.exp((c - 1 - idx)[:, None] * lg[None, :])
        cd[p] = np.where(same, np.exp(c * lg)[:, None], 0.0)
    bd = same.astype(np.float32)
    m64 = same.astype(np.float32) / RET_HEAD_DIM
    f = lambda a: jnp.asarray(a, dtype=F32)
    return f(dec), f(xi), f(zeta), f(cd), f(bd), jnp.asarray(m64, dtype=BF16)


def _retention(proj, gn_w, batch, seq):
    t = proj.shape[0]
    c = RET_CHUNK
    n = seq // c
    dec, xi, zeta, cd, bd, m64 = _retention_tables(c)
    col = lambda j: pl.BlockSpec((c, D_RET), lambda b, i, j=j: (b * n + i, j))
    const3 = lambda shape: pl.BlockSpec(shape, lambda b, i: (0, 0, 0))
    const2 = lambda shape: pl.BlockSpec(shape, lambda b, i: (0, 0))
    return pl.pallas_call(
        _retention_kernel,
        grid=(batch, n),
        in_specs=[col(0), col(1), col(2), col(3),
                  const3((RET_HEADS, c, c)), const3((HEAD_PAIRS, c, LANES)),
                  const3((HEAD_PAIRS, c, LANES)), const3((HEAD_PAIRS, LANES, LANES)),
                  const2((LANES, LANES)), const2((LANES, LANES)), const2((1, D_RET))],
        out_specs=pl.BlockSpec((c, D_RET), lambda b, i: (b * n + i, 0)),
        out_shape=jax.ShapeDtypeStruct((t, D_RET), BF16),
        scratch_shapes=[pltpu.VMEM((HEAD_PAIRS, LANES, LANES), F32)],
        compiler_params=_params(("arbitrary", "arbitrary")),
        name="retention",
    )(proj, proj, proj, proj, dec, xi, zeta, cd, bd, m64, gn_w)


def _mix_router_kernel(seq, x_ref, ret_ref, u_ref, uprev_ref, band_ref, pw_ref, ps_ref, wout_ref,
                       nw_ref, rwh_ref, rwl_ref, rb_ref,
                       h1_ref, hnp_ref, code_ref, gates_ref, cnt_ref, uext_ref):
    i = pl.program_id(0)
    tm = x_ref.shape[0]
    t0 = lax.rem(i * tm, seq)

    @pl.when(i == 0)
    def _():
        cnt_ref[...] = jnp.zeros_like(cnt_ref)

    prev = uprev_ref[...]
    uext_ref[0:POOL_HISTORY, :] = jnp.where(t0 == 0, jnp.zeros_like(prev), prev)
    uext_ref[POOL_HISTORY:, :] = u_ref[...]

    row = lax.broadcasted_iota(jnp.int32, (tm, 1), 0)
    t_seq = (t0 + row + 1).astype(F32)
    mixed = []
    for gi, w in enumerate(POOL_WINDOWS):
        sl = slice(gi * POOL_GROUP_DIM, (gi + 1) * POOL_GROUP_DIM)
        wsum = _dot(band_ref[gi], uext_ref[:, sl])
        count = jnp.minimum(t_seq, float(w))
        pooled = wsum / count - u_ref[:, sl].astype(F32)
        mixed.append(_dot(pooled.astype(BF16), pw_ref[gi]))
    pool = (jnp.concatenate(mixed, axis=1) * ps_ref[...]).astype(BF16)

    h1 = (x_ref[...] + _dot(ret_ref[...], wout_ref[0:D_RET, :])
          + _dot(pool, wout_ref[D_RET:, :]))
    h1_ref[...] = h1
    hn = _rms(h1, nw_ref[...])
    hn_hi = hn.astype(BF16)
    half = D_MODEL // 2
    hnp_ref[...] = _pack_bf16_pair(hn[:, :half], hn[:, half:])

    hn_lo = (hn - hn_hi.astype(F32)).astype(BF16)
    logits = (_dot(hn_hi, rwh_ref[...]) + _dot(hn_lo, rwh_ref[...])
              + _dot(hn_hi, rwl_ref[...]) + rb_ref[...])

    lane = lax.broadcasted_iota(jnp.int32, (tm, LANES), 1).astype(F32)
    code = jnp.zeros((tm, LANES), F32)
    vals = []
    work = logits
    for k in range(TOP_K):
        m = jnp.max(work, axis=-1, keepdims=True)
        idx = jnp.min(jnp.where(work == m, lane, float(LANES)), axis=-1, keepdims=True)
        chosen = lane == idx
        code = jnp.where(chosen, float(k + 1), code)
        work = jnp.where(chosen, -jnp.inf, work)
        vals.append(m)
    exps = [jnp.exp(v - vals[0]) for v in vals]
    denom = exps[0] + exps[1] + exps[2] + exps[3]
    gates = jnp.zeros((tm, LANES), F32)
    for k in range(TOP_K):
        gates = jnp.where(lane == float(k), exps[k] / denom, gates)
    code_ref[...] = code
    gates_ref[...] = gates
    cnt_ref[...] += jnp.sum((code > 0).astype(F32), axis=0, keepdims=True)


def _pool_bands(tm):
    r = np.arange(tm)[:, None]
    s = np.arange(tm + POOL_HISTORY)[None, :] - POOL_HISTORY
    bands = [((s <= r) & (s > r - w)) for w in POOL_WINDOWS]
    return jnp.asarray(np.stack(bands).astype(np.float32), dtype=BF16)


def _mix_router(x2, ret, proj, pool_w_bf, pool_scale, w_out_bf, norm_w, rw_hi, rw_lo, rb, seq):
    t = x2.shape[0]
    tm = TOKEN_TILE
    hist_blocks = tm // POOL_HISTORY
    u_col = (4 * D_RET) // D_POOL
    tile = lambda width: pl.BlockSpec((tm, width), lambda i: (i, 0))
    const2 = lambda shape: pl.BlockSpec(shape, lambda i: (0, 0))
    const3 = lambda shape: pl.BlockSpec(shape, lambda i: (0, 0, 0))
    return pl.pallas_call(
        functools.partial(_mix_router_kernel, seq),
        grid=(t // tm,),
        in_specs=[
            tile(D_MODEL), tile(D_RET),
            pl.BlockSpec((tm, D_POOL), lambda i: (i, u_col)),
            pl.BlockSpec((POOL_HISTORY, D_POOL), lambda i: (jnp.maximum(i * hist_blocks - 1, 0), u_col)),
            const3((len(POOL_WINDOWS), tm, tm + POOL_HISTORY)),
            const3((len(POOL_WINDOWS), POOL_GROUP_DIM, POOL_GROUP_DIM)),
            const2((1, D_POOL)), const2((D_MODEL, D_MODEL)), const2((1, D_MODEL)),
            const2((D_MODEL, LANES)), const2((D_MODEL, LANES)), const2((1, LANES)),
        ],
        out_specs=[tile(D_MODEL), tile(D_MODEL // 2), tile(LANES), tile(LANES), const2((1, LANES))],
        out_shape=[
            jax.ShapeDtypeStruct((t, D_MODEL), F32),
            jax.ShapeDtypeStruct((t, D_MODEL // 2), U32),
            jax.ShapeDtypeStruct((t, LANES), F32),
            jax.ShapeDtypeStruct((t, LANES), F32),
            jax.ShapeDtypeStruct((1, LANES), F32),
        ],
        scratch_shapes=[pltpu.VMEM((tm + POOL_HISTORY, D_POOL), BF16)],
        compiler_params=_params(("arbitrary",)),
        name="mix_router",
    )(x2, ret, proj, proj, _pool_bands(tm), pool_w_bf, pool_scale, w_out_bf, norm_w, rw_hi, rw_lo, rb)


def _positions_kernel(code_ref, off_ref, tri_ref, pos_ref, carry_ref):
    @pl.when(pl.program_id(0) == 0)
    def _():
        carry_ref[...] = jnp.zeros_like(carry_ref)

    code = code_ref[...]
    tm = code.shape[0]
    sel = (code > 0).astype(BF16)
    carry = carry_ref[...]
    rank = _dot(tri_ref[...], sel) + (carry + off_ref[...])
    carry_ref[...] = carry + jnp.sum(sel.astype(F32), axis=0, keepdims=True)
    lane = lax.broadcasted_iota(jnp.int32, (tm, LANES), 1)
    pos = jnp.zeros((tm, LANES), F32)
    for k in range(TOP_K):
        pk = jnp.sum(jnp.where(code == float(k + 1), rank, 0.0), axis=-1, keepdims=True)
        pos = jnp.where(lane == k, pk, pos)
    pos_ref[...] = pos.astype(jnp.int32)


def _positions(code, offsets):
    t = code.shape[0]
    tm = TOKEN_TILE
    tri = jnp.asarray(np.tril(np.ones((tm, tm), np.float32), -1), dtype=BF16)
    return pl.pallas_call(
        _positions_kernel,
        grid=(t // tm,),
        in_specs=[pl.BlockSpec((tm, LANES), lambda i: (i, 0)),
                  pl.BlockSpec((1, LANES), lambda i: (0, 0)),
                  pl.BlockSpec((tm, tm), lambda i: (0, 0))],
        out_specs=pl.BlockSpec((tm, LANES), lambda i: (i, 0)),
        out_shape=jax.ShapeDtypeStruct((t, LANES), jnp.int32),
        scratch_shapes=[pltpu.VMEM((1, LANES), F32)],
        compiler_params=_params(("arbitrary",)),
        name="positions",
    )(code, offsets, tri)


def _row_copy_kernel(src_idx_ref, dst_idx_ref, src_ref, dst_ref, sem):
    n = src_idx_ref.shape[1]

    def row_dma(j):
        return pltpu.make_async_copy(src_ref.at[pl.ds(src_idx_ref[0, j], 1)],
                                     dst_ref.at[pl.ds(dst_idx_ref[0, j], 1)], sem)

    def issue(j, carry):
        row_dma(j).start()
        return carry

    lax.fori_loop(0, n, issue, 0)

    def drain(j, carry):
        row_dma(j).wait()
        return carry

    lax.fori_loop(0, n, drain, 0)


def _row_copy(src, src_idx, dst_idx, n_dst):
    n = src_idx.shape[0]
    rows = COPY_ROWS
    idx_spec = pl.BlockSpec((None, 1, rows), lambda i: (i, 0, 0), memory_space=pltpu.SMEM)
    return pl.pallas_call(
        _row_copy_kernel,
        grid=(n // rows,),
        in_specs=[idx_spec, idx_spec, pl.BlockSpec(memory_space=pl.ANY)],
        out_specs=pl.BlockSpec(memory_space=pl.ANY),
        out_shape=jax.ShapeDtypeStruct((n_dst, src.shape[1]), src.dtype),
        scratch_shapes=[pltpu.SemaphoreType.DMA],
        compiler_params=_params(("arbitrary",)),
        name="row_copy",
    )(src_idx.reshape(n // rows, 1, rows), dst_idx.reshape(n // rows, 1, rows), src)


def _experts_kernel(tile_ref, exp_ref, lo_ref, hi_ref, first_ref, valid_ref,
                    xs_ref, wg_ref, wu_ref, bg_ref, bu_ref, wd_ref, bd_ref, y_ref):
    i = pl.program_id(0)
    tm = xs_ref.shape[0]

    @pl.when(valid_ref[i] == 1)
    def _():
        x_lo, x_hi = _unpack_bf16_pair(xs_ref[...])
        x = jnp.concatenate([x_lo.astype(BF16), x_hi.astype(BF16)], axis=1)
        acc = jnp.zeros((tm, D_MODEL), F32)
        for c in range(D_MODEL // EXPERT_COL_CHUNK):
            cs = slice(c * EXPERT_COL_CHUNK, (c + 1) * EXPERT_COL_CHUNK)
            gate = jnp.minimum(_dot(x, wg_ref[:, cs]) + bg_ref[:, cs], SWIGLU_LIMIT)
            up = jnp.clip(_dot(x, wu_ref[:, cs]) + bu_ref[:, cs], -SWIGLU_LIMIT, SWIGLU_LIMIT)
            act = (up + 1.0) * (gate * _sigmoid(SWIGLU_ALPHA * gate))
            acc = acc + _dot(act.astype(BF16), wd_ref[cs, :])
        y = acc + bd_ref[...]
        half = D_MODEL // 2
        packed = _pack_bf16_pair(y[:, :half], y[:, half:])
        rows = tile_ref[i] * tm + lax.broadcasted_iota(jnp.int32, (tm, 1), 0)
        mine = (rows >= lo_ref[i]) & (rows < hi_ref[i])

        @pl.when(first_ref[i] == 1)
        def _():
            y_ref[...] = packed

        @pl.when(first_ref[i] == 0)
        def _():
            y_ref[...] = jnp.where(mine, packed, y_ref[...])


def _expert_items(counts, n_rows, tm):
    n_tiles = n_rows // tm
    n_items = n_tiles + N_EXPERTS - 1
    ends = jnp.cumsum(counts)
    starts = ends - counts
    first_tile = starts // tm
    last_tile = jnp.maximum(ends - 1, 0) // tm
    per_expert = jnp.where(counts > 0, last_tile - first_tile + 1, 0)
    item_end = jnp.cumsum(per_expert)
    item_start = item_end - per_expert
    total = item_end[-1]
    item = jnp.arange(n_items, dtype=jnp.int32)
    valid = item < total
    clamped = jnp.minimum(item, total - 1)
    expert = jnp.sum((item_end[None, :] <= clamped[:, None]).astype(jnp.int32), axis=1)
    tile = first_tile[expert] + clamped - item_start[expert]
    prev_tile = jnp.concatenate([jnp.full((1,), -1, jnp.int32), tile[:-1]])
    first = (tile != prev_tile) & valid
    i32 = lambda a: a.astype(jnp.int32)
    return (i32(tile), i32(expert), i32(starts[expert]), i32(ends[expert]), i32(first), i32(valid))


def _experts(xs, items, wg, wu, bg, bu, wd, bd):
    n_rows = xs.shape[0]
    tm = EXPERT_TILE
    n_items = items[0].shape[0]
    half = D_MODEL // 2
    row_tile = pl.BlockSpec((tm, half), lambda i, tile, *_: (tile[i], 0))
    weight = pl.BlockSpec((None, D_MODEL, D_MODEL), lambda i, tile, exp, *_: (exp[i], 0, 0))
    bias = pl.BlockSpec((None, 1, D_MODEL), lambda i, tile, exp, *_: (exp[i], 0, 0))
    return pl.pallas_call(
        _experts_kernel,
        grid_spec=pltpu.PrefetchScalarGridSpec(
            num_scalar_prefetch=6,
            grid=(n_items,),
            in_specs=[row_tile, weight, weight, bias, bias, weight, bias],
            out_specs=row_tile,
        ),
        out_shape=jax.ShapeDtypeStruct((n_rows, half), U32),
        compiler_params=_params(("arbitrary",)),
        name="experts",
    )(*items, xs, wg, wu, bg, bu, wd, bd)


def _tail_kernel(h1_ref, yu_ref, gates_ref, p_ref, nple_ref, wg_ref, wp_ref, nfin_ref, out_ref):
    gates = gates_ref[...]
    lo = None
    hi = None
    for k in range(TOP_K):
        gk = gates[:, k:k + 1]
        yl, yh = _unpack_bf16_pair(yu_ref[k])
        lo = gk * yl if lo is None else lo + gk * yl
        hi = gk * yh if hi is None else hi + gk * yh
    h2 = h1_ref[...] + jnp.concatenate([lo, hi], axis=1)
    hn = _rms(h2, nple_ref[...]).astype(BF16)
    gate = _sigmoid(_dot(hn, wg_ref[...]))
    h3 = h2 + gate * _dot(p_ref[...].astype(BF16), wp_ref[...])
    out_ref[...] = _rms(h3, nfin_ref[...])


def _tail(h1, yu, gates, p2, norm_ple_w, ple_gate_bf, ple_proj_bf, final_norm_w):
    t = h1.shape[0]
    tm = TOKEN_TILE
    half = D_MODEL // 2
    tile = lambda width: pl.BlockSpec((tm, width), lambda i: (i, 0))
    const2 = lambda shape: pl.BlockSpec(shape, lambda i: (0, 0))
    return pl.pallas_call(
        _tail_kernel,
        grid=(t // tm,),
        in_specs=[tile(D_MODEL), pl.BlockSpec((TOP_K, tm, half), lambda i: (0, i, 0)),
                  tile(LANES), tile(PLE_DIM), const2((1, D_MODEL)),
                  const2((D_MODEL, D_MODEL)), const2((PLE_DIM, D_MODEL)), const2((1, D_MODEL))],
        out_specs=tile(D_MODEL),
        out_shape=jax.ShapeDtypeStruct((t, D_MODEL), F32),
        compiler_params=_params(("parallel",)),
        name="tail",
    )(h1, yu, gates, p2, norm_ple_w, ple_gate_bf, ple_proj_bf, final_norm_w)


def kernel(x, p, positions, w_in, w_out, ret_gn_w, pool_w, pool_scale, norm_mix_w, norm_moe_w, router_w, router_b, expert_w_gate_up, expert_b_gate_up, expert_w_down, expert_b_down, norm_ple_w, ple_gate_w, ple_proj_w, final_norm_w):
    batch, seq, d = x.shape
    depth = w_in.shape[0]
    assert depth == 1 and d == D_MODEL and seq % TOKEN_TILE == 0 and seq % RET_CHUNK == 0
    t = batch * seq
    n_rows = t * TOP_K
    assert n_rows % EXPERT_TILE == 0 and n_rows % COPY_ROWS == 0
    row = lambda a: a.reshape(1, -1).astype(F32)

    h = x.reshape(t, d)
    pos2 = positions.reshape(t, 1)
    for l in range(depth):
        proj = _in_proj(h, pos2, row(norm_mix_w[l]), w_in[l].astype(BF16))
        ret = _retention(proj, row(ret_gn_w[l]), batch, seq)

        rw = jnp.pad(router_w[l].astype(F32), ((0, 0), (0, LANES - N_EXPERTS)))
        rw_hi = rw.astype(BF16)
        rw_lo = (rw - rw_hi.astype(F32)).astype(BF16)
        rb = jnp.pad(router_b[l].astype(F32), (0, LANES - N_EXPERTS), constant_values=NEG_BIG).reshape(1, LANES)
        h1, hn_packed, code, gates, counts = _mix_router(
            h, ret, proj, pool_w[l].astype(BF16), row(pool_scale[l]), w_out[l].astype(BF16),
            row(norm_moe_w[l]), rw_hi, rw_lo, rb, seq)

        counts_i = counts[0, :N_EXPERTS].astype(jnp.int32)
        starts = jnp.cumsum(counts_i) - counts_i
        offsets = jnp.pad(starts.astype(F32), (0, LANES - N_EXPERTS)).reshape(1, LANES)
        pos = _positions(code, offsets)
        pos_km = pos[:, :TOP_K].T.reshape(n_rows)
        token_km = jnp.tile(jnp.arange(t, dtype=jnp.int32), TOP_K)
        slot_km = jnp.arange(n_rows, dtype=jnp.int32)

        xs = _row_copy(hn_packed, token_km, pos_km, n_rows)
        items = _expert_items(counts_i, n_rows, EXPERT_TILE)
        wgu = expert_w_gate_up[l].reshape(N_EXPERTS, D_MODEL, D_MODEL, 2)
        bgu = expert_b_gate_up[l].reshape(N_EXPERTS, 1, D_MODEL, 2).astype(F32)
        y = _experts(xs, items, wgu[..., 0].astype(BF16), wgu[..., 1].astype(BF16),
                     bgu[..., 0], bgu[..., 1], expert_w_down[l].astype(BF16),
                     expert_b_down[l].reshape(N_EXPERTS, 1, D_MODEL).astype(F32))
        yu = _row_copy(y, pos_km, slot_km, n_rows).reshape(TOP_K, t, d // 2)

        h = _tail(h1, yu, gates, p[l].reshape(t, PLE_DIM), row(norm_ple_w[l]),
                  ple_gate_w[l].astype(BF16), ple_proj_w[l].astype(BF16), row(final_norm_w))
    return h.reshape(batch, seq, d)
```

```python
import functools
import math

import numpy as np
import jax
import jax.numpy as jnp
from jax import lax
from jax.experimental import pallas as pl
from jax.experimental.pallas import tpu as pltpu
from jax.experimental.pallas import tpu_sc as plsc

D_MODEL = 1024
D_RET = 512
D_POOL = 512
RET_HEADS = 8
RET_HEAD_DIM = 64
HEAD_PAIRS = RET_HEADS // 2
ROPE_BASE = 10000.0
POOL_WINDOWS = (2, 4, 8, 16)
POOL_GROUP_DIM = 128
POOL_HISTORY = 16
D_IN_PROJ = 4 * D_RET + D_POOL
N_EXPERTS = 32
TOP_K = 4
SWIGLU_LIMIT = 7.0
SWIGLU_ALPHA = 1.702
PLE_DIM = 256
NORM_EPS = 1e-5
GN_EPS = 1e-5

LANES = 128
NEG_BIG = -1e30

TOKEN_TILE = 512
RET_CHUNK = 256
EXPERT_TILE = 512
EXPERT_COL_CHUNK = 512
SC_WINDOW = 128
VMEM_LIMIT = 56 * 1024 * 1024

F32 = jnp.float32
BF16 = jnp.bfloat16
U32 = jnp.uint32


def _params(semantics):
    return pltpu.CompilerParams(dimension_semantics=semantics, vmem_limit_bytes=VMEM_LIMIT)


def _dot(a, b):
    return jnp.dot(a, b, preferred_element_type=F32)


def _dot_nt(a, b):
    return lax.dot_general(a, b, (((1,), (1,)), ((), ())), preferred_element_type=F32)


def _dot_tn(a, b):
    return lax.dot_general(a, b, (((0,), (0,)), ((), ())), preferred_element_type=F32)


def _rms(x, w):
    ms = jnp.mean(x * x, axis=-1, keepdims=True)
    return x * lax.rsqrt(ms + NORM_EPS) * w


def _sigmoid(z):
    return 1.0 / (1.0 + jnp.exp(-z))


def _pack_bf16_pair(lo, hi):
    lo_bits = pltpu.bitcast(lo.astype(BF16).astype(F32), U32) >> 16
    hi_bits = pltpu.bitcast(hi.astype(BF16).astype(F32), U32) & jnp.uint32(0xFFFF0000)
    return lo_bits | hi_bits


def _unpack_bf16_pair(packed):
    lo = pltpu.bitcast(packed << 16, F32)
    hi = pltpu.bitcast(packed & jnp.uint32(0xFFFF0000), F32)
    return lo, hi


def _in_proj_kernel(x_ref, pos_ref, nw_ref, w_ref, freq_ref, out_ref):
    hn = _rms(x_ref[...], nw_ref[...]).astype(BF16)
    ang = pos_ref[...].astype(F32) * freq_ref[0:1, :]
    cos = jnp.cos(ang)
    sin = jnp.sin(ang)
    reps = D_RET // LANES
    cos_t = jnp.concatenate([cos] * reps, axis=1)
    sin_up = jnp.concatenate([sin * freq_ref[1:2, :]] * reps, axis=1)
    sin_dn = jnp.concatenate([sin * freq_ref[2:3, :]] * reps, axis=1)
    half = RET_HEAD_DIM // 2

    def rope(v):
        return (v * cos_t + pltpu.roll(v, D_RET - half, 1) * sin_up
                + pltpu.roll(v, half, 1) * sin_dn)

    q = rope(_dot(hn, w_ref[:, 0:D_RET]))
    out_ref[:, 0:D_RET] = q.astype(BF16)
    k = rope(_dot(hn, w_ref[:, D_RET:2 * D_RET])) * (RET_HEAD_DIM ** -0.5)
    out_ref[:, D_RET:2 * D_RET] = k.astype(BF16)
    out_ref[:, 2 * D_RET:] = _dot(hn, w_ref[:, 2 * D_RET:]).astype(BF16)


def _in_proj(x2, pos2, norm_w, w_in_bf):
    t = x2.shape[0]
    tm = TOKEN_TILE
    j = np.arange(LANES)
    half = RET_HEAD_DIM // 2
    inv_freq = ROPE_BASE ** (-(np.arange(half, dtype=np.float32)) / half)
    freq = np.zeros((8, LANES), np.float32)
    freq[0] = inv_freq[j % half]
    freq[1] = np.where(j % RET_HEAD_DIM < half, -1.0, 0.0)
    freq[2] = np.where(j % RET_HEAD_DIM >= half, 1.0, 0.0)
    return pl.pallas_call(
        _in_proj_kernel,
        grid=(t // tm,),
        in_specs=[
            pl.BlockSpec((tm, D_MODEL), lambda i: (i, 0)),
            pl.BlockSpec((tm, 1), lambda i: (i, 0)),
            pl.BlockSpec((1, D_MODEL), lambda i: (0, 0)),
            pl.BlockSpec((D_MODEL, D_IN_PROJ), lambda i: (0, 0)),
            pl.BlockSpec((8, LANES), lambda i: (0, 0)),
        ],
        out_specs=pl.BlockSpec((tm, D_IN_PROJ), lambda i: (i, 0)),
        out_shape=jax.ShapeDtypeStruct((t, D_IN_PROJ), BF16),
        compiler_params=_params(("parallel",)),
        name="in_proj",
    )(x2, pos2, norm_w, w_in_bf, jnp.asarray(freq))


def _retention_kernel(q_ref, k_ref, v_ref, g_ref, dec_ref, xi_ref, zeta_ref, cd_ref, bd_ref,
                      m64_ref, gnw_ref, out_ref, state_ref):
    @pl.when(pl.program_id(1) == 0)
    def _():
        state_ref[...] = jnp.zeros_like(state_ref)

    lane = lax.broadcasted_iota(jnp.int32, (1, LANES), 1)
    m64 = m64_ref[...]
    bd = bd_ref[...]

    def group_mean(v):
        hi = v.astype(BF16)
        lo = (v - hi.astype(F32)).astype(BF16)
        return _dot(hi, m64) + _dot(lo, m64)

    for p in range(HEAD_PAIRS):
        sl = slice(p * LANES, (p + 1) * LANES)
        qp = q_ref[:, sl]
        kp = k_ref[:, sl]
        vp = v_ref[:, sl]
        y = None
        for hh in range(2):
            in_head = (lane >= RET_HEAD_DIM) == bool(hh)
            qm = jnp.where(in_head, qp, jnp.zeros_like(qp))
            vm = jnp.where(in_head, vp, jnp.zeros_like(vp))
            scores = _dot_nt(qm, kp) * dec_ref[2 * p + hh]
            part = _dot(scores.astype(BF16), vm)
            y = part if y is None else y + part
        st = state_ref[p]
        y = y + _dot((qp.astype(F32) * xi_ref[p]).astype(BF16), st.astype(BF16))
        kz = (kp.astype(F32) * zeta_ref[p]).astype(BF16)
        state_ref[p] = cd_ref[p] * st + _dot_tn(kz, vp) * bd
        mu = group_mean(y)
        d = y - mu
        var = group_mean(d * d)
        yn = d * lax.rsqrt(var + GN_EPS) * gnw_ref[:, sl]
        g = g_ref[:, sl].astype(F32)
        out_ref[:, sl] = (yn * g * _sigmoid(g)).astype(BF16)


def _retention_tables(c):
    h = np.arange(RET_HEADS, dtype=np.float64)
    log_gamma = np.log1p(-np.power(2.0, -5.0 - h))
    idx = np.arange(c, dtype=np.float64)
    rel = idx[:, None] - idx[None, :]
    dec = np.where(rel >= 0, np.exp(np.where(rel >= 0, rel, 0.0)[None] * log_gamma[:, None, None]), 0.0)
    lane_head = np.arange(LANES) // RET_HEAD_DIM
    xi = np.zeros((HEAD_PAIRS, c, LANES))
    zeta = np.zeros((HEAD_PAIRS, c, LANES))
    cd = np.zeros((HEAD_PAIRS, LANES, LANES))
    same = lane_head[:, None] == lane_head[None, :]
    for p in range(HEAD_PAIRS):
        lg = log_gamma[2 * p + lane_head]
        xi[p] = np.exp((idx + 1.0)[:, None] * lg[None, :])
        zeta[p] = np.exp((c - 1 - idx)[:, None] * lg[None, :])
        cd[p] = np.where(same, np.exp(c * lg)[:, None], 0.0)
    bd = same.astype(np.float32)
    m64 = same.astype(np.float32) / RET_HEAD_DIM
    f = lambda a: jnp.asarray(a, dtype=F32)
    return f(dec), f(xi), f(zeta), f(cd), f(bd), jnp.asarray(m64, dtype=BF16)


def _retention(proj, gn_w, batch, seq):
    t = proj.shape[0]
    c = RET_CHUNK
    n = seq // c
    dec, xi, zeta, cd, bd, m64 = _retention_tables(c)
    col = lambda j: pl.BlockSpec((c, D_RET), lambda b, i, j=j: (b * n + i, j))
    const3 = lambda shape: pl.BlockSpec(shape, lambda b, i: (0, 0, 0))
    const2 = lambda shape: pl.BlockSpec(shape, lambda b, i: (0, 0))
    return pl.pallas_call(
        _retention_kernel,
        grid=(batch, n),
        in_specs=[col(0), col(1), col(2), col(3),
                  const3((RET_HEADS, c, c)), const3((HEAD_PAIRS, c, LANES)),
                  const3((HEAD_PAIRS, c, LANES)), const3((HEAD_PAIRS, LANES, LANES)),
                  const2((LANES, LANES)), const2((LANES, LANES)), const2((1, D_RET))],
        out_specs=pl.BlockSpec((c, D_RET), lambda b, i: (b * n + i, 0)),
        out_shape=jax.ShapeDtypeStruct((t, D_RET), BF16),
        scratch_shapes=[pltpu.VMEM((HEAD_PAIRS, LANES, LANES), F32)],
        compiler_params=_params(("arbitrary", "arbitrary")),
        name="retention",
    )(proj, proj, proj, proj, dec, xi, zeta, cd, bd, m64, gn_w)


def _mix_router_kernel(seq, x_ref, ret_ref, u_ref, uprev_ref, band_ref, pw_ref, ps_ref, wout_ref,
                       nw_ref, rwh_ref, rwl_ref, rb_ref,
                       h1_ref, hnp_ref, code_ref, gates_ref, cnt_ref, uext_ref):
    i = pl.program_id(0)
    tm = x_ref.shape[0]
    t0 = lax.rem(i * tm, seq)

    @pl.when(i == 0)
    def _():
        cnt_ref[...] = jnp.zeros_like(cnt_ref)

    prev = uprev_ref[...]
    uext_ref[0:POOL_HISTORY, :] = jnp.where(t0 == 0, jnp.zeros_like(prev), prev)
    uext_ref[POOL_HISTORY:, :] = u_ref[...]

    row = lax.broadcasted_iota(jnp.int32, (tm, 1), 0)
    t_seq = (t0 + row + 1).astype(F32)
    mixed = []
    for gi, w in enumerate(POOL_WINDOWS):
        sl = slice(gi * POOL_GROUP_DIM, (gi + 1) * POOL_GROUP_DIM)
        wsum = _dot(band_ref[gi], uext_ref[:, sl])
        count = jnp.minimum(t_seq, float(w))
        pooled = wsum / count - u_ref[:, sl].astype(F32)
        mixed.append(_dot(pooled.astype(BF16), pw_ref[gi]))
    pool = (jnp.concatenate(mixed, axis=1) * ps_ref[...]).astype(BF16)

    h1 = (x_ref[...] + _dot(ret_ref[...], wout_ref[0:D_RET, :])
          + _dot(pool, wout_ref[D_RET:, :]))
    h1_ref[...] = h1
    hn = _rms(h1, nw_ref[...])
    hn_hi = hn.astype(BF16)
    half = D_MODEL // 2
    hnp_ref[...] = _pack_bf16_pair(hn[:, :half], hn[:, half:])

    hn_lo = (hn - hn_hi.astype(F32)).astype(BF16)
    logits = (_dot(hn_hi, rwh_ref[...]) + _dot(hn_lo, rwh_ref[...])
              + _dot(hn_hi, rwl_ref[...]) + rb_ref[...])

    lane = lax.broadcasted_iota(jnp.int32, (tm, LANES), 1).astype(F32)
    code = jnp.zeros((tm, LANES), F32)
    vals = []
    work = logits
    for k in range(TOP_K):
        m = jnp.max(work, axis=-1, keepdims=True)
        idx = jnp.min(jnp.where(work == m, lane, float(LANES)), axis=-1, keepdims=True)
        chosen = lane == idx
        code = jnp.where(chosen, float(k + 1), code)
        work = jnp.where(chosen, -jnp.inf, work)
        vals.append(m)
    exps = [jnp.exp(v - vals[0]) for v in vals]
    denom = exps[0] + exps[1] + exps[2] + exps[3]
    gates = jnp.zeros((tm, LANES), F32)
    for k in range(TOP_K):
        gates = jnp.where(lane == float(k), exps[k] / denom, gates)
    code_ref[...] = code
    gates_ref[...] = gates
    cnt_ref[...] += jnp.sum((code > 0).astype(F32), axis=0, keepdims=True)


def _pool_bands(tm):
    r = np.arange(tm)[:, None]
    s = np.arange(tm + POOL_HISTORY)[None, :] - POOL_HISTORY
    bands = [((s <= r) & (s > r - w)) for w in POOL_WINDOWS]
    return jnp.asarray(np.stack(bands).astype(np.float32), dtype=BF16)


def _mix_router(x2, ret, proj, pool_w_bf, pool_scale, w_out_bf, norm_w, rw_hi, rw_lo, rb, seq):
    t = x2.shape[0]
    tm = TOKEN_TILE
    hist_blocks = tm // POOL_HISTORY
    u_col = (4 * D_RET) // D_POOL
    tile = lambda width: pl.BlockSpec((tm, width), lambda i: (i, 0))
    const2 = lambda shape: pl.BlockSpec(shape, lambda i: (0, 0))
    const3 = lambda shape: pl.BlockSpec(shape, lambda i: (0, 0, 0))
    return pl.pallas_call(
        functools.partial(_mix_router_kernel, seq),
        grid=(t // tm,),
        in_specs=[
            tile(D_MODEL), tile(D_RET),
            pl.BlockSpec((tm, D_POOL), lambda i: (i, u_col)),
            pl.BlockSpec((POOL_HISTORY, D_POOL), lambda i: (jnp.maximum(i * hist_blocks - 1, 0), u_col)),
            const3((len(POOL_WINDOWS), tm, tm + POOL_HISTORY)),
            const3((len(POOL_WINDOWS), POOL_GROUP_DIM, POOL_GROUP_DIM)),
            const2((1, D_POOL)), const2((D_MODEL, D_MODEL)), const2((1, D_MODEL)),
            const2((D_MODEL, LANES)), const2((D_MODEL, LANES)), const2((1, LANES)),
        ],
        out_specs=[tile(D_MODEL), tile(D_MODEL // 2), tile(LANES), tile(LANES), const2((1, LANES))],
        out_shape=[
            jax.ShapeDtypeStruct((t, D_MODEL), F32),
            jax.ShapeDtypeStruct((t, D_MODEL // 2), U32),
            jax.ShapeDtypeStruct((t, LANES), F32),
            jax.ShapeDtypeStruct((t, LANES), F32),
            jax.ShapeDtypeStruct((1, LANES), F32),
        ],
        scratch_shapes=[pltpu.VMEM((tm + POOL_HISTORY, D_POOL), BF16)],
        compiler_params=_params(("arbitrary",)),
        name="mix_router",
    )(x2, ret, proj, proj, _pool_bands(tm), pool_w_bf, pool_scale, w_out_bf, norm_w, rw_hi, rw_lo, rb)


def _positions_kernel(code_ref, off_ref, tri_ref, pos_ref, carry_ref):
    @pl.when(pl.program_id(0) == 0)
    def _():
        carry_ref[...] = jnp.zeros_like(carry_ref)

    code = code_ref[...]
    tm = code.shape[0]
    sel = (code > 0).astype(BF16)
    carry = carry_ref[...]
    rank = _dot(tri_ref[...], sel) + (carry + off_ref[...])
    carry_ref[...] = carry + jnp.sum(sel.astype(F32), axis=0, keepdims=True)
    lane = lax.broadcasted_iota(jnp.int32, (tm, LANES), 1)
    pos = jnp.zeros((tm, LANES), F32)
    for k in range(TOP_K):
        pk = jnp.sum(jnp.where(code == float(k + 1), rank, 0.0), axis=-1, keepdims=True)
        pos = jnp.where(lane == k, pk, pos)
    pos_ref[...] = pos.astype(jnp.int32)


def _positions(code, offsets):
    t = code.shape[0]
    tm = TOKEN_TILE
    tri = jnp.asarray(np.tril(np.ones((tm, tm), np.float32), -1), dtype=BF16)
    return pl.pallas_call(
        _positions_kernel,
        grid=(t // tm,),
        in_specs=[pl.BlockSpec((tm, LANES), lambda i: (i, 0)),
                  pl.BlockSpec((1, LANES), lambda i: (0, 0)),
                  pl.BlockSpec((tm, tm), lambda i: (0, 0))],
        out_specs=pl.BlockSpec((tm, LANES), lambda i: (i, 0)),
        out_shape=jax.ShapeDtypeStruct((t, LANES), jnp.int32),
        scratch_shapes=[pltpu.VMEM((1, LANES), F32)],
        compiler_params=_params(("arbitrary",)),
        name="positions",
    )(code, offsets, tri)


def _sc_mesh():
    return plsc.VectorSubcoreMesh(core_axis_name="core", subcore_axis_name="subcore")


def _sc_workers():
    info = plsc.get_sparse_core_info()
    return info.num_cores, info.num_cores * info.num_subcores


def _sc_gather(src, idx):
    n = idx.shape[0]
    d = src.shape[1]
    num_cores, workers = _sc_workers()
    per = n // workers
    assert per * workers == n and per % SC_WINDOW == 0

    @functools.partial(
        pl.kernel, out_type=jax.ShapeDtypeStruct((n, d), src.dtype), mesh=_sc_mesh(),
        scratch_types=[pltpu.VMEM((SC_WINDOW,), jnp.int32), pltpu.VMEM((SC_WINDOW, d), src.dtype),
                       pltpu.SemaphoreType.DMA],
        name="sc_gather")
    def gather(src_hbm, idx_hbm, out_hbm, idx_v, rows_v, sem):
        wid = lax.axis_index("subcore") * num_cores + lax.axis_index("core")

        @pl.loop(0, per // SC_WINDOW)
        def _(j):
            base = wid * per + j * SC_WINDOW
            pltpu.sync_copy(idx_hbm.at[pl.ds(base, SC_WINDOW)], idx_v)
            pltpu.async_copy(src_hbm.at[idx_v], rows_v, sem).wait()
            pltpu.sync_copy(rows_v, out_hbm.at[pl.ds(base, SC_WINDOW)])

    return gather(src, idx)


def _sc_scatter(src, idx):
    t, d = src.shape
    n = idx.shape[0]
    num_cores, workers = _sc_workers()
    per = n // workers
    assert per * workers == n and per % SC_WINDOW == 0 and t % SC_WINDOW == 0

    @functools.partial(
        pl.kernel, out_type=jax.ShapeDtypeStruct((n, d), src.dtype), mesh=_sc_mesh(),
        scratch_types=[pltpu.VMEM((SC_WINDOW,), jnp.int32), pltpu.VMEM((SC_WINDOW, d), src.dtype),
                       pltpu.SemaphoreType.DMA],
        name="sc_scatter")
    def scatter(src_hbm, idx_hbm, out_hbm, idx_v, rows_v, sem):
        wid = lax.axis_index("subcore") * num_cores + lax.axis_index("core")

        @pl.loop(0, per // SC_WINDOW)
        def _(j):
            base = wid * per + j * SC_WINDOW
            pltpu.sync_copy(idx_hbm.at[pl.ds(base, SC_WINDOW)], idx_v)
            pltpu.sync_copy(src_hbm.at[pl.ds(lax.rem(base, t), SC_WINDOW)], rows_v)
            pltpu.async_copy(rows_v, out_hbm.at[idx_v], sem).wait()

    return scatter(src, idx)


def _experts_kernel(tile_ref, exp_ref, lo_ref, hi_ref, first_ref, valid_ref,
                    xs_ref, wg_ref, wu_ref, bg_ref, bu_ref, wd_ref, bd_ref, y_ref):
    i = pl.program_id(0)
    tm = xs_ref.shape[0]

    @pl.when(valid_ref[i] == 1)
    def _():
        x_lo, x_hi = _unpack_bf16_pair(xs_ref[...])
        x = jnp.concatenate([x_lo.astype(BF16), x_hi.astype(BF16)], axis=1)
        acc = jnp.zeros((tm, D_MODEL), F32)
        for c in range(D_MODEL // EXPERT_COL_CHUNK):
            cs = slice(c * EXPERT_COL_CHUNK, (c + 1) * EXPERT_COL_CHUNK)
            gate = jnp.minimum(_dot(x, wg_ref[:, cs]) + bg_ref[:, cs], SWIGLU_LIMIT)
            up = jnp.clip(_dot(x, wu_ref[:, cs]) + bu_ref[:, cs], -SWIGLU_LIMIT, SWIGLU_LIMIT)
            act = (up + 1.0) * (gate * _sigmoid(SWIGLU_ALPHA * gate))
            acc = acc + _dot(act.astype(BF16), wd_ref[cs, :])
        y = acc + bd_ref[...]
        half = D_MODEL // 2
        packed = _pack_bf16_pair(y[:, :half], y[:, half:])
        rows = tile_ref[i] * tm + lax.broadcasted_iota(jnp.int32, (tm, 1), 0)
        mine = (rows >= lo_ref[i]) & (rows < hi_ref[i])

        @pl.when(first_ref[i] == 1)
        def _():
            y_ref[...] = packed

        @pl.when(first_ref[i] == 0)
        def _():
            y_ref[...] = jnp.where(mine, packed, y_ref[...])


def _expert_items(counts, n_rows, tm):
    n_tiles = n_rows // tm
    n_items = n_tiles + N_EXPERTS - 1
    ends = jnp.cumsum(counts)
    starts = ends - counts
    first_tile = starts // tm
    last_tile = jnp.maximum(ends - 1, 0) // tm
    per_expert = jnp.where(counts > 0, last_tile - first_tile + 1, 0)
    item_end = jnp.cumsum(per_expert)
    item_start = item_end - per_expert
    total = item_end[-1]
    item = jnp.arange(n_items, dtype=jnp.int32)
    valid = item < total
    clamped = jnp.minimum(item, total - 1)
    expert = jnp.sum((item_end[None, :] <= clamped[:, None]).astype(jnp.int32), axis=1)
    tile = first_tile[expert] + clamped - item_start[expert]
    prev_tile = jnp.concatenate([jnp.full((1,), -1, jnp.int32), tile[:-1]])
    first = (tile != prev_tile) & valid
    i32 = lambda a: a.astype(jnp.int32)
    return (i32(tile), i32(expert), i32(starts[expert]), i32(ends[expert]), i32(first), i32(valid))


def _experts(xs, items, wg, wu, bg, bu, wd, bd):
    n_rows = xs.shape[0]
    tm = EXPERT_TILE
    n_items = items[0].shape[0]
    half = D_MODEL // 2
    row_tile = pl.BlockSpec((tm, half), lambda i, tile, *_: (tile[i], 0))
    weight = pl.BlockSpec((None, D_MODEL, D_MODEL), lambda i, tile, exp, *_: (exp[i], 0, 0))
    bias = pl.BlockSpec((None, 1, D_MODEL), lambda i, tile, exp, *_: (exp[i], 0, 0))
    return pl.pallas_call(
        _experts_kernel,
        grid_spec=pltpu.PrefetchScalarGridSpec(
            num_scalar_prefetch=6,
            grid=(n_items,),
            in_specs=[row_tile, weight, weight, bias, bias, weight, bias],
            out_specs=row_tile,
        ),
        out_shape=jax.ShapeDtypeStruct((n_rows, half), U32),
        compiler_params=_params(("arbitrary",)),
        name="experts",
    )(*items, xs, wg, wu, bg, bu, wd, bd)


def _tail_kernel(h1_ref, yu_ref, gates_ref, p_ref, nple_ref, wg_ref, wp_ref, nfin_ref, out_ref):
    gates = gates_ref[...]
    lo = None
    hi = None
    for k in range(TOP_K):
        gk = gates[:, k:k + 1]
        yl, yh = _unpack_bf16_pair(yu_ref[k])
        lo = gk * yl if lo is None else lo + gk * yl
        hi = gk * yh if hi is None else hi + gk * yh
    h2 = h1_ref[...] + jnp.concatenate([lo, hi], axis=1)
    hn = _rms(h2, nple_ref[...]).astype(BF16)
    gate = _sigmoid(_dot(hn, wg_ref[...]))
    h3 = h2 + gate * _dot(p_ref[...].astype(BF16), wp_ref[...])
    out_ref[...] = _rms(h3, nfin_ref[...])


def _tail(h1, yu, gates, p2, norm_ple_w, ple_gate_bf, ple_proj_bf, final_norm_w):
    t = h1.shape[0]
    tm = TOKEN_TILE
    half = D_MODEL // 2
    tile = lambda width: pl.BlockSpec((tm, width), lambda i: (i, 0))
    const2 = lambda shape: pl.BlockSpec(shape, lambda i: (0, 0))
    return pl.pallas_call(
        _tail_kernel,
        grid=(t // tm,),
        in_specs=[tile(D_MODEL), pl.BlockSpec((TOP_K, tm, half), lambda i: (0, i, 0)),
                  tile(LANES), tile(PLE_DIM), const2((1, D_MODEL)),
                  const2((D_MODEL, D_MODEL)), const2((PLE_DIM, D_MODEL)), const2((1, D_MODEL))],
        out_specs=tile(D_MODEL),
        out_shape=jax.ShapeDtypeStruct((t, D_MODEL), F32),
        compiler_params=_params(("parallel",)),
        name="tail",
    )(h1, yu, gates, p2, norm_ple_w, ple_gate_bf, ple_proj_bf, final_norm_w)


def kernel(x, p, positions, w_in, w_out, ret_gn_w, pool_w, pool_scale, norm_mix_w, norm_moe_w, router_w, router_b, expert_w_gate_up, expert_b_gate_up, expert_w_down, expert_b_down, norm_ple_w, ple_gate_w, ple_proj_w, final_norm_w):
    batch, seq, d = x.shape
    depth = w_in.shape[0]
    assert depth == 1 and d == D_MODEL and seq % TOKEN_TILE == 0 and seq % RET_CHUNK == 0
    t = batch * seq
    n_rows = t * TOP_K
    assert n_rows % EXPERT_TILE == 0
    row = lambda a: a.reshape(1, -1).astype(F32)

    h = x.reshape(t, d)
    pos2 = positions.reshape(t, 1)
    for l in range(depth):
        proj = _in_proj(h, pos2, row(norm_mix_w[l]), w_in[l].astype(BF16))
        ret = _retention(proj, row(ret_gn_w[l]), batch, seq)

        rw = jnp.pad(router_w[l].astype(F32), ((0, 0), (0, LANES - N_EXPERTS)))
        rw_hi = rw.astype(BF16)
        rw_lo = (rw - rw_hi.astype(F32)).astype(BF16)
        rb = jnp.pad(router_b[l].astype(F32), (0, LANES - N_EXPERTS), constant_values=NEG_BIG).reshape(1, LANES)
        h1, hn_packed, code, gates, counts = _mix_router(
            h, ret, proj, pool_w[l].astype(BF16), row(pool_scale[l]), w_out[l].astype(BF16),
            row(norm_moe_w[l]), rw_hi, rw_lo, rb, seq)

        counts_i = counts[0, :N_EXPERTS].astype(jnp.int32)
        starts = jnp.cumsum(counts_i) - counts_i
        offsets = jnp.pad(starts.astype(F32), (0, LANES - N_EXPERTS)).reshape(1, LANES)
        pos = _positions(code, offsets)
        pos_km = pos[:, :TOP_K].T.reshape(n_rows)

        xs = _sc_scatter(hn_packed, pos_km)
        items = _expert_items(counts_i, n_rows, EXPERT_TILE)
        wgu = expert_w_gate_up[l].reshape(N_EXPERTS, D_MODEL, D_MODEL, 2)
        bgu = expert_b_gate_up[l].reshape(N_EXPERTS, 1, D_MODEL, 2).astype(F32)
        y = _experts(xs, items, wgu[..., 0].astype(BF16), wgu[..., 1].astype(BF16),
                     bgu[..., 0], bgu[..., 1], expert_w_down[l].astype(BF16),
                     expert_b_down[l].reshape(N_EXPERTS, 1, D_MODEL).astype(F32))
        yu = _sc_gather(y, pos_km).reshape(TOP_K, t, d // 2)

        h = _tail(h1, yu, gates, p[l].reshape(t, PLE_DIM), row(norm_ple_w[l]),
                  ple_gate_w[l].astype(BF16), ple_proj_w[l].astype(BF16), row(final_norm_w))
    return h.reshape(batch, seq, d)
```

```python
import functools
import math

import numpy as np
import jax
import jax.numpy as jnp
from jax import lax
from jax.experimental import pallas as pl
from jax.experimental.pallas import tpu as pltpu
from jax.experimental.pallas import tpu_sc as plsc

D_MODEL = 1024
D_RET = 512
D_POOL = 512
RET_HEADS = 8
RET_HEAD_DIM = 64
HEAD_PAIRS = RET_HEADS // 2
ROPE_BASE = 10000.0
POOL_WINDOWS = (2, 4, 8, 16)
POOL_GROUP_DIM = 128
POOL_HISTORY = 16
POOL_BLOCK = 128
D_IN_PROJ = 4 * D_RET + D_POOL
N_EXPERTS = 32
TOP_K = 4
SWIGLU_LIMIT = 7.0
SWIGLU_ALPHA = 1.702
PLE_DIM = 256
NORM_EPS = 1e-5
GN_EPS = 1e-5

LANES = 128
NEG_BIG = -1e30

TOKEN_TILE = 512
RET_CHUNK = 256
RET_STEP_CHUNKS = 4
EXPERT_TILE = 512
EXPERT_COL_CHUNK = 512
SC_WINDOW = 128
VMEM_LIMIT = 56 * 1024 * 1024

F32 = jnp.float32
BF16 = jnp.bfloat16
U32 = jnp.uint32


def _params(semantics):
    return pltpu.CompilerParams(dimension_semantics=semantics, vmem_limit_bytes=VMEM_LIMIT)


def _dot(a, b):
    return jnp.dot(a, b, preferred_element_type=F32)


def _dot_nt(a, b):
    return lax.dot_general(a, b, (((1,), (1,)), ((), ())), preferred_element_type=F32)


def _dot_tn(a, b):
    return lax.dot_general(a, b, (((0,), (0,)), ((), ())), preferred_element_type=F32)


def _rms(x, w):
    ms = jnp.mean(x * x, axis=-1, keepdims=True)
    return x * lax.rsqrt(ms + NORM_EPS) * w


def _sigmoid(z):
    return 1.0 / (1.0 + jnp.exp(-z))


def _pack_bf16_pair(lo, hi):
    lo_bits = pltpu.bitcast(lo.astype(BF16).astype(F32), U32) >> 16
    hi_bits = pltpu.bitcast(hi.astype(BF16).astype(F32), U32) & jnp.uint32(0xFFFF0000)
    return lo_bits | hi_bits


def _unpack_bf16_pair(packed):
    lo = pltpu.bitcast(packed << 16, F32)
    hi = pltpu.bitcast(packed & jnp.uint32(0xFFFF0000), F32)
    return lo, hi


def _in_proj_kernel(x_ref, pos_ref, nw_ref, w_ref, freq_ref, out_ref):
    hn = _rms(x_ref[...], nw_ref[...]).astype(BF16)
    ang = pos_ref[...].astype(F32) * freq_ref[0:1, :]
    cos = jnp.cos(ang)
    sin = jnp.sin(ang)
    slab = 2 * LANES
    cos_t = jnp.concatenate([cos, cos], axis=1)
    sin_up = jnp.concatenate([sin * freq_ref[1:2, :]] * 2, axis=1)
    sin_dn = jnp.concatenate([sin * freq_ref[2:3, :]] * 2, axis=1)
    half = RET_HEAD_DIM // 2

    for s in range(2 * D_RET // slab):
        sl = slice(s * slab, (s + 1) * slab)
        v = _dot(hn, w_ref[:, sl])
        v = v * cos_t + pltpu.roll(v, slab - half, 1) * sin_up + pltpu.roll(v, half, 1) * sin_dn
        if s >= D_RET // slab:
            v = v * (RET_HEAD_DIM ** -0.5)
        out_ref[:, sl] = v.astype(BF16)
    for s in range(2 * D_RET // 512, D_IN_PROJ // 512):
        sl = slice(s * 512, (s + 1) * 512)
        out_ref[:, sl] = _dot(hn, w_ref[:, sl]).astype(BF16)


def _in_proj(x2, pos2, norm_w, w_in_bf):
    t = x2.shape[0]
    tm = TOKEN_TILE
    j = np.arange(LANES)
    half = RET_HEAD_DIM // 2
    inv_freq = ROPE_BASE ** (-(np.arange(half, dtype=np.float32)) / half)
    freq = np.zeros((8, LANES), np.float32)
    freq[0] = inv_freq[j % half]
    freq[1] = np.where(j % RET_HEAD_DIM < half, -1.0, 0.0)
    freq[2] = np.where(j % RET_HEAD_DIM >= half, 1.0, 0.0)
    return pl.pallas_call(
        _in_proj_kernel,
        grid=(t // tm,),
        in_specs=[
            pl.BlockSpec((tm, D_MODEL), lambda i: (i, 0)),
            pl.BlockSpec((tm, 1), lambda i: (i, 0)),
            pl.BlockSpec((1, D_MODEL), lambda i: (0, 0)),
            pl.BlockSpec((D_MODEL, D_IN_PROJ), lambda i: (0, 0)),
            pl.BlockSpec((8, LANES), lambda i: (0, 0)),
        ],
        out_specs=pl.BlockSpec((tm, D_IN_PROJ), lambda i: (i, 0)),
        out_shape=jax.ShapeDtypeStruct((t, D_IN_PROJ), BF16),
        compiler_params=_params(("parallel",)),
        name="in_proj",
    )(x2, pos2, norm_w, w_in_bf, jnp.asarray(freq))


def _retention_kernel(q_ref, k_ref, v_ref, g_ref, dec_ref, xi_ref, zeta_ref, cd_ref, bd_ref,
                      m64_ref, eye_ref, gnw_ref, out_ref, state_ref):
    @pl.when(pl.program_id(1) == 0)
    def _():
        state_ref[...] = jnp.zeros_like(state_ref)

    c = dec_ref.shape[1]
    lane = lax.broadcasted_iota(jnp.int32, (1, LANES), 1)
    m64 = m64_ref[...]
    bd = bd_ref[...]
    eye = eye_ref[...]

    def group_mean(v):
        hi = v.astype(BF16)
        lo = (v - hi.astype(F32)).astype(BF16)
        return _dot(jnp.concatenate([hi, lo], axis=1), m64)

    for ci in range(q_ref.shape[0] // c):
        rows = slice(ci * c, (ci + 1) * c)
        for p in range(HEAD_PAIRS):
            sl = slice(p * LANES, (p + 1) * LANES)
            qp = q_ref[rows, sl]
            kp = k_ref[rows, sl]
            vp = v_ref[rows, sl]
            y = None
            for hh in range(2):
                in_head = (lane >= RET_HEAD_DIM) == bool(hh)
                qm = jnp.where(in_head, qp, jnp.zeros_like(qp))
                vm = jnp.where(in_head, vp, jnp.zeros_like(vp))
                scores = _dot_nt(qm, kp) * dec_ref[2 * p + hh]
                part = _dot(scores.astype(BF16), vm)
                y = part if y is None else y + part
            st = state_ref[p]
            y = y + _dot((qp.astype(F32) * xi_ref[p]).astype(BF16), st.astype(BF16))
            kz = (kp.astype(F32) * zeta_ref[p]).astype(BF16)
            kz_t = _dot_nt(eye, kz).astype(BF16)
            state_ref[p] = cd_ref[p] * st + _dot(kz_t, vp) * bd
            mu = group_mean(y)
            var = group_mean(y * y) - mu * mu
            yn = (y - mu) * lax.rsqrt(var + GN_EPS) * gnw_ref[:, sl]
            g = g_ref[rows, sl].astype(F32)
            out_ref[rows, sl] = (yn * g * _sigmoid(g)).astype(BF16)


def _retention_tables(c):
    h = np.arange(RET_HEADS, dtype=np.float64)
    log_gamma = np.log1p(-np.power(2.0, -5.0 - h))
    idx = np.arange(c, dtype=np.float64)
    rel = idx[:, None] - idx[None, :]
    dec = np.where(rel >= 0, np.exp(np.where(rel >= 0, rel, 0.0)[None] * log_gamma[:, None, None]), 0.0)
    lane_head = np.arange(LANES) // RET_HEAD_DIM
    xi = np.zeros((HEAD_PAIRS, c, LANES))
    zeta = np.zeros((HEAD_PAIRS, c, LANES))
    cd = np.zeros((HEAD_PAIRS, LANES, LANES))
    same = lane_head[:, None] == lane_head[None, :]
    for p in range(HEAD_PAIRS):
        lg = log_gamma[2 * p + lane_head]
        xi[p] = np.exp((idx + 1.0)[:, None] * lg[None, :])
        zeta[p] = np.exp((c - 1 - idx)[:, None] * lg[None, :])
        cd[p] = np.where(same, np.exp(c * lg)[:, None], 0.0)
    bd = same.astype(np.float32)
    m64 = np.concatenate([same, same], axis=0).astype(np.float32) / RET_HEAD_DIM
    f = lambda a: jnp.asarray(a, dtype=F32)
    return (f(dec), f(xi), f(zeta), f(cd), f(bd), jnp.asarray(m64, dtype=BF16),
            jnp.asarray(np.eye(LANES, dtype=np.float32), dtype=BF16))


def _retention(proj, gn_w, batch, seq):
    t = proj.shape[0]
    c = RET_CHUNK
    rows = RET_STEP_CHUNKS * c
    n = seq // rows
    dec, xi, zeta, cd, bd, m64, eye = _retention_tables(c)
    col = lambda j: pl.BlockSpec((rows, D_RET), lambda b, i, j=j: (b * n + i, j))
    const3 = lambda shape: pl.BlockSpec(shape, lambda b, i: (0, 0, 0))
    const2 = lambda shape: pl.BlockSpec(shape, lambda b, i: (0, 0))
    return pl.pallas_call(
        _retention_kernel,
        grid=(batch, n),
        in_specs=[col(0), col(1), col(2), col(3),
                  const3((RET_HEADS, c, c)), const3((HEAD_PAIRS, c, LANES)),
                  const3((HEAD_PAIRS, c, LANES)), const3((HEAD_PAIRS, LANES, LANES)),
                  const2((LANES, LANES)), const2((2 * LANES, LANES)), const2((LANES, LANES)),
                  const2((1, D_RET))],
        out_specs=pl.BlockSpec((rows, D_RET), lambda b, i: (b * n + i, 0)),
        out_shape=jax.ShapeDtypeStruct((t, D_RET), BF16),
        scratch_shapes=[pltpu.VMEM((HEAD_PAIRS, LANES, LANES), F32)],
        compiler_params=_params(("arbitrary", "arbitrary")),
        name="retention",
    )(proj, proj, proj, proj, dec, xi, zeta, cd, bd, m64, eye, gn_w)


def _mix_router_kernel(seq, x_ref, ret_ref, u_ref, uprev_ref, band_ref, pw_ref, ps_ref, wout_ref,
                       nw_ref, rw_ref, rb_ref,
                       h1_ref, hnp_ref, code_ref, gates_ref, cnt_ref, uext_ref):
    i = pl.program_id(0)
    tm = x_ref.shape[0]
    t0 = lax.rem(i * tm, seq)

    @pl.when(i == 0)
    def _():
        cnt_ref[...] = jnp.zeros_like(cnt_ref)

    prev = uprev_ref[...]
    uext_ref[0:POOL_HISTORY, :] = jnp.where(t0 == 0, jnp.zeros_like(prev), prev)
    uext_ref[POOL_HISTORY:, :] = u_ref[...]

    row = lax.broadcasted_iota(jnp.int32, (tm, 1), 0)
    t_seq = (t0 + row + 1).astype(F32)
    mixed = []
    for gi, w in enumerate(POOL_WINDOWS):
        sl = slice(gi * POOL_GROUP_DIM, (gi + 1) * POOL_GROUP_DIM)
        wsum = jnp.concatenate(
            [_dot(band_ref[gi], uext_ref[r0:r0 + POOL_BLOCK + POOL_HISTORY, sl])
             for r0 in range(0, tm, POOL_BLOCK)], axis=0)
        count = jnp.minimum(t_seq, float(w))
        pooled = wsum / count - u_ref[:, sl].astype(F32)
        mixed.append(_dot(pooled.astype(BF16), pw_ref[gi]))
    pool = (jnp.concatenate(mixed, axis=1) * ps_ref[...]).astype(BF16)

    h1 = (x_ref[...] + _dot(ret_ref[...], wout_ref[0:D_RET, :])
          + _dot(pool, wout_ref[D_RET:, :]))
    h1_ref[...] = h1
    hn = _rms(h1, nw_ref[...])
    hn_hi = hn.astype(BF16)
    half = D_MODEL // 2
    hnp_ref[...] = _pack_bf16_pair(hn[:, :half], hn[:, half:])

    hn_lo = (hn - hn_hi.astype(F32)).astype(BF16)
    both = _dot(hn_hi, rw_ref[...])
    logits = (both[:, :LANES] + both[:, LANES:] + _dot(hn_lo, rw_ref[:, :LANES])
              + rb_ref[...])

    lane = lax.broadcasted_iota(jnp.int32, (tm, LANES), 1).astype(F32)
    code = jnp.zeros((tm, LANES), F32)
    vals = []
    work = logits
    for k in range(TOP_K):
        m = jnp.max(work, axis=-1, keepdims=True)
        idx = jnp.min(jnp.where(work == m, lane, float(LANES)), axis=-1, keepdims=True)
        chosen = lane == idx
        code = jnp.where(chosen, float(k + 1), code)
        work = jnp.where(chosen, -jnp.inf, work)
        vals.append(m)
    exps = [jnp.exp(v - vals[0]) for v in vals]
    denom = exps[0] + exps[1] + exps[2] + exps[3]
    gates = jnp.zeros((tm, LANES), F32)
    for k in range(TOP_K):
        gates = jnp.where(lane == float(k), exps[k] / denom, gates)
    code_ref[...] = code
    gates_ref[...] = gates
    cnt_ref[...] += jnp.sum((code > 0).astype(F32), axis=0, keepdims=True)


def _pool_bands():
    r = np.arange(POOL_BLOCK)[:, None]
    s = np.arange(POOL_BLOCK + POOL_HISTORY)[None, :] - POOL_HISTORY
    bands = [((s <= r) & (s > r - w)) for w in POOL_WINDOWS]
    return jnp.asarray(np.stack(bands).astype(np.float32), dtype=BF16)


def _mix_router(x2, ret, proj, pool_w_bf, pool_scale, w_out_bf, norm_w, rw_split, rb, seq):
    t = x2.shape[0]
    tm = TOKEN_TILE
    hist_blocks = tm // POOL_HISTORY
    u_col = (4 * D_RET) // D_POOL
    tile = lambda width: pl.BlockSpec((tm, width), lambda i: (i, 0))
    const2 = lambda shape: pl.BlockSpec(shape, lambda i: (0, 0))
    const3 = lambda shape: pl.BlockSpec(shape, lambda i: (0, 0, 0))
    return pl.pallas_call(
        functools.partial(_mix_router_kernel, seq),
        grid=(t // tm,),
        in_specs=[
            tile(D_MODEL), tile(D_RET),
            pl.BlockSpec((tm, D_POOL), lambda i: (i, u_col)),
            pl.BlockSpec((POOL_HISTORY, D_POOL), lambda i: (jnp.maximum(i * hist_blocks - 1, 0), u_col)),
            const3((len(POOL_WINDOWS), POOL_BLOCK, POOL_BLOCK + POOL_HISTORY)),
            const3((len(POOL_WINDOWS), POOL_GROUP_DIM, POOL_GROUP_DIM)),
            const2((1, D_POOL)), const2((D_MODEL, D_MODEL)), const2((1, D_MODEL)),
            const2((D_MODEL, 2 * LANES)), const2((1, LANES)),
        ],
        out_specs=[tile(D_MODEL), tile(D_MODEL // 2), tile(LANES), tile(LANES), const2((1, LANES))],
        out_shape=[
            jax.ShapeDtypeStruct((t, D_MODEL), F32),
            jax.ShapeDtypeStruct((t, D_MODEL // 2), U32),
            jax.ShapeDtypeStruct((t, LANES), F32),
            jax.ShapeDtypeStruct((t, LANES), F32),
            jax.ShapeDtypeStruct((1, LANES), F32),
        ],
        scratch_shapes=[pltpu.VMEM((tm + POOL_HISTORY, D_POOL), BF16)],
        compiler_params=_params(("arbitrary",)),
        name="mix_router",
    )(x2, ret, proj, proj, _pool_bands(), pool_w_bf, pool_scale, w_out_bf, norm_w, rw_split, rb)


def _positions_kernel(code_ref, off_ref, tri_ref, pos_ref, carry_ref):
    @pl.when(pl.program_id(0) == 0)
    def _():
        carry_ref[...] = jnp.zeros_like(carry_ref)

    code = code_ref[...]
    tm = code.shape[0]
    sel = (code > 0).astype(BF16)
    carry = carry_ref[...]
    rank = _dot(tri_ref[...], sel) + (carry + off_ref[...])
    carry_ref[...] = carry + jnp.sum(sel.astype(F32), axis=0, keepdims=True)
    lane = lax.broadcasted_iota(jnp.int32, (tm, LANES), 1)
    pos = jnp.zeros((tm, LANES), F32)
    for k in range(TOP_K):
        pk = jnp.sum(jnp.where(code == float(k + 1), rank, 0.0), axis=-1, keepdims=True)
        pos = jnp.where(lane == k, pk, pos)
    pos_ref[...] = pos.astype(jnp.int32)


def _positions(code, offsets):
    t = code.shape[0]
    tm = TOKEN_TILE
    tri = jnp.asarray(np.tril(np.ones((tm, tm), np.float32), -1), dtype=BF16)
    return pl.pallas_call(
        _positions_kernel,
        grid=(t // tm,),
        in_specs=[pl.BlockSpec((tm, LANES), lambda i: (i, 0)),
                  pl.BlockSpec((1, LANES), lambda i: (0, 0)),
                  pl.BlockSpec((tm, tm), lambda i: (0, 0))],
        out_specs=pl.BlockSpec((tm, LANES), lambda i: (i, 0)),
        out_shape=jax.ShapeDtypeStruct((t, LANES), jnp.int32),
        scratch_shapes=[pltpu.VMEM((1, LANES), F32)],
        compiler_params=_params(("arbitrary",)),
        name="positions",
    )(code, offsets, tri)


def _sc_mesh():
    return plsc.VectorSubcoreMesh(core_axis_name="core", subcore_axis_name="subcore")


def _sc_workers():
    info = plsc.get_sparse_core_info()
    return info.num_cores, info.num_cores * info.num_subcores


def _sc_gather(src, idx):
    n = idx.shape[0]
    d = src.shape[1]
    w = SC_WINDOW // 2
    num_cores, workers = _sc_workers()
    per = n // workers
    pairs = per // (2 * w)
    assert per * workers == n and pairs * 2 * w == per

    @functools.partial(
        pl.kernel, out_type=jax.ShapeDtypeStruct((n, d), src.dtype), mesh=_sc_mesh(),
        scratch_types=[pltpu.VMEM((w,), jnp.int32), pltpu.VMEM((w,), jnp.int32),
                       pltpu.VMEM((w, d), src.dtype), pltpu.VMEM((w, d), src.dtype),
                       pltpu.SemaphoreType.DMA, pltpu.SemaphoreType.DMA,
                       pltpu.SemaphoreType.DMA, pltpu.SemaphoreType.DMA],
        name="sc_gather")
    def gather(src_hbm, idx_hbm, out_hbm, idx0, idx1, rows0, rows1, g0, g1, w0, w1):
        first = (lax.axis_index("subcore") * num_cores + lax.axis_index("core")) * per

        def start_gather(win, idx_v, rows_v, sem):
            pltpu.sync_copy(idx_hbm.at[pl.ds(first + win * w, w)], idx_v)
            pltpu.async_copy(src_hbm.at[idx_v], rows_v, sem)

        def wait_gather(idx_v, rows_v, sem):
            pltpu.make_async_copy(src_hbm.at[idx_v], rows_v, sem).wait()

        def start_write(win, rows_v, sem):
            pltpu.async_copy(rows_v, out_hbm.at[pl.ds(first + win * w, w)], sem)

        def wait_write(rows_v, sem):
            pltpu.make_async_copy(rows_v, out_hbm.at[pl.ds(first, w)], sem).wait()

        start_gather(0, idx0, rows0, g0)

        @pl.loop(0, pairs)
        def _(j):
            even = 2 * j

            @pl.when(j > 0)
            def _():
                wait_write(rows1, w1)

            start_gather(even + 1, idx1, rows1, g1)
            wait_gather(idx0, rows0, g0)
            start_write(even, rows0, w0)
            wait_write(rows0, w0)

            @pl.when(j + 1 < pairs)
            def _():
                start_gather(even + 2, idx0, rows0, g0)

            wait_gather(idx1, rows1, g1)
            start_write(even + 1, rows1, w1)

        wait_write(rows1, w1)

    return gather(src, idx)


def _sc_scatter(src, idx):
    t, d = src.shape
    w = SC_WINDOW
    num_cores, workers = _sc_workers()
    per = t // workers
    assert idx.shape == (TOP_K, t) and per * workers == t and per % w == 0

    @functools.partial(
        pl.kernel, out_type=jax.ShapeDtypeStruct((TOP_K * t, d), src.dtype), mesh=_sc_mesh(),
        scratch_types=[pltpu.VMEM((TOP_K, w), jnp.int32), pltpu.VMEM((w, d), src.dtype),
                       pltpu.SemaphoreType.DMA],
        name="sc_scatter")
    def scatter(src_hbm, idx_hbm, out_hbm, idx_v, rows_v, sem):
        first = (lax.axis_index("subcore") * num_cores + lax.axis_index("core")) * per

        @pl.loop(0, per // w)
        def _(j):
            base = first + j * w
            for k in range(TOP_K):
                pltpu.sync_copy(idx_hbm.at[k, pl.ds(base, w)], idx_v.at[k])
            pltpu.sync_copy(src_hbm.at[pl.ds(base, w)], rows_v)
            copies = [pltpu.async_copy(rows_v, out_hbm.at[idx_v.at[k]], sem) for k in range(TOP_K)]
            for c in copies:
                c.wait()

    return scatter(src, idx)


def _experts_kernel(tile_ref, exp_ref, lo_ref, hi_ref, first_ref, valid_ref, fresh_ref,
                    xs_ref, wgu_ref, bg_ref, bu_ref, wdn_ref, bd_ref, perm_ref, y_ref,
                    wg_ref, wu_ref, wd_ref):
    i = pl.program_id(0)
    tm = xs_ref.shape[0]

    @pl.when(fresh_ref[i] == 1)
    def _():
        perm = perm_ref[...]
        pair = 2 * LANES
        for c in range(2 * D_MODEL // pair):
            sel = _dot(wgu_ref[:, c * pair:(c + 1) * pair].astype(BF16), perm).astype(BF16)
            wg_ref[:, c * LANES:(c + 1) * LANES] = sel[:, :LANES]
            wu_ref[:, c * LANES:(c + 1) * LANES] = sel[:, LANES:]
        wd_ref[...] = wdn_ref[...].astype(BF16)

    @pl.when(valid_ref[i] == 1)
    def _():
        x_lo, x_hi = _unpack_bf16_pair(xs_ref[...])
        x = jnp.concatenate([x_lo.astype(BF16), x_hi.astype(BF16)], axis=1)
        acc = jnp.zeros((tm, D_MODEL), F32)
        for c in range(D_MODEL // EXPERT_COL_CHUNK):
            cs = slice(c * EXPERT_COL_CHUNK, (c + 1) * EXPERT_COL_CHUNK)
            gate = jnp.minimum(_dot(x, wg_ref[:, cs]) + bg_ref[:, cs], SWIGLU_LIMIT)
            up = jnp.clip(_dot(x, wu_ref[:, cs]) + bu_ref[:, cs], -SWIGLU_LIMIT, SWIGLU_LIMIT)
            act = (up + 1.0) * (gate * _sigmoid(SWIGLU_ALPHA * gate))
            acc = acc + _dot(act.astype(BF16), wd_ref[cs, :])
        y = acc + bd_ref[...]
        half = D_MODEL // 2
        packed = _pack_bf16_pair(y[:, :half], y[:, half:])
        rows = tile_ref[i] * tm + lax.broadcasted_iota(jnp.int32, (tm, 1), 0)
        mine = (rows >= lo_ref[i]) & (rows < hi_ref[i])

        @pl.when(first_ref[i] == 1)
        def _():
            y_ref[...] = packed

        @pl.when(first_ref[i] == 0)
        def _():
            y_ref[...] = jnp.where(mine, packed, y_ref[...])


def _expert_items(counts, n_rows, tm):
    n_tiles = n_rows // tm
    n_items = n_tiles + N_EXPERTS - 1
    ends = jnp.cumsum(counts)
    starts = ends - counts
    first_tile = starts // tm
    last_tile = jnp.maximum(ends - 1, 0) // tm
    per_expert = jnp.where(counts > 0, last_tile - first_tile + 1, 0)
    item_end = jnp.cumsum(per_expert)
    item_start = item_end - per_expert
    total = item_end[-1]
    item = jnp.arange(n_items, dtype=jnp.int32)
    valid = item < total
    clamped = jnp.minimum(item, total - 1)
    expert = jnp.sum((item_end[None, :] <= clamped[:, None]).astype(jnp.int32), axis=1)
    tile = first_tile[expert] + clamped - item_start[expert]
    prev_tile = jnp.concatenate([jnp.full((1,), -1, jnp.int32), tile[:-1]])
    first = (tile != prev_tile) & valid
    prev_expert = jnp.concatenate([jnp.full((1,), -1, jnp.int32), expert[:-1]])
    fresh = (expert != prev_expert) & valid
    i32 = lambda a: a.astype(jnp.int32)
    return (i32(tile), i32(expert), i32(starts[expert]), i32(ends[expert]), i32(first), i32(valid),
            i32(fresh))


def _experts(xs, items, w_gate_up, bg, bu, w_down, bd):
    n_rows = xs.shape[0]
    tm = EXPERT_TILE
    n_items = items[0].shape[0]
    half = D_MODEL // 2
    j = np.arange(2 * LANES)
    perm = np.zeros((2 * LANES, 2 * LANES), np.float32)
    perm[j, np.where(j % 2 == 0, j // 2, LANES + j // 2)] = 1.0
    row_tile = pl.BlockSpec((tm, half), lambda i, tile, *_: (tile[i], 0))
    by_expert = lambda *shape: pl.BlockSpec((None,) + shape, lambda i, tile, exp, *_: (exp[i], 0, 0))
    return pl.pallas_call(
        _experts_kernel,
        grid_spec=pltpu.PrefetchScalarGridSpec(
            num_scalar_prefetch=7,
            grid=(n_items,),
            in_specs=[row_tile, by_expert(D_MODEL, 2 * D_MODEL), by_expert(1, D_MODEL),
                      by_expert(1, D_MODEL), by_expert(D_MODEL, D_MODEL), by_expert(1, D_MODEL),
                      pl.BlockSpec((2 * LANES, 2 * LANES), lambda i, *_: (0, 0))],
            out_specs=row_tile,
            scratch_shapes=[pltpu.VMEM((D_MODEL, D_MODEL), BF16)] * 3,
        ),
        out_shape=jax.ShapeDtypeStruct((n_rows, half), U32),
        compiler_params=_params(("arbitrary",)),
        name="experts",
    )(*items, xs, w_gate_up, bg, bu, w_down, bd, jnp.asarray(perm, dtype=BF16))


def _tail_kernel(h1_ref, yu_ref, gates_ref, p_ref, nple_ref, wg_ref, wp_ref, nfin_ref, out_ref):
    gates = gates_ref[...]
    lo = None
    hi = None
    for k in range(TOP_K):
        gk = gates[:, k:k + 1]
        yl, yh = _unpack_bf16_pair(yu_ref[k])
        lo = gk * yl if lo is None else lo + gk * yl
        hi = gk * yh if hi is None else hi + gk * yh
    h2 = h1_ref[...] + jnp.concatenate([lo, hi], axis=1)
    hn = _rms(h2, nple_ref[...]).astype(BF16)
    gate = _sigmoid(_dot(hn, wg_ref[...]))
    h3 = h2 + gate * _dot(p_ref[...].astype(BF16), wp_ref[...])
    out_ref[...] = _rms(h3, nfin_ref[...])


def _tail(h1, yu, gates, p2, norm_ple_w, ple_gate_bf, ple_proj_bf, final_norm_w):
    t = h1.shape[0]
    tm = TOKEN_TILE
    half = D_MODEL // 2
    tile = lambda width: pl.BlockSpec((tm, width), lambda i: (i, 0))
    const2 = lambda shape: pl.BlockSpec(shape, lambda i: (0, 0))
    return pl.pallas_call(
        _tail_kernel,
        grid=(t // tm,),
        in_specs=[tile(D_MODEL), pl.BlockSpec((TOP_K, tm, half), lambda i: (0, i, 0)),
                  tile(LANES), tile(PLE_DIM), const2((1, D_MODEL)),
                  const2((D_MODEL, D_MODEL)), const2((PLE_DIM, D_MODEL)), const2((1, D_MODEL))],
        out_specs=tile(D_MODEL),
        out_shape=jax.ShapeDtypeStruct((t, D_MODEL), F32),
        compiler_params=_params(("parallel",)),
        name="tail",
    )(h1, yu, gates, p2, norm_ple_w, ple_gate_bf, ple_proj_bf, final_norm_w)


def kernel(x, p, positions, w_in, w_out, ret_gn_w, pool_w, pool_scale, norm_mix_w, norm_moe_w, router_w, router_b, expert_w_gate_up, expert_b_gate_up, expert_w_down, expert_b_down, norm_ple_w, ple_gate_w, ple_proj_w, final_norm_w):
    batch, seq, d = x.shape
    depth = w_in.shape[0]
    assert depth == 1 and d == D_MODEL and seq % TOKEN_TILE == 0
    assert seq % (RET_CHUNK * RET_STEP_CHUNKS) == 0
    t = batch * seq
    n_rows = t * TOP_K
    assert n_rows % EXPERT_TILE == 0
    row = lambda a: a.reshape(1, -1).astype(F32)

    h = x.reshape(t, d)
    pos2 = positions.reshape(t, 1)
    for l in range(depth):
        proj = _in_proj(h, pos2, row(norm_mix_w[l]), w_in[l].astype(BF16))
        ret = _retention(proj, row(ret_gn_w[l]), batch, seq)

        rw = jnp.pad(router_w[l].astype(F32), ((0, 0), (0, LANES - N_EXPERTS)))
        rw_hi = rw.astype(BF16)
        rw_lo = (rw - rw_hi.astype(F32)).astype(BF16)
        rw_split = jnp.concatenate([rw_hi, rw_lo], axis=1)
        rb = jnp.pad(router_b[l].astype(F32), (0, LANES - N_EXPERTS), constant_values=NEG_BIG).reshape(1, LANES)
        h1, hn_packed, code, gates, counts = _mix_router(
            h, ret, proj, pool_w[l].astype(BF16), row(pool_scale[l]), w_out[l].astype(BF16),
            row(norm_moe_w[l]), rw_split, rb, seq)

        counts_i = counts[0, :N_EXPERTS].astype(jnp.int32)
        starts = jnp.cumsum(counts_i) - counts_i
        offsets = jnp.pad(starts.astype(F32), (0, LANES - N_EXPERTS)).reshape(1, LANES)
        pos = _positions(code, offsets)
        pos_km = pos[:, :TOP_K].T

        xs = _sc_scatter(hn_packed, pos_km)
        items = _expert_items(counts_i, n_rows, EXPERT_TILE)
        bgu = expert_b_gate_up[l].reshape(N_EXPERTS, 1, D_MODEL, 2).astype(F32)
        y = _experts(xs, items, expert_w_gate_up[l], bgu[..., 0], bgu[..., 1], expert_w_down[l],
                     expert_b_down[l].reshape(N_EXPERTS, 1, D_MODEL).astype(F32))
        yu = _sc_gather(y, pos_km.reshape(n_rows)).reshape(TOP_K, t, d // 2)

        h = _tail(h1, yu, gates, p[l].reshape(t, PLE_DIM), row(norm_ple_w[l]),
                  ple_gate_w[l].astype(BF16), ple_proj_w[l].astype(BF16), row(final_norm_w))
    return h.reshape(batch, seq, d)
```

```python
import functools
import math

import numpy as np
import jax
import jax.numpy as jnp
from jax import lax
from jax.experimental import pallas as pl
from jax.experimental.pallas import tpu as pltpu
from jax.experimental.pallas import tpu_sc as plsc

D_MODEL = 1024
D_RET = 512
D_POOL = 512
RET_HEADS = 8
RET_HEAD_DIM = 64
HEAD_PAIRS = RET_HEADS // 2
ROPE_BASE = 10000.0
POOL_WINDOWS = (2, 4, 8, 16)
POOL_GROUP_DIM = 128
POOL_HISTORY = 16
POOL_BLOCK = 128
D_IN_PROJ = 4 * D_RET + D_POOL
N_EXPERTS = 32
TOP_K = 4
SWIGLU_LIMIT = 7.0
SWIGLU_ALPHA = 1.702
PLE_DIM = 256
NORM_EPS = 1e-5
GN_EPS = 1e-5

LANES = 128
NEG_BIG = -1e30

TOKEN_TILE = 512
RET_CHUNK = 256
RET_STEP_CHUNKS = 4
EXPERT_TILE = 512
EXPERT_COL_CHUNK = 512
SC_WINDOW = 128
TAIL_SPANS = 4
VMEM_LIMIT = 56 * 1024 * 1024

F32 = jnp.float32
BF16 = jnp.bfloat16
U32 = jnp.uint32


def _params(semantics):
    return pltpu.CompilerParams(dimension_semantics=semantics, vmem_limit_bytes=VMEM_LIMIT)


def _dot(a, b):
    return jnp.dot(a, b, preferred_element_type=F32)


def _dot_nt(a, b):
    return lax.dot_general(a, b, (((1,), (1,)), ((), ())), preferred_element_type=F32)


def _dot_tn(a, b):
    return lax.dot_general(a, b, (((0,), (0,)), ((), ())), preferred_element_type=F32)


def _rms(x, w):
    ms = jnp.mean(x * x, axis=-1, keepdims=True)
    return x * lax.rsqrt(ms + NORM_EPS) * w


def _sigmoid(z):
    return 1.0 / (1.0 + jnp.exp(-z))


def _pack_bf16_pair(lo, hi):
    lo_bits = pltpu.bitcast(lo.astype(BF16).astype(F32), U32) >> 16
    hi_bits = pltpu.bitcast(hi.astype(BF16).astype(F32), U32) & jnp.uint32(0xFFFF0000)
    return lo_bits | hi_bits


def _unpack_bf16_pair(packed):
    lo = pltpu.bitcast(packed << 16, F32)
    hi = pltpu.bitcast(packed & jnp.uint32(0xFFFF0000), F32)
    return lo, hi


def _in_proj_kernel(x_ref, pos_ref, nw_ref, w_ref, freq_ref, out_ref):
    hn = _rms(x_ref[...], nw_ref[...]).astype(BF16)
    ang = pos_ref[...].astype(F32) * freq_ref[0:1, :]
    cos = jnp.cos(ang)
    sin = jnp.sin(ang)
    slab = 2 * LANES
    cos_t = jnp.concatenate([cos, cos], axis=1)
    sin_up = jnp.concatenate([sin * freq_ref[1:2, :]] * 2, axis=1)
    sin_dn = jnp.concatenate([sin * freq_ref[2:3, :]] * 2, axis=1)
    half = RET_HEAD_DIM // 2

    for s in range(2 * D_RET // slab):
        sl = slice(s * slab, (s + 1) * slab)
        v = _dot(hn, w_ref[:, sl])
        v = v * cos_t + pltpu.roll(v, slab - half, 1) * sin_up + pltpu.roll(v, half, 1) * sin_dn
        if s >= D_RET // slab:
            v = v * (RET_HEAD_DIM ** -0.5)
        out_ref[:, sl] = v.astype(BF16)
    for s in range(2 * D_RET // 512, D_IN_PROJ // 512):
        sl = slice(s * 512, (s + 1) * 512)
        out_ref[:, sl] = _dot(hn, w_ref[:, sl]).astype(BF16)


def _in_proj(x2, pos2, norm_w, w_in_bf):
    t = x2.shape[0]
    tm = TOKEN_TILE
    j = np.arange(LANES)
    half = RET_HEAD_DIM // 2
    inv_freq = ROPE_BASE ** (-(np.arange(half, dtype=np.float32)) / half)
    freq = np.zeros((8, LANES), np.float32)
    freq[0] = inv_freq[j % half]
    freq[1] = np.where(j % RET_HEAD_DIM < half, -1.0, 0.0)
    freq[2] = np.where(j % RET_HEAD_DIM >= half, 1.0, 0.0)
    return pl.pallas_call(
        _in_proj_kernel,
        grid=(t // tm,),
        in_specs=[
            pl.BlockSpec((tm, D_MODEL), lambda i: (i, 0)),
            pl.BlockSpec((tm, 1), lambda i: (i, 0)),
            pl.BlockSpec((1, D_MODEL), lambda i: (0, 0)),
            pl.BlockSpec((D_MODEL, D_IN_PROJ), lambda i: (0, 0)),
            pl.BlockSpec((8, LANES), lambda i: (0, 0)),
        ],
        out_specs=pl.BlockSpec((tm, D_IN_PROJ), lambda i: (i, 0)),
        out_shape=jax.ShapeDtypeStruct((t, D_IN_PROJ), BF16),
        compiler_params=_params(("parallel",)),
        name="in_proj",
    )(x2, pos2, norm_w, w_in_bf, jnp.asarray(freq))


def _retention_kernel(q_ref, k_ref, v_ref, g_ref, dec_ref, xi_ref, zeta_ref, cd_ref, bd_ref,
                      m64_ref, eye_ref, gnw_ref, out_ref, state_ref):
    @pl.when(pl.program_id(1) == 0)
    def _():
        state_ref[...] = jnp.zeros_like(state_ref)

    c = dec_ref.shape[1]
    lane = lax.broadcasted_iota(jnp.int32, (1, LANES), 1)
    m64 = m64_ref[...]
    bd = bd_ref[...]
    eye = eye_ref[...]

    def group_mean(v):
        hi = v.astype(BF16)
        lo = (v - hi.astype(F32)).astype(BF16)
        return _dot(jnp.concatenate([hi, lo], axis=1), m64)

    for ci in range(q_ref.shape[0] // c):
        rows = slice(ci * c, (ci + 1) * c)
        for p in range(HEAD_PAIRS):
            sl = slice(p * LANES, (p + 1) * LANES)
            qp = q_ref[rows, sl]
            kp = k_ref[rows, sl]
            vp = v_ref[rows, sl]
            y = None
            for hh in range(2):
                in_head = (lane >= RET_HEAD_DIM) == bool(hh)
                qm = jnp.where(in_head, qp, jnp.zeros_like(qp))
                vm = jnp.where(in_head, vp, jnp.zeros_like(vp))
                scores = _dot_nt(qm, kp) * dec_ref[2 * p + hh]
                part = _dot(scores.astype(BF16), vm)
                y = part if y is None else y + part
            st = state_ref[p]
            y = y + _dot((qp.astype(F32) * xi_ref[p]).astype(BF16), st.astype(BF16))
            kz = (kp.astype(F32) * zeta_ref[p]).astype(BF16)
            kz_t = _dot_nt(eye, kz).astype(BF16)
            state_ref[p] = cd_ref[p] * st + _dot(kz_t, vp) * bd
            mu = group_mean(y)
            var = group_mean(y * y) - mu * mu
            yn = (y - mu) * lax.rsqrt(var + GN_EPS) * gnw_ref[:, sl]
            g = g_ref[rows, sl].astype(F32)
            out_ref[rows, sl] = (yn * g * _sigmoid(g)).astype(BF16)


def _retention_tables(c):
    h = np.arange(RET_HEADS, dtype=np.float64)
    log_gamma = np.log1p(-np.power(2.0, -5.0 - h))
    idx = np.arange(c, dtype=np.float64)
    rel = idx[:, None] - idx[None, :]
    dec = np.where(rel >= 0, np.exp(np.where(rel >= 0, rel, 0.0)[None] * log_gamma[:, None, None]), 0.0)
    lane_head = np.arange(LANES) // RET_HEAD_DIM
    xi = np.zeros((HEAD_PAIRS, c, LANES))
    zeta = np.zeros((HEAD_PAIRS, c, LANES))
    cd = np.zeros((HEAD_PAIRS, LANES, LANES))
    same = lane_head[:, None] == lane_head[None, :]
    for p in range(HEAD_PAIRS):
        lg = log_gamma[2 * p + lane_head]
        xi[p] = np.exp((idx + 1.0)[:, None] * lg[None, :])
        zeta[p] = np.exp((c - 1 - idx)[:, None] * lg[None, :])
        cd[p] = np.where(same, np.exp(c * lg)[:, None], 0.0)
    bd = same.astype(np.float32)
    m64 = np.concatenate([same, same], axis=0).astype(np.float32) / RET_HEAD_DIM
    f = lambda a: jnp.asarray(a, dtype=F32)
    return (f(dec), f(xi), f(zeta), f(cd), f(bd), jnp.asarray(m64, dtype=BF16),
            jnp.asarray(np.eye(LANES, dtype=np.float32), dtype=BF16))


def _retention(proj, gn_w, batch, seq):
    t = proj.shape[0]
    c = RET_CHUNK
    rows = RET_STEP_CHUNKS * c
    n = seq // rows
    dec, xi, zeta, cd, bd, m64, eye = _retention_tables(c)
    col = lambda j: pl.BlockSpec((rows, D_RET), lambda b, i, j=j: (b * n + i, j))
    const3 = lambda shape: pl.BlockSpec(shape, lambda b, i: (0, 0, 0))
    const2 = lambda shape: pl.BlockSpec(shape, lambda b, i: (0, 0))
    return pl.pallas_call(
        _retention_kernel,
        grid=(batch, n),
        in_specs=[col(0), col(1), col(2), col(3),
                  const3((RET_HEADS, c, c)), const3((HEAD_PAIRS, c, LANES)),
                  const3((HEAD_PAIRS, c, LANES)), const3((HEAD_PAIRS, LANES, LANES)),
                  const2((LANES, LANES)), const2((2 * LANES, LANES)), const2((LANES, LANES)),
                  const2((1, D_RET))],
        out_specs=pl.BlockSpec((rows, D_RET), lambda b, i: (b * n + i, 0)),
        out_shape=jax.ShapeDtypeStruct((t, D_RET), BF16),
        scratch_shapes=[pltpu.VMEM((HEAD_PAIRS, LANES, LANES), F32)],
        compiler_params=_params(("arbitrary", "arbitrary")),
        name="retention",
    )(proj, proj, proj, proj, dec, xi, zeta, cd, bd, m64, eye, gn_w)


def _mix_router_kernel(seq, x_ref, ret_ref, u_ref, uprev_ref, band_ref, pw_ref, ps_ref, wout_ref,
                       nw_ref, rw_ref, rb_ref,
                       h1_ref, hnp_ref, code_ref, gates_ref, cnt_ref, uext_ref):
    i = pl.program_id(0)
    tm = x_ref.shape[0]
    t0 = lax.rem(i * tm, seq)

    @pl.when(i == 0)
    def _():
        cnt_ref[...] = jnp.zeros_like(cnt_ref)

    prev = uprev_ref[...]
    uext_ref[0:POOL_HISTORY, :] = jnp.where(t0 == 0, jnp.zeros_like(prev), prev)
    uext_ref[POOL_HISTORY:, :] = u_ref[...]

    row = lax.broadcasted_iota(jnp.int32, (tm, 1), 0)
    t_seq = (t0 + row + 1).astype(F32)
    mixed = []
    for gi, w in enumerate(POOL_WINDOWS):
        sl = slice(gi * POOL_GROUP_DIM, (gi + 1) * POOL_GROUP_DIM)
        wsum = jnp.concatenate(
            [_dot(band_ref[gi], uext_ref[r0:r0 + POOL_BLOCK + POOL_HISTORY, sl])
             for r0 in range(0, tm, POOL_BLOCK)], axis=0)
        count = jnp.minimum(t_seq, float(w))
        pooled = wsum / count - u_ref[:, sl].astype(F32)
        mixed.append(_dot(pooled.astype(BF16), pw_ref[gi]))
    pool = (jnp.concatenate(mixed, axis=1) * ps_ref[...]).astype(BF16)

    h1 = (x_ref[...] + _dot(ret_ref[...], wout_ref[0:D_RET, :])
          + _dot(pool, wout_ref[D_RET:, :]))
    h1_ref[...] = h1
    hn = _rms(h1, nw_ref[...])
    hn_hi = hn.astype(BF16)
    half = D_MODEL // 2
    hnp_ref[...] = _pack_bf16_pair(hn[:, :half], hn[:, half:])

    hn_lo = (hn - hn_hi.astype(F32)).astype(BF16)
    both = _dot(hn_hi, rw_ref[...])
    logits = (both[:, :LANES] + both[:, LANES:] + _dot(hn_lo, rw_ref[:, :LANES])
              + rb_ref[...])

    lane = lax.broadcasted_iota(jnp.int32, (tm, LANES), 1).astype(F32)
    code = jnp.zeros((tm, LANES), F32)
    vals = []
    work = logits
    for k in range(TOP_K):
        m = jnp.max(work, axis=-1, keepdims=True)
        idx = jnp.min(jnp.where(work == m, lane, float(LANES)), axis=-1, keepdims=True)
        chosen = lane == idx
        code = jnp.where(chosen, float(k + 1), code)
        work = jnp.where(chosen, -jnp.inf, work)
        vals.append(m)
    exps = [jnp.exp(v - vals[0]) for v in vals]
    denom = exps[0] + exps[1] + exps[2] + exps[3]
    gates = jnp.zeros((tm, LANES), F32)
    for k in range(TOP_K):
        gates = jnp.where(lane == float(k), exps[k] / denom, gates)
    code_ref[...] = code
    gates_ref[...] = gates
    cnt_ref[...] += jnp.sum((code > 0).astype(F32), axis=0, keepdims=True)


def _pool_bands():
    r = np.arange(POOL_BLOCK)[:, None]
    s = np.arange(POOL_BLOCK + POOL_HISTORY)[None, :] - POOL_HISTORY
    bands = [((s <= r) & (s > r - w)) for w in POOL_WINDOWS]
    return jnp.asarray(np.stack(bands).astype(np.float32), dtype=BF16)


def _mix_router(x2, ret, proj, pool_w_bf, pool_scale, w_out_bf, norm_w, rw_split, rb, seq):
    t = x2.shape[0]
    tm = TOKEN_TILE
    hist_blocks = tm // POOL_HISTORY
    u_col = (4 * D_RET) // D_POOL
    tile = lambda width: pl.BlockSpec((tm, width), lambda i: (i, 0))
    const2 = lambda shape: pl.BlockSpec(shape, lambda i: (0, 0))
    const3 = lambda shape: pl.BlockSpec(shape, lambda i: (0, 0, 0))
    return pl.pallas_call(
        functools.partial(_mix_router_kernel, seq),
        grid=(t // tm,),
        in_specs=[
            tile(D_MODEL), tile(D_RET),
            pl.BlockSpec((tm, D_POOL), lambda i: (i, u_col)),
            pl.BlockSpec((POOL_HISTORY, D_POOL), lambda i: (jnp.maximum(i * hist_blocks - 1, 0), u_col)),
            const3((len(POOL_WINDOWS), POOL_BLOCK, POOL_BLOCK + POOL_HISTORY)),
            const3((len(POOL_WINDOWS), POOL_GROUP_DIM, POOL_GROUP_DIM)),
            const2((1, D_POOL)), const2((D_MODEL, D_MODEL)), const2((1, D_MODEL)),
            const2((D_MODEL, 2 * LANES)), const2((1, LANES)),
        ],
        out_specs=[tile(D_MODEL), tile(D_MODEL // 2), tile(LANES), tile(LANES), const2((1, LANES))],
        out_shape=[
            jax.ShapeDtypeStruct((t, D_MODEL), F32),
            jax.ShapeDtypeStruct((t, D_MODEL // 2), U32),
            jax.ShapeDtypeStruct((t, LANES), F32),
            jax.ShapeDtypeStruct((t, LANES), F32),
            jax.ShapeDtypeStruct((1, LANES), F32),
        ],
        scratch_shapes=[pltpu.VMEM((tm + POOL_HISTORY, D_POOL), BF16)],
        compiler_params=_params(("arbitrary",)),
        name="mix_router",
    )(x2, ret, proj, proj, _pool_bands(), pool_w_bf, pool_scale, w_out_bf, norm_w, rw_split, rb)


def _positions_kernel(code_ref, off_ref, tri_ref, pos_ref, carry_ref):
    @pl.when(pl.program_id(0) == 0)
    def _():
        carry_ref[...] = jnp.zeros_like(carry_ref)

    code = code_ref[...]
    tm = code.shape[0]
    sel = (code > 0).astype(BF16)
    carry = carry_ref[...]
    rank = _dot(tri_ref[...], sel) + (carry + off_ref[...])
    carry_ref[...] = carry + jnp.sum(sel.astype(F32), axis=0, keepdims=True)
    lane = lax.broadcasted_iota(jnp.int32, (tm, LANES), 1)
    pos = jnp.zeros((tm, LANES), F32)
    for k in range(TOP_K):
        pk = jnp.sum(jnp.where(code == float(k + 1), rank, 0.0), axis=-1, keepdims=True)
        pos = jnp.where(lane == k, pk, pos)
    pos_ref[...] = pos.astype(jnp.int32)


def _positions(code, offsets):
    t = code.shape[0]
    tm = TOKEN_TILE
    tri = jnp.asarray(np.tril(np.ones((tm, tm), np.float32), -1), dtype=BF16)
    return pl.pallas_call(
        _positions_kernel,
        grid=(t // tm,),
        in_specs=[pl.BlockSpec((tm, LANES), lambda i: (i, 0)),
                  pl.BlockSpec((1, LANES), lambda i: (0, 0)),
                  pl.BlockSpec((tm, tm), lambda i: (0, 0))],
        out_specs=pl.BlockSpec((tm, LANES), lambda i: (i, 0)),
        out_shape=jax.ShapeDtypeStruct((t, LANES), jnp.int32),
        scratch_shapes=[pltpu.VMEM((1, LANES), F32)],
        compiler_params=_params(("arbitrary",)),
        name="positions",
    )(code, offsets, tri)


def _sc_mesh():
    return plsc.VectorSubcoreMesh(core_axis_name="core", subcore_axis_name="subcore")


def _sc_workers():
    info = plsc.get_sparse_core_info()
    return info.num_cores, info.num_cores * info.num_subcores


def _sc_gather(src, idx):
    n = idx.shape[0]
    d = src.shape[1]
    w = SC_WINDOW // 2
    num_cores, workers = _sc_workers()
    per = n // workers
    pairs = per // (2 * w)
    assert per * workers == n and pairs * 2 * w == per

    @functools.partial(
        pl.kernel, out_type=jax.ShapeDtypeStruct((n, d), src.dtype), mesh=_sc_mesh(),
        scratch_types=[pltpu.VMEM((w,), jnp.int32), pltpu.VMEM((w,), jnp.int32),
                       pltpu.VMEM((w, d), src.dtype), pltpu.VMEM((w, d), src.dtype),
                       pltpu.SemaphoreType.DMA, pltpu.SemaphoreType.DMA,
                       pltpu.SemaphoreType.DMA, pltpu.SemaphoreType.DMA],
        name="sc_gather")
    def gather(src_hbm, idx_hbm, out_hbm, idx0, idx1, rows0, rows1, g0, g1, w0, w1):
        first = (lax.axis_index("subcore") * num_cores + lax.axis_index("core")) * per

        def start_gather(win, idx_v, rows_v, sem):
            pltpu.sync_copy(idx_hbm.at[pl.ds(first + win * w, w)], idx_v)
            pltpu.async_copy(src_hbm.at[idx_v], rows_v, sem)

        def wait_gather(idx_v, rows_v, sem):
            pltpu.make_async_copy(src_hbm.at[idx_v], rows_v, sem).wait()

        def start_write(win, rows_v, sem):
            pltpu.async_copy(rows_v, out_hbm.at[pl.ds(first + win * w, w)], sem)

        def wait_write(rows_v, sem):
            pltpu.make_async_copy(rows_v, out_hbm.at[pl.ds(first, w)], sem).wait()

        start_gather(0, idx0, rows0, g0)

        @pl.loop(0, pairs)
        def _(j):
            even = 2 * j

            @pl.when(j > 0)
            def _():
                wait_write(rows1, w1)

            start_gather(even + 1, idx1, rows1, g1)
            wait_gather(idx0, rows0, g0)
            start_write(even, rows0, w0)
            wait_write(rows0, w0)

            @pl.when(j + 1 < pairs)
            def _():
                start_gather(even + 2, idx0, rows0, g0)

            wait_gather(idx1, rows1, g1)
            start_write(even + 1, rows1, w1)

        wait_write(rows1, w1)

    return gather(src, idx)


def _sc_scatter(src, idx):
    t, d = src.shape
    w = SC_WINDOW
    num_cores, workers = _sc_workers()
    per = t // workers
    assert idx.shape == (TOP_K, t) and per * workers == t and per % w == 0

    @functools.partial(
        pl.kernel, out_type=jax.ShapeDtypeStruct((TOP_K * t, d), src.dtype), mesh=_sc_mesh(),
        scratch_types=[pltpu.VMEM((TOP_K, w), jnp.int32), pltpu.VMEM((w, d), src.dtype),
                       pltpu.SemaphoreType.DMA],
        name="sc_scatter")
    def scatter(src_hbm, idx_hbm, out_hbm, idx_v, rows_v, sem):
        first = (lax.axis_index("subcore") * num_cores + lax.axis_index("core")) * per

        @pl.loop(0, per // w)
        def _(j):
            base = first + j * w
            for k in range(TOP_K):
                pltpu.sync_copy(idx_hbm.at[k, pl.ds(base, w)], idx_v.at[k])
            pltpu.sync_copy(src_hbm.at[pl.ds(base, w)], rows_v)
            copies = [pltpu.async_copy(rows_v, out_hbm.at[idx_v.at[k]], sem) for k in range(TOP_K)]
            for c in copies:
                c.wait()

    return scatter(src, idx)


def _experts_kernel(tile_ref, exp_ref, lo_ref, hi_ref, first_ref, valid_ref, fetch_ref, prep_ref,
                    prep_slot_ref, slot_ref,
                    xs_ref, wgu_ref, bg_ref, bu_ref, wdn_ref, bd_ref, perm_ref, y_ref,
                    wg_ref, wu_ref, wd_ref):
    i = pl.program_id(0)
    tm = xs_ref.shape[0]

    @pl.when(prep_ref[i] == 1)
    def _():
        ps = prep_slot_ref[i]
        perm = perm_ref[...]
        pair = 2 * LANES
        for c in range(2 * D_MODEL // pair):
            sel = _dot(wgu_ref[:, c * pair:(c + 1) * pair].astype(BF16), perm).astype(BF16)
            wg_ref[ps, :, c * LANES:(c + 1) * LANES] = sel[:, :LANES]
            wu_ref[ps, :, c * LANES:(c + 1) * LANES] = sel[:, LANES:]
        wd_ref[ps] = wdn_ref[...].astype(BF16)

    @pl.when(valid_ref[i] == 1)
    def _():
        s = slot_ref[i]
        x_lo, x_hi = _unpack_bf16_pair(xs_ref[...])
        x = jnp.concatenate([x_lo.astype(BF16), x_hi.astype(BF16)], axis=1)
        acc = jnp.zeros((tm, D_MODEL), F32)
        for c in range(D_MODEL // EXPERT_COL_CHUNK):
            cs = slice(c * EXPERT_COL_CHUNK, (c + 1) * EXPERT_COL_CHUNK)
            gate = jnp.minimum(_dot(x, wg_ref[s, :, cs]) + bg_ref[:, cs], SWIGLU_LIMIT)
            up = jnp.clip(_dot(x, wu_ref[s, :, cs]) + bu_ref[:, cs], -SWIGLU_LIMIT, SWIGLU_LIMIT)
            act = (up + 1.0) * (gate * _sigmoid(SWIGLU_ALPHA * gate))
            acc = acc + _dot(act.astype(BF16), wd_ref[s, cs, :])
        y = acc + bd_ref[...]
        half = D_MODEL // 2
        packed = _pack_bf16_pair(y[:, :half], y[:, half:])
        rows = tile_ref[i] * tm + lax.broadcasted_iota(jnp.int32, (tm, 1), 0)
        mine = (rows >= lo_ref[i]) & (rows < hi_ref[i])

        @pl.when(first_ref[i] == 1)
        def _():
            y_ref[...] = packed

        @pl.when(first_ref[i] == 0)
        def _():
            y_ref[...] = jnp.where(mine, packed, y_ref[...])


def _expert_items(counts, n_rows, tm):
    n_tiles = n_rows // tm
    n_items = n_tiles + N_EXPERTS - 1
    ends = jnp.cumsum(counts)
    starts = ends - counts
    first_tile = starts // tm
    last_tile = jnp.maximum(ends - 1, 0) // tm
    per_expert = jnp.where(counts > 0, last_tile - first_tile + 1, 0)
    item_end = jnp.cumsum(per_expert)
    item_start = item_end - per_expert
    total = item_end[-1]
    item = jnp.arange(n_items, dtype=jnp.int32)
    valid = item < total
    clamped = jnp.minimum(item, total - 1)
    expert = jnp.sum((item_end[None, :] <= clamped[:, None]).astype(jnp.int32), axis=1)
    tile = first_tile[expert] + clamped - item_start[expert]
    prev_tile = jnp.concatenate([jnp.full((1,), -1, jnp.int32), tile[:-1]])
    first = (tile != prev_tile) & valid

    used = counts > 0
    ordinal = jnp.cumsum(used.astype(jnp.int32)) - 1
    eidx = jnp.arange(N_EXPERTS, dtype=jnp.int32)
    later = used[None, :] & (eidx[None, :] > eidx[:, None])
    has_next = jnp.any(later, axis=1)
    nxt = jnp.where(has_next, jnp.argmax(later, axis=1).astype(jnp.int32), eidx)
    next_expert = jnp.concatenate([expert[1:], jnp.full((1,), -1, jnp.int32)])
    next_valid = jnp.concatenate([valid[1:], jnp.zeros((1,), bool)])
    last_of_expert = valid & ((next_expert != expert) | ~next_valid)
    prep = last_of_expert & has_next[expert]

    lead = lambda a, v: jnp.concatenate([jnp.full((1,), v, jnp.int32), a.astype(jnp.int32)])
    return (lead(tile, tile[0]), lead(expert, expert[0]), lead(starts[expert], 0), lead(ends[expert], 0),
            lead(first, 0), lead(valid, 0), lead(nxt[expert], expert[0]), lead(prep, 1),
            lead(ordinal[nxt[expert]] % 2, ordinal[expert[0]] % 2), lead(ordinal[expert] % 2, 0))


def _experts(xs, items, w_gate_up, bg, bu, w_down, bd):
    n_rows = xs.shape[0]
    tm = EXPERT_TILE
    n_items = items[0].shape[0]
    half = D_MODEL // 2
    j = np.arange(2 * LANES)
    perm = np.zeros((2 * LANES, 2 * LANES), np.float32)
    perm[j, np.where(j % 2 == 0, j // 2, LANES + j // 2)] = 1.0
    row_tile = pl.BlockSpec((tm, half), lambda i, tile, *_: (tile[i], 0))
    by_expert = lambda *shape: pl.BlockSpec((None,) + shape, lambda i, tile, exp, *_: (exp[i], 0, 0))
    by_fetch = lambda *shape: pl.BlockSpec(
        (None,) + shape, lambda i, tile, exp, lo, hi, first, valid, fetch, *_: (fetch[i], 0, 0))
    return pl.pallas_call(
        _experts_kernel,
        grid_spec=pltpu.PrefetchScalarGridSpec(
            num_scalar_prefetch=len(items),
            grid=(n_items,),
            in_specs=[row_tile, by_fetch(D_MODEL, 2 * D_MODEL), by_expert(1, D_MODEL),
                      by_expert(1, D_MODEL), by_fetch(D_MODEL, D_MODEL), by_expert(1, D_MODEL),
                      pl.BlockSpec((2 * LANES, 2 * LANES), lambda i, *_: (0, 0))],
            out_specs=row_tile,
            scratch_shapes=[pltpu.VMEM((2, D_MODEL, D_MODEL), BF16)] * 3,
        ),
        out_shape=jax.ShapeDtypeStruct((n_rows, half), U32),
        compiler_params=_params(("arbitrary",)),
        name="experts",
    )(*items, xs, w_gate_up, bg, bu, w_down, bd, jnp.asarray(perm, dtype=BF16))


def _tail_kernel(h1_ref, yu_ref, gates_ref, p_ref, nple_ref, wg_ref, wp_ref, nfin_ref, *rest):
    out_ref = rest[-1]
    gates = gates_ref[...]
    lo = None
    hi = None
    for k in range(TOP_K):
        gk = gates[:, k:k + 1]
        yl, yh = _unpack_bf16_pair(yu_ref[k])
        lo = gk * yl if lo is None else lo + gk * yl
        hi = gk * yh if hi is None else hi + gk * yh
    h2 = h1_ref[...] + jnp.concatenate([lo, hi], axis=1)
    hn = _rms(h2, nple_ref[...]).astype(BF16)
    gate = _sigmoid(_dot(hn, wg_ref[...]))
    h3 = h2 + gate * _dot(p_ref[...].astype(BF16), wp_ref[...])
    out_ref[...] = _rms(h3, nfin_ref[...])


def _tail(h1, yu, gates, p2, norm_ple_w, ple_gate_bf, ple_proj_bf, final_norm_w, first_token, partial):
    t = h1.shape[0]
    tm = TOKEN_TILE
    half = D_MODEL // 2
    off = first_token // tm
    tile = lambda width: pl.BlockSpec((tm, width), lambda i: (i + off, 0))
    const2 = lambda shape: pl.BlockSpec(shape, lambda i: (0, 0))
    in_specs = [tile(D_MODEL), pl.BlockSpec((TOP_K, tm, half), lambda i: (0, i, 0)),
                tile(LANES), tile(PLE_DIM), const2((1, D_MODEL)),
                const2((D_MODEL, D_MODEL)), const2((PLE_DIM, D_MODEL)), const2((1, D_MODEL))]
    args = [h1, yu, gates, p2, norm_ple_w, ple_gate_bf, ple_proj_bf, final_norm_w]
    aliases = {}
    if partial is not None:
        in_specs.append(pl.BlockSpec(memory_space=pl.ANY))
        args.append(partial)
        aliases = {len(args) - 1: 0}
    return pl.pallas_call(
        _tail_kernel,
        grid=(yu.shape[1] // tm,),
        in_specs=in_specs,
        out_specs=tile(D_MODEL),
        out_shape=jax.ShapeDtypeStruct((t, D_MODEL), F32),
        input_output_aliases=aliases,
        compiler_params=_params(("parallel",)),
        name="tail",
    )(*args)


def kernel(x, p, positions, w_in, w_out, ret_gn_w, pool_w, pool_scale, norm_mix_w, norm_moe_w, router_w, router_b, expert_w_gate_up, expert_b_gate_up, expert_w_down, expert_b_down, norm_ple_w, ple_gate_w, ple_proj_w, final_norm_w):
    batch, seq, d = x.shape
    depth = w_in.shape[0]
    assert depth == 1 and d == D_MODEL and seq % TOKEN_TILE == 0
    assert seq % (RET_CHUNK * RET_STEP_CHUNKS) == 0
    t = batch * seq
    n_rows = t * TOP_K
    assert n_rows % EXPERT_TILE == 0
    row = lambda a: a.reshape(1, -1).astype(F32)

    h = x.reshape(t, d)
    pos2 = positions.reshape(t, 1)
    for l in range(depth):
        proj = _in_proj(h, pos2, row(norm_mix_w[l]), w_in[l].astype(BF16))
        ret = _retention(proj, row(ret_gn_w[l]), batch, seq)

        rw = jnp.pad(router_w[l].astype(F32), ((0, 0), (0, LANES - N_EXPERTS)))
        rw_hi = rw.astype(BF16)
        rw_lo = (rw - rw_hi.astype(F32)).astype(BF16)
        rw_split = jnp.concatenate([rw_hi, rw_lo], axis=1)
        rb = jnp.pad(router_b[l].astype(F32), (0, LANES - N_EXPERTS), constant_values=NEG_BIG).reshape(1, LANES)
        h1, hn_packed, code, gates, counts = _mix_router(
            h, ret, proj, pool_w[l].astype(BF16), row(pool_scale[l]), w_out[l].astype(BF16),
            row(norm_moe_w[l]), rw_split, rb, seq)

        counts_i = counts[0, :N_EXPERTS].astype(jnp.int32)
        starts = jnp.cumsum(counts_i) - counts_i
        offsets = jnp.pad(starts.astype(F32), (0, LANES - N_EXPERTS)).reshape(1, LANES)
        pos = _positions(code, offsets)
        pos_km = pos[:, :TOP_K].T

        xs = _sc_scatter(hn_packed, pos_km)
        items = _expert_items(counts_i, n_rows, EXPERT_TILE)
        bgu = expert_b_gate_up[l].reshape(N_EXPERTS, 1, D_MODEL, 2).astype(F32)
        y = _experts(xs, items, expert_w_gate_up[l], bgu[..., 0], bgu[..., 1], expert_w_down[l],
                     expert_b_down[l].reshape(N_EXPERTS, 1, D_MODEL).astype(F32))

        span = t // TAIL_SPANS
        h = None
        for c in range(TAIL_SPANS):
            idx = pos_km[:, c * span:(c + 1) * span].reshape(TOP_K * span)
            yu = _sc_gather(y, idx).reshape(TOP_K, span, d // 2)
            h = _tail(h1, yu, gates, p[l].reshape(t, PLE_DIM), row(norm_ple_w[l]),
                      ple_gate_w[l].astype(BF16), ple_proj_w[l].astype(BF16), row(final_norm_w),
                      c * span, h)
    return h.reshape(batch, seq, d)
```

```python
import functools
import math

import numpy as np
import jax
import jax.numpy as jnp
from jax import lax
from jax.experimental import pallas as pl
from jax.experimental.pallas import tpu as pltpu
from jax.experimental.pallas import tpu_sc as plsc

D_MODEL = 1024
D_RET = 512
D_POOL = 512
RET_HEADS = 8
RET_HEAD_DIM = 64
HEAD_PAIRS = RET_HEADS // 2
ROPE_BASE = 10000.0
POOL_WINDOWS = (2, 4, 8, 16)
POOL_GROUP_DIM = 128
POOL_HISTORY = 16
POOL_BLOCK = 128
D_IN_PROJ = 4 * D_RET + D_POOL
N_EXPERTS = 32
TOP_K = 4
SWIGLU_LIMIT = 7.0
SWIGLU_ALPHA = 1.702
PLE_DIM = 256
NORM_EPS = 1e-5
GN_EPS = 1e-5

LANES = 128
NEG_BIG = -1e30

TOKEN_TILE = 512
RET_CHUNK = 256
RET_STEP_CHUNKS = 4
EXPERT_TILE = 512
EXPERT_COL_CHUNK = 512
SC_WINDOW = 128
VMEM_LIMIT = 56 * 1024 * 1024

F32 = jnp.float32
BF16 = jnp.bfloat16
U32 = jnp.uint32


def _params(semantics):
    return pltpu.CompilerParams(dimension_semantics=semantics, vmem_limit_bytes=VMEM_LIMIT)


def _dot(a, b):
    return jnp.dot(a, b, preferred_element_type=F32)


def _dot_nt(a, b):
    return lax.dot_general(a, b, (((1,), (1,)), ((), ())), preferred_element_type=F32)


def _dot_tn(a, b):
    return lax.dot_general(a, b, (((0,), (0,)), ((), ())), preferred_element_type=F32)


def _rms(x, w):
    ms = jnp.mean(x * x, axis=-1, keepdims=True)
    return x * lax.rsqrt(ms + NORM_EPS) * w


def _sigmoid(z):
    return 1.0 / (1.0 + jnp.exp(-z))


def _pack_bf16_pair(lo, hi):
    lo_bits = pltpu.bitcast(lo.astype(BF16).astype(F32), U32) >> 16
    hi_bits = pltpu.bitcast(hi.astype(BF16).astype(F32), U32) & jnp.uint32(0xFFFF0000)
    return lo_bits | hi_bits


def _unpack_bf16_pair(packed):
    lo = pltpu.bitcast(packed << 16, F32)
    hi = pltpu.bitcast(packed & jnp.uint32(0xFFFF0000), F32)
    return lo, hi


def _in_proj_kernel(x_ref, pos_ref, nw_ref, w_ref, freq_ref, out_ref):
    hn = _rms(x_ref[...], nw_ref[...]).astype(BF16)
    ang = pos_ref[...].astype(F32) * freq_ref[0:1, :]
    cos = jnp.cos(ang)
    sin = jnp.sin(ang)
    slab = 2 * LANES
    cos_t = jnp.concatenate([cos, cos], axis=1)
    sin_up = jnp.concatenate([sin * freq_ref[1:2, :]] * 2, axis=1)
    sin_dn = jnp.concatenate([sin * freq_ref[2:3, :]] * 2, axis=1)
    half = RET_HEAD_DIM // 2

    for s in range(2 * D_RET // slab):
        sl = slice(s * slab, (s + 1) * slab)
        v = _dot(hn, w_ref[:, sl])
        v = v * cos_t + pltpu.roll(v, slab - half, 1) * sin_up + pltpu.roll(v, half, 1) * sin_dn
        if s >= D_RET // slab:
            v = v * (RET_HEAD_DIM ** -0.5)
        out_ref[:, sl] = v.astype(BF16)
    for s in range(2 * D_RET // 512, D_IN_PROJ // 512):
        sl = slice(s * 512, (s + 1) * 512)
        out_ref[:, sl] = _dot(hn, w_ref[:, sl]).astype(BF16)


def _in_proj(x2, pos2, norm_w, w_in_bf):
    t = x2.shape[0]
    tm = TOKEN_TILE
    j = np.arange(LANES)
    half = RET_HEAD_DIM // 2
    inv_freq = ROPE_BASE ** (-(np.arange(half, dtype=np.float32)) / half)
    freq = np.zeros((8, LANES), np.float32)
    freq[0] = inv_freq[j % half]
    freq[1] = np.where(j % RET_HEAD_DIM < half, -1.0, 0.0)
    freq[2] = np.where(j % RET_HEAD_DIM >= half, 1.0, 0.0)
    return pl.pallas_call(
        _in_proj_kernel,
        grid=(t // tm,),
        in_specs=[
            pl.BlockSpec((tm, D_MODEL), lambda i: (i, 0)),
            pl.BlockSpec((tm, 1), lambda i: (i, 0)),
            pl.BlockSpec((1, D_MODEL), lambda i: (0, 0)),
            pl.BlockSpec((D_MODEL, D_IN_PROJ), lambda i: (0, 0)),
            pl.BlockSpec((8, LANES), lambda i: (0, 0)),
        ],
        out_specs=pl.BlockSpec((tm, D_IN_PROJ), lambda i: (i, 0)),
        out_shape=jax.ShapeDtypeStruct((t, D_IN_PROJ), BF16),
        compiler_params=_params(("parallel",)),
        name="in_proj",
    )(x2, pos2, norm_w, w_in_bf, jnp.asarray(freq))


def _retention_kernel(q_ref, k_ref, v_ref, g_ref, dec_ref, xi_ref, zeta_ref, cd_ref, bd_ref,
                      m64_ref, eye_ref, gnw_ref, out_ref, state_ref):
    @pl.when(pl.program_id(1) == 0)
    def _():
        state_ref[...] = jnp.zeros_like(state_ref)

    c = dec_ref.shape[1]
    lane = lax.broadcasted_iota(jnp.int32, (1, LANES), 1)
    m64 = m64_ref[...]
    bd = bd_ref[...]
    eye = eye_ref[...]

    def group_mean(v):
        hi = v.astype(BF16)
        lo = (v - hi.astype(F32)).astype(BF16)
        return _dot(jnp.concatenate([hi, lo], axis=1), m64)

    for ci in range(q_ref.shape[0] // c):
        rows = slice(ci * c, (ci + 1) * c)
        for p in range(HEAD_PAIRS):
            sl = slice(p * LANES, (p + 1) * LANES)
            qp = q_ref[rows, sl]
            kp = k_ref[rows, sl]
            vp = v_ref[rows, sl]
            y = None
            for hh in range(2):
                in_head = (lane >= RET_HEAD_DIM) == bool(hh)
                qm = jnp.where(in_head, qp, jnp.zeros_like(qp))
                vm = jnp.where(in_head, vp, jnp.zeros_like(vp))
                scores = _dot_nt(qm, kp) * dec_ref[2 * p + hh]
                part = _dot(scores.astype(BF16), vm)
                y = part if y is None else y + part
            st = state_ref[p]
            y = y + _dot((qp.astype(F32) * xi_ref[p]).astype(BF16), st.astype(BF16))
            kz = (kp.astype(F32) * zeta_ref[p]).astype(BF16)
            kz_t = _dot_nt(eye, kz).astype(BF16)
            state_ref[p] = cd_ref[p] * st + _dot(kz_t, vp) * bd
            mu = group_mean(y)
            var = group_mean(y * y) - mu * mu
            yn = (y - mu) * lax.rsqrt(var + GN_EPS) * gnw_ref[:, sl]
            g = g_ref[rows, sl].astype(F32)
            out_ref[rows, sl] = (yn * g * _sigmoid(g)).astype(BF16)


def _retention_tables(c):
    h = np.arange(RET_HEADS, dtype=np.float64)
    log_gamma = np.log1p(-np.power(2.0, -5.0 - h))
    idx = np.arange(c, dtype=np.float64)
    rel = idx[:, None] - idx[None, :]
    dec = np.where(rel >= 0, np.exp(np.where(rel >= 0, rel, 0.0)[None] * log_gamma[:, None, None]), 0.0)
    lane_head = np.arange(LANES) // RET_HEAD_DIM
    xi = np.zeros((HEAD_PAIRS, c, LANES))
    zeta = np.zeros((HEAD_PAIRS, c, LANES))
    cd = np.zeros((HEAD_PAIRS, LANES, LANES))
    same = lane_head[:, None] == lane_head[None, :]
    for p in range(HEAD_PAIRS):
        lg = log_gamma[2 * p + lane_head]
        xi[p] = np.exp((idx + 1.0)[:, None] * lg[None, :])
        zeta[p] = np.exp((c - 1 - idx)[:, None] * lg[None, :])
        cd[p] = np.where(same, np.exp(c * lg)[:, None], 0.0)
    bd = same.astype(np.float32)
    m64 = np.concatenate([same, same], axis=0).astype(np.float32) / RET_HEAD_DIM
    f = lambda a: jnp.asarray(a, dtype=F32)
    return (f(dec), f(xi), f(zeta), f(cd), f(bd), jnp.asarray(m64, dtype=BF16),
            jnp.asarray(np.eye(LANES, dtype=np.float32), dtype=BF16))


def _retention(proj, gn_w, batch, seq):
    t = proj.shape[0]
    c = RET_CHUNK
    rows = RET_STEP_CHUNKS * c
    n = seq // rows
    dec, xi, zeta, cd, bd, m64, eye = _retention_tables(c)
    col = lambda j: pl.BlockSpec((rows, D_RET), lambda b, i, j=j: (b * n + i, j))
    const3 = lambda shape: pl.BlockSpec(shape, lambda b, i: (0, 0, 0))
    const2 = lambda shape: pl.BlockSpec(shape, lambda b, i: (0, 0))
    return pl.pallas_call(
        _retention_kernel,
        grid=(batch, n),
        in_specs=[col(0), col(1), col(2), col(3),
                  const3((RET_HEADS, c, c)), const3((HEAD_PAIRS, c, LANES)),
                  const3((HEAD_PAIRS, c, LANES)), const3((HEAD_PAIRS, LANES, LANES)),
                  const2((LANES, LANES)), const2((2 * LANES, LANES)), const2((LANES, LANES)),
                  const2((1, D_RET))],
        out_specs=pl.BlockSpec((rows, D_RET), lambda b, i: (b * n + i, 0)),
        out_shape=jax.ShapeDtypeStruct((t, D_RET), BF16),
        scratch_shapes=[pltpu.VMEM((HEAD_PAIRS, LANES, LANES), F32)],
        compiler_params=_params(("arbitrary", "arbitrary")),
        name="retention",
    )(proj, proj, proj, proj, dec, xi, zeta, cd, bd, m64, eye, gn_w)


def _mix_router_kernel(seq, x_ref, ret_ref, u_ref, uprev_ref, band_ref, pw_ref, ps_ref, wout_ref,
                       nw_ref, rw_ref, rb_ref,
                       h1_ref, hnp_ref, code_ref, gates_ref, cnt_ref, uext_ref):
    i = pl.program_id(0)
    tm = x_ref.shape[0]
    t0 = lax.rem(i * tm, seq)

    @pl.when(i == 0)
    def _():
        cnt_ref[...] = jnp.zeros_like(cnt_ref)

    prev = uprev_ref[...]
    uext_ref[0:POOL_HISTORY, :] = jnp.where(t0 == 0, jnp.zeros_like(prev), prev)
    uext_ref[POOL_HISTORY:, :] = u_ref[...]

    row = lax.broadcasted_iota(jnp.int32, (tm, 1), 0)
    t_seq = (t0 + row + 1).astype(F32)
    mixed = []
    for gi, w in enumerate(POOL_WINDOWS):
        sl = slice(gi * POOL_GROUP_DIM, (gi + 1) * POOL_GROUP_DIM)
        wsum = jnp.concatenate(
            [_dot(band_ref[gi], uext_ref[r0:r0 + POOL_BLOCK + POOL_HISTORY, sl])
             for r0 in range(0, tm, POOL_BLOCK)], axis=0)
        count = jnp.minimum(t_seq, float(w))
        pooled = wsum / count - u_ref[:, sl].astype(F32)
        mixed.append(_dot(pooled.astype(BF16), pw_ref[gi]))
    pool = (jnp.concatenate(mixed, axis=1) * ps_ref[...]).astype(BF16)

    h1 = (x_ref[...] + _dot(ret_ref[...], wout_ref[0:D_RET, :])
          + _dot(pool, wout_ref[D_RET:, :]))
    h1_ref[...] = h1
    hn = _rms(h1, nw_ref[...])
    hn_hi = hn.astype(BF16)
    half = D_MODEL // 2
    hnp_ref[...] = _pack_bf16_pair(hn[:, :half], hn[:, half:])

    hn_lo = (hn - hn_hi.astype(F32)).astype(BF16)
    both = _dot(hn_hi, rw_ref[...])
    logits = (both[:, :LANES] + both[:, LANES:] + _dot(hn_lo, rw_ref[:, :LANES])
              + rb_ref[...])

    lane = lax.broadcasted_iota(jnp.int32, (tm, LANES), 1).astype(F32)
    code = jnp.zeros((tm, LANES), F32)
    vals = []
    work = logits
    for k in range(TOP_K):
        m = jnp.max(work, axis=-1, keepdims=True)
        idx = jnp.min(jnp.where(work == m, lane, float(LANES)), axis=-1, keepdims=True)
        chosen = lane == idx
        code = jnp.where(chosen, float(k + 1), code)
        work = jnp.where(chosen, -jnp.inf, work)
        vals.append(m)
    exps = [jnp.exp(v - vals[0]) for v in vals]
    denom = exps[0] + exps[1] + exps[2] + exps[3]
    gates = jnp.zeros((tm, LANES), F32)
    for k in range(TOP_K):
        gates = jnp.where(lane == float(k), exps[k] / denom, gates)
    code_ref[...] = code
    gates_ref[...] = gates
    cnt_ref[...] += jnp.sum((code > 0).astype(F32), axis=0, keepdims=True)


def _pool_bands():
    r = np.arange(POOL_BLOCK)[:, None]
    s = np.arange(POOL_BLOCK + POOL_HISTORY)[None, :] - POOL_HISTORY
    bands = [((s <= r) & (s > r - w)) for w in POOL_WINDOWS]
    return jnp.asarray(np.stack(bands).astype(np.float32), dtype=BF16)


def _mix_router(x2, ret, proj, pool_w_bf, pool_scale, w_out_bf, norm_w, rw_split, rb, seq):
    t = x2.shape[0]
    tm = TOKEN_TILE
    hist_blocks = tm // POOL_HISTORY
    u_col = (4 * D_RET) // D_POOL
    tile = lambda width: pl.BlockSpec((tm, width), lambda i: (i, 0))
    const2 = lambda shape: pl.BlockSpec(shape, lambda i: (0, 0))
    const3 = lambda shape: pl.BlockSpec(shape, lambda i: (0, 0, 0))
    return pl.pallas_call(
        functools.partial(_mix_router_kernel, seq),
        grid=(t // tm,),
        in_specs=[
            tile(D_MODEL), tile(D_RET),
            pl.BlockSpec((tm, D_POOL), lambda i: (i, u_col)),
            pl.BlockSpec((POOL_HISTORY, D_POOL), lambda i: (jnp.maximum(i * hist_blocks - 1, 0), u_col)),
            const3((len(POOL_WINDOWS), POOL_BLOCK, POOL_BLOCK + POOL_HISTORY)),
            const3((len(POOL_WINDOWS), POOL_GROUP_DIM, POOL_GROUP_DIM)),
            const2((1, D_POOL)), const2((D_MODEL, D_MODEL)), const2((1, D_MODEL)),
            const2((D_MODEL, 2 * LANES)), const2((1, LANES)),
        ],
        out_specs=[tile(D_MODEL), tile(D_MODEL // 2), tile(LANES), tile(LANES), const2((1, LANES))],
        out_shape=[
            jax.ShapeDtypeStruct((t, D_MODEL), F32),
            jax.ShapeDtypeStruct((t, D_MODEL // 2), U32),
            jax.ShapeDtypeStruct((t, LANES), F32),
            jax.ShapeDtypeStruct((t, LANES), F32),
            jax.ShapeDtypeStruct((1, LANES), F32),
        ],
        scratch_shapes=[pltpu.VMEM((tm + POOL_HISTORY, D_POOL), BF16)],
        compiler_params=_params(("arbitrary",)),
        name="mix_router",
    )(x2, ret, proj, proj, _pool_bands(), pool_w_bf, pool_scale, w_out_bf, norm_w, rw_split, rb)


def _positions_kernel(code_ref, off_ref, tri_ref, pos_ref, carry_ref):
    @pl.when(pl.program_id(0) == 0)
    def _():
        carry_ref[...] = jnp.zeros_like(carry_ref)

    code = code_ref[...]
    tm = code.shape[0]
    sel = (code > 0).astype(BF16)
    carry = carry_ref[...]
    rank = _dot(tri_ref[...], sel) + (carry + off_ref[...])
    carry_ref[...] = carry + jnp.sum(sel.astype(F32), axis=0, keepdims=True)
    lane = lax.broadcasted_iota(jnp.int32, (tm, LANES), 1)
    pos = jnp.zeros((tm, LANES), F32)
    for k in range(TOP_K):
        pk = jnp.sum(jnp.where(code == float(k + 1), rank, 0.0), axis=-1, keepdims=True)
        pos = jnp.where(lane == k, pk, pos)
    pos_ref[...] = pos.astype(jnp.int32)


def _positions(code, offsets):
    t = code.shape[0]
    tm = TOKEN_TILE
    tri = jnp.asarray(np.tril(np.ones((tm, tm), np.float32), -1), dtype=BF16)
    return pl.pallas_call(
        _positions_kernel,
        grid=(t // tm,),
        in_specs=[pl.BlockSpec((tm, LANES), lambda i: (i, 0)),
                  pl.BlockSpec((1, LANES), lambda i: (0, 0)),
                  pl.BlockSpec((tm, tm), lambda i: (0, 0))],
        out_specs=pl.BlockSpec((tm, LANES), lambda i: (i, 0)),
        out_shape=jax.ShapeDtypeStruct((t, LANES), jnp.int32),
        scratch_shapes=[pltpu.VMEM((1, LANES), F32)],
        compiler_params=_params(("arbitrary",)),
        name="positions",
    )(code, offsets, tri)


def _sc_mesh():
    return plsc.VectorSubcoreMesh(core_axis_name="core", subcore_axis_name="subcore")


def _sc_workers():
    info = plsc.get_sparse_core_info()
    return info.num_cores, info.num_cores * info.num_subcores


def _sc_gather(src, idx):
    n = idx.shape[0]
    d = src.shape[1]
    w = SC_WINDOW // 2
    num_cores, workers = _sc_workers()
    per = n // workers
    pairs = per // (2 * w)
    assert per * workers == n and pairs * 2 * w == per

    @functools.partial(
        pl.kernel, out_type=jax.ShapeDtypeStruct((n, d), src.dtype), mesh=_sc_mesh(),
        scratch_types=[pltpu.VMEM((w,), jnp.int32), pltpu.VMEM((w,), jnp.int32),
                       pltpu.VMEM((w, d), src.dtype), pltpu.VMEM((w, d), src.dtype),
                       pltpu.SemaphoreType.DMA, pltpu.SemaphoreType.DMA,
                       pltpu.SemaphoreType.DMA, pltpu.SemaphoreType.DMA],
        name="sc_gather")
    def gather(src_hbm, idx_hbm, out_hbm, idx0, idx1, rows0, rows1, g0, g1, w0, w1):
        first = (lax.axis_index("subcore") * num_cores + lax.axis_index("core")) * per

        def start_gather(win, idx_v, rows_v, sem):
            pltpu.sync_copy(idx_hbm.at[pl.ds(first + win * w, w)], idx_v)
            pltpu.async_copy(src_hbm.at[idx_v], rows_v, sem)

        def wait_gather(idx_v, rows_v, sem):
            pltpu.make_async_copy(src_hbm.at[idx_v], rows_v, sem).wait()

        def start_write(win, rows_v, sem):
            pltpu.async_copy(rows_v, out_hbm.at[pl.ds(first + win * w, w)], sem)

        def wait_write(rows_v, sem):
            pltpu.make_async_copy(rows_v, out_hbm.at[pl.ds(first, w)], sem).wait()

        start_gather(0, idx0, rows0, g0)

        @pl.loop(0, pairs)
        def _(j):
            even = 2 * j

            @pl.when(j > 0)
            def _():
                wait_write(rows1, w1)

            start_gather(even + 1, idx1, rows1, g1)
            wait_gather(idx0, rows0, g0)
            start_write(even, rows0, w0)
            wait_write(rows0, w0)

            @pl.when(j + 1 < pairs)
            def _():
                start_gather(even + 2, idx0, rows0, g0)

            wait_gather(idx1, rows1, g1)
            start_write(even + 1, rows1, w1)

        wait_write(rows1, w1)

    return gather(src, idx)


def _sc_scatter(src, idx):
    t, d = src.shape
    w = SC_WINDOW
    num_cores, workers = _sc_workers()
    per = t // workers
    assert idx.shape == (TOP_K, t) and per * workers == t and per % w == 0

    @functools.partial(
        pl.kernel, out_type=jax.ShapeDtypeStruct((TOP_K * t, d), src.dtype), mesh=_sc_mesh(),
        scratch_types=[pltpu.VMEM((TOP_K, w), jnp.int32), pltpu.VMEM((w, d), src.dtype),
                       pltpu.SemaphoreType.DMA],
        name="sc_scatter")
    def scatter(src_hbm, idx_hbm, out_hbm, idx_v, rows_v, sem):
        first = (lax.axis_index("subcore") * num_cores + lax.axis_index("core")) * per

        @pl.loop(0, per // w)
        def _(j):
            base = first + j * w
            for k in range(TOP_K):
                pltpu.sync_copy(idx_hbm.at[k, pl.ds(base, w)], idx_v.at[k])
            pltpu.sync_copy(src_hbm.at[pl.ds(base, w)], rows_v)
            copies = [pltpu.async_copy(rows_v, out_hbm.at[idx_v.at[k]], sem) for k in range(TOP_K)]
            for c in copies:
                c.wait()

    return scatter(src, idx)


def _experts_kernel(items_ref, xs_ref, wgu_ref, bg_ref, bu_ref, wdn_ref, bd_ref, perm_ref, y_ref,
                    wg_ref, wu_ref, wd_ref):
    i = pl.program_id(0)
    tm = xs_ref.shape[0]

    @pl.when(items_ref[I_PREP, i] == 1)
    def _():
        ps = items_ref[I_PREP_SLOT, i]
        perm = perm_ref[...]
        pair = 2 * LANES
        for c in range(2 * D_MODEL // pair):
            sel = _dot(wgu_ref[:, c * pair:(c + 1) * pair].astype(BF16), perm).astype(BF16)
            wg_ref[ps, :, c * LANES:(c + 1) * LANES] = sel[:, :LANES]
            wu_ref[ps, :, c * LANES:(c + 1) * LANES] = sel[:, LANES:]
        wd_ref[ps] = wdn_ref[...].astype(BF16)

    def expert_mlp(r0, r1):
        n = r1 - r0
        s = items_ref[I_SLOT, i]
        x_lo, x_hi = _unpack_bf16_pair(xs_ref[r0:r1, :])
        x = jnp.concatenate([x_lo.astype(BF16), x_hi.astype(BF16)], axis=1)
        acc = jnp.zeros((n, D_MODEL), F32)
        for c in range(D_MODEL // EXPERT_COL_CHUNK):
            cs = slice(c * EXPERT_COL_CHUNK, (c + 1) * EXPERT_COL_CHUNK)
            gate = jnp.minimum(_dot(x, wg_ref[s, :, cs]) + bg_ref[:, cs], SWIGLU_LIMIT)
            up = jnp.clip(_dot(x, wu_ref[s, :, cs]) + bu_ref[:, cs], -SWIGLU_LIMIT, SWIGLU_LIMIT)
            act = (up + 1.0) * (gate * _sigmoid(SWIGLU_ALPHA * gate))
            acc = acc + _dot(act.astype(BF16), wd_ref[s, cs, :])
        y = acc + bd_ref[...]
        half = D_MODEL // 2
        packed = _pack_bf16_pair(y[:, :half], y[:, half:])
        rows = items_ref[I_TILE, i] * tm + r0 + lax.broadcasted_iota(jnp.int32, (n, 1), 0)
        mine = (rows >= items_ref[I_LO, i]) & (rows < items_ref[I_HI, i])

        @pl.when(items_ref[I_FIRST, i] == 1)
        def _():
            y_ref[r0:r1, :] = packed
            if r0 > 0:
                y_ref[0:r0, :] = jnp.zeros((r0, half), U32)
            if r1 < tm:
                y_ref[r1:tm, :] = jnp.zeros((tm - r1, half), U32)

        @pl.when(items_ref[I_FIRST, i] == 0)
        def _():
            y_ref[r0:r1, :] = jnp.where(mine, packed, y_ref[r0:r1, :])

    for mode, (r0, r1) in ((MODE_FULL, (0, tm)), (MODE_LOWER, (0, tm // 2)), (MODE_UPPER, (tm // 2, tm))):
        pl.when(items_ref[I_MODE, i] == mode)(functools.partial(expert_mlp, r0, r1))


(I_TILE, I_EXPERT, I_LO, I_HI, I_FIRST, I_MODE, I_FETCH, I_PREP, I_PREP_SLOT, I_SLOT) = range(10)
ITEM_FIELDS = 16
(MODE_SKIP, MODE_FULL, MODE_LOWER, MODE_UPPER) = range(4)


def _plan_kernel(tm, n_tiles, cnt_ref, off_ref, items_ref):
    n = items_ref.shape[1]
    cnt_r = cnt_ref[...]
    sub = lax.broadcasted_iota(jnp.int32, (LANES, LANES), 0).astype(F32)
    lan = lax.broadcasted_iota(jnp.int32, (LANES, LANES), 1).astype(F32)
    big = float(4 * LANES)

    def col(row):
        return jnp.sum(jnp.where(lan == sub, row, 0.0), axis=1, keepdims=True)

    def row(column):
        return jnp.sum(jnp.where(lan == sub, column, 0.0), axis=0, keepdims=True)

    def prefix(r):
        return jnp.sum(jnp.where(lan <= sub, r, 0.0), axis=1, keepdims=True)

    cnt = col(cnt_r)
    ends = prefix(cnt_r)
    starts = ends - cnt
    off_ref[...] = row(starts)
    used = cnt > 0.0
    inv_tm = 1.0 / tm
    first_tile = jnp.floor(starts * inv_tm)
    last_tile = jnp.floor(jnp.maximum(ends - 1.0, 0.0) * inv_tm)
    per = jnp.where(used, last_tile - first_tile + 1.0, 0.0)
    item_end = prefix(row(per))
    item_start = item_end - per
    used_r = row(used.astype(F32))
    ordinal = prefix(used_r) - 1.0
    nxt = jnp.min(jnp.where((lan > sub) & (used_r > 0.0), lan, big), axis=1, keepdims=True)
    has_next = nxt < big
    e_col = sub[:, 0:1]
    nxt = jnp.where(has_next, nxt, e_col)
    e_first = jnp.min(jnp.where(used_r > 0.0, lan[0:1, :], big), axis=1, keepdims=True)
    e_last = jnp.max(jnp.where(used_r > 0.0, lan[0:1, :], -1.0), axis=1, keepdims=True)
    total = jnp.max(item_end, axis=0, keepdims=True)

    item = lax.broadcasted_iota(jnp.int32, (LANES, n), 1).astype(F32) - 1.0
    mine = ((item >= item_start) & (item < item_end)).astype(F32)
    pick = lambda column: jnp.sum(mine * column, axis=0, keepdims=True)
    item_r = item[0:1, :]
    valid = pick(jnp.ones_like(cnt))
    lead = item_r < 0.0
    past = item_r >= total
    new_tile = 1.0 - pick(((item == item_start) & (starts - first_tile * tm > 0.0)).astype(F32))

    def put(field, value):
        items_ref[field:field + 1, :] = value.astype(jnp.int32)

    tile = pick(first_tile - item_start) + item_r
    lo = pick(starts)
    hi = pick(ends)
    needs_lower = lo < tile * tm + 0.5 * tm
    needs_upper = hi > tile * tm + 0.5 * tm
    mode = jnp.where(needs_lower & needs_upper, float(MODE_FULL),
                     jnp.where(needs_lower, float(MODE_LOWER), float(MODE_UPPER)))
    put(I_TILE, jnp.where(past, float(n_tiles - 1), jnp.where(lead, 0.0, tile)))
    put(I_EXPERT, jnp.where(past, e_last, jnp.where(lead, e_first, pick(e_col))))
    put(I_LO, lo)
    put(I_HI, hi)
    put(I_FIRST, valid * new_tile)
    put(I_MODE, valid * mode)
    put(I_FETCH, jnp.where(past, e_last, jnp.where(lead, e_first, pick(nxt))))
    put(I_PREP, jnp.where(lead, 1.0, pick(((item == item_end - 1.0) & has_next).astype(F32))))
    put(I_PREP_SLOT, pick(ordinal + 1.0 - 2.0 * jnp.floor((ordinal + 1.0) * 0.5)))
    put(I_SLOT, pick(ordinal - 2.0 * jnp.floor(ordinal * 0.5)))
    for field in range(I_SLOT + 1, ITEM_FIELDS):
        put(field, jnp.zeros_like(valid))


def _plan(counts, n_rows, tm):
    n_tiles = n_rows // tm
    n_items = n_tiles + N_EXPERTS
    width = -(-n_items // LANES) * LANES
    offsets, items = pl.pallas_call(
        functools.partial(_plan_kernel, float(tm), n_tiles),
        out_shape=[jax.ShapeDtypeStruct((1, LANES), F32),
                   jax.ShapeDtypeStruct((ITEM_FIELDS, width), jnp.int32)],
        name="plan",
    )(counts)
    return offsets, items, n_items


def _experts(xs, items, n_items, w_gate_up, bg, bu, w_down, bd):
    n_rows = xs.shape[0]
    tm = EXPERT_TILE
    half = D_MODEL // 2
    j = np.arange(2 * LANES)
    perm = np.zeros((2 * LANES, 2 * LANES), np.float32)
    perm[j, np.where(j % 2 == 0, j // 2, LANES + j // 2)] = 1.0
    row_tile = pl.BlockSpec((tm, half), lambda i, items: (items[I_TILE, i], 0))
    by_expert = lambda *shape: pl.BlockSpec((None,) + shape, lambda i, items: (items[I_EXPERT, i], 0, 0))
    by_fetch = lambda *shape: pl.BlockSpec((None,) + shape, lambda i, items: (items[I_FETCH, i], 0, 0))
    return pl.pallas_call(
        _experts_kernel,
        grid_spec=pltpu.PrefetchScalarGridSpec(
            num_scalar_prefetch=1,
            grid=(n_items,),
            in_specs=[row_tile, by_fetch(D_MODEL, 2 * D_MODEL), by_expert(1, D_MODEL),
                      by_expert(1, D_MODEL), by_fetch(D_MODEL, D_MODEL), by_expert(1, D_MODEL),
                      pl.BlockSpec((2 * LANES, 2 * LANES), lambda i, items: (0, 0))],
            out_specs=row_tile,
            scratch_shapes=[pltpu.VMEM((2, D_MODEL, D_MODEL), BF16)] * 3,
        ),
        out_shape=jax.ShapeDtypeStruct((n_rows, half), U32),
        compiler_params=_params(("arbitrary",)),
        name="experts",
    )(items, xs, w_gate_up, bg, bu, w_down, bd, jnp.asarray(perm, dtype=BF16))


def _tail_kernel(h1_ref, yu_ref, gates_ref, p_ref, nple_ref, wg_ref, wp_ref, nfin_ref, out_ref):
    gates = gates_ref[...]
    lo = None
    hi = None
    for k in range(TOP_K):
        gk = gates[:, k:k + 1]
        yl, yh = _unpack_bf16_pair(yu_ref[k])
        lo = gk * yl if lo is None else lo + gk * yl
        hi = gk * yh if hi is None else hi + gk * yh
    h2 = h1_ref[...] + jnp.concatenate([lo, hi], axis=1)
    hn = _rms(h2, nple_ref[...]).astype(BF16)
    gate = _sigmoid(_dot(hn, wg_ref[...]))
    h3 = h2 + gate * _dot(p_ref[...].astype(BF16), wp_ref[...])
    out_ref[...] = _rms(h3, nfin_ref[...])


def _tail(h1, yu, gates, p2, norm_ple_w, ple_gate_bf, ple_proj_bf, final_norm_w):
    t = h1.shape[0]
    tm = TOKEN_TILE
    half = D_MODEL // 2
    tile = lambda width: pl.BlockSpec((tm, width), lambda i: (i, 0))
    const2 = lambda shape: pl.BlockSpec(shape, lambda i: (0, 0))
    return pl.pallas_call(
        _tail_kernel,
        grid=(t // tm,),
        in_specs=[tile(D_MODEL), pl.BlockSpec((TOP_K, tm, half), lambda i: (0, i, 0)),
                  tile(LANES), tile(PLE_DIM), const2((1, D_MODEL)),
                  const2((D_MODEL, D_MODEL)), const2((PLE_DIM, D_MODEL)), const2((1, D_MODEL))],
        out_specs=tile(D_MODEL),
        out_shape=jax.ShapeDtypeStruct((t, D_MODEL), F32),
        compiler_params=_params(("parallel",)),
        name="tail",
    )(h1, yu, gates, p2, norm_ple_w, ple_gate_bf, ple_proj_bf, final_norm_w)


def kernel(x, p, positions, w_in, w_out, ret_gn_w, pool_w, pool_scale, norm_mix_w, norm_moe_w, router_w, router_b, expert_w_gate_up, expert_b_gate_up, expert_w_down, expert_b_down, norm_ple_w, ple_gate_w, ple_proj_w, final_norm_w):
    batch, seq, d = x.shape
    depth = w_in.shape[0]
    assert depth == 1 and d == D_MODEL and seq % TOKEN_TILE == 0
    assert seq % (RET_CHUNK * RET_STEP_CHUNKS) == 0
    t = batch * seq
    n_rows = t * TOP_K
    assert n_rows % EXPERT_TILE == 0
    row = lambda a: a.reshape(1, -1).astype(F32)

    h = x.reshape(t, d)
    pos2 = positions.reshape(t, 1)
    for l in range(depth):
        proj = _in_proj(h, pos2, row(norm_mix_w[l]), w_in[l].astype(BF16))
        ret = _retention(proj, row(ret_gn_w[l]), batch, seq)

        rw = jnp.pad(router_w[l].astype(F32), ((0, 0), (0, LANES - N_EXPERTS)))
        rw_hi = rw.astype(BF16)
        rw_lo = (rw - rw_hi.astype(F32)).astype(BF16)
        rw_split = jnp.concatenate([rw_hi, rw_lo], axis=1)
        rb = jnp.pad(router_b[l].astype(F32), (0, LANES - N_EXPERTS), constant_values=NEG_BIG).reshape(1, LANES)
        h1, hn_packed, code, gates, counts = _mix_router(
            h, ret, proj, pool_w[l].astype(BF16), row(pool_scale[l]), w_out[l].astype(BF16),
            row(norm_moe_w[l]), rw_split, rb, seq)

        offsets, items, n_items = _plan(counts, n_rows, EXPERT_TILE)
        pos = _positions(code, offsets)
        pos_km = pos[:, :TOP_K].T

        xs = _sc_scatter(hn_packed, pos_km)
        bgu = expert_b_gate_up[l].reshape(N_EXPERTS, 1, D_MODEL, 2).astype(F32)
        y = _experts(xs, items, n_items, expert_w_gate_up[l], bgu[..., 0], bgu[..., 1], expert_w_down[l],
                     expert_b_down[l].reshape(N_EXPERTS, 1, D_MODEL).astype(F32))
        yu = _sc_gather(y, pos_km.reshape(n_rows)).reshape(TOP_K, t, d // 2)

        h = _tail(h1, yu, gates, p[l].reshape(t, PLE_DIM), row(norm_ple_w[l]),
                  ple_gate_w[l].astype(BF16), ple_proj_w[l].astype(BF16), row(final_norm_w))
    return h.reshape(batch, seq, d)
```

```python
import functools
import math

import numpy as np
import jax
import jax.numpy as jnp
from jax import lax
from jax.experimental import pallas as pl
from jax.experimental.pallas import tpu as pltpu
from jax.experimental.pallas import tpu_sc as plsc

D_MODEL = 1024
D_RET = 512
D_POOL = 512
RET_HEADS = 8
RET_HEAD_DIM = 64
HEAD_PAIRS = RET_HEADS // 2
ROPE_BASE = 10000.0
POOL_WINDOWS = (2, 4, 8, 16)
POOL_GROUP_DIM = 128
POOL_HISTORY = 16
POOL_BLOCK = 128
D_IN_PROJ = 4 * D_RET + D_POOL
N_EXPERTS = 32
TOP_K = 4
SWIGLU_LIMIT = 7.0
SWIGLU_ALPHA = 1.702
PLE_DIM = 256
NORM_EPS = 1e-5
GN_EPS = 1e-5

LANES = 128
NEG_BIG = -1e30

TOKEN_TILE = 512
RET_CHUNK = 256
RET_STEP_CHUNKS = 4
EXPERT_TILE = 512
EXPERT_COL_CHUNK = 512
SC_WINDOW = 128
BATCH_GROUPS = 2
VMEM_LIMIT = 56 * 1024 * 1024

F32 = jnp.float32
BF16 = jnp.bfloat16
U32 = jnp.uint32


def _params(semantics):
    return pltpu.CompilerParams(dimension_semantics=semantics, vmem_limit_bytes=VMEM_LIMIT)


def _dot(a, b):
    return jnp.dot(a, b, preferred_element_type=F32)


def _dot_nt(a, b):
    return lax.dot_general(a, b, (((1,), (1,)), ((), ())), preferred_element_type=F32)


def _dot_tn(a, b):
    return lax.dot_general(a, b, (((0,), (0,)), ((), ())), preferred_element_type=F32)


def _rms(x, w):
    ms = jnp.mean(x * x, axis=-1, keepdims=True)
    return x * lax.rsqrt(ms + NORM_EPS) * w


def _sigmoid(z):
    return 1.0 / (1.0 + jnp.exp(-z))


def _pack_bf16_pair(lo, hi):
    lo_bits = pltpu.bitcast(lo.astype(BF16).astype(F32), U32) >> 16
    hi_bits = pltpu.bitcast(hi.astype(BF16).astype(F32), U32) & jnp.uint32(0xFFFF0000)
    return lo_bits | hi_bits


def _unpack_bf16_pair(packed):
    lo = pltpu.bitcast(packed << 16, F32)
    hi = pltpu.bitcast(packed & jnp.uint32(0xFFFF0000), F32)
    return lo, hi


def _in_proj_kernel(x_ref, pos_ref, nw_ref, w_ref, freq_ref, out_ref):
    hn = _rms(x_ref[...], nw_ref[...]).astype(BF16)
    ang = pos_ref[...].astype(F32) * freq_ref[0:1, :]
    cos = jnp.cos(ang)
    sin = jnp.sin(ang)
    slab = 2 * LANES
    cos_t = jnp.concatenate([cos, cos], axis=1)
    sin_up = jnp.concatenate([sin * freq_ref[1:2, :]] * 2, axis=1)
    sin_dn = jnp.concatenate([sin * freq_ref[2:3, :]] * 2, axis=1)
    half = RET_HEAD_DIM // 2

    for s in range(2 * D_RET // slab):
        sl = slice(s * slab, (s + 1) * slab)
        v = _dot(hn, w_ref[:, sl])
        v = v * cos_t + pltpu.roll(v, slab - half, 1) * sin_up + pltpu.roll(v, half, 1) * sin_dn
        if s >= D_RET // slab:
            v = v * (RET_HEAD_DIM ** -0.5)
        out_ref[:, sl] = v.astype(BF16)
    for s in range(2 * D_RET // 512, D_IN_PROJ // 512):
        sl = slice(s * 512, (s + 1) * 512)
        out_ref[:, sl] = _dot(hn, w_ref[:, sl]).astype(BF16)


def _in_proj(x2, pos2, norm_w, w_in_bf, tile0, t):
    tm = TOKEN_TILE
    j = np.arange(LANES)
    half = RET_HEAD_DIM // 2
    inv_freq = ROPE_BASE ** (-(np.arange(half, dtype=np.float32)) / half)
    freq = np.zeros((8, LANES), np.float32)
    freq[0] = inv_freq[j % half]
    freq[1] = np.where(j % RET_HEAD_DIM < half, -1.0, 0.0)
    freq[2] = np.where(j % RET_HEAD_DIM >= half, 1.0, 0.0)
    return pl.pallas_call(
        _in_proj_kernel,
        grid=(t // tm,),
        in_specs=[
            pl.BlockSpec((tm, D_MODEL), lambda i: (i + tile0, 0)),
            pl.BlockSpec((tm, 1), lambda i: (i + tile0, 0)),
            pl.BlockSpec((1, D_MODEL), lambda i: (0, 0)),
            pl.BlockSpec((D_MODEL, D_IN_PROJ), lambda i: (0, 0)),
            pl.BlockSpec((8, LANES), lambda i: (0, 0)),
        ],
        out_specs=pl.BlockSpec((tm, D_IN_PROJ), lambda i: (i, 0)),
        out_shape=jax.ShapeDtypeStruct((t, D_IN_PROJ), BF16),
        compiler_params=_params(("parallel",)),
        name="in_proj",
    )(x2, pos2, norm_w, w_in_bf, jnp.asarray(freq))


def _retention_kernel(q_ref, k_ref, v_ref, g_ref, dec_ref, xi_ref, zeta_ref, cd_ref, bd_ref,
                      m64_ref, eye_ref, gnw_ref, out_ref, state_ref):
    @pl.when(pl.program_id(1) == 0)
    def _():
        state_ref[...] = jnp.zeros_like(state_ref)

    c = dec_ref.shape[1]
    lane = lax.broadcasted_iota(jnp.int32, (1, LANES), 1)
    m64 = m64_ref[...]
    bd = bd_ref[...]
    eye = eye_ref[...]

    def group_mean(v):
        hi = v.astype(BF16)
        lo = (v - hi.astype(F32)).astype(BF16)
        return _dot(jnp.concatenate([hi, lo], axis=1), m64)

    for ci in range(q_ref.shape[0] // c):
        rows = slice(ci * c, (ci + 1) * c)
        for p in range(HEAD_PAIRS):
            sl = slice(p * LANES, (p + 1) * LANES)
            qp = q_ref[rows, sl]
            kp = k_ref[rows, sl]
            vp = v_ref[rows, sl]
            y = None
            for hh in range(2):
                in_head = (lane >= RET_HEAD_DIM) == bool(hh)
                qm = jnp.where(in_head, qp, jnp.zeros_like(qp))
                vm = jnp.where(in_head, vp, jnp.zeros_like(vp))
                scores = _dot_nt(qm, kp) * dec_ref[2 * p + hh]
                part = _dot(scores.astype(BF16), vm)
                y = part if y is None else y + part
            st = state_ref[p]
            y = y + _dot((qp.astype(F32) * xi_ref[p]).astype(BF16), st.astype(BF16))
            kz = (kp.astype(F32) * zeta_ref[p]).astype(BF16)
            kz_t = _dot_nt(eye, kz).astype(BF16)
            state_ref[p] = cd_ref[p] * st + _dot(kz_t, vp) * bd
            mu = group_mean(y)
            var = group_mean(y * y) - mu * mu
            yn = (y - mu) * lax.rsqrt(var + GN_EPS) * gnw_ref[:, sl]
            g = g_ref[rows, sl].astype(F32)
            out_ref[rows, sl] = (yn * g * _sigmoid(g)).astype(BF16)


def _retention_tables(c):
    h = np.arange(RET_HEADS, dtype=np.float64)
    log_gamma = np.log1p(-np.power(2.0, -5.0 - h))
    idx = np.arange(c, dtype=np.float64)
    rel = idx[:, None] - idx[None, :]
    dec = np.where(rel >= 0, np.exp(np.where(rel >= 0, rel, 0.0)[None] * log_gamma[:, None, None]), 0.0)
    lane_head = np.arange(LANES) // RET_HEAD_DIM
    xi = np.zeros((HEAD_PAIRS, c, LANES))
    zeta = np.zeros((HEAD_PAIRS, c, LANES))
    cd = np.zeros((HEAD_PAIRS, LANES, LANES))
    same = lane_head[:, None] == lane_head[None, :]
    for p in range(HEAD_PAIRS):
        lg = log_gamma[2 * p + lane_head]
        xi[p] = np.exp((idx + 1.0)[:, None] * lg[None, :])
        zeta[p] = np.exp((c - 1 - idx)[:, None] * lg[None, :])
        cd[p] = np.where(same, np.exp(c * lg)[:, None], 0.0)
    bd = same.astype(np.float32)
    m64 = np.concatenate([same, same], axis=0).astype(np.float32) / RET_HEAD_DIM
    f = lambda a: jnp.asarray(a, dtype=F32)
    return (f(dec), f(xi), f(zeta), f(cd), f(bd), jnp.asarray(m64, dtype=BF16),
            jnp.asarray(np.eye(LANES, dtype=np.float32), dtype=BF16))


def _retention(proj, gn_w, batch, seq):
    t = proj.shape[0]
    c = RET_CHUNK
    rows = RET_STEP_CHUNKS * c
    n = seq // rows
    dec, xi, zeta, cd, bd, m64, eye = _retention_tables(c)
    col = lambda j: pl.BlockSpec((rows, D_RET), lambda b, i, j=j: (b * n + i, j))
    const3 = lambda shape: pl.BlockSpec(shape, lambda b, i: (0, 0, 0))
    const2 = lambda shape: pl.BlockSpec(shape, lambda b, i: (0, 0))
    return pl.pallas_call(
        _retention_kernel,
        grid=(batch, n),
        in_specs=[col(0), col(1), col(2), col(3),
                  const3((RET_HEADS, c, c)), const3((HEAD_PAIRS, c, LANES)),
                  const3((HEAD_PAIRS, c, LANES)), const3((HEAD_PAIRS, LANES, LANES)),
                  const2((LANES, LANES)), const2((2 * LANES, LANES)), const2((LANES, LANES)),
                  const2((1, D_RET))],
        out_specs=pl.BlockSpec((rows, D_RET), lambda b, i: (b * n + i, 0)),
        out_shape=jax.ShapeDtypeStruct((t, D_RET), BF16),
        scratch_shapes=[pltpu.VMEM((HEAD_PAIRS, LANES, LANES), F32)],
        compiler_params=_params(("arbitrary", "arbitrary")),
        name="retention",
    )(proj, proj, proj, proj, dec, xi, zeta, cd, bd, m64, eye, gn_w)


def _mix_router_kernel(seq, x_ref, ret_ref, u_ref, uprev_ref, band_ref, pw_ref, ps_ref, wout_ref,
                       nw_ref, rw_ref, rb_ref,
                       h1_ref, hnp_ref, code_ref, gates_ref, cnt_ref, uext_ref):
    i = pl.program_id(0)
    tm = x_ref.shape[0]
    t0 = lax.rem(i * tm, seq)

    @pl.when(i == 0)
    def _():
        cnt_ref[...] = jnp.zeros_like(cnt_ref)

    prev = uprev_ref[...]
    uext_ref[0:POOL_HISTORY, :] = jnp.where(t0 == 0, jnp.zeros_like(prev), prev)
    uext_ref[POOL_HISTORY:, :] = u_ref[...]

    row = lax.broadcasted_iota(jnp.int32, (tm, 1), 0)
    t_seq = (t0 + row + 1).astype(F32)
    mixed = []
    for gi, w in enumerate(POOL_WINDOWS):
        sl = slice(gi * POOL_GROUP_DIM, (gi + 1) * POOL_GROUP_DIM)
        wsum = jnp.concatenate(
            [_dot(band_ref[gi], uext_ref[r0:r0 + POOL_BLOCK + POOL_HISTORY, sl])
             for r0 in range(0, tm, POOL_BLOCK)], axis=0)
        count = jnp.minimum(t_seq, float(w))
        pooled = wsum / count - u_ref[:, sl].astype(F32)
        mixed.append(_dot(pooled.astype(BF16), pw_ref[gi]))
    pool = (jnp.concatenate(mixed, axis=1) * ps_ref[...]).astype(BF16)

    h1 = (x_ref[...] + _dot(ret_ref[...], wout_ref[0:D_RET, :])
          + _dot(pool, wout_ref[D_RET:, :]))
    h1_ref[...] = h1
    hn = _rms(h1, nw_ref[...])
    hn_hi = hn.astype(BF16)
    half = D_MODEL // 2
    hnp_ref[...] = _pack_bf16_pair(hn[:, :half], hn[:, half:])

    hn_lo = (hn - hn_hi.astype(F32)).astype(BF16)
    both = _dot(hn_hi, rw_ref[...])
    logits = (both[:, :LANES] + both[:, LANES:] + _dot(hn_lo, rw_ref[:, :LANES])
              + rb_ref[...])

    lane = lax.broadcasted_iota(jnp.int32, (tm, LANES), 1).astype(F32)
    code = jnp.zeros((tm, LANES), F32)
    vals = []
    work = logits
    for k in range(TOP_K):
        m = jnp.max(work, axis=-1, keepdims=True)
        idx = jnp.min(jnp.where(work == m, lane, float(LANES)), axis=-1, keepdims=True)
        chosen = lane == idx
        code = jnp.where(chosen, float(k + 1), code)
        work = jnp.where(chosen, -jnp.inf, work)
        vals.append(m)
    exps = [jnp.exp(v - vals[0]) for v in vals]
    denom = exps[0] + exps[1] + exps[2] + exps[3]
    gates = jnp.zeros((tm, LANES), F32)
    for k in range(TOP_K):
        gates = jnp.where(lane == float(k), exps[k] / denom, gates)
    code_ref[...] = code
    gates_ref[...] = gates
    cnt_ref[...] += jnp.sum((code > 0).astype(F32), axis=0, keepdims=True)


def _pool_bands():
    r = np.arange(POOL_BLOCK)[:, None]
    s = np.arange(POOL_BLOCK + POOL_HISTORY)[None, :] - POOL_HISTORY
    bands = [((s <= r) & (s > r - w)) for w in POOL_WINDOWS]
    return jnp.asarray(np.stack(bands).astype(np.float32), dtype=BF16)


def _mix_router(x2, ret, proj, pool_w_bf, pool_scale, w_out_bf, norm_w, rw_split, rb, seq, tile0):
    t = ret.shape[0]
    tm = TOKEN_TILE
    hist_blocks = tm // POOL_HISTORY
    u_col = (4 * D_RET) // D_POOL
    tile = lambda width: pl.BlockSpec((tm, width), lambda i: (i, 0))
    const2 = lambda shape: pl.BlockSpec(shape, lambda i: (0, 0))
    const3 = lambda shape: pl.BlockSpec(shape, lambda i: (0, 0, 0))
    return pl.pallas_call(
        functools.partial(_mix_router_kernel, seq),
        grid=(t // tm,),
        in_specs=[
            pl.BlockSpec((tm, D_MODEL), lambda i: (i + tile0, 0)), tile(D_RET),
            pl.BlockSpec((tm, D_POOL), lambda i: (i, u_col)),
            pl.BlockSpec((POOL_HISTORY, D_POOL), lambda i: (jnp.maximum(i * hist_blocks - 1, 0), u_col)),
            const3((len(POOL_WINDOWS), POOL_BLOCK, POOL_BLOCK + POOL_HISTORY)),
            const3((len(POOL_WINDOWS), POOL_GROUP_DIM, POOL_GROUP_DIM)),
            const2((1, D_POOL)), const2((D_MODEL, D_MODEL)), const2((1, D_MODEL)),
            const2((D_MODEL, 2 * LANES)), const2((1, LANES)),
        ],
        out_specs=[tile(D_MODEL), tile(D_MODEL // 2), tile(LANES), tile(LANES), const2((1, LANES))],
        out_shape=[
            jax.ShapeDtypeStruct((t, D_MODEL), F32),
            jax.ShapeDtypeStruct((t, D_MODEL // 2), U32),
            jax.ShapeDtypeStruct((t, LANES), F32),
            jax.ShapeDtypeStruct((t, LANES), F32),
            jax.ShapeDtypeStruct((1, LANES), F32),
        ],
        scratch_shapes=[pltpu.VMEM((tm + POOL_HISTORY, D_POOL), BF16)],
        compiler_params=_params(("arbitrary",)),
        name="mix_router",
    )(x2, ret, proj, proj, _pool_bands(), pool_w_bf, pool_scale, w_out_bf, norm_w, rw_split, rb)


def _positions_kernel(code_ref, off_ref, tri_ref, pos_ref, carry_ref):
    @pl.when(pl.program_id(0) == 0)
    def _():
        carry_ref[...] = jnp.zeros_like(carry_ref)

    code = code_ref[...]
    tm = code.shape[0]
    sel = (code > 0).astype(BF16)
    carry = carry_ref[...]
    rank = _dot(tri_ref[...], sel) + (carry + off_ref[...])
    carry_ref[...] = carry + jnp.sum(sel.astype(F32), axis=0, keepdims=True)
    lane = lax.broadcasted_iota(jnp.int32, (tm, LANES), 1)
    pos = jnp.zeros((tm, LANES), F32)
    for k in range(TOP_K):
        pk = jnp.sum(jnp.where(code == float(k + 1), rank, 0.0), axis=-1, keepdims=True)
        pos = jnp.where(lane == k, pk, pos)
    pos_ref[...] = pos.astype(jnp.int32)


def _positions(code, offsets):
    t = code.shape[0]
    tm = TOKEN_TILE
    tri = jnp.asarray(np.tril(np.ones((tm, tm), np.float32), -1), dtype=BF16)
    return pl.pallas_call(
        _positions_kernel,
        grid=(t // tm,),
        in_specs=[pl.BlockSpec((tm, LANES), lambda i: (i, 0)),
                  pl.BlockSpec((1, LANES), lambda i: (0, 0)),
                  pl.BlockSpec((tm, tm), lambda i: (0, 0))],
        out_specs=pl.BlockSpec((tm, LANES), lambda i: (i, 0)),
        out_shape=jax.ShapeDtypeStruct((t, LANES), jnp.int32),
        scratch_shapes=[pltpu.VMEM((1, LANES), F32)],
        compiler_params=_params(("arbitrary",)),
        name="positions",
    )(code, offsets, tri)


def _sc_mesh():
    return plsc.VectorSubcoreMesh(core_axis_name="core", subcore_axis_name="subcore")


def _sc_workers():
    info = plsc.get_sparse_core_info()
    return info.num_cores, info.num_cores * info.num_subcores


def _sc_gather(src, idx):
    n = idx.shape[0]
    d = src.shape[1]
    w = SC_WINDOW // 2
    num_cores, workers = _sc_workers()
    per = n // workers
    pairs = per // (2 * w)
    assert per * workers == n and pairs * 2 * w == per

    @functools.partial(
        pl.kernel, out_type=jax.ShapeDtypeStruct((n, d), src.dtype), mesh=_sc_mesh(),
        scratch_types=[pltpu.VMEM((w,), jnp.int32), pltpu.VMEM((w,), jnp.int32),
                       pltpu.VMEM((w, d), src.dtype), pltpu.VMEM((w, d), src.dtype),
                       pltpu.SemaphoreType.DMA, pltpu.SemaphoreType.DMA,
                       pltpu.SemaphoreType.DMA, pltpu.SemaphoreType.DMA],
        name="sc_gather")
    def gather(src_hbm, idx_hbm, out_hbm, idx0, idx1, rows0, rows1, g0, g1, w0, w1):
        first = (lax.axis_index("subcore") * num_cores + lax.axis_index("core")) * per

        def start_gather(win, idx_v, rows_v, sem):
            pltpu.sync_copy(idx_hbm.at[pl.ds(first + win * w, w)], idx_v)
            pltpu.async_copy(src_hbm.at[idx_v], rows_v, sem)

        def wait_gather(idx_v, rows_v, sem):
            pltpu.make_async_copy(src_hbm.at[idx_v], rows_v, sem).wait()

        def start_write(win, rows_v, sem):
            pltpu.async_copy(rows_v, out_hbm.at[pl.ds(first + win * w, w)], sem)

        def wait_write(rows_v, sem):
            pltpu.make_async_copy(rows_v, out_hbm.at[pl.ds(first, w)], sem).wait()

        start_gather(0, idx0, rows0, g0)

        @pl.loop(0, pairs)
        def _(j):
            even = 2 * j

            @pl.when(j > 0)
            def _():
                wait_write(rows1, w1)

            start_gather(even + 1, idx1, rows1, g1)
            wait_gather(idx0, rows0, g0)
            start_write(even, rows0, w0)
            wait_write(rows0, w0)

            @pl.when(j + 1 < pairs)
            def _():
                start_gather(even + 2, idx0, rows0, g0)

            wait_gather(idx1, rows1, g1)
            start_write(even + 1, rows1, w1)

        wait_write(rows1, w1)

    return gather(src, idx)


def _sc_scatter(src, idx):
    t, d = src.shape
    w = SC_WINDOW
    num_cores, workers = _sc_workers()
    per = t // workers
    assert idx.shape == (TOP_K, t) and per * workers == t and per % w == 0

    @functools.partial(
        pl.kernel, out_type=jax.ShapeDtypeStruct((TOP_K * t, d), src.dtype), mesh=_sc_mesh(),
        scratch_types=[pltpu.VMEM((TOP_K, w), jnp.int32), pltpu.VMEM((w, d), src.dtype),
                       pltpu.SemaphoreType.DMA],
        name="sc_scatter")
    def scatter(src_hbm, idx_hbm, out_hbm, idx_v, rows_v, sem):
        first = (lax.axis_index("subcore") * num_cores + lax.axis_index("core")) * per

        @pl.loop(0, per // w)
        def _(j):
            base = first + j * w
            for k in range(TOP_K):
                pltpu.sync_copy(idx_hbm.at[k, pl.ds(base, w)], idx_v.at[k])
            pltpu.sync_copy(src_hbm.at[pl.ds(base, w)], rows_v)
            copies = [pltpu.async_copy(rows_v, out_hbm.at[idx_v.at[k]], sem) for k in range(TOP_K)]
            for c in copies:
                c.wait()

    return scatter(src, idx)


def _experts_kernel(items_ref, xs_ref, wgu_ref, bg_ref, bu_ref, wdn_ref, bd_ref, perm_ref, y_ref,
                    wg_ref, wu_ref, wd_ref):
    i = pl.program_id(0)
    tm = xs_ref.shape[0]

    @pl.when(items_ref[I_PREP, i] == 1)
    def _():
        ps = items_ref[I_PREP_SLOT, i]
        perm = perm_ref[...]
        pair = 2 * LANES
        for c in range(2 * D_MODEL // pair):
            sel = _dot(wgu_ref[:, c * pair:(c + 1) * pair].astype(BF16), perm).astype(BF16)
            wg_ref[ps, :, c * LANES:(c + 1) * LANES] = sel[:, :LANES]
            wu_ref[ps, :, c * LANES:(c + 1) * LANES] = sel[:, LANES:]
        wd_ref[ps] = wdn_ref[...].astype(BF16)

    def expert_mlp(r0, r1):
        n = r1 - r0
        s = items_ref[I_SLOT, i]
        x_lo, x_hi = _unpack_bf16_pair(xs_ref[r0:r1, :])
        x = jnp.concatenate([x_lo.astype(BF16), x_hi.astype(BF16)], axis=1)
        acc = jnp.zeros((n, D_MODEL), F32)
        for c in range(D_MODEL // EXPERT_COL_CHUNK):
            cs = slice(c * EXPERT_COL_CHUNK, (c + 1) * EXPERT_COL_CHUNK)
            gate = jnp.minimum(_dot(x, wg_ref[s, :, cs]) + bg_ref[:, cs], SWIGLU_LIMIT)
            up = jnp.clip(_dot(x, wu_ref[s, :, cs]) + bu_ref[:, cs], -SWIGLU_LIMIT, SWIGLU_LIMIT)
            act = (up + 1.0) * (gate * _sigmoid(SWIGLU_ALPHA * gate))
            acc = acc + _dot(act.astype(BF16), wd_ref[s, cs, :])
        y = acc + bd_ref[...]
        half = D_MODEL // 2
        packed = _pack_bf16_pair(y[:, :half], y[:, half:])
        rows = items_ref[I_TILE, i] * tm + r0 + lax.broadcasted_iota(jnp.int32, (n, 1), 0)
        mine = (rows >= items_ref[I_LO, i]) & (rows < items_ref[I_HI, i])

        @pl.when(items_ref[I_FIRST, i] == 1)
        def _():
            y_ref[r0:r1, :] = packed
            if r0 > 0:
                y_ref[0:r0, :] = jnp.zeros((r0, half), U32)
            if r1 < tm:
                y_ref[r1:tm, :] = jnp.zeros((tm - r1, half), U32)

        @pl.when(items_ref[I_FIRST, i] == 0)
        def _():
            y_ref[r0:r1, :] = jnp.where(mine, packed, y_ref[r0:r1, :])

    for mode, (r0, r1) in ((MODE_FULL, (0, tm)), (MODE_LOWER, (0, tm // 2)), (MODE_UPPER, (tm // 2, tm))):
        pl.when(items_ref[I_MODE, i] == mode)(functools.partial(expert_mlp, r0, r1))


(I_TILE, I_EXPERT, I_LO, I_HI, I_FIRST, I_MODE, I_FETCH, I_PREP, I_PREP_SLOT, I_SLOT) = range(10)
ITEM_FIELDS = 16
(MODE_SKIP, MODE_FULL, MODE_LOWER, MODE_UPPER) = range(4)


def _plan_kernel(tm, n_tiles, cnt_ref, off_ref, items_ref):
    n = items_ref.shape[1]
    cnt_r = cnt_ref[...]
    sub = lax.broadcasted_iota(jnp.int32, (LANES, LANES), 0).astype(F32)
    lan = lax.broadcasted_iota(jnp.int32, (LANES, LANES), 1).astype(F32)
    big = float(4 * LANES)

    def col(row):
        return jnp.sum(jnp.where(lan == sub, row, 0.0), axis=1, keepdims=True)

    def row(column):
        return jnp.sum(jnp.where(lan == sub, column, 0.0), axis=0, keepdims=True)

    def prefix(r):
        return jnp.sum(jnp.where(lan <= sub, r, 0.0), axis=1, keepdims=True)

    cnt = col(cnt_r)
    ends = prefix(cnt_r)
    starts = ends - cnt
    off_ref[...] = row(starts)
    used = cnt > 0.0
    inv_tm = 1.0 / tm
    first_tile = jnp.floor(starts * inv_tm)
    last_tile = jnp.floor(jnp.maximum(ends - 1.0, 0.0) * inv_tm)
    per = jnp.where(used, last_tile - first_tile + 1.0, 0.0)
    item_end = prefix(row(per))
    item_start = item_end - per
    used_r = row(used.astype(F32))
    ordinal = prefix(used_r) - 1.0
    nxt = jnp.min(jnp.where((lan > sub) & (used_r > 0.0), lan, big), axis=1, keepdims=True)
    has_next = nxt < big
    e_col = sub[:, 0:1]
    nxt = jnp.where(has_next, nxt, e_col)
    e_first = jnp.min(jnp.where(used_r > 0.0, lan[0:1, :], big), axis=1, keepdims=True)
    e_last = jnp.max(jnp.where(used_r > 0.0, lan[0:1, :], -1.0), axis=1, keepdims=True)
    total = jnp.max(item_end, axis=0, keepdims=True)

    item = lax.broadcasted_iota(jnp.int32, (LANES, n), 1).astype(F32) - 1.0
    mine = ((item >= item_start) & (item < item_end)).astype(F32)
    pick = lambda column: jnp.sum(mine * column, axis=0, keepdims=True)
    item_r = item[0:1, :]
    valid = pick(jnp.ones_like(cnt))
    lead = item_r < 0.0
    past = item_r >= total
    new_tile = 1.0 - pick(((item == item_start) & (starts - first_tile * tm > 0.0)).astype(F32))

    def put(field, value):
        items_ref[field:field + 1, :] = value.astype(jnp.int32)

    tile = pick(first_tile - item_start) + item_r
    lo = pick(starts)
    hi = pick(ends)
    needs_lower = lo < tile * tm + 0.5 * tm
    needs_upper = hi > tile * tm + 0.5 * tm
    mode = jnp.where(needs_lower & needs_upper, float(MODE_FULL),
                     jnp.where(needs_lower, float(MODE_LOWER), float(MODE_UPPER)))
    put(I_TILE, jnp.where(past, float(n_tiles - 1), jnp.where(lead, 0.0, tile)))
    put(I_EXPERT, jnp.where(past, e_last, jnp.where(lead, e_first, pick(e_col))))
    put(I_LO, lo)
    put(I_HI, hi)
    put(I_FIRST, valid * new_tile)
    put(I_MODE, valid * mode)
    put(I_FETCH, jnp.where(past, e_last, jnp.where(lead, e_first, pick(nxt))))
    put(I_PREP, jnp.where(lead, 1.0, pick(((item == item_end - 1.0) & has_next).astype(F32))))
    put(I_PREP_SLOT, pick(ordinal + 1.0 - 2.0 * jnp.floor((ordinal + 1.0) * 0.5)))
    put(I_SLOT, pick(ordinal - 2.0 * jnp.floor(ordinal * 0.5)))
    for field in range(I_SLOT + 1, ITEM_FIELDS):
        put(field, jnp.zeros_like(valid))


def _plan(counts, n_rows, tm):
    n_tiles = n_rows // tm
    n_items = n_tiles + N_EXPERTS
    width = -(-n_items // LANES) * LANES
    offsets, items = pl.pallas_call(
        functools.partial(_plan_kernel, float(tm), n_tiles),
        out_shape=[jax.ShapeDtypeStruct((1, LANES), F32),
                   jax.ShapeDtypeStruct((ITEM_FIELDS, width), jnp.int32)],
        name="plan",
    )(counts)
    return offsets, items, n_items


def _experts(xs, items, n_items, w_gate_up, bg, bu, w_down, bd):
    n_rows = xs.shape[0]
    tm = EXPERT_TILE
    half = D_MODEL // 2
    j = np.arange(2 * LANES)
    perm = np.zeros((2 * LANES, 2 * LANES), np.float32)
    perm[j, np.where(j % 2 == 0, j // 2, LANES + j // 2)] = 1.0
    row_tile = pl.BlockSpec((tm, half), lambda i, items: (items[I_TILE, i], 0))
    by_expert = lambda *shape: pl.BlockSpec((None,) + shape, lambda i, items: (items[I_EXPERT, i], 0, 0))
    by_fetch = lambda *shape: pl.BlockSpec((None,) + shape, lambda i, items: (items[I_FETCH, i], 0, 0))
    return pl.pallas_call(
        _experts_kernel,
        grid_spec=pltpu.PrefetchScalarGridSpec(
            num_scalar_prefetch=1,
            grid=(n_items,),
            in_specs=[row_tile, by_fetch(D_MODEL, 2 * D_MODEL), by_expert(1, D_MODEL),
                      by_expert(1, D_MODEL), by_fetch(D_MODEL, D_MODEL), by_expert(1, D_MODEL),
                      pl.BlockSpec((2 * LANES, 2 * LANES), lambda i, items: (0, 0))],
            out_specs=row_tile,
            scratch_shapes=[pltpu.VMEM((2, D_MODEL, D_MODEL), BF16)] * 3,
        ),
        out_shape=jax.ShapeDtypeStruct((n_rows, half), U32),
        compiler_params=_params(("arbitrary",)),
        name="experts",
    )(items, xs, w_gate_up, bg, bu, w_down, bd, jnp.asarray(perm, dtype=BF16))


def _tail_kernel(h1_ref, yu_ref, gates_ref, p_ref, nple_ref, wg_ref, wp_ref, nfin_ref, *rest):
    out_ref = rest[-1]
    gates = gates_ref[...]
    lo = None
    hi = None
    for k in range(TOP_K):
        gk = gates[:, k:k + 1]
        yl, yh = _unpack_bf16_pair(yu_ref[k])
        lo = gk * yl if lo is None else lo + gk * yl
        hi = gk * yh if hi is None else hi + gk * yh
    h2 = h1_ref[...] + jnp.concatenate([lo, hi], axis=1)
    hn = _rms(h2, nple_ref[...]).astype(BF16)
    gate = _sigmoid(_dot(hn, wg_ref[...]))
    h3 = h2 + gate * _dot(p_ref[...].astype(BF16), wp_ref[...])
    out_ref[...] = _rms(h3, nfin_ref[...])


def _tail(h1, yu, gates, p2, norm_ple_w, ple_gate_bf, ple_proj_bf, final_norm_w, tile0, result):
    t = h1.shape[0]
    tm = TOKEN_TILE
    half = D_MODEL // 2
    tile = lambda width: pl.BlockSpec((tm, width), lambda i: (i, 0))
    shifted = lambda width: pl.BlockSpec((tm, width), lambda i: (i + tile0, 0))
    const2 = lambda shape: pl.BlockSpec(shape, lambda i: (0, 0))
    in_specs = [tile(D_MODEL), pl.BlockSpec((TOP_K, tm, half), lambda i: (0, i, 0)),
                tile(LANES), shifted(PLE_DIM), const2((1, D_MODEL)),
                const2((D_MODEL, D_MODEL)), const2((PLE_DIM, D_MODEL)), const2((1, D_MODEL))]
    args = [h1, yu, gates, p2, norm_ple_w, ple_gate_bf, ple_proj_bf, final_norm_w]
    aliases = {}
    if result is not None:
        in_specs.append(pl.BlockSpec(memory_space=pl.ANY))
        args.append(result)
        aliases = {len(args) - 1: 0}
    return pl.pallas_call(
        _tail_kernel,
        grid=(t // tm,),
        in_specs=in_specs,
        out_specs=shifted(D_MODEL),
        out_shape=jax.ShapeDtypeStruct((p2.shape[0], D_MODEL), F32),
        input_output_aliases=aliases,
        compiler_params=_params(("parallel",)),
        name="tail",
    )(*args)


def kernel(x, p, positions, w_in, w_out, ret_gn_w, pool_w, pool_scale, norm_mix_w, norm_moe_w, router_w, router_b, expert_w_gate_up, expert_b_gate_up, expert_w_down, expert_b_down, norm_ple_w, ple_gate_w, ple_proj_w, final_norm_w):
    batch, seq, d = x.shape
    depth = w_in.shape[0]
    assert depth == 1 and d == D_MODEL and seq % TOKEN_TILE == 0
    assert seq % (RET_CHUNK * RET_STEP_CHUNKS) == 0
    groups = BATCH_GROUPS if batch % BATCH_GROUPS == 0 else 1
    t = batch * seq
    tg = t // groups
    n_rows = tg * TOP_K
    assert n_rows % EXPERT_TILE == 0
    row = lambda a: a.reshape(1, -1).astype(F32)
    l = 0

    x2 = x.reshape(t, d)
    pos2 = positions.reshape(t, 1)
    p2 = p[l].reshape(t, PLE_DIM)
    w_in_bf = w_in[l].astype(BF16)
    rw = jnp.pad(router_w[l].astype(F32), ((0, 0), (0, LANES - N_EXPERTS)))
    rw_hi = rw.astype(BF16)
    rw_lo = (rw - rw_hi.astype(F32)).astype(BF16)
    rw_split = jnp.concatenate([rw_hi, rw_lo], axis=1)
    rb = jnp.pad(router_b[l].astype(F32), (0, LANES - N_EXPERTS), constant_values=NEG_BIG).reshape(1, LANES)
    bgu = expert_b_gate_up[l].reshape(N_EXPERTS, 1, D_MODEL, 2).astype(F32)
    bdn = expert_b_down[l].reshape(N_EXPERTS, 1, D_MODEL).astype(F32)
    pool_w_bf, w_out_bf = pool_w[l].astype(BF16), w_out[l].astype(BF16)
    ple_gate_bf, ple_proj_bf = ple_gate_w[l].astype(BF16), ple_proj_w[l].astype(BF16)

    out = None
    for gi in range(groups):
        tile0 = gi * (tg // TOKEN_TILE)
        proj = _in_proj(x2, pos2, row(norm_mix_w[l]), w_in_bf, tile0, tg)
        ret = _retention(proj, row(ret_gn_w[l]), batch // groups, seq)
        h1, hn_packed, code, gates, counts = _mix_router(
            x2, ret, proj, pool_w_bf, row(pool_scale[l]), w_out_bf, row(norm_moe_w[l]), rw_split, rb,
            seq, tile0)

        offsets, items, n_items = _plan(counts, n_rows, EXPERT_TILE)
        pos = _positions(code, offsets)
        pos_km = pos[:, :TOP_K].T

        xs = _sc_scatter(hn_packed, pos_km)
        y = _experts(xs, items, n_items, expert_w_gate_up[l], bgu[..., 0], bgu[..., 1],
                     expert_w_down[l], bdn)
        yu = _sc_gather(y, pos_km.reshape(n_rows)).reshape(TOP_K, tg, d // 2)

        out = _tail(h1, yu, gates, p2, row(norm_ple_w[l]), ple_gate_bf, ple_proj_bf,
                    row(final_norm_w), tile0, out)
    return out.reshape(batch, seq, d)
```

```python
import functools
import math

import numpy as np
import jax
import jax.numpy as jnp
from jax import lax
from jax.experimental import pallas as pl
from jax.experimental.pallas import tpu as pltpu
from jax.experimental.pallas import tpu_sc as plsc

D_MODEL = 1024
D_RET = 512
D_POOL = 512
RET_HEADS = 8
RET_HEAD_DIM = 64
HEAD_PAIRS = RET_HEADS // 2
ROPE_BASE = 10000.0
POOL_WINDOWS = (2, 4, 8, 16)
POOL_GROUP_DIM = 128
POOL_HISTORY = 16
POOL_BLOCK = 128
D_IN_PROJ = 4 * D_RET + D_POOL
N_EXPERTS = 32
TOP_K = 4
SWIGLU_LIMIT = 7.0
SWIGLU_ALPHA = 1.702
PLE_DIM = 256
NORM_EPS = 1e-5
GN_EPS = 1e-5

LANES = 128
NEG_BIG = -1e30

TOKEN_TILE = 512
RET_CHUNK = 256
RET_STEP_CHUNKS = 4
EXPERT_TILE = 512
EXPERT_COL_CHUNK = 512
SC_WINDOW = 128
BATCH_GROUPS = 2
VMEM_LIMIT = 56 * 1024 * 1024

F32 = jnp.float32
BF16 = jnp.bfloat16
U32 = jnp.uint32


def _params(semantics):
    return pltpu.CompilerParams(dimension_semantics=semantics, vmem_limit_bytes=VMEM_LIMIT)


def _dot(a, b):
    return jnp.dot(a, b, preferred_element_type=F32)


def _dot_nt(a, b):
    return lax.dot_general(a, b, (((1,), (1,)), ((), ())), preferred_element_type=F32)


def _dot_tn(a, b):
    return lax.dot_general(a, b, (((0,), (0,)), ((), ())), preferred_element_type=F32)


def _rms(x, w):
    ms = jnp.mean(x * x, axis=-1, keepdims=True)
    return x * lax.rsqrt(ms + NORM_EPS) * w


def _sigmoid(z):
    return 1.0 / (1.0 + jnp.exp(-z))


def _pack_bf16_pair(lo, hi):
    lo_bits = pltpu.bitcast(lo.astype(BF16).astype(F32), U32) >> 16
    hi_bits = pltpu.bitcast(hi.astype(BF16).astype(F32), U32) & jnp.uint32(0xFFFF0000)
    return lo_bits | hi_bits


def _unpack_bf16_pair(packed):
    lo = pltpu.bitcast(packed << 16, F32)
    hi = pltpu.bitcast(packed & jnp.uint32(0xFFFF0000), F32)
    return lo, hi


def _in_proj_kernel(x_ref, pos_ref, nw_ref, w_ref, freq_ref, out_ref):
    hn = _rms(x_ref[...], nw_ref[...]).astype(BF16)
    ang = pos_ref[...].astype(F32) * freq_ref[0:1, :]
    cos = jnp.cos(ang)
    sin = jnp.sin(ang)
    slab = 2 * LANES
    cos_t = jnp.concatenate([cos, cos], axis=1)
    sin_up = jnp.concatenate([sin * freq_ref[1:2, :]] * 2, axis=1)
    sin_dn = jnp.concatenate([sin * freq_ref[2:3, :]] * 2, axis=1)
    half = RET_HEAD_DIM // 2

    for s in range(2 * D_RET // slab):
        sl = slice(s * slab, (s + 1) * slab)
        v = _dot(hn, w_ref[:, sl])
        v = v * cos_t + pltpu.roll(v, slab - half, 1) * sin_up + pltpu.roll(v, half, 1) * sin_dn
        if s >= D_RET // slab:
            v = v * (RET_HEAD_DIM ** -0.5)
        out_ref[:, sl] = v.astype(BF16)
    for s in range(2 * D_RET // 512, D_IN_PROJ // 512):
        sl = slice(s * 512, (s + 1) * 512)
        out_ref[:, sl] = _dot(hn, w_ref[:, sl]).astype(BF16)


def _in_proj(x2, pos2, norm_w, w_in_bf, tile0, t):
    tm = TOKEN_TILE
    j = np.arange(LANES)
    half = RET_HEAD_DIM // 2
    inv_freq = ROPE_BASE ** (-(np.arange(half, dtype=np.float32)) / half)
    freq = np.zeros((8, LANES), np.float32)
    freq[0] = inv_freq[j % half]
    freq[1] = np.where(j % RET_HEAD_DIM < half, -1.0, 0.0)
    freq[2] = np.where(j % RET_HEAD_DIM >= half, 1.0, 0.0)
    return pl.pallas_call(
        _in_proj_kernel,
        grid=(t // tm,),
        in_specs=[
            pl.BlockSpec((tm, D_MODEL), lambda i: (i + tile0, 0)),
            pl.BlockSpec((tm, 1), lambda i: (i + tile0, 0)),
            pl.BlockSpec((1, D_MODEL), lambda i: (0, 0)),
            pl.BlockSpec((D_MODEL, D_IN_PROJ), lambda i: (0, 0)),
            pl.BlockSpec((8, LANES), lambda i: (0, 0)),
        ],
        out_specs=pl.BlockSpec((tm, D_IN_PROJ), lambda i: (i, 0)),
        out_shape=jax.ShapeDtypeStruct((t, D_IN_PROJ), BF16),
        compiler_params=_params(("parallel",)),
        name="in_proj",
    )(x2, pos2, norm_w, w_in_bf, jnp.asarray(freq))


def _retention_kernel(q_ref, k_ref, v_ref, g_ref, dec_ref, xi_ref, zeta_ref, cd_ref, bd_ref,
                      m64_ref, eye_ref, gnw_ref, out_ref, state_ref):
    @pl.when(pl.program_id(1) == 0)
    def _():
        state_ref[...] = jnp.zeros_like(state_ref)

    c = dec_ref.shape[1]
    lane = lax.broadcasted_iota(jnp.int32, (1, LANES), 1)
    m64 = m64_ref[...]
    bd = bd_ref[...]
    eye = eye_ref[...]

    def group_mean(v):
        hi = v.astype(BF16)
        lo = (v - hi.astype(F32)).astype(BF16)
        return _dot(jnp.concatenate([hi, lo], axis=1), m64)

    for ci in range(q_ref.shape[0] // c):
        rows = slice(ci * c, (ci + 1) * c)
        for p in range(HEAD_PAIRS):
            sl = slice(p * LANES, (p + 1) * LANES)
            qp = q_ref[rows, sl]
            kp = k_ref[rows, sl]
            vp = v_ref[rows, sl]
            y = None
            for hh in range(2):
                in_head = (lane >= RET_HEAD_DIM) == bool(hh)
                qm = jnp.where(in_head, qp, jnp.zeros_like(qp))
                vm = jnp.where(in_head, vp, jnp.zeros_like(vp))
                scores = _dot_nt(qm, kp) * dec_ref[2 * p + hh]
                part = _dot(scores.astype(BF16), vm)
                y = part if y is None else y + part
            st = state_ref[p]
            y = y + _dot((qp.astype(F32) * xi_ref[p]).astype(BF16), st.astype(BF16))
            kz = (kp.astype(F32) * zeta_ref[p]).astype(BF16)
            kz_t = _dot_nt(eye, kz).astype(BF16)
            state_ref[p] = cd_ref[p] * st + _dot(kz_t, vp) * bd
            mu = group_mean(y)
            var = group_mean(y * y) - mu * mu
            yn = (y - mu) * lax.rsqrt(var + GN_EPS) * gnw_ref[:, sl]
            g = g_ref[rows, sl].astype(F32)
            out_ref[rows, sl] = (yn * g * _sigmoid(g)).astype(BF16)


def _retention_tables(c):
    h = np.arange(RET_HEADS, dtype=np.float64)
    log_gamma = np.log1p(-np.power(2.0, -5.0 - h))
    idx = np.arange(c, dtype=np.float64)
    rel = idx[:, None] - idx[None, :]
    dec = np.where(rel >= 0, np.exp(np.where(rel >= 0, rel, 0.0)[None] * log_gamma[:, None, None]), 0.0)
    lane_head = np.arange(LANES) // RET_HEAD_DIM
    xi = np.zeros((HEAD_PAIRS, c, LANES))
    zeta = np.zeros((HEAD_PAIRS, c, LANES))
    cd = np.zeros((HEAD_PAIRS, LANES, LANES))
    same = lane_head[:, None] == lane_head[None, :]
    for p in range(HEAD_PAIRS):
        lg = log_gamma[2 * p + lane_head]
        xi[p] = np.exp((idx + 1.0)[:, None] * lg[None, :])
        zeta[p] = np.exp((c - 1 - idx)[:, None] * lg[None, :])
        cd[p] = np.where(same, np.exp(c * lg)[:, None], 0.0)
    bd = same.astype(np.float32)
    m64 = np.concatenate([same, same], axis=0).astype(np.float32) / RET_HEAD_DIM
    f = lambda a: jnp.asarray(a, dtype=F32)
    return (f(dec), f(xi), f(zeta), f(cd), f(bd), jnp.asarray(m64, dtype=BF16),
            jnp.asarray(np.eye(LANES, dtype=np.float32), dtype=BF16))


def _retention(proj, gn_w, batch, seq):
    t = proj.shape[0]
    c = RET_CHUNK
    rows = RET_STEP_CHUNKS * c
    n = seq // rows
    dec, xi, zeta, cd, bd, m64, eye = _retention_tables(c)
    col = lambda j: pl.BlockSpec((rows, D_RET), lambda b, i, j=j: (b * n + i, j))
    const3 = lambda shape: pl.BlockSpec(shape, lambda b, i: (0, 0, 0))
    const2 = lambda shape: pl.BlockSpec(shape, lambda b, i: (0, 0))
    return pl.pallas_call(
        _retention_kernel,
        grid=(batch, n),
        in_specs=[col(0), col(1), col(2), col(3),
                  const3((RET_HEADS, c, c)), const3((HEAD_PAIRS, c, LANES)),
                  const3((HEAD_PAIRS, c, LANES)), const3((HEAD_PAIRS, LANES, LANES)),
                  const2((LANES, LANES)), const2((2 * LANES, LANES)), const2((LANES, LANES)),
                  const2((1, D_RET))],
        out_specs=pl.BlockSpec((rows, D_RET), lambda b, i: (b * n + i, 0)),
        out_shape=jax.ShapeDtypeStruct((t, D_RET), BF16),
        scratch_shapes=[pltpu.VMEM((HEAD_PAIRS, LANES, LANES), F32)],
        compiler_params=_params(("arbitrary", "arbitrary")),
        name="retention",
    )(proj, proj, proj, proj, dec, xi, zeta, cd, bd, m64, eye, gn_w)


def _mix_router_kernel(seq, x_ref, ret_ref, u_ref, uprev_ref, band_ref, pw_ref, ps_ref, wout_ref,
                       nw_ref, rw_ref, rb_ref,
                       h1_ref, hnp_ref, code_ref, gates_ref, cnt_ref, uext_ref):
    i = pl.program_id(0)
    tm = x_ref.shape[0]
    t0 = lax.rem(i * tm, seq)

    @pl.when(i == 0)
    def _():
        cnt_ref[...] = jnp.zeros_like(cnt_ref)

    prev = uprev_ref[...]
    uext_ref[0:POOL_HISTORY, :] = jnp.where(t0 == 0, jnp.zeros_like(prev), prev)
    uext_ref[POOL_HISTORY:, :] = u_ref[...]

    row = lax.broadcasted_iota(jnp.int32, (tm, 1), 0)
    t_seq = (t0 + row + 1).astype(F32)
    mixed = []
    for gi, w in enumerate(POOL_WINDOWS):
        sl = slice(gi * POOL_GROUP_DIM, (gi + 1) * POOL_GROUP_DIM)
        wsum = jnp.concatenate(
            [_dot(band_ref[gi], uext_ref[r0:r0 + POOL_BLOCK + POOL_HISTORY, sl])
             for r0 in range(0, tm, POOL_BLOCK)], axis=0)
        count = jnp.minimum(t_seq, float(w))
        pooled = wsum / count - u_ref[:, sl].astype(F32)
        mixed.append(_dot(pooled.astype(BF16), pw_ref[gi]))
    pool = (jnp.concatenate(mixed, axis=1) * ps_ref[...]).astype(BF16)

    h1 = (x_ref[...] + _dot(ret_ref[...], wout_ref[0:D_RET, :])
          + _dot(pool, wout_ref[D_RET:, :]))
    h1_ref[...] = h1
    hn = _rms(h1, nw_ref[...])
    hn_hi = hn.astype(BF16)
    half = D_MODEL // 2
    hnp_ref[...] = _pack_bf16_pair(hn[:, :half], hn[:, half:])

    hn_lo = (hn - hn_hi.astype(F32)).astype(BF16)
    both = _dot(hn_hi, rw_ref[...])
    logits = (both[:, :LANES] + both[:, LANES:] + _dot(hn_lo, rw_ref[:, :LANES])
              + rb_ref[...])

    lane = lax.broadcasted_iota(jnp.int32, (tm, LANES), 1).astype(F32)
    code = jnp.zeros((tm, LANES), F32)
    vals = []
    work = logits
    for k in range(TOP_K):
        m = jnp.max(work, axis=-1, keepdims=True)
        idx = jnp.min(jnp.where(work == m, lane, float(LANES)), axis=-1, keepdims=True)
        chosen = lane == idx
        code = jnp.where(chosen, float(k + 1), code)
        work = jnp.where(chosen, -jnp.inf, work)
        vals.append(m)
    exps = [jnp.exp(v - vals[0]) for v in vals]
    denom = exps[0] + exps[1] + exps[2] + exps[3]
    gates = jnp.zeros((tm, LANES), F32)
    for k in range(TOP_K):
        gates = jnp.where(lane == float(k), exps[k] / denom, gates)
    code_ref[...] = code
    gates_ref[...] = gates
    cnt_ref[...] += jnp.sum((code > 0).astype(F32), axis=0, keepdims=True)


def _pool_bands():
    r = np.arange(POOL_BLOCK)[:, None]
    s = np.arange(POOL_BLOCK + POOL_HISTORY)[None, :] - POOL_HISTORY
    bands = [((s <= r) & (s > r - w)) for w in POOL_WINDOWS]
    return jnp.asarray(np.stack(bands).astype(np.float32), dtype=BF16)


def _mix_router(x2, ret, proj, pool_w_bf, pool_scale, w_out_bf, norm_w, rw_split, rb, seq, tile0):
    t = ret.shape[0]
    tm = TOKEN_TILE
    hist_blocks = tm // POOL_HISTORY
    u_col = (4 * D_RET) // D_POOL
    tile = lambda width: pl.BlockSpec((tm, width), lambda i: (i, 0))
    const2 = lambda shape: pl.BlockSpec(shape, lambda i: (0, 0))
    const3 = lambda shape: pl.BlockSpec(shape, lambda i: (0, 0, 0))
    return pl.pallas_call(
        functools.partial(_mix_router_kernel, seq),
        grid=(t // tm,),
        in_specs=[
            pl.BlockSpec((tm, D_MODEL), lambda i: (i + tile0, 0)), tile(D_RET),
            pl.BlockSpec((tm, D_POOL), lambda i: (i, u_col)),
            pl.BlockSpec((POOL_HISTORY, D_POOL), lambda i: (jnp.maximum(i * hist_blocks - 1, 0), u_col)),
            const3((len(POOL_WINDOWS), POOL_BLOCK, POOL_BLOCK + POOL_HISTORY)),
            const3((len(POOL_WINDOWS), POOL_GROUP_DIM, POOL_GROUP_DIM)),
            const2((1, D_POOL)), const2((D_MODEL, D_MODEL)), const2((1, D_MODEL)),
            const2((D_MODEL, 2 * LANES)), const2((1, LANES)),
        ],
        out_specs=[tile(D_MODEL), tile(D_MODEL // 2), tile(LANES), tile(LANES), const2((1, LANES))],
        out_shape=[
            jax.ShapeDtypeStruct((t, D_MODEL), F32),
            jax.ShapeDtypeStruct((t, D_MODEL // 2), U32),
            jax.ShapeDtypeStruct((t, LANES), F32),
            jax.ShapeDtypeStruct((t, LANES), F32),
            jax.ShapeDtypeStruct((1, LANES), F32),
        ],
        scratch_shapes=[pltpu.VMEM((tm + POOL_HISTORY, D_POOL), BF16)],
        compiler_params=_params(("arbitrary",)),
        name="mix_router",
    )(x2, ret, proj, proj, _pool_bands(), pool_w_bf, pool_scale, w_out_bf, norm_w, rw_split, rb)


def _positions_kernel(code_ref, off_ref, tri_ref, pos_ref, carry_ref):
    @pl.when(pl.program_id(0) == 0)
    def _():
        carry_ref[...] = jnp.zeros_like(carry_ref)

    code = code_ref[...]
    tm = code.shape[0]
    sel = (code > 0).astype(BF16)
    carry = carry_ref[...]
    rank = _dot(tri_ref[...], sel) + (carry + off_ref[...])
    carry_ref[...] = carry + jnp.sum(sel.astype(F32), axis=0, keepdims=True)
    lane = lax.broadcasted_iota(jnp.int32, (tm, LANES), 1)
    pos = jnp.zeros((tm, LANES), F32)
    for k in range(TOP_K):
        pk = jnp.sum(jnp.where(code == float(k + 1), rank, 0.0), axis=-1, keepdims=True)
        pos = jnp.where(lane == k, pk, pos)
    pos_ref[...] = pos.astype(jnp.int32)


def _positions(code, offsets):
    t = code.shape[0]
    tm = TOKEN_TILE
    tri = jnp.asarray(np.tril(np.ones((tm, tm), np.float32), -1), dtype=BF16)
    return pl.pallas_call(
        _positions_kernel,
        grid=(t // tm,),
        in_specs=[pl.BlockSpec((tm, LANES), lambda i: (i, 0)),
                  pl.BlockSpec((1, LANES), lambda i: (0, 0)),
                  pl.BlockSpec((tm, tm), lambda i: (0, 0))],
        out_specs=pl.BlockSpec((tm, LANES), lambda i: (i, 0)),
        out_shape=jax.ShapeDtypeStruct((t, LANES), jnp.int32),
        scratch_shapes=[pltpu.VMEM((1, LANES), F32)],
        compiler_params=_params(("arbitrary",)),
        name="positions",
    )(code, offsets, tri)


def _sc_mesh():
    return plsc.VectorSubcoreMesh(core_axis_name="core", subcore_axis_name="subcore")


def _sc_workers():
    info = plsc.get_sparse_core_info()
    return info.num_cores, info.num_cores * info.num_subcores


def _sc_gather(src, idx):
    n = idx.shape[0]
    d = src.shape[1]
    w = SC_WINDOW // 2
    num_cores, workers = _sc_workers()
    per = n // workers
    pairs = per // (2 * w)
    assert per * workers == n and pairs * 2 * w == per

    @functools.partial(
        pl.kernel, out_type=jax.ShapeDtypeStruct((n, d), src.dtype), mesh=_sc_mesh(),
        scratch_types=[pltpu.VMEM((w,), jnp.int32), pltpu.VMEM((w,), jnp.int32),
                       pltpu.VMEM((w, d), src.dtype), pltpu.VMEM((w, d), src.dtype),
                       pltpu.SemaphoreType.DMA, pltpu.SemaphoreType.DMA,
                       pltpu.SemaphoreType.DMA, pltpu.SemaphoreType.DMA],
        name="sc_gather")
    def gather(src_hbm, idx_hbm, out_hbm, idx0, idx1, rows0, rows1, g0, g1, w0, w1):
        first = (lax.axis_index("subcore") * num_cores + lax.axis_index("core")) * per

        def start_gather(win, idx_v, rows_v, sem):
            pltpu.sync_copy(idx_hbm.at[pl.ds(first + win * w, w)], idx_v)
            pltpu.async_copy(src_hbm.at[idx_v], rows_v, sem)

        def wait_gather(idx_v, rows_v, sem):
            pltpu.make_async_copy(src_hbm.at[idx_v], rows_v, sem).wait()

        def start_write(win, rows_v, sem):
            pltpu.async_copy(rows_v, out_hbm.at[pl.ds(first + win * w, w)], sem)

        def wait_write(rows_v, sem):
            pltpu.make_async_copy(rows_v, out_hbm.at[pl.ds(first, w)], sem).wait()

        start_gather(0, idx0, rows0, g0)

        @pl.loop(0, pairs)
        def _(j):
            even = 2 * j

            @pl.when(j > 0)
            def _():
                wait_write(rows1, w1)

            start_gather(even + 1, idx1, rows1, g1)
            wait_gather(idx0, rows0, g0)
            start_write(even, rows0, w0)
            wait_write(rows0, w0)

            @pl.when(j + 1 < pairs)
            def _():
                start_gather(even + 2, idx0, rows0, g0)

            wait_gather(idx1, rows1, g1)
            start_write(even + 1, rows1, w1)

        wait_write(rows1, w1)

    return gather(src, idx)


def _sc_scatter(src, idx):
    t, d = src.shape
    w = SC_WINDOW
    num_cores, workers = _sc_workers()
    per = t // workers
    assert idx.shape == (TOP_K, t) and per * workers == t and per % w == 0

    @functools.partial(
        pl.kernel, out_type=jax.ShapeDtypeStruct((TOP_K * t, d), src.dtype), mesh=_sc_mesh(),
        scratch_types=[pltpu.VMEM((TOP_K, w), jnp.int32), pltpu.VMEM((w, d), src.dtype),
                       pltpu.SemaphoreType.DMA],
        name="sc_scatter")
    def scatter(src_hbm, idx_hbm, out_hbm, idx_v, rows_v, sem):
        first = (lax.axis_index("subcore") * num_cores + lax.axis_index("core")) * per

        @pl.loop(0, per // w)
        def _(j):
            base = first + j * w
            for k in range(TOP_K):
                pltpu.sync_copy(idx_hbm.at[k, pl.ds(base, w)], idx_v.at[k])
            pltpu.sync_copy(src_hbm.at[pl.ds(base, w)], rows_v)
            copies = [pltpu.async_copy(rows_v, out_hbm.at[idx_v.at[k]], sem) for k in range(TOP_K)]
            for c in copies:
                c.wait()

    return scatter(src, idx)


def _experts_kernel(items_ref, xs_ref, wgu_hbm, bg_ref, bu_ref, wdn_hbm, bd_ref, perm_ref, y_ref,
                    wgu_ref, wdn_ref, wg_ref, wu_ref, wd_ref, sems):
    i = pl.program_id(0)
    tm = xs_ref.shape[0]

    def weight_copies():
        e = items_ref[I_FETCH, i]
        return (pltpu.make_async_copy(wgu_hbm.at[e], wgu_ref, sems.at[0]),
                pltpu.make_async_copy(wdn_hbm.at[e], wdn_ref, sems.at[1]))

    @pl.when(items_ref[I_START, i] == 1)
    def _():
        for copy in weight_copies():
            copy.start()

    @pl.when(items_ref[I_PREP, i] == 1)
    def _():
        for copy in weight_copies():
            copy.wait()
        ps = items_ref[I_PREP_SLOT, i]
        perm = perm_ref[...]
        pair = 2 * LANES
        for c in range(2 * D_MODEL // pair):
            sel = _dot(wgu_ref[:, c * pair:(c + 1) * pair].astype(BF16), perm).astype(BF16)
            wg_ref[ps, :, c * LANES:(c + 1) * LANES] = sel[:, :LANES]
            wu_ref[ps, :, c * LANES:(c + 1) * LANES] = sel[:, LANES:]
        wd_ref[ps] = wdn_ref[...].astype(BF16)

    def expert_mlp(r0, r1):
        n = r1 - r0
        s = items_ref[I_SLOT, i]
        x_lo, x_hi = _unpack_bf16_pair(xs_ref[r0:r1, :])
        x = jnp.concatenate([x_lo.astype(BF16), x_hi.astype(BF16)], axis=1)
        acc = jnp.zeros((n, D_MODEL), F32)
        for c in range(D_MODEL // EXPERT_COL_CHUNK):
            cs = slice(c * EXPERT_COL_CHUNK, (c + 1) * EXPERT_COL_CHUNK)
            gate = jnp.minimum(_dot(x, wg_ref[s, :, cs]) + bg_ref[:, cs], SWIGLU_LIMIT)
            up = jnp.clip(_dot(x, wu_ref[s, :, cs]) + bu_ref[:, cs], -SWIGLU_LIMIT, SWIGLU_LIMIT)
            act = (up + 1.0) * (gate * _sigmoid(SWIGLU_ALPHA * gate))
            acc = acc + _dot(act.astype(BF16), wd_ref[s, cs, :])
        y = acc + bd_ref[...]
        half = D_MODEL // 2
        packed = _pack_bf16_pair(y[:, :half], y[:, half:])
        rows = items_ref[I_TILE, i] * tm + r0 + lax.broadcasted_iota(jnp.int32, (n, 1), 0)
        mine = (rows >= items_ref[I_LO, i]) & (rows < items_ref[I_HI, i])

        @pl.when(items_ref[I_FIRST, i] == 1)
        def _():
            y_ref[r0:r1, :] = packed
            if r0 > 0:
                y_ref[0:r0, :] = jnp.zeros((r0, half), U32)
            if r1 < tm:
                y_ref[r1:tm, :] = jnp.zeros((tm - r1, half), U32)

        @pl.when(items_ref[I_FIRST, i] == 0)
        def _():
            y_ref[r0:r1, :] = jnp.where(mine, packed, y_ref[r0:r1, :])

    for mode, (r0, r1) in ((MODE_FULL, (0, tm)), (MODE_LOWER, (0, tm // 2)), (MODE_UPPER, (tm // 2, tm))):
        pl.when(items_ref[I_MODE, i] == mode)(functools.partial(expert_mlp, r0, r1))


(I_TILE, I_EXPERT, I_LO, I_HI, I_FIRST, I_MODE, I_FETCH, I_START, I_PREP, I_PREP_SLOT, I_SLOT) = range(11)
ITEM_FIELDS = 16
(MODE_SKIP, MODE_FULL, MODE_LOWER, MODE_UPPER) = range(4)


def _plan_kernel(tm, n_tiles, cnt_ref, off_ref, items_ref):
    n = items_ref.shape[1]
    cnt_r = cnt_ref[...]
    sub = lax.broadcasted_iota(jnp.int32, (LANES, LANES), 0).astype(F32)
    lan = lax.broadcasted_iota(jnp.int32, (LANES, LANES), 1).astype(F32)
    big = float(4 * LANES)

    def col(row):
        return jnp.sum(jnp.where(lan == sub, row, 0.0), axis=1, keepdims=True)

    def row(column):
        return jnp.sum(jnp.where(lan == sub, column, 0.0), axis=0, keepdims=True)

    def prefix(r):
        return jnp.sum(jnp.where(lan <= sub, r, 0.0), axis=1, keepdims=True)

    cnt = col(cnt_r)
    ends = prefix(cnt_r)
    starts = ends - cnt
    off_ref[...] = row(starts)
    used = cnt > 0.0
    inv_tm = 1.0 / tm
    first_tile = jnp.floor(starts * inv_tm)
    last_tile = jnp.floor(jnp.maximum(ends - 1.0, 0.0) * inv_tm)
    per = jnp.where(used, last_tile - first_tile + 1.0, 0.0)
    item_end = prefix(row(per))
    item_start = item_end - per
    used_r = row(used.astype(F32))
    ordinal = prefix(used_r) - 1.0
    nxt = jnp.min(jnp.where((lan > sub) & (used_r > 0.0), lan, big), axis=1, keepdims=True)
    has_next = nxt < big
    e_col = sub[:, 0:1]
    nxt = jnp.where(has_next, nxt, e_col)
    e_first = jnp.min(jnp.where(used_r > 0.0, lan[0:1, :], big), axis=1, keepdims=True)
    e_last = jnp.max(jnp.where(used_r > 0.0, lan[0:1, :], -1.0), axis=1, keepdims=True)
    total = jnp.max(item_end, axis=0, keepdims=True)

    item = lax.broadcasted_iota(jnp.int32, (LANES, n), 1).astype(F32) - 1.0
    mine = ((item >= item_start) & (item < item_end)).astype(F32)
    pick = lambda column: jnp.sum(mine * column, axis=0, keepdims=True)
    item_r = item[0:1, :]
    valid = pick(jnp.ones_like(cnt))
    lead = item_r < 0.0
    past = item_r >= total
    new_tile = 1.0 - pick(((item == item_start) & (starts - first_tile * tm > 0.0)).astype(F32))

    def put(field, value):
        items_ref[field:field + 1, :] = value.astype(jnp.int32)

    tile = pick(first_tile - item_start) + item_r
    lo = pick(starts)
    hi = pick(ends)
    needs_lower = lo < tile * tm + 0.5 * tm
    needs_upper = hi > tile * tm + 0.5 * tm
    mode = jnp.where(needs_lower & needs_upper, float(MODE_FULL),
                     jnp.where(needs_lower, float(MODE_LOWER), float(MODE_UPPER)))
    put(I_TILE, jnp.where(past, float(n_tiles - 1), jnp.where(lead, 0.0, tile)))
    put(I_EXPERT, jnp.where(past, e_last, jnp.where(lead, e_first, pick(e_col))))
    put(I_LO, lo)
    put(I_HI, hi)
    put(I_FIRST, valid * new_tile)
    put(I_MODE, valid * mode)
    put(I_FETCH, jnp.where(past, e_last, jnp.where(lead, e_first, pick(nxt))))
    put(I_START, jnp.where(lead, 1.0, pick(((item == item_start) & has_next).astype(F32))))
    put(I_PREP, jnp.where(lead, 1.0, pick(((item == item_end - 1.0) & has_next).astype(F32))))
    put(I_PREP_SLOT, pick(ordinal + 1.0 - 2.0 * jnp.floor((ordinal + 1.0) * 0.5)))
    put(I_SLOT, pick(ordinal - 2.0 * jnp.floor(ordinal * 0.5)))
    for field in range(I_SLOT + 1, ITEM_FIELDS):
        put(field, jnp.zeros_like(valid))


def _plan(counts, n_rows, tm):
    n_tiles = n_rows // tm
    n_items = n_tiles + N_EXPERTS
    width = -(-n_items // LANES) * LANES
    offsets, items = pl.pallas_call(
        functools.partial(_plan_kernel, float(tm), n_tiles),
        out_shape=[jax.ShapeDtypeStruct((1, LANES), F32),
                   jax.ShapeDtypeStruct((ITEM_FIELDS, width), jnp.int32)],
        name="plan",
    )(counts)
    return offsets, items, n_items


def _experts(xs, items, n_items, w_gate_up, bg, bu, w_down, bd):
    n_rows = xs.shape[0]
    tm = EXPERT_TILE
    half = D_MODEL // 2
    j = np.arange(2 * LANES)
    perm = np.zeros((2 * LANES, 2 * LANES), np.float32)
    perm[j, np.where(j % 2 == 0, j // 2, LANES + j // 2)] = 1.0
    row_tile = pl.BlockSpec((tm, half), lambda i, items: (items[I_TILE, i], 0))
    by_expert = lambda *shape: pl.BlockSpec((None,) + shape, lambda i, items: (items[I_EXPERT, i], 0, 0))
    in_hbm = pl.BlockSpec(memory_space=pl.ANY)
    return pl.pallas_call(
        _experts_kernel,
        grid_spec=pltpu.PrefetchScalarGridSpec(
            num_scalar_prefetch=1,
            grid=(n_items,),
            in_specs=[row_tile, in_hbm, by_expert(1, D_MODEL), by_expert(1, D_MODEL), in_hbm,
                      by_expert(1, D_MODEL),
                      pl.BlockSpec((2 * LANES, 2 * LANES), lambda i, items: (0, 0))],
            out_specs=row_tile,
            scratch_shapes=[pltpu.VMEM((D_MODEL, 2 * D_MODEL), F32), pltpu.VMEM((D_MODEL, D_MODEL), F32)]
            + [pltpu.VMEM((2, D_MODEL, D_MODEL), BF16)] * 3 + [pltpu.SemaphoreType.DMA((2,))],
        ),
        out_shape=jax.ShapeDtypeStruct((n_rows, half), U32),
        compiler_params=_params(("arbitrary",)),
        name="experts",
    )(items, xs, w_gate_up, bg, bu, w_down, bd, jnp.asarray(perm, dtype=BF16))


def _tail_kernel(h1_ref, yu_ref, gates_ref, p_ref, nple_ref, wg_ref, wp_ref, nfin_ref, *rest):
    out_ref = rest[-1]
    gates = gates_ref[...]
    lo = None
    hi = None
    for k in range(TOP_K):
        gk = gates[:, k:k + 1]
        yl, yh = _unpack_bf16_pair(yu_ref[k])
        lo = gk * yl if lo is None else lo + gk * yl
        hi = gk * yh if hi is None else hi + gk * yh
    h2 = h1_ref[...] + jnp.concatenate([lo, hi], axis=1)
    hn = _rms(h2, nple_ref[...]).astype(BF16)
    gate = _sigmoid(_dot(hn, wg_ref[...]))
    h3 = h2 + gate * _dot(p_ref[...].astype(BF16), wp_ref[...])
    out_ref[...] = _rms(h3, nfin_ref[...])


def _tail(h1, yu, gates, p2, norm_ple_w, ple_gate_bf, ple_proj_bf, final_norm_w, tile0, result):
    t = h1.shape[0]
    tm = TOKEN_TILE
    half = D_MODEL // 2
    tile = lambda width: pl.BlockSpec((tm, width), lambda i: (i, 0))
    shifted = lambda width: pl.BlockSpec((tm, width), lambda i: (i + tile0, 0))
    const2 = lambda shape: pl.BlockSpec(shape, lambda i: (0, 0))
    in_specs = [tile(D_MODEL), pl.BlockSpec((TOP_K, tm, half), lambda i: (0, i, 0)),
                tile(LANES), shifted(PLE_DIM), const2((1, D_MODEL)),
                const2((D_MODEL, D_MODEL)), const2((PLE_DIM, D_MODEL)), const2((1, D_MODEL))]
    args = [h1, yu, gates, p2, norm_ple_w, ple_gate_bf, ple_proj_bf, final_norm_w]
    aliases = {}
    if result is not None:
        in_specs.append(pl.BlockSpec(memory_space=pl.ANY))
        args.append(result)
        aliases = {len(args) - 1: 0}
    return pl.pallas_call(
        _tail_kernel,
        grid=(t // tm,),
        in_specs=in_specs,
        out_specs=shifted(D_MODEL),
        out_shape=jax.ShapeDtypeStruct((p2.shape[0], D_MODEL), F32),
        input_output_aliases=aliases,
        compiler_params=_params(("parallel",)),
        name="tail",
    )(*args)


def kernel(x, p, positions, w_in, w_out, ret_gn_w, pool_w, pool_scale, norm_mix_w, norm_moe_w, router_w, router_b, expert_w_gate_up, expert_b_gate_up, expert_w_down, expert_b_down, norm_ple_w, ple_gate_w, ple_proj_w, final_norm_w):
    batch, seq, d = x.shape
    depth = w_in.shape[0]
    assert depth == 1 and d == D_MODEL and seq % TOKEN_TILE == 0
    assert seq % (RET_CHUNK * RET_STEP_CHUNKS) == 0
    groups = BATCH_GROUPS if batch % BATCH_GROUPS == 0 else 1
    t = batch * seq
    tg = t // groups
    n_rows = tg * TOP_K
    assert n_rows % EXPERT_TILE == 0
    row = lambda a: a.reshape(1, -1).astype(F32)
    l = 0

    x2 = x.reshape(t, d)
    pos2 = positions.reshape(t, 1)
    p2 = p[l].reshape(t, PLE_DIM)
    w_in_bf = w_in[l].astype(BF16)
    rw = jnp.pad(router_w[l].astype(F32), ((0, 0), (0, LANES - N_EXPERTS)))
    rw_hi = rw.astype(BF16)
    rw_lo = (rw - rw_hi.astype(F32)).astype(BF16)
    rw_split = jnp.concatenate([rw_hi, rw_lo], axis=1)
    rb = jnp.pad(router_b[l].astype(F32), (0, LANES - N_EXPERTS), constant_values=NEG_BIG).reshape(1, LANES)
    bgu = expert_b_gate_up[l].reshape(N_EXPERTS, 1, D_MODEL, 2).astype(F32)
    bdn = expert_b_down[l].reshape(N_EXPERTS, 1, D_MODEL).astype(F32)
    pool_w_bf, w_out_bf = pool_w[l].astype(BF16), w_out[l].astype(BF16)
    ple_gate_bf, ple_proj_bf = ple_gate_w[l].astype(BF16), ple_proj_w[l].astype(BF16)

    out = None
    for gi in range(groups):
        tile0 = gi * (tg // TOKEN_TILE)
        proj = _in_proj(x2, pos2, row(norm_mix_w[l]), w_in_bf, tile0, tg)
        ret = _retention(proj, row(ret_gn_w[l]), batch // groups, seq)
        h1, hn_packed, code, gates, counts = _mix_router(
            x2, ret, proj, pool_w_bf, row(pool_scale[l]), w_out_bf, row(norm_moe_w[l]), rw_split, rb,
            seq, tile0)

        offsets, items, n_items = _plan(counts, n_rows, EXPERT_TILE)
        pos = _positions(code, offsets)
        pos_km = pos[:, :TOP_K].T

        xs = _sc_scatter(hn_packed, pos_km)
        y = _experts(xs, items, n_items, expert_w_gate_up[l], bgu[..., 0], bgu[..., 1],
                     expert_w_down[l], bdn)
        yu = _sc_gather(y, pos_km.reshape(n_rows)).reshape(TOP_K, tg, d // 2)

        out = _tail(h1, yu, gates, p2, row(norm_ple_w[l]), ple_gate_bf, ple_proj_bf,
                    row(final_norm_w), tile0, out)
    return out.reshape(batch, seq, d)
```

```python
import functools
import math

import numpy as np
import jax
import jax.numpy as jnp
from jax import lax
from jax.experimental import pallas as pl
from jax.experimental.pallas import tpu as pltpu
from jax.experimental.pallas import tpu_sc as plsc

D_MODEL = 1024
D_RET = 512
D_POOL = 512
RET_HEADS = 8
RET_HEAD_DIM = 64
HEAD_PAIRS = RET_HEADS // 2
ROPE_BASE = 10000.0
POOL_WINDOWS = (2, 4, 8, 16)
POOL_GROUP_DIM = 128
POOL_HISTORY = 16
POOL_BLOCK = 128
D_IN_PROJ = 4 * D_RET + D_POOL
N_EXPERTS = 32
TOP_K = 4
SWIGLU_LIMIT = 7.0
SWIGLU_ALPHA = 1.702
PLE_DIM = 256
NORM_EPS = 1e-5
GN_EPS = 1e-5

LANES = 128
NEG_BIG = -1e30

TOKEN_TILE = 512
RET_CHUNK = 256
RET_STEP_CHUNKS = 4
EXPERT_TILE = 512
EXPERT_COL_CHUNK = 512
SC_WINDOW = 128
BATCH_GROUPS = 2
VMEM_LIMIT = 56 * 1024 * 1024

F32 = jnp.float32
BF16 = jnp.bfloat16
U32 = jnp.uint32


def _params(semantics):
    return pltpu.CompilerParams(dimension_semantics=semantics, vmem_limit_bytes=VMEM_LIMIT)


def _dot(a, b):
    return jnp.dot(a, b, preferred_element_type=F32)


def _dot_nt(a, b):
    return lax.dot_general(a, b, (((1,), (1,)), ((), ())), preferred_element_type=F32)


def _dot_tn(a, b):
    return lax.dot_general(a, b, (((0,), (0,)), ((), ())), preferred_element_type=F32)


def _rms(x, w):
    ms = jnp.mean(x * x, axis=-1, keepdims=True)
    return x * lax.rsqrt(ms + NORM_EPS) * w


def _sigmoid(z):
    return 1.0 / (1.0 + jnp.exp(-z))


def _pack_bf16_pair(lo, hi):
    lo_bits = pltpu.bitcast(lo.astype(BF16).astype(F32), U32) >> 16
    hi_bits = pltpu.bitcast(hi.astype(BF16).astype(F32), U32) & jnp.uint32(0xFFFF0000)
    return lo_bits | hi_bits


def _unpack_bf16_pair(packed):
    lo = pltpu.bitcast(packed << 16, F32)
    hi = pltpu.bitcast(packed & jnp.uint32(0xFFFF0000), F32)
    return lo, hi


def _project_rows(x, pos, nw_ref, w_ref, freq_ref, qkvg_ref, u_ref, rows):
    hn = _rms(x, nw_ref[...]).astype(BF16)
    ang = pos.astype(F32) * freq_ref[0:1, :]
    cos = jnp.cos(ang)
    sin = jnp.sin(ang)
    slab = 2 * LANES
    cos_t = jnp.concatenate([cos, cos], axis=1)
    sin_up = jnp.concatenate([sin * freq_ref[1:2, :]] * 2, axis=1)
    sin_dn = jnp.concatenate([sin * freq_ref[2:3, :]] * 2, axis=1)
    half = RET_HEAD_DIM // 2

    for s in range(2 * D_RET // slab):
        sl = slice(s * slab, (s + 1) * slab)
        v = _dot(hn, w_ref[:, sl])
        v = v * cos_t + pltpu.roll(v, slab - half, 1) * sin_up + pltpu.roll(v, half, 1) * sin_dn
        if s >= D_RET // slab:
            v = v * (RET_HEAD_DIM ** -0.5)
        qkvg_ref[rows, sl] = v.astype(BF16)
    for s in range(2 * D_RET // 512, 4 * D_RET // 512):
        sl = slice(s * 512, (s + 1) * 512)
        qkvg_ref[rows, sl] = _dot(hn, w_ref[:, sl]).astype(BF16)
    u_ref[rows, :] = _dot(hn, w_ref[:, 4 * D_RET:]).astype(BF16)


def _rope_table():
    j = np.arange(LANES)
    half = RET_HEAD_DIM // 2
    inv_freq = ROPE_BASE ** (-(np.arange(half, dtype=np.float32)) / half)
    freq = np.zeros((8, LANES), np.float32)
    freq[0] = inv_freq[j % half]
    freq[1] = np.where(j % RET_HEAD_DIM < half, -1.0, 0.0)
    freq[2] = np.where(j % RET_HEAD_DIM >= half, 1.0, 0.0)
    return jnp.asarray(freq)


def _retention_kernel(x_ref, pos_ref, nw_ref, w_ref, freq_ref, dec_ref, xi_ref, zeta_ref, cd_ref, bd_ref,
                      m64_ref, eye_ref, gnw_ref, out_ref, u_ref, state_ref, qkvg_ref):
    @pl.when(pl.program_id(1) == 0)
    def _():
        state_ref[...] = jnp.zeros_like(state_ref)

    c = dec_ref.shape[1]
    n_chunks = x_ref.shape[0] // c
    lane = lax.broadcasted_iota(jnp.int32, (1, LANES), 1)
    m64 = m64_ref[...]
    bd = bd_ref[...]
    eye = eye_ref[...]

    def group_mean(v):
        hi = v.astype(BF16)
        lo = (v - hi.astype(F32)).astype(BF16)
        return _dot(jnp.concatenate([hi, lo], axis=1), m64)

    for ci in range(n_chunks):
        rows = slice(ci * c, (ci + 1) * c)
        _project_rows(x_ref[rows, :], pos_ref[rows, :], nw_ref, w_ref, freq_ref, qkvg_ref, u_ref, rows)
        for p in range(HEAD_PAIRS):
            sl = slice(p * LANES, (p + 1) * LANES)
            qp = qkvg_ref[rows, sl]
            kp = qkvg_ref[rows, D_RET + p * LANES:D_RET + (p + 1) * LANES]
            vp = qkvg_ref[rows, 2 * D_RET + p * LANES:2 * D_RET + (p + 1) * LANES]
            y = None
            for hh in range(2):
                in_head = (lane >= RET_HEAD_DIM) == bool(hh)
                qm = jnp.where(in_head, qp, jnp.zeros_like(qp))
                vm = jnp.where(in_head, vp, jnp.zeros_like(vp))
                scores = _dot_nt(qm, kp) * dec_ref[2 * p + hh]
                part = _dot(scores.astype(BF16), vm)
                y = part if y is None else y + part
            st = state_ref[p]
            y = y + _dot((qp.astype(F32) * xi_ref[p]).astype(BF16), st.astype(BF16))
            kz = (kp.astype(F32) * zeta_ref[p]).astype(BF16)
            kz_t = _dot_nt(eye, kz).astype(BF16)
            state_ref[p] = cd_ref[p] * st + _dot(kz_t, vp) * bd
            mu = group_mean(y)
            var = group_mean(y * y) - mu * mu
            yn = (y - mu) * lax.rsqrt(var + GN_EPS) * gnw_ref[:, sl]
            g = qkvg_ref[rows, 3 * D_RET + p * LANES:3 * D_RET + (p + 1) * LANES].astype(F32)
            out_ref[rows, sl] = (yn * g * _sigmoid(g)).astype(BF16)


def _retention_tables(c):
    h = np.arange(RET_HEADS, dtype=np.float64)
    log_gamma = np.log1p(-np.power(2.0, -5.0 - h))
    idx = np.arange(c, dtype=np.float64)
    rel = idx[:, None] - idx[None, :]
    dec = np.where(rel >= 0, np.exp(np.where(rel >= 0, rel, 0.0)[None] * log_gamma[:, None, None]), 0.0)
    lane_head = np.arange(LANES) // RET_HEAD_DIM
    xi = np.zeros((HEAD_PAIRS, c, LANES))
    zeta = np.zeros((HEAD_PAIRS, c, LANES))
    cd = np.zeros((HEAD_PAIRS, LANES, LANES))
    same = lane_head[:, None] == lane_head[None, :]
    for p in range(HEAD_PAIRS):
        lg = log_gamma[2 * p + lane_head]
        xi[p] = np.exp((idx + 1.0)[:, None] * lg[None, :])
        zeta[p] = np.exp((c - 1 - idx)[:, None] * lg[None, :])
        cd[p] = np.where(same, np.exp(c * lg)[:, None], 0.0)
    bd = same.astype(np.float32)
    m64 = np.concatenate([same, same], axis=0).astype(np.float32) / RET_HEAD_DIM
    f = lambda a: jnp.asarray(a, dtype=F32)
    return (f(dec), f(xi), f(zeta), f(cd), f(bd), jnp.asarray(m64, dtype=BF16),
            jnp.asarray(np.eye(LANES, dtype=np.float32), dtype=BF16))


def _retention(x2, pos2, norm_w, w_in_bf, gn_w, block0, batch, seq):
    c = RET_CHUNK
    rows = RET_STEP_CHUNKS * c
    n = seq // rows
    t = batch * seq
    dec, xi, zeta, cd, bd, m64, eye = _retention_tables(c)
    const3 = lambda shape: pl.BlockSpec(shape, lambda b, i: (0, 0, 0))
    const2 = lambda shape: pl.BlockSpec(shape, lambda b, i: (0, 0))
    out_tile = pl.BlockSpec((rows, D_RET), lambda b, i: (b * n + i, 0))
    return pl.pallas_call(
        _retention_kernel,
        grid=(batch, n),
        in_specs=[pl.BlockSpec((rows, D_MODEL), lambda b, i: (block0 + b * n + i, 0)),
                  pl.BlockSpec((rows, 1), lambda b, i: (block0 + b * n + i, 0)),
                  const2((1, D_MODEL)), const2((D_MODEL, D_IN_PROJ)), const2((8, LANES)),
                  const3((RET_HEADS, c, c)), const3((HEAD_PAIRS, c, LANES)),
                  const3((HEAD_PAIRS, c, LANES)), const3((HEAD_PAIRS, LANES, LANES)),
                  const2((LANES, LANES)), const2((2 * LANES, LANES)), const2((LANES, LANES)),
                  const2((1, D_RET))],
        out_specs=[out_tile, out_tile],
        out_shape=[jax.ShapeDtypeStruct((t, D_RET), BF16), jax.ShapeDtypeStruct((t, D_POOL), BF16)],
        scratch_shapes=[pltpu.VMEM((HEAD_PAIRS, LANES, LANES), F32), pltpu.VMEM((rows, 4 * D_RET), BF16)],
        compiler_params=_params(("arbitrary", "arbitrary")),
        name="retention",
    )(x2, pos2, norm_w, w_in_bf, _rope_table(), dec, xi, zeta, cd, bd, m64, eye, gn_w)


def _mix_router_kernel(seq, x_ref, ret_ref, u_ref, uprev_ref, band_ref, pw_ref, ps_ref, wout_ref,
                       nw_ref, rw_ref, rb_ref,
                       h1_ref, hnp_ref, code_ref, gates_ref, cnt_ref, uext_ref):
    i = pl.program_id(0)
    tm = x_ref.shape[0]
    t0 = lax.rem(i * tm, seq)

    @pl.when(i == 0)
    def _():
        cnt_ref[...] = jnp.zeros_like(cnt_ref)

    prev = uprev_ref[...]
    uext_ref[0:POOL_HISTORY, :] = jnp.where(t0 == 0, jnp.zeros_like(prev), prev)
    uext_ref[POOL_HISTORY:, :] = u_ref[...]

    row = lax.broadcasted_iota(jnp.int32, (tm, 1), 0)
    t_seq = (t0 + row + 1).astype(F32)
    mixed = []
    for gi, w in enumerate(POOL_WINDOWS):
        sl = slice(gi * POOL_GROUP_DIM, (gi + 1) * POOL_GROUP_DIM)
        wsum = jnp.concatenate(
            [_dot(band_ref[gi], uext_ref[r0:r0 + POOL_BLOCK + POOL_HISTORY, sl])
             for r0 in range(0, tm, POOL_BLOCK)], axis=0)
        count = jnp.minimum(t_seq, float(w))
        pooled = wsum / count - u_ref[:, sl].astype(F32)
        mixed.append(_dot(pooled.astype(BF16), pw_ref[gi]))
    pool = (jnp.concatenate(mixed, axis=1) * ps_ref[...]).astype(BF16)

    h1 = (x_ref[...] + _dot(ret_ref[...], wout_ref[0:D_RET, :])
          + _dot(pool, wout_ref[D_RET:, :]))
    h1_ref[...] = h1
    hn = _rms(h1, nw_ref[...])
    hn_hi = hn.astype(BF16)
    half = D_MODEL // 2
    hnp_ref[...] = _pack_bf16_pair(hn[:, :half], hn[:, half:])

    hn_lo = (hn - hn_hi.astype(F32)).astype(BF16)
    both = _dot(hn_hi, rw_ref[...])
    logits = (both[:, :LANES] + both[:, LANES:] + _dot(hn_lo, rw_ref[:, :LANES])
              + rb_ref[...])

    lane = lax.broadcasted_iota(jnp.int32, (tm, LANES), 1).astype(F32)
    code = jnp.zeros((tm, LANES), F32)
    vals = []
    work = logits
    for k in range(TOP_K):
        m = jnp.max(work, axis=-1, keepdims=True)
        idx = jnp.min(jnp.where(work == m, lane, float(LANES)), axis=-1, keepdims=True)
        chosen = lane == idx
        code = jnp.where(chosen, float(k + 1), code)
        work = jnp.where(chosen, -jnp.inf, work)
        vals.append(m)
    exps = [jnp.exp(v - vals[0]) for v in vals]
    denom = exps[0] + exps[1] + exps[2] + exps[3]
    gates = jnp.zeros((tm, LANES), F32)
    for k in range(TOP_K):
        gates = jnp.where(lane == float(k), exps[k] / denom, gates)
    code_ref[...] = code
    gates_ref[...] = gates
    cnt_ref[...] += jnp.sum((code > 0).astype(F32), axis=0, keepdims=True)


def _pool_bands():
    r = np.arange(POOL_BLOCK)[:, None]
    s = np.arange(POOL_BLOCK + POOL_HISTORY)[None, :] - POOL_HISTORY
    bands = [((s <= r) & (s > r - w)) for w in POOL_WINDOWS]
    return jnp.asarray(np.stack(bands).astype(np.float32), dtype=BF16)


def _mix_router(x2, ret, u, pool_w_bf, pool_scale, w_out_bf, norm_w, rw_split, rb, seq, tile0):
    t = ret.shape[0]
    tm = TOKEN_TILE
    hist_blocks = tm // POOL_HISTORY
    tile = lambda width: pl.BlockSpec((tm, width), lambda i: (i, 0))
    const2 = lambda shape: pl.BlockSpec(shape, lambda i: (0, 0))
    const3 = lambda shape: pl.BlockSpec(shape, lambda i: (0, 0, 0))
    return pl.pallas_call(
        functools.partial(_mix_router_kernel, seq),
        grid=(t // tm,),
        in_specs=[
            pl.BlockSpec((tm, D_MODEL), lambda i: (i + tile0, 0)), tile(D_RET), tile(D_POOL),
            pl.BlockSpec((POOL_HISTORY, D_POOL), lambda i: (jnp.maximum(i * hist_blocks - 1, 0), 0)),
            const3((len(POOL_WINDOWS), POOL_BLOCK, POOL_BLOCK + POOL_HISTORY)),
            const3((len(POOL_WINDOWS), POOL_GROUP_DIM, POOL_GROUP_DIM)),
            const2((1, D_POOL)), const2((D_MODEL, D_MODEL)), const2((1, D_MODEL)),
            const2((D_MODEL, 2 * LANES)), const2((1, LANES)),
        ],
        out_specs=[tile(D_MODEL), tile(D_MODEL // 2), tile(LANES), tile(LANES), const2((1, LANES))],
        out_shape=[
            jax.ShapeDtypeStruct((t, D_MODEL), F32),
            jax.ShapeDtypeStruct((t, D_MODEL // 2), U32),
            jax.ShapeDtypeStruct((t, LANES), F32),
            jax.ShapeDtypeStruct((t, LANES), F32),
            jax.ShapeDtypeStruct((1, LANES), F32),
        ],
        scratch_shapes=[pltpu.VMEM((tm + POOL_HISTORY, D_POOL), BF16)],
        compiler_params=_params(("arbitrary",)),
        name="mix_router",
    )(x2, ret, u, u, _pool_bands(), pool_w_bf, pool_scale, w_out_bf, norm_w, rw_split, rb)


def _positions_kernel(code_ref, off_ref, tri_ref, pos_ref, carry_ref):
    @pl.when(pl.program_id(0) == 0)
    def _():
        carry_ref[...] = jnp.zeros_like(carry_ref)

    code = code_ref[...]
    tm = code.shape[0]
    sel = (code > 0).astype(BF16)
    carry = carry_ref[...]
    rank = _dot(tri_ref[...], sel) + (carry + off_ref[...])
    carry_ref[...] = carry + jnp.sum(sel.astype(F32), axis=0, keepdims=True)
    lane = lax.broadcasted_iota(jnp.int32, (tm, LANES), 1)
    pos = jnp.zeros((tm, LANES), F32)
    for k in range(TOP_K):
        pk = jnp.sum(jnp.where(code == float(k + 1), rank, 0.0), axis=-1, keepdims=True)
        pos = jnp.where(lane == k, pk, pos)
    pos_ref[...] = pos.astype(jnp.int32)


def _positions(code, offsets):
    t = code.shape[0]
    tm = TOKEN_TILE
    tri = jnp.asarray(np.tril(np.ones((tm, tm), np.float32), -1), dtype=BF16)
    return pl.pallas_call(
        _positions_kernel,
        grid=(t // tm,),
        in_specs=[pl.BlockSpec((tm, LANES), lambda i: (i, 0)),
                  pl.BlockSpec((1, LANES), lambda i: (0, 0)),
                  pl.BlockSpec((tm, tm), lambda i: (0, 0))],
        out_specs=pl.BlockSpec((tm, LANES), lambda i: (i, 0)),
        out_shape=jax.ShapeDtypeStruct((t, LANES), jnp.int32),
        scratch_shapes=[pltpu.VMEM((1, LANES), F32)],
        compiler_params=_params(("arbitrary",)),
        name="positions",
    )(code, offsets, tri)


def _sc_mesh():
    return plsc.VectorSubcoreMesh(core_axis_name="core", subcore_axis_name="subcore")


def _sc_workers():
    info = plsc.get_sparse_core_info()
    return info.num_cores, info.num_cores * info.num_subcores


def _sc_gather(src, idx):
    n = idx.shape[0]
    d = src.shape[1]
    w = SC_WINDOW // 2
    num_cores, workers = _sc_workers()
    per = n // workers
    pairs = per // (2 * w)
    assert per * workers == n and pairs * 2 * w == per

    @functools.partial(
        pl.kernel, out_type=jax.ShapeDtypeStruct((n, d), src.dtype), mesh=_sc_mesh(),
        scratch_types=[pltpu.VMEM((w,), jnp.int32), pltpu.VMEM((w,), jnp.int32),
                       pltpu.VMEM((w, d), src.dtype), pltpu.VMEM((w, d), src.dtype),
                       pltpu.SemaphoreType.DMA, pltpu.SemaphoreType.DMA,
                       pltpu.SemaphoreType.DMA, pltpu.SemaphoreType.DMA],
        name="sc_gather")
    def gather(src_hbm, idx_hbm, out_hbm, idx0, idx1, rows0, rows1, g0, g1, w0, w1):
        first = (lax.axis_index("subcore") * num_cores + lax.axis_index("core")) * per

        def start_gather(win, idx_v, rows_v, sem):
            pltpu.sync_copy(idx_hbm.at[pl.ds(first + win * w, w)], idx_v)
            pltpu.async_copy(src_hbm.at[idx_v], rows_v, sem)

        def wait_gather(idx_v, rows_v, sem):
            pltpu.make_async_copy(src_hbm.at[idx_v], rows_v, sem).wait()

        def start_write(win, rows_v, sem):
            pltpu.async_copy(rows_v, out_hbm.at[pl.ds(first + win * w, w)], sem)

        def wait_write(rows_v, sem):
            pltpu.make_async_copy(rows_v, out_hbm.at[pl.ds(first, w)], sem).wait()

        start_gather(0, idx0, rows0, g0)

        @pl.loop(0, pairs)
        def _(j):
            even = 2 * j

            @pl.when(j > 0)
            def _():
                wait_write(rows1, w1)

            start_gather(even + 1, idx1, rows1, g1)
            wait_gather(idx0, rows0, g0)
            start_write(even, rows0, w0)
            wait_write(rows0, w0)

            @pl.when(j + 1 < pairs)
            def _():
                start_gather(even + 2, idx0, rows0, g0)

            wait_gather(idx1, rows1, g1)
            start_write(even + 1, rows1, w1)

        wait_write(rows1, w1)

    return gather(src, idx)


def _sc_scatter(src, idx):
    t, d = src.shape
    w = SC_WINDOW
    num_cores, workers = _sc_workers()
    per = t // workers
    assert idx.shape == (TOP_K, t) and per * workers == t and per % w == 0

    @functools.partial(
        pl.kernel, out_type=jax.ShapeDtypeStruct((TOP_K * t, d), src.dtype), mesh=_sc_mesh(),
        scratch_types=[pltpu.VMEM((TOP_K, w), jnp.int32), pltpu.VMEM((w, d), src.dtype),
                       pltpu.SemaphoreType.DMA],
        name="sc_scatter")
    def scatter(src_hbm, idx_hbm, out_hbm, idx_v, rows_v, sem):
        first = (lax.axis_index("subcore") * num_cores + lax.axis_index("core")) * per

        @pl.loop(0, per // w)
        def _(j):
            base = first + j * w
            for k in range(TOP_K):
                pltpu.sync_copy(idx_hbm.at[k, pl.ds(base, w)], idx_v.at[k])
            pltpu.sync_copy(src_hbm.at[pl.ds(base, w)], rows_v)
            copies = [pltpu.async_copy(rows_v, out_hbm.at[idx_v.at[k]], sem) for k in range(TOP_K)]
            for c in copies:
                c.wait()

    return scatter(src, idx)


def _experts_kernel(items_ref, xs_ref, wgu_hbm, bg_ref, bu_ref, wdn_hbm, bd_ref, perm_ref, y_ref,
                    wgu_ref, wdn_ref, wg_ref, wu_ref, wd_ref, sems):
    i = pl.program_id(0)
    tm = xs_ref.shape[0]

    def weight_copies():
        e = items_ref[I_FETCH, i]
        return (pltpu.make_async_copy(wgu_hbm.at[e], wgu_ref, sems.at[0]),
                pltpu.make_async_copy(wdn_hbm.at[e], wdn_ref, sems.at[1]))

    @pl.when(items_ref[I_START, i] == 1)
    def _():
        for copy in weight_copies():
            copy.start()

    @pl.when(items_ref[I_PREP, i] == 1)
    def _():
        for copy in weight_copies():
            copy.wait()
        ps = items_ref[I_PREP_SLOT, i]
        perm = perm_ref[...]
        pair = 2 * LANES
        for c in range(2 * D_MODEL // pair):
            sel = _dot(wgu_ref[:, c * pair:(c + 1) * pair].astype(BF16), perm).astype(BF16)
            wg_ref[ps, :, c * LANES:(c + 1) * LANES] = sel[:, :LANES]
            wu_ref[ps, :, c * LANES:(c + 1) * LANES] = sel[:, LANES:]
        wd_ref[ps] = wdn_ref[...].astype(BF16)

    def expert_mlp(r0, r1):
        n = r1 - r0
        s = items_ref[I_SLOT, i]
        x_lo, x_hi = _unpack_bf16_pair(xs_ref[r0:r1, :])
        x = jnp.concatenate([x_lo.astype(BF16), x_hi.astype(BF16)], axis=1)
        acc = jnp.zeros((n, D_MODEL), F32)
        for c in range(D_MODEL // EXPERT_COL_CHUNK):
            cs = slice(c * EXPERT_COL_CHUNK, (c + 1) * EXPERT_COL_CHUNK)
            gate = jnp.minimum(_dot(x, wg_ref[s, :, cs]) + bg_ref[:, cs], SWIGLU_LIMIT)
            up = jnp.clip(_dot(x, wu_ref[s, :, cs]) + bu_ref[:, cs], -SWIGLU_LIMIT, SWIGLU_LIMIT)
            act = (up + 1.0) * (gate * _sigmoid(SWIGLU_ALPHA * gate))
            acc = acc + _dot(act.astype(BF16), wd_ref[s, cs, :])
        y = acc + bd_ref[...]
        half = D_MODEL // 2
        packed = _pack_bf16_pair(y[:, :half], y[:, half:])
        rows = items_ref[I_TILE, i] * tm + r0 + lax.broadcasted_iota(jnp.int32, (n, 1), 0)
        mine = (rows >= items_ref[I_LO, i]) & (rows < items_ref[I_HI, i])

        @pl.when(items_ref[I_FIRST, i] == 1)
        def _():
            y_ref[r0:r1, :] = packed
            if r0 > 0:
                y_ref[0:r0, :] = jnp.zeros((r0, half), U32)
            if r1 < tm:
                y_ref[r1:tm, :] = jnp.zeros((tm - r1, half), U32)

        @pl.when(items_ref[I_FIRST, i] == 0)
        def _():
            y_ref[r0:r1, :] = jnp.where(mine, packed, y_ref[r0:r1, :])

    for mode, (r0, r1) in ((MODE_FULL, (0, tm)), (MODE_LOWER, (0, tm // 2)), (MODE_UPPER, (tm // 2, tm))):
        pl.when(items_ref[I_MODE, i] == mode)(functools.partial(expert_mlp, r0, r1))


(I_TILE, I_EXPERT, I_LO, I_HI, I_FIRST, I_MODE, I_FETCH, I_START, I_PREP, I_PREP_SLOT, I_SLOT) = range(11)
ITEM_FIELDS = 16
(MODE_SKIP, MODE_FULL, MODE_LOWER, MODE_UPPER) = range(4)


def _plan_kernel(tm, n_tiles, cnt_ref, off_ref, items_ref):
    n = items_ref.shape[1]
    cnt_r = cnt_ref[...]
    sub = lax.broadcasted_iota(jnp.int32, (LANES, LANES), 0).astype(F32)
    lan = lax.broadcasted_iota(jnp.int32, (LANES, LANES), 1).astype(F32)
    big = float(4 * LANES)

    def col(row):
        return jnp.sum(jnp.where(lan == sub, row, 0.0), axis=1, keepdims=True)

    def row(column):
        return jnp.sum(jnp.where(lan == sub, column, 0.0), axis=0, keepdims=True)

    def prefix(r):
        return jnp.sum(jnp.where(lan <= sub, r, 0.0), axis=1, keepdims=True)

    cnt = col(cnt_r)
    ends = prefix(cnt_r)
    starts = ends - cnt
    off_ref[...] = row(starts)
    used = cnt > 0.0
    inv_tm = 1.0 / tm
    first_tile = jnp.floor(starts * inv_tm)
    last_tile = jnp.floor(jnp.maximum(ends - 1.0, 0.0) * inv_tm)
    per = jnp.where(used, last_tile - first_tile + 1.0, 0.0)
    item_end = prefix(row(per))
    item_start = item_end - per
    used_r = row(used.astype(F32))
    ordinal = prefix(used_r) - 1.0
    nxt = jnp.min(jnp.where((lan > sub) & (used_r > 0.0), lan, big), axis=1, keepdims=True)
    has_next = nxt < big
    e_col = sub[:, 0:1]
    nxt = jnp.where(has_next, nxt, e_col)
    e_first = jnp.min(jnp.where(used_r > 0.0, lan[0:1, :], big), axis=1, keepdims=True)
    e_last = jnp.max(jnp.where(used_r > 0.0, lan[0:1, :], -1.0), axis=1, keepdims=True)
    total = jnp.max(item_end, axis=0, keepdims=True)

    item = lax.broadcasted_iota(jnp.int32, (LANES, n), 1).astype(F32) - 1.0
    mine = ((item >= item_start) & (item < item_end)).astype(F32)
    pick = lambda column: jnp.sum(mine * column, axis=0, keepdims=True)
    item_r = item[0:1, :]
    valid = pick(jnp.ones_like(cnt))
    lead = item_r < 0.0
    past = item_r >= total
    new_tile = 1.0 - pick(((item == item_start) & (starts - first_tile * tm > 0.0)).astype(F32))

    def put(field, value):
        items_ref[field:field + 1, :] = value.astype(jnp.int32)

    tile = pick(first_tile - item_start) + item_r
    lo = pick(starts)
    hi = pick(ends)
    needs_lower = lo < tile * tm + 0.5 * tm
    needs_upper = hi > tile * tm + 0.5 * tm
    mode = jnp.where(needs_lower & needs_upper, float(MODE_FULL),
                     jnp.where(needs_lower, float(MODE_LOWER), float(MODE_UPPER)))
    put(I_TILE, jnp.where(past, float(n_tiles - 1), jnp.where(lead, 0.0, tile)))
    put(I_EXPERT, jnp.where(past, e_last, jnp.where(lead, e_first, pick(e_col))))
    put(I_LO, lo)
    put(I_HI, hi)
    put(I_FIRST, valid * new_tile)
    put(I_MODE, valid * mode)
    put(I_FETCH, jnp.where(past, e_last, jnp.where(lead, e_first, pick(nxt))))
    put(I_START, jnp.where(lead, 1.0, pick(((item == item_start) & has_next).astype(F32))))
    put(I_PREP, jnp.where(lead, 1.0, pick(((item == item_end - 1.0) & has_next).astype(F32))))
    put(I_PREP_SLOT, pick(ordinal + 1.0 - 2.0 * jnp.floor((ordinal + 1.0) * 0.5)))
    put(I_SLOT, pick(ordinal - 2.0 * jnp.floor(ordinal * 0.5)))
    for field in range(I_SLOT + 1, ITEM_FIELDS):
        put(field, jnp.zeros_like(valid))


def _plan(counts, n_rows, tm):
    n_tiles = n_rows // tm
    n_items = n_tiles + N_EXPERTS
    width = -(-n_items // LANES) * LANES
    offsets, items = pl.pallas_call(
        functools.partial(_plan_kernel, float(tm), n_tiles),
        out_shape=[jax.ShapeDtypeStruct((1, LANES), F32),
                   jax.ShapeDtypeStruct((ITEM_FIELDS, width), jnp.int32)],
        name="plan",
    )(counts)
    return offsets, items, n_items


def _experts(xs, items, n_items, w_gate_up, bg, bu, w_down, bd):
    n_rows = xs.shape[0]
    tm = EXPERT_TILE
    half = D_MODEL // 2
    j = np.arange(2 * LANES)
    perm = np.zeros((2 * LANES, 2 * LANES), np.float32)
    perm[j, np.where(j % 2 == 0, j // 2, LANES + j // 2)] = 1.0
    row_tile = pl.BlockSpec((tm, half), lambda i, items: (items[I_TILE, i], 0))
    by_expert = lambda *shape: pl.BlockSpec((None,) + shape, lambda i, items: (items[I_EXPERT, i], 0, 0))
    in_hbm = pl.BlockSpec(memory_space=pl.ANY)
    return pl.pallas_call(
        _experts_kernel,
        grid_spec=pltpu.PrefetchScalarGridSpec(
            num_scalar_prefetch=1,
            grid=(n_items,),
            in_specs=[row_tile, in_hbm, by_expert(1, D_MODEL), by_expert(1, D_MODEL), in_hbm,
                      by_expert(1, D_MODEL),
                      pl.BlockSpec((2 * LANES, 2 * LANES), lambda i, items: (0, 0))],
            out_specs=row_tile,
            scratch_shapes=[pltpu.VMEM((D_MODEL, 2 * D_MODEL), F32), pltpu.VMEM((D_MODEL, D_MODEL), F32)]
            + [pltpu.VMEM((2, D_MODEL, D_MODEL), BF16)] * 3 + [pltpu.SemaphoreType.DMA((2,))],
        ),
        out_shape=jax.ShapeDtypeStruct((n_rows, half), U32),
        compiler_params=_params(("arbitrary",)),
        name="experts",
    )(items, xs, w_gate_up, bg, bu, w_down, bd, jnp.asarray(perm, dtype=BF16))


def _tail_kernel(h1_ref, yu_ref, gates_ref, p_ref, nple_ref, wg_ref, wp_ref, nfin_ref, *rest):
    out_ref = rest[-1]
    gates = gates_ref[...]
    lo = None
    hi = None
    for k in range(TOP_K):
        gk = gates[:, k:k + 1]
        yl, yh = _unpack_bf16_pair(yu_ref[k])
        lo = gk * yl if lo is None else lo + gk * yl
        hi = gk * yh if hi is None else hi + gk * yh
    h2 = h1_ref[...] + jnp.concatenate([lo, hi], axis=1)
    hn = _rms(h2, nple_ref[...]).astype(BF16)
    gate = _sigmoid(_dot(hn, wg_ref[...]))
    h3 = h2 + gate * _dot(p_ref[...].astype(BF16), wp_ref[...])
    out_ref[...] = _rms(h3, nfin_ref[...])


def _tail(h1, yu, gates, p2, norm_ple_w, ple_gate_bf, ple_proj_bf, final_norm_w, tile0, result):
    t = h1.shape[0]
    tm = TOKEN_TILE
    half = D_MODEL // 2
    tile = lambda width: pl.BlockSpec((tm, width), lambda i: (i, 0))
    shifted = lambda width: pl.BlockSpec((tm, width), lambda i: (i + tile0, 0))
    const2 = lambda shape: pl.BlockSpec(shape, lambda i: (0, 0))
    in_specs = [tile(D_MODEL), pl.BlockSpec((TOP_K, tm, half), lambda i: (0, i, 0)),
                tile(LANES), shifted(PLE_DIM), const2((1, D_MODEL)),
                const2((D_MODEL, D_MODEL)), const2((PLE_DIM, D_MODEL)), const2((1, D_MODEL))]
    args = [h1, yu, gates, p2, norm_ple_w, ple_gate_bf, ple_proj_bf, final_norm_w]
    aliases = {}
    if result is not None:
        in_specs.append(pl.BlockSpec(memory_space=pl.ANY))
        args.append(result)
        aliases = {len(args) - 1: 0}
    return pl.pallas_call(
        _tail_kernel,
        grid=(t // tm,),
        in_specs=in_specs,
        out_specs=shifted(D_MODEL),
        out_shape=jax.ShapeDtypeStruct((p2.shape[0], D_MODEL), F32),
        input_output_aliases=aliases,
        compiler_params=_params(("parallel",)),
        name="tail",
    )(*args)


def kernel(x, p, positions, w_in, w_out, ret_gn_w, pool_w, pool_scale, norm_mix_w, norm_moe_w, router_w, router_b, expert_w_gate_up, expert_b_gate_up, expert_w_down, expert_b_down, norm_ple_w, ple_gate_w, ple_proj_w, final_norm_w):
    batch, seq, d = x.shape
    depth = w_in.shape[0]
    assert depth == 1 and d == D_MODEL and seq % TOKEN_TILE == 0
    assert seq % (RET_CHUNK * RET_STEP_CHUNKS) == 0
    groups = BATCH_GROUPS if batch % BATCH_GROUPS == 0 else 1
    t = batch * seq
    tg = t // groups
    n_rows = tg * TOP_K
    assert n_rows % EXPERT_TILE == 0
    row = lambda a: a.reshape(1, -1).astype(F32)
    l = 0

    x2 = x.reshape(t, d)
    pos2 = positions.reshape(t, 1)
    p2 = p[l].reshape(t, PLE_DIM)
    w_in_bf = w_in[l].astype(BF16)
    rw = jnp.pad(router_w[l].astype(F32), ((0, 0), (0, LANES - N_EXPERTS)))
    rw_hi = rw.astype(BF16)
    rw_lo = (rw - rw_hi.astype(F32)).astype(BF16)
    rw_split = jnp.concatenate([rw_hi, rw_lo], axis=1)
    rb = jnp.pad(router_b[l].astype(F32), (0, LANES - N_EXPERTS), constant_values=NEG_BIG).reshape(1, LANES)
    bgu = expert_b_gate_up[l].reshape(N_EXPERTS, 1, D_MODEL, 2).astype(F32)
    bdn = expert_b_down[l].reshape(N_EXPERTS, 1, D_MODEL).astype(F32)
    pool_w_bf, w_out_bf = pool_w[l].astype(BF16), w_out[l].astype(BF16)
    ple_gate_bf, ple_proj_bf = ple_gate_w[l].astype(BF16), ple_proj_w[l].astype(BF16)

    out = None
    for gi in range(groups):
        tile0 = gi * (tg // TOKEN_TILE)
        ret, u = _retention(x2, pos2, row(norm_mix_w[l]), w_in_bf, row(ret_gn_w[l]),
                            gi * (tg // (RET_CHUNK * RET_STEP_CHUNKS)), batch // groups, seq)
        h1, hn_packed, code, gates, counts = _mix_router(
            x2, ret, u, pool_w_bf, row(pool_scale[l]), w_out_bf, row(norm_moe_w[l]), rw_split, rb,
            seq, tile0)

        offsets, items, n_items = _plan(counts, n_rows, EXPERT_TILE)
        pos = _positions(code, offsets)
        pos_km = pos[:, :TOP_K].T

        xs = _sc_scatter(hn_packed, pos_km)
        y = _experts(xs, items, n_items, expert_w_gate_up[l], bgu[..., 0], bgu[..., 1],
                     expert_w_down[l], bdn)
        yu = _sc_gather(y, pos_km.reshape(n_rows)).reshape(TOP_K, tg, d // 2)

        out = _tail(h1, yu, gates, p2, row(norm_ple_w[l]), ple_gate_bf, ple_proj_bf,
                    row(final_norm_w), tile0, out)
    return out.reshape(batch, seq, d)
```

```python
import functools
import math

import numpy as np
import jax
import jax.numpy as jnp
from jax import lax
from jax.experimental import pallas as pl
from jax.experimental.pallas import tpu as pltpu
from jax.experimental.pallas import tpu_sc as plsc

D_MODEL = 1024
D_RET = 512
D_POOL = 512
RET_HEADS = 8
RET_HEAD_DIM = 64
HEAD_PAIRS = RET_HEADS // 2
ROPE_BASE = 10000.0
POOL_WINDOWS = (2, 4, 8, 16)
POOL_GROUP_DIM = 128
POOL_HISTORY = 16
POOL_BLOCK = 128
D_IN_PROJ = 4 * D_RET + D_POOL
N_EXPERTS = 32
TOP_K = 4
SWIGLU_LIMIT = 7.0
SWIGLU_ALPHA = 1.702
PLE_DIM = 256
NORM_EPS = 1e-5
GN_EPS = 1e-5

LANES = 128
NEG_BIG = -1e30

TOKEN_TILE = 512
RET_CHUNK = 256
RET_STEP_CHUNKS = 4
EXPERT_TILE = 512
EXPERT_COL_CHUNK = 512
SC_WINDOW = 128
BATCH_GROUPS = 2
VMEM_LIMIT = 56 * 1024 * 1024

F32 = jnp.float32
BF16 = jnp.bfloat16
U32 = jnp.uint32


def _params(semantics):
    return pltpu.CompilerParams(dimension_semantics=semantics, vmem_limit_bytes=VMEM_LIMIT)


def _dot(a, b):
    return jnp.dot(a, b, preferred_element_type=F32)


def _dot_nt(a, b):
    return lax.dot_general(a, b, (((1,), (1,)), ((), ())), preferred_element_type=F32)


def _dot_tn(a, b):
    return lax.dot_general(a, b, (((0,), (0,)), ((), ())), preferred_element_type=F32)


def _rms(x, w):
    ms = jnp.mean(x * x, axis=-1, keepdims=True)
    return x * lax.rsqrt(ms + NORM_EPS) * w


def _sigmoid(z):
    return 1.0 / (1.0 + jnp.exp(-z))


def _pack_bf16_pair(lo, hi):
    lo_bits = pltpu.bitcast(lo.astype(BF16).astype(F32), U32) >> 16
    hi_bits = pltpu.bitcast(hi.astype(BF16).astype(F32), U32) & jnp.uint32(0xFFFF0000)
    return lo_bits | hi_bits


def _unpack_bf16_pair(packed):
    lo = pltpu.bitcast(packed << 16, F32)
    hi = pltpu.bitcast(packed & jnp.uint32(0xFFFF0000), F32)
    return lo, hi


def _project_rows(x, pos, nw_ref, w_ref, freq_ref, qkvg_ref, u_ref, rows):
    hn = _rms(x, nw_ref[...]).astype(BF16)
    ang = pos.astype(F32) * freq_ref[0:1, :]
    cos = jnp.cos(ang)
    sin = jnp.sin(ang)
    slab = 2 * LANES
    cos_t = jnp.concatenate([cos, cos], axis=1)
    sin_up = jnp.concatenate([sin * freq_ref[1:2, :]] * 2, axis=1)
    sin_dn = jnp.concatenate([sin * freq_ref[2:3, :]] * 2, axis=1)
    half = RET_HEAD_DIM // 2

    for s in range(2 * D_RET // slab):
        sl = slice(s * slab, (s + 1) * slab)
        v = _dot(hn, w_ref[:, sl])
        v = v * cos_t + pltpu.roll(v, slab - half, 1) * sin_up + pltpu.roll(v, half, 1) * sin_dn
        if s >= D_RET // slab:
            v = v * (RET_HEAD_DIM ** -0.5)
        qkvg_ref[rows, sl] = v.astype(BF16)
    for s in range(2 * D_RET // 512, 4 * D_RET // 512):
        sl = slice(s * 512, (s + 1) * 512)
        qkvg_ref[rows, sl] = _dot(hn, w_ref[:, sl]).astype(BF16)
    u_ref[rows, :] = _dot(hn, w_ref[:, 4 * D_RET:]).astype(BF16)


def _rope_table():
    j = np.arange(LANES)
    half = RET_HEAD_DIM // 2
    inv_freq = ROPE_BASE ** (-(np.arange(half, dtype=np.float32)) / half)
    freq = np.zeros((8, LANES), np.float32)
    freq[0] = inv_freq[j % half]
    freq[1] = np.where(j % RET_HEAD_DIM < half, -1.0, 0.0)
    freq[2] = np.where(j % RET_HEAD_DIM >= half, 1.0, 0.0)
    return jnp.asarray(freq)


def _retention_kernel(x_ref, pos_ref, nw_ref, w_ref, freq_ref, dec_ref, xi_ref, zeta_ref, cd_ref, bd_ref,
                      m64_ref, eye_ref, gnw_ref, out_ref, u_ref, state_ref, qkvg_ref):
    @pl.when(pl.program_id(1) == 0)
    def _():
        state_ref[...] = jnp.zeros_like(state_ref)

    c = dec_ref.shape[1]
    n_chunks = x_ref.shape[0] // c
    lane = lax.broadcasted_iota(jnp.int32, (1, LANES), 1)
    m64 = m64_ref[...]
    bd = bd_ref[...]
    eye = eye_ref[...]

    def group_mean(v):
        hi = v.astype(BF16)
        lo = (v - hi.astype(F32)).astype(BF16)
        return _dot(jnp.concatenate([hi, lo], axis=1), m64)

    for ci in range(n_chunks):
        rows = slice(ci * c, (ci + 1) * c)
        _project_rows(x_ref[rows, :], pos_ref[rows, :], nw_ref, w_ref, freq_ref, qkvg_ref, u_ref, rows)
        for p in range(HEAD_PAIRS):
            sl = slice(p * LANES, (p + 1) * LANES)
            qp = qkvg_ref[rows, sl]
            kp = qkvg_ref[rows, D_RET + p * LANES:D_RET + (p + 1) * LANES]
            vp = qkvg_ref[rows, 2 * D_RET + p * LANES:2 * D_RET + (p + 1) * LANES]
            y = None
            for hh in range(2):
                in_head = (lane >= RET_HEAD_DIM) == bool(hh)
                qm = jnp.where(in_head, qp, jnp.zeros_like(qp))
                vm = jnp.where(in_head, vp, jnp.zeros_like(vp))
                scores = _dot_nt(qm, kp) * dec_ref[2 * p + hh]
                part = _dot(scores.astype(BF16), vm)
                y = part if y is None else y + part
            st = state_ref[p]
            y = y + _dot((qp.astype(F32) * xi_ref[p]).astype(BF16), st.astype(BF16))
            kz = (kp.astype(F32) * zeta_ref[p]).astype(BF16)
            kz_t = _dot_nt(eye, kz).astype(BF16)
            state_ref[p] = cd_ref[p] * st + _dot(kz_t, vp) * bd
            mu = group_mean(y)
            var = group_mean(y * y) - mu * mu
            yn = (y - mu) * lax.rsqrt(var + GN_EPS) * gnw_ref[:, sl]
            g = qkvg_ref[rows, 3 * D_RET + p * LANES:3 * D_RET + (p + 1) * LANES].astype(F32)
            out_ref[rows, sl] = (yn * g * _sigmoid(g)).astype(BF16)


def _retention_tables(c):
    h = np.arange(RET_HEADS, dtype=np.float64)
    log_gamma = np.log1p(-np.power(2.0, -5.0 - h))
    idx = np.arange(c, dtype=np.float64)
    rel = idx[:, None] - idx[None, :]
    dec = np.where(rel >= 0, np.exp(np.where(rel >= 0, rel, 0.0)[None] * log_gamma[:, None, None]), 0.0)
    lane_head = np.arange(LANES) // RET_HEAD_DIM
    xi = np.zeros((HEAD_PAIRS, c, LANES))
    zeta = np.zeros((HEAD_PAIRS, c, LANES))
    cd = np.zeros((HEAD_PAIRS, LANES, LANES))
    same = lane_head[:, None] == lane_head[None, :]
    for p in range(HEAD_PAIRS):
        lg = log_gamma[2 * p + lane_head]
        xi[p] = np.exp((idx + 1.0)[:, None] * lg[None, :])
        zeta[p] = np.exp((c - 1 - idx)[:, None] * lg[None, :])
        cd[p] = np.where(same, np.exp(c * lg)[:, None], 0.0)
    bd = same.astype(np.float32)
    m64 = np.concatenate([same, same], axis=0).astype(np.float32) / RET_HEAD_DIM
    f = lambda a: jnp.asarray(a, dtype=F32)
    return (f(dec), f(xi), f(zeta), f(cd), f(bd), jnp.asarray(m64, dtype=BF16),
            jnp.asarray(np.eye(LANES, dtype=np.float32), dtype=BF16))


def _retention(x2, pos2, norm_w, w_in_bf, gn_w, block0, batch, seq):
    c = RET_CHUNK
    rows = RET_STEP_CHUNKS * c
    n = seq // rows
    t = batch * seq
    dec, xi, zeta, cd, bd, m64, eye = _retention_tables(c)
    const3 = lambda shape: pl.BlockSpec(shape, lambda b, i: (0, 0, 0))
    const2 = lambda shape: pl.BlockSpec(shape, lambda b, i: (0, 0))
    out_tile = pl.BlockSpec((rows, D_RET), lambda b, i: (b * n + i, 0))
    return pl.pallas_call(
        _retention_kernel,
        grid=(batch, n),
        in_specs=[pl.BlockSpec((rows, D_MODEL), lambda b, i: (block0 + b * n + i, 0)),
                  pl.BlockSpec((rows, 1), lambda b, i: (block0 + b * n + i, 0)),
                  const2((1, D_MODEL)), const2((D_MODEL, D_IN_PROJ)), const2((8, LANES)),
                  const3((RET_HEADS, c, c)), const3((HEAD_PAIRS, c, LANES)),
                  const3((HEAD_PAIRS, c, LANES)), const3((HEAD_PAIRS, LANES, LANES)),
                  const2((LANES, LANES)), const2((2 * LANES, LANES)), const2((LANES, LANES)),
                  const2((1, D_RET))],
        out_specs=[out_tile, out_tile],
        out_shape=[jax.ShapeDtypeStruct((t, D_RET), BF16), jax.ShapeDtypeStruct((t, D_POOL), BF16)],
        scratch_shapes=[pltpu.VMEM((HEAD_PAIRS, LANES, LANES), F32), pltpu.VMEM((rows, 4 * D_RET), BF16)],
        compiler_params=_params(("arbitrary", "arbitrary")),
        name="retention",
    )(x2, pos2, norm_w, w_in_bf, _rope_table(), dec, xi, zeta, cd, bd, m64, eye, gn_w)


def _mix_router_kernel(seq, x_ref, ret_ref, u_ref, uprev_ref, band_ref, pw_ref, ps_ref, wout_ref,
                       nw_ref, rw_ref, rb_ref,
                       h1_ref, hnp_ref, code_ref, gates_ref, cnt_ref, uext_ref):
    i = pl.program_id(0)
    tm = x_ref.shape[0]
    t0 = lax.rem(i * tm, seq)

    @pl.when(i == 0)
    def _():
        cnt_ref[...] = jnp.zeros_like(cnt_ref)

    prev = uprev_ref[...]
    uext_ref[0:POOL_HISTORY, :] = jnp.where(t0 == 0, jnp.zeros_like(prev), prev)
    uext_ref[POOL_HISTORY:, :] = u_ref[...]

    row = lax.broadcasted_iota(jnp.int32, (tm, 1), 0)
    t_seq = (t0 + row + 1).astype(F32)
    mixed = []
    for gi, w in enumerate(POOL_WINDOWS):
        sl = slice(gi * POOL_GROUP_DIM, (gi + 1) * POOL_GROUP_DIM)
        wsum = jnp.concatenate(
            [_dot(band_ref[gi], uext_ref[r0:r0 + POOL_BLOCK + POOL_HISTORY, sl])
             for r0 in range(0, tm, POOL_BLOCK)], axis=0)
        count = jnp.minimum(t_seq, float(w))
        pooled = wsum / count - u_ref[:, sl].astype(F32)
        mixed.append(_dot(pooled.astype(BF16), pw_ref[gi]))
    pool = (jnp.concatenate(mixed, axis=1) * ps_ref[...]).astype(BF16)

    h1 = (x_ref[...] + _dot(ret_ref[...], wout_ref[0:D_RET, :])
          + _dot(pool, wout_ref[D_RET:, :]))
    h1_ref[...] = h1
    hn = _rms(h1, nw_ref[...])
    hn_hi = hn.astype(BF16)
    half = D_MODEL // 2
    hnp_ref[...] = _pack_bf16_pair(hn[:, :half], hn[:, half:])

    hn_lo = (hn - hn_hi.astype(F32)).astype(BF16)
    both = _dot(hn_hi, rw_ref[...])
    logits = (both[:, :LANES] + both[:, LANES:] + _dot(hn_lo, rw_ref[:, :LANES])
              + rb_ref[...])

    lane = lax.broadcasted_iota(jnp.int32, (tm, LANES), 1).astype(F32)
    code = jnp.zeros((tm, LANES), F32)
    vals = []
    work = logits
    for k in range(TOP_K):
        m = jnp.max(work, axis=-1, keepdims=True)
        idx = jnp.min(jnp.where(work == m, lane, float(LANES)), axis=-1, keepdims=True)
        chosen = lane == idx
        code = jnp.where(chosen, float(k + 1), code)
        work = jnp.where(chosen, -jnp.inf, work)
        vals.append(m)
    exps = [jnp.exp(v - vals[0]) for v in vals]
    denom = exps[0] + exps[1] + exps[2] + exps[3]
    gates = jnp.zeros((tm, LANES), F32)
    for k in range(TOP_K):
        gates = jnp.where(lane == float(k), exps[k] / denom, gates)
    code_ref[...] = code
    gates_ref[...] = gates
    cnt_ref[...] += jnp.sum((code > 0).astype(F32), axis=0, keepdims=True)


def _pool_bands():
    r = np.arange(POOL_BLOCK)[:, None]
    s = np.arange(POOL_BLOCK + POOL_HISTORY)[None, :] - POOL_HISTORY
    bands = [((s <= r) & (s > r - w)) for w in POOL_WINDOWS]
    return jnp.asarray(np.stack(bands).astype(np.float32), dtype=BF16)


def _mix_router(x2, ret, u, pool_w_bf, pool_scale, w_out_bf, norm_w, rw_split, rb, seq, tile0):
    t = ret.shape[0]
    tm = TOKEN_TILE
    hist_blocks = tm // POOL_HISTORY
    tile = lambda width: pl.BlockSpec((tm, width), lambda i: (i, 0))
    const2 = lambda shape: pl.BlockSpec(shape, lambda i: (0, 0))
    const3 = lambda shape: pl.BlockSpec(shape, lambda i: (0, 0, 0))
    return pl.pallas_call(
        functools.partial(_mix_router_kernel, seq),
        grid=(t // tm,),
        in_specs=[
            pl.BlockSpec((tm, D_MODEL), lambda i: (i + tile0, 0)), tile(D_RET), tile(D_POOL),
            pl.BlockSpec((POOL_HISTORY, D_POOL), lambda i: (jnp.maximum(i * hist_blocks - 1, 0), 0)),
            const3((len(POOL_WINDOWS), POOL_BLOCK, POOL_BLOCK + POOL_HISTORY)),
            const3((len(POOL_WINDOWS), POOL_GROUP_DIM, POOL_GROUP_DIM)),
            const2((1, D_POOL)), const2((D_MODEL, D_MODEL)), const2((1, D_MODEL)),
            const2((D_MODEL, 2 * LANES)), const2((1, LANES)),
        ],
        out_specs=[tile(D_MODEL), tile(D_MODEL // 2), tile(LANES), tile(LANES), const2((1, LANES))],
        out_shape=[
            jax.ShapeDtypeStruct((t, D_MODEL), F32),
            jax.ShapeDtypeStruct((t, D_MODEL // 2), U32),
            jax.ShapeDtypeStruct((t, LANES), F32),
            jax.ShapeDtypeStruct((t, LANES), F32),
            jax.ShapeDtypeStruct((1, LANES), F32),
        ],
        scratch_shapes=[pltpu.VMEM((tm + POOL_HISTORY, D_POOL), BF16)],
        compiler_params=_params(("arbitrary",)),
        name="mix_router",
    )(x2, ret, u, u, _pool_bands(), pool_w_bf, pool_scale, w_out_bf, norm_w, rw_split, rb)


def _positions_kernel(code_ref, off_ref, tri_ref, pos_ref, carry_ref):
    @pl.when(pl.program_id(0) == 0)
    def _():
        carry_ref[...] = jnp.zeros_like(carry_ref)

    code = code_ref[...]
    tm = code.shape[0]
    sel = (code > 0).astype(BF16)
    carry = carry_ref[...]
    rank = _dot(tri_ref[...], sel) + (carry + off_ref[...])
    carry_ref[...] = carry + jnp.sum(sel.astype(F32), axis=0, keepdims=True)
    lane = lax.broadcasted_iota(jnp.int32, (tm, LANES), 1)
    pos = jnp.zeros((tm, LANES), F32)
    for k in range(TOP_K):
        pk = jnp.sum(jnp.where(code == float(k + 1), rank, 0.0), axis=-1, keepdims=True)
        pos = jnp.where(lane == k, pk, pos)
    pos_ref[...] = pos.astype(jnp.int32)


def _positions(code, offsets):
    t = code.shape[0]
    tm = TOKEN_TILE
    tri = jnp.asarray(np.tril(np.ones((tm, tm), np.float32), -1), dtype=BF16)
    return pl.pallas_call(
        _positions_kernel,
        grid=(t // tm,),
        in_specs=[pl.BlockSpec((tm, LANES), lambda i: (i, 0)),
                  pl.BlockSpec((1, LANES), lambda i: (0, 0)),
                  pl.BlockSpec((tm, tm), lambda i: (0, 0))],
        out_specs=pl.BlockSpec((tm, LANES), lambda i: (i, 0)),
        out_shape=jax.ShapeDtypeStruct((t, LANES), jnp.int32),
        scratch_shapes=[pltpu.VMEM((1, LANES), F32)],
        compiler_params=_params(("arbitrary",)),
        name="positions",
    )(code, offsets, tri)


def _sc_mesh():
    return plsc.VectorSubcoreMesh(core_axis_name="core", subcore_axis_name="subcore")


def _sc_workers():
    info = plsc.get_sparse_core_info()
    return info.num_cores, info.num_cores * info.num_subcores


def _sc_gather(src, idx):
    n = idx.shape[0]
    d = src.shape[1]
    w = SC_WINDOW // 2
    num_cores, workers = _sc_workers()
    per = n // workers
    pairs = per // (2 * w)
    assert per * workers == n and pairs * 2 * w == per

    @functools.partial(
        pl.kernel, out_type=jax.ShapeDtypeStruct((n, d), src.dtype), mesh=_sc_mesh(),
        scratch_types=[pltpu.VMEM((w,), jnp.int32), pltpu.VMEM((w,), jnp.int32),
                       pltpu.VMEM((w, d), src.dtype), pltpu.VMEM((w, d), src.dtype),
                       pltpu.SemaphoreType.DMA, pltpu.SemaphoreType.DMA,
                       pltpu.SemaphoreType.DMA, pltpu.SemaphoreType.DMA],
        name="sc_gather")
    def gather(src_hbm, idx_hbm, out_hbm, idx0, idx1, rows0, rows1, g0, g1, w0, w1):
        first = (lax.axis_index("subcore") * num_cores + lax.axis_index("core")) * per

        def start_gather(win, idx_v, rows_v, sem):
            pltpu.sync_copy(idx_hbm.at[pl.ds(first + win * w, w)], idx_v)
            pltpu.async_copy(src_hbm.at[idx_v], rows_v, sem)

        def wait_gather(idx_v, rows_v, sem):
            pltpu.make_async_copy(src_hbm.at[idx_v], rows_v, sem).wait()

        def start_write(win, rows_v, sem):
            pltpu.async_copy(rows_v, out_hbm.at[pl.ds(first + win * w, w)], sem)

        def wait_write(rows_v, sem):
            pltpu.make_async_copy(rows_v, out_hbm.at[pl.ds(first, w)], sem).wait()

        start_gather(0, idx0, rows0, g0)

        @pl.loop(0, pairs)
        def _(j):
            even = 2 * j

            @pl.when(j > 0)
            def _():
                wait_write(rows1, w1)

            start_gather(even + 1, idx1, rows1, g1)
            wait_gather(idx0, rows0, g0)
            start_write(even, rows0, w0)
            wait_write(rows0, w0)

            @pl.when(j + 1 < pairs)
            def _():
                start_gather(even + 2, idx0, rows0, g0)

            wait_gather(idx1, rows1, g1)
            start_write(even + 1, rows1, w1)

        wait_write(rows1, w1)

    return gather(src, idx)


def _sc_scatter(src, idx):
    t, d = src.shape
    w = SC_WINDOW
    num_cores, workers = _sc_workers()
    per = t // workers
    assert idx.shape == (TOP_K, t) and per * workers == t and per % w == 0

    @functools.partial(
        pl.kernel, out_type=jax.ShapeDtypeStruct((TOP_K * t, d), src.dtype), mesh=_sc_mesh(),
        scratch_types=[pltpu.VMEM((TOP_K, w), jnp.int32), pltpu.VMEM((w, d), src.dtype),
                       pltpu.SemaphoreType.DMA],
        name="sc_scatter")
    def scatter(src_hbm, idx_hbm, out_hbm, idx_v, rows_v, sem):
        first = (lax.axis_index("subcore") * num_cores + lax.axis_index("core")) * per

        @pl.loop(0, per // w)
        def _(j):
            base = first + j * w
            for k in range(TOP_K):
                pltpu.sync_copy(idx_hbm.at[k, pl.ds(base, w)], idx_v.at[k])
            pltpu.sync_copy(src_hbm.at[pl.ds(base, w)], rows_v)
            copies = [pltpu.async_copy(rows_v, out_hbm.at[idx_v.at[k]], sem) for k in range(TOP_K)]
            for c in copies:
                c.wait()

    return scatter(src, idx)


def _experts_kernel(items_ref, xs_ref, wgu_hbm, bg_ref, bu_ref, wdn_hbm, bd_ref, perm_ref, y_ref,
                    wgu_ref, wdn_ref, wg_ref, wu_ref, wd_ref, sems):
    i = pl.program_id(0)
    tm = xs_ref.shape[0]

    def weight_copies():
        e = items_ref[I_FETCH, i]
        return (pltpu.make_async_copy(wgu_hbm.at[e], wgu_ref, sems.at[0]),
                pltpu.make_async_copy(wdn_hbm.at[e], wdn_ref, sems.at[1]))

    @pl.when(items_ref[I_START, i] == 1)
    def _():
        for copy in weight_copies():
            copy.start()

    @pl.when(items_ref[I_PREP, i] == 1)
    def _():
        for copy in weight_copies():
            copy.wait()
        ps = items_ref[I_PREP_SLOT, i]
        perm = perm_ref[...]
        pair = 2 * LANES
        for c in range(2 * D_MODEL // pair):
            sel = _dot(wgu_ref[:, c * pair:(c + 1) * pair].astype(BF16), perm).astype(BF16)
            wg_ref[ps, :, c * LANES:(c + 1) * LANES] = sel[:, :LANES]
            wu_ref[ps, :, c * LANES:(c + 1) * LANES] = sel[:, LANES:]
        wd_ref[ps] = wdn_ref[...].astype(BF16)

    def expert_mlp(r0, r1):
        n = r1 - r0
        s = items_ref[I_SLOT, i]
        x_lo, x_hi = _unpack_bf16_pair(xs_ref[r0:r1, :])
        x = jnp.concatenate([x_lo.astype(BF16), x_hi.astype(BF16)], axis=1)
        acts = []
        chunks = [slice(c * EXPERT_COL_CHUNK, (c + 1) * EXPERT_COL_CHUNK)
                  for c in range(D_MODEL // EXPERT_COL_CHUNK)]
        for cs in chunks:
            gate = jnp.minimum(_dot(x, wg_ref[s, :, cs]) + bg_ref[:, cs], SWIGLU_LIMIT)
            up = jnp.clip(_dot(x, wu_ref[s, :, cs]) + bu_ref[:, cs], -SWIGLU_LIMIT, SWIGLU_LIMIT)
            acts.append(((up + 1.0) * (gate * _sigmoid(SWIGLU_ALPHA * gate))).astype(BF16))
        rows = items_ref[I_TILE, i] * tm + r0 + lax.broadcasted_iota(jnp.int32, (n, 1), 0)
        mine = (rows >= items_ref[I_LO, i]) & (rows < items_ref[I_HI, i])

        half, quarter = D_MODEL // 2, D_MODEL // 4
        for h in range(2):
            hs = slice(h * half, (h + 1) * half)
            yh = bd_ref[:, hs]
            for cs, act in zip(chunks, acts):
                yh = yh + _dot(act, wd_ref[s, cs, hs])
            cols = slice(h * quarter, (h + 1) * quarter)
            packed = _pack_bf16_pair(yh[:, :quarter], yh[:, quarter:])
            y_ref[r0:r1, cols] = jnp.where(mine, packed, y_ref[r0:r1, cols])

    @pl.when(items_ref[I_FIRST, i] == 1)
    def _():
        y_ref[...] = jnp.zeros(y_ref.shape, U32)

    for mode, (r0, r1) in ((MODE_FULL, (0, tm)), (MODE_LOWER, (0, tm // 2)), (MODE_UPPER, (tm // 2, tm))):
        pl.when(items_ref[I_MODE, i] == mode)(functools.partial(expert_mlp, r0, r1))


(I_TILE, I_EXPERT, I_LO, I_HI, I_FIRST, I_MODE, I_FETCH, I_START, I_PREP, I_PREP_SLOT, I_SLOT) = range(11)
ITEM_FIELDS = 16
(MODE_SKIP, MODE_FULL, MODE_LOWER, MODE_UPPER) = range(4)


def _plan_kernel(tm, n_tiles, cnt_ref, off_ref, items_ref):
    n = items_ref.shape[1]
    cnt_r = cnt_ref[...]
    sub = lax.broadcasted_iota(jnp.int32, (LANES, LANES), 0).astype(F32)
    lan = lax.broadcasted_iota(jnp.int32, (LANES, LANES), 1).astype(F32)
    big = float(4 * LANES)

    def col(row):
        return jnp.sum(jnp.where(lan == sub, row, 0.0), axis=1, keepdims=True)

    def row(column):
        return jnp.sum(jnp.where(lan == sub, column, 0.0), axis=0, keepdims=True)

    def prefix(r):
        return jnp.sum(jnp.where(lan <= sub, r, 0.0), axis=1, keepdims=True)

    cnt = col(cnt_r)
    ends = prefix(cnt_r)
    starts = ends - cnt
    off_ref[...] = row(starts)
    used = cnt > 0.0
    inv_tm = 1.0 / tm
    first_tile = jnp.floor(starts * inv_tm)
    last_tile = jnp.floor(jnp.maximum(ends - 1.0, 0.0) * inv_tm)
    per = jnp.where(used, last_tile - first_tile + 1.0, 0.0)
    item_end = prefix(row(per))
    item_start = item_end - per
    used_r = row(used.astype(F32))
    ordinal = prefix(used_r) - 1.0
    nxt = jnp.min(jnp.where((lan > sub) & (used_r > 0.0), lan, big), axis=1, keepdims=True)
    has_next = nxt < big
    e_col = sub[:, 0:1]
    nxt = jnp.where(has_next, nxt, e_col)
    e_first = jnp.min(jnp.where(used_r > 0.0, lan[0:1, :], big), axis=1, keepdims=True)
    e_last = jnp.max(jnp.where(used_r > 0.0, lan[0:1, :], -1.0), axis=1, keepdims=True)
    total = jnp.max(item_end, axis=0, keepdims=True)

    item = lax.broadcasted_iota(jnp.int32, (LANES, n), 1).astype(F32) - 1.0
    mine = ((item >= item_start) & (item < item_end)).astype(F32)
    pick = lambda column: jnp.sum(mine * column, axis=0, keepdims=True)
    item_r = item[0:1, :]
    valid = pick(jnp.ones_like(cnt))
    lead = item_r < 0.0
    past = item_r >= total
    new_tile = 1.0 - pick(((item == item_start) & (starts - first_tile * tm > 0.0)).astype(F32))

    def put(field, value):
        items_ref[field:field + 1, :] = value.astype(jnp.int32)

    tile = pick(first_tile - item_start) + item_r
    lo = pick(starts)
    hi = pick(ends)
    needs_lower = lo < tile * tm + 0.5 * tm
    needs_upper = hi > tile * tm + 0.5 * tm
    mode = jnp.where(needs_lower & needs_upper, float(MODE_FULL),
                     jnp.where(needs_lower, float(MODE_LOWER), float(MODE_UPPER)))
    put(I_TILE, jnp.where(past, float(n_tiles - 1), jnp.where(lead, 0.0, tile)))
    put(I_EXPERT, jnp.where(past, e_last, jnp.where(lead, e_first, pick(e_col))))
    put(I_LO, lo)
    put(I_HI, hi)
    put(I_FIRST, valid * new_tile)
    put(I_MODE, valid * mode)
    put(I_FETCH, jnp.where(past, e_last, jnp.where(lead, e_first, pick(nxt))))
    put(I_START, jnp.where(lead, 1.0, pick(((item == item_start) & has_next).astype(F32))))
    put(I_PREP, jnp.where(lead, 1.0, pick(((item == item_end - 1.0) & has_next).astype(F32))))
    put(I_PREP_SLOT, pick(ordinal + 1.0 - 2.0 * jnp.floor((ordinal + 1.0) * 0.5)))
    put(I_SLOT, pick(ordinal - 2.0 * jnp.floor(ordinal * 0.5)))
    for field in range(I_SLOT + 1, ITEM_FIELDS):
        put(field, jnp.zeros_like(valid))


def _plan(counts, n_rows, tm):
    n_tiles = n_rows // tm
    n_items = n_tiles + N_EXPERTS
    width = -(-n_items // LANES) * LANES
    offsets, items = pl.pallas_call(
        functools.partial(_plan_kernel, float(tm), n_tiles),
        out_shape=[jax.ShapeDtypeStruct((1, LANES), F32),
                   jax.ShapeDtypeStruct((ITEM_FIELDS, width), jnp.int32)],
        name="plan",
    )(counts)
    return offsets, items, n_items


def _experts(xs, items, n_items, w_gate_up, bg, bu, w_down, bd):
    n_rows = xs.shape[0]
    tm = EXPERT_TILE
    half = D_MODEL // 2
    j = np.arange(2 * LANES)
    perm = np.zeros((2 * LANES, 2 * LANES), np.float32)
    perm[j, np.where(j % 2 == 0, j // 2, LANES + j // 2)] = 1.0
    row_tile = pl.BlockSpec((tm, half), lambda i, items: (items[I_TILE, i], 0))
    by_expert = lambda *shape: pl.BlockSpec((None,) + shape, lambda i, items: (items[I_EXPERT, i], 0, 0))
    in_hbm = pl.BlockSpec(memory_space=pl.ANY)
    return pl.pallas_call(
        _experts_kernel,
        grid_spec=pltpu.PrefetchScalarGridSpec(
            num_scalar_prefetch=1,
            grid=(n_items,),
            in_specs=[row_tile, in_hbm, by_expert(1, D_MODEL), by_expert(1, D_MODEL), in_hbm,
                      by_expert(1, D_MODEL),
                      pl.BlockSpec((2 * LANES, 2 * LANES), lambda i, items: (0, 0))],
            out_specs=row_tile,
            scratch_shapes=[pltpu.VMEM((D_MODEL, 2 * D_MODEL), F32), pltpu.VMEM((D_MODEL, D_MODEL), F32)]
            + [pltpu.VMEM((2, D_MODEL, D_MODEL), BF16)] * 3 + [pltpu.SemaphoreType.DMA((2,))],
        ),
        out_shape=jax.ShapeDtypeStruct((n_rows, half), U32),
        compiler_params=_params(("arbitrary",)),
        name="experts",
    )(items, xs, w_gate_up, bg, bu, w_down, bd, jnp.asarray(perm, dtype=BF16))


def _tail_kernel(h1_ref, yu_ref, gates_ref, p_ref, nple_ref, wg_ref, wp_ref, nfin_ref, *rest):
    out_ref = rest[-1]
    gates = gates_ref[...]
    lo = None
    hi = None
    for k in range(TOP_K):
        gk = gates[:, k:k + 1]
        yl, yh = _unpack_bf16_pair(yu_ref[k])
        lo = gk * yl if lo is None else lo + gk * yl
        hi = gk * yh if hi is None else hi + gk * yh
    quarter = D_MODEL // 4
    h2 = h1_ref[...] + jnp.concatenate([lo[:, :quarter], hi[:, :quarter], lo[:, quarter:], hi[:, quarter:]],
                                       axis=1)
    hn = _rms(h2, nple_ref[...]).astype(BF16)
    gate = _sigmoid(_dot(hn, wg_ref[...]))
    h3 = h2 + gate * _dot(p_ref[...].astype(BF16), wp_ref[...])
    out_ref[...] = _rms(h3, nfin_ref[...])


def _tail(h1, yu, gates, p2, norm_ple_w, ple_gate_bf, ple_proj_bf, final_norm_w, tile0, result):
    t = h1.shape[0]
    tm = TOKEN_TILE
    half = D_MODEL // 2
    tile = lambda width: pl.BlockSpec((tm, width), lambda i: (i, 0))
    shifted = lambda width: pl.BlockSpec((tm, width), lambda i: (i + tile0, 0))
    const2 = lambda shape: pl.BlockSpec(shape, lambda i: (0, 0))
    in_specs = [tile(D_MODEL), pl.BlockSpec((TOP_K, tm, half), lambda i: (0, i, 0)),
                tile(LANES), shifted(PLE_DIM), const2((1, D_MODEL)),
                const2((D_MODEL, D_MODEL)), const2((PLE_DIM, D_MODEL)), const2((1, D_MODEL))]
    args = [h1, yu, gates, p2, norm_ple_w, ple_gate_bf, ple_proj_bf, final_norm_w]
    aliases = {}
    if result is not None:
        in_specs.append(pl.BlockSpec(memory_space=pl.ANY))
        args.append(result)
        aliases = {len(args) - 1: 0}
    return pl.pallas_call(
        _tail_kernel,
        grid=(t // tm,),
        in_specs=in_specs,
        out_specs=shifted(D_MODEL),
        out_shape=jax.ShapeDtypeStruct((p2.shape[0], D_MODEL), F32),
        input_output_aliases=aliases,
        compiler_params=_params(("parallel",)),
        name="tail",
    )(*args)


def kernel(x, p, positions, w_in, w_out, ret_gn_w, pool_w, pool_scale, norm_mix_w, norm_moe_w, router_w, router_b, expert_w_gate_up, expert_b_gate_up, expert_w_down, expert_b_down, norm_ple_w, ple_gate_w, ple_proj_w, final_norm_w):
    batch, seq, d = x.shape
    depth = w_in.shape[0]
    assert depth == 1 and d == D_MODEL and seq % TOKEN_TILE == 0
    assert seq % (RET_CHUNK * RET_STEP_CHUNKS) == 0
    groups = BATCH_GROUPS if batch % BATCH_GROUPS == 0 else 1
    t = batch * seq
    tg = t // groups
    n_rows = tg * TOP_K
    assert n_rows % EXPERT_TILE == 0
    row = lambda a: a.reshape(1, -1).astype(F32)
    l = 0

    x2 = x.reshape(t, d)
    pos2 = positions.reshape(t, 1)
    p2 = p[l].reshape(t, PLE_DIM)
    w_in_bf = w_in[l].astype(BF16)
    rw = jnp.pad(router_w[l].astype(F32), ((0, 0), (0, LANES - N_EXPERTS)))
    rw_hi = rw.astype(BF16)
    rw_lo = (rw - rw_hi.astype(F32)).astype(BF16)
    rw_split = jnp.concatenate([rw_hi, rw_lo], axis=1)
    rb = jnp.pad(router_b[l].astype(F32), (0, LANES - N_EXPERTS), constant_values=NEG_BIG).reshape(1, LANES)
    bgu = expert_b_gate_up[l].reshape(N_EXPERTS, 1, D_MODEL, 2).astype(F32)
    bdn = expert_b_down[l].reshape(N_EXPERTS, 1, D_MODEL).astype(F32)
    pool_w_bf, w_out_bf = pool_w[l].astype(BF16), w_out[l].astype(BF16)
    ple_gate_bf, ple_proj_bf = ple_gate_w[l].astype(BF16), ple_proj_w[l].astype(BF16)

    out = None
    for gi in range(groups):
        tile0 = gi * (tg // TOKEN_TILE)
        ret, u = _retention(x2, pos2, row(norm_mix_w[l]), w_in_bf, row(ret_gn_w[l]),
                            gi * (tg // (RET_CHUNK * RET_STEP_CHUNKS)), batch // groups, seq)
        h1, hn_packed, code, gates, counts = _mix_router(
            x2, ret, u, pool_w_bf, row(pool_scale[l]), w_out_bf, row(norm_moe_w[l]), rw_split, rb,
            seq, tile0)

        offsets, items, n_items = _plan(counts, n_rows, EXPERT_TILE)
        pos = _positions(code, offsets)
        pos_km = pos[:, :TOP_K].T

        xs = _sc_scatter(hn_packed, pos_km)
        y = _experts(xs, items, n_items, expert_w_gate_up[l], bgu[..., 0], bgu[..., 1],
                     expert_w_down[l], bdn)
        yu = _sc_gather(y, pos_km.reshape(n_rows)).reshape(TOP_K, tg, d // 2)

        out = _tail(h1, yu, gates, p2, row(norm_ple_w[l]), ple_gate_bf, ple_proj_bf,
                    row(final_norm_w), tile0, out)
    return out.reshape(batch, seq, d)
```

```python
import functools
import math

import numpy as np
import jax
import jax.numpy as jnp
from jax import lax
from jax.experimental import pallas as pl
from jax.experimental.pallas import tpu as pltpu
from jax.experimental.pallas import tpu_sc as plsc

D_MODEL = 1024
D_RET = 512
D_POOL = 512
RET_HEADS = 8
RET_HEAD_DIM = 64
HEAD_PAIRS = RET_HEADS // 2
ROPE_BASE = 10000.0
POOL_WINDOWS = (2, 4, 8, 16)
POOL_GROUP_DIM = 128
POOL_HISTORY = 16
POOL_BLOCK = 128
D_IN_PROJ = 4 * D_RET + D_POOL
N_EXPERTS = 32
TOP_K = 4
SWIGLU_LIMIT = 7.0
SWIGLU_ALPHA = 1.702
PLE_DIM = 256
NORM_EPS = 1e-5
GN_EPS = 1e-5

LANES = 128
NEG_BIG = -1e30

TOKEN_TILE = 512
POSITIONS_TILE = 1024
POS_ROWS = 8
RET_CHUNK = 256
RET_STEP_CHUNKS = 4
EXPERT_TILE = 512
EXPERT_COL_CHUNK = 512
SC_WINDOW = 128
BATCH_GROUPS = 2
LAST_GROUP_SPANS = 2
VMEM_LIMIT = 56 * 1024 * 1024

F32 = jnp.float32
BF16 = jnp.bfloat16
U32 = jnp.uint32


def _params(semantics):
    return pltpu.CompilerParams(dimension_semantics=semantics, vmem_limit_bytes=VMEM_LIMIT)


def _dot(a, b):
    return jnp.dot(a, b, preferred_element_type=F32)


def _dot_nt(a, b):
    return lax.dot_general(a, b, (((1,), (1,)), ((), ())), preferred_element_type=F32)


def _dot_tn(a, b):
    return lax.dot_general(a, b, (((0,), (0,)), ((), ())), preferred_element_type=F32)


def _rms(x, w):
    ms = jnp.mean(x * x, axis=-1, keepdims=True)
    return x * lax.rsqrt(ms + NORM_EPS) * w


def _sigmoid(z):
    return 1.0 / (1.0 + jnp.exp(-z))


def _pack_bf16_pair(lo, hi):
    lo_bits = pltpu.bitcast(lo.astype(BF16).astype(F32), U32) >> 16
    hi_bits = pltpu.bitcast(hi.astype(BF16).astype(F32), U32) & jnp.uint32(0xFFFF0000)
    return lo_bits | hi_bits


def _unpack_bf16_pair(packed):
    lo = pltpu.bitcast(packed << 16, F32)
    hi = pltpu.bitcast(packed & jnp.uint32(0xFFFF0000), F32)
    return lo, hi


def _project_rows(x, pos, nw_ref, w_ref, freq_ref, qkvg_ref, u_ref, rows):
    hn = _rms(x, nw_ref[...]).astype(BF16)
    ang = pos.astype(F32) * freq_ref[0:1, :]
    cos = jnp.cos(ang)
    sin = jnp.sin(ang)
    slab = 2 * LANES
    cos_t = jnp.concatenate([cos, cos], axis=1)
    sin_up = jnp.concatenate([sin * freq_ref[1:2, :]] * 2, axis=1)
    sin_dn = jnp.concatenate([sin * freq_ref[2:3, :]] * 2, axis=1)
    half = RET_HEAD_DIM // 2

    for s in range(2 * D_RET // slab):
        sl = slice(s * slab, (s + 1) * slab)
        v = _dot(hn, w_ref[:, sl])
        v = v * cos_t + pltpu.roll(v, slab - half, 1) * sin_up + pltpu.roll(v, half, 1) * sin_dn
        if s >= D_RET // slab:
            v = v * (RET_HEAD_DIM ** -0.5)
        qkvg_ref[rows, sl] = v.astype(BF16)
    for s in range(2 * D_RET // 512, 4 * D_RET // 512):
        sl = slice(s * 512, (s + 1) * 512)
        qkvg_ref[rows, sl] = _dot(hn, w_ref[:, sl]).astype(BF16)
    u_ref[rows, :] = _dot(hn, w_ref[:, 4 * D_RET:]).astype(BF16)


def _rope_table():
    j = np.arange(LANES)
    half = RET_HEAD_DIM // 2
    inv_freq = ROPE_BASE ** (-(np.arange(half, dtype=np.float32)) / half)
    freq = np.zeros((8, LANES), np.float32)
    freq[0] = inv_freq[j % half]
    freq[1] = np.where(j % RET_HEAD_DIM < half, -1.0, 0.0)
    freq[2] = np.where(j % RET_HEAD_DIM >= half, 1.0, 0.0)
    return jnp.asarray(freq)


def _retention_kernel(x_ref, pos_ref, nw_ref, w_ref, freq_ref, dec_ref, xi_ref, zeta_ref, cd_ref, bd_ref,
                      m64_ref, eye_ref, gnw_ref, out_ref, u_ref, state_ref, qkvg_ref):
    @pl.when(pl.program_id(1) == 0)
    def _():
        state_ref[...] = jnp.zeros_like(state_ref)

    c = dec_ref.shape[1]
    n_chunks = x_ref.shape[0] // c
    lane = lax.broadcasted_iota(jnp.int32, (1, LANES), 1)
    m64 = m64_ref[...]
    bd = bd_ref[...]
    eye = eye_ref[...]

    def group_mean(v):
        hi = v.astype(BF16)
        lo = (v - hi.astype(F32)).astype(BF16)
        return _dot(jnp.concatenate([hi, lo], axis=1), m64)

    for ci in range(n_chunks):
        rows = slice(ci * c, (ci + 1) * c)
        _project_rows(x_ref[rows, :], pos_ref[rows, :], nw_ref, w_ref, freq_ref, qkvg_ref, u_ref, rows)
        for p in range(HEAD_PAIRS):
            sl = slice(p * LANES, (p + 1) * LANES)
            qp = qkvg_ref[rows, sl]
            kp = qkvg_ref[rows, D_RET + p * LANES:D_RET + (p + 1) * LANES]
            vp = qkvg_ref[rows, 2 * D_RET + p * LANES:2 * D_RET + (p + 1) * LANES]
            y = None
            for hh in range(2):
                in_head = (lane >= RET_HEAD_DIM) == bool(hh)
                qm = jnp.where(in_head, qp, jnp.zeros_like(qp))
                vm = jnp.where(in_head, vp, jnp.zeros_like(vp))
                scores = _dot_nt(qm, kp) * dec_ref[2 * p + hh]
                part = _dot(scores.astype(BF16), vm)
                y = part if y is None else y + part
            st = state_ref[p]
            y = y + _dot((qp.astype(F32) * xi_ref[p]).astype(BF16), st.astype(BF16))
            kz = (kp.astype(F32) * zeta_ref[p]).astype(BF16)
            kz_t = _dot_nt(eye, kz).astype(BF16)
            state_ref[p] = cd_ref[p] * st + _dot(kz_t, vp) * bd
            mu = group_mean(y)
            var = group_mean(y * y) - mu * mu
            yn = (y - mu) * lax.rsqrt(var + GN_EPS) * gnw_ref[:, sl]
            g = qkvg_ref[rows, 3 * D_RET + p * LANES:3 * D_RET + (p + 1) * LANES].astype(F32)
            out_ref[rows, sl] = (yn * g * _sigmoid(g)).astype(BF16)


def _retention_tables(c):
    h = np.arange(RET_HEADS, dtype=np.float64)
    log_gamma = np.log1p(-np.power(2.0, -5.0 - h))
    idx = np.arange(c, dtype=np.float64)
    rel = idx[:, None] - idx[None, :]
    dec = np.where(rel >= 0, np.exp(np.where(rel >= 0, rel, 0.0)[None] * log_gamma[:, None, None]), 0.0)
    lane_head = np.arange(LANES) // RET_HEAD_DIM
    xi = np.zeros((HEAD_PAIRS, c, LANES))
    zeta = np.zeros((HEAD_PAIRS, c, LANES))
    cd = np.zeros((HEAD_PAIRS, LANES, LANES))
    same = lane_head[:, None] == lane_head[None, :]
    for p in range(HEAD_PAIRS):
        lg = log_gamma[2 * p + lane_head]
        xi[p] = np.exp((idx + 1.0)[:, None] * lg[None, :])
        zeta[p] = np.exp((c - 1 - idx)[:, None] * lg[None, :])
        cd[p] = np.where(same, np.exp(c * lg)[:, None], 0.0)
    bd = same.astype(np.float32)
    m64 = np.concatenate([same, same], axis=0).astype(np.float32) / RET_HEAD_DIM
    f = lambda a: jnp.asarray(a, dtype=F32)
    return (f(dec), f(xi), f(zeta), f(cd), f(bd), jnp.asarray(m64, dtype=BF16),
            jnp.asarray(np.eye(LANES, dtype=np.float32), dtype=BF16))


def _retention(x2, pos2, norm_w, w_in_bf, gn_w, block0, batch, seq):
    c = RET_CHUNK
    rows = RET_STEP_CHUNKS * c
    n = seq // rows
    t = batch * seq
    dec, xi, zeta, cd, bd, m64, eye = _retention_tables(c)
    const3 = lambda shape: pl.BlockSpec(shape, lambda b, i: (0, 0, 0))
    const2 = lambda shape: pl.BlockSpec(shape, lambda b, i: (0, 0))
    out_tile = pl.BlockSpec((rows, D_RET), lambda b, i: (b * n + i, 0))
    return pl.pallas_call(
        _retention_kernel,
        grid=(batch, n),
        in_specs=[pl.BlockSpec((rows, D_MODEL), lambda b, i: (block0 + b * n + i, 0)),
                  pl.BlockSpec((rows, 1), lambda b, i: (block0 + b * n + i, 0)),
                  const2((1, D_MODEL)), const2((D_MODEL, D_IN_PROJ)), const2((8, LANES)),
                  const3((RET_HEADS, c, c)), const3((HEAD_PAIRS, c, LANES)),
                  const3((HEAD_PAIRS, c, LANES)), const3((HEAD_PAIRS, LANES, LANES)),
                  const2((LANES, LANES)), const2((2 * LANES, LANES)), const2((LANES, LANES)),
                  const2((1, D_RET))],
        out_specs=[out_tile, out_tile],
        out_shape=[jax.ShapeDtypeStruct((t, D_RET), BF16), jax.ShapeDtypeStruct((t, D_POOL), BF16)],
        scratch_shapes=[pltpu.VMEM((HEAD_PAIRS, LANES, LANES), F32), pltpu.VMEM((rows, 4 * D_RET), BF16)],
        compiler_params=_params(("arbitrary", "arbitrary")),
        name="retention",
    )(x2, pos2, norm_w, w_in_bf, _rope_table(), dec, xi, zeta, cd, bd, m64, eye, gn_w)


def _mix_router_kernel(seq, x_ref, ret_ref, u_ref, uprev_ref, band_ref, pw_ref, ps_ref, wout_ref,
                       nw_ref, rw_ref, rb_ref,
                       h1_ref, hnp_ref, code_ref, gates_ref, cnt_ref, uext_ref):
    i = pl.program_id(0)
    tm = x_ref.shape[0]
    t0 = lax.rem(i * tm, seq)

    @pl.when(i == 0)
    def _():
        cnt_ref[...] = jnp.zeros_like(cnt_ref)

    prev = uprev_ref[...]
    uext_ref[0:POOL_HISTORY, :] = jnp.where(t0 == 0, jnp.zeros_like(prev), prev)
    uext_ref[POOL_HISTORY:, :] = u_ref[...]

    row = lax.broadcasted_iota(jnp.int32, (tm, 1), 0)
    t_seq = (t0 + row + 1).astype(F32)
    mixed = []
    for gi, w in enumerate(POOL_WINDOWS):
        sl = slice(gi * POOL_GROUP_DIM, (gi + 1) * POOL_GROUP_DIM)
        wsum = jnp.concatenate(
            [_dot(band_ref[gi], uext_ref[r0:r0 + POOL_BLOCK + POOL_HISTORY, sl])
             for r0 in range(0, tm, POOL_BLOCK)], axis=0)
        count = jnp.minimum(t_seq, float(w))
        pooled = wsum / count - u_ref[:, sl].astype(F32)
        mixed.append(_dot(pooled.astype(BF16), pw_ref[gi]))
    pool = (jnp.concatenate(mixed, axis=1) * ps_ref[...]).astype(BF16)

    h1 = (x_ref[...] + _dot(ret_ref[...], wout_ref[0:D_RET, :])
          + _dot(pool, wout_ref[D_RET:, :]))
    h1_ref[...] = h1
    hn = _rms(h1, nw_ref[...])
    hn_hi = hn.astype(BF16)
    half = D_MODEL // 2
    hnp_ref[...] = _pack_bf16_pair(hn[:, :half], hn[:, half:])

    hn_lo = (hn - hn_hi.astype(F32)).astype(BF16)
    both = _dot(hn_hi, rw_ref[...])
    logits = (both[:, :LANES] + both[:, LANES:] + _dot(hn_lo, rw_ref[:, :LANES])
              + rb_ref[...])

    lane = lax.broadcasted_iota(jnp.int32, (tm, LANES), 1).astype(F32)
    code = jnp.zeros((tm, LANES), F32)
    vals = []
    work = logits
    for k in range(TOP_K):
        m = jnp.max(work, axis=-1, keepdims=True)
        idx = jnp.min(jnp.where(work == m, lane, float(LANES)), axis=-1, keepdims=True)
        chosen = lane == idx
        code = jnp.where(chosen, float(k + 1), code)
        work = jnp.where(chosen, -jnp.inf, work)
        vals.append(m)
    exps = [jnp.exp(v - vals[0]) for v in vals]
    denom = exps[0] + exps[1] + exps[2] + exps[3]
    gates = jnp.zeros((tm, LANES), F32)
    for k in range(TOP_K):
        gates = jnp.where(lane == float(k), exps[k] / denom, gates)
    code_ref[...] = code
    gates_ref[...] = gates
    cnt_ref[...] += jnp.sum((code > 0).astype(F32), axis=0, keepdims=True)


def _pool_bands():
    r = np.arange(POOL_BLOCK)[:, None]
    s = np.arange(POOL_BLOCK + POOL_HISTORY)[None, :] - POOL_HISTORY
    bands = [((s <= r) & (s > r - w)) for w in POOL_WINDOWS]
    return jnp.asarray(np.stack(bands).astype(np.float32), dtype=BF16)


def _mix_router(x2, ret, u, pool_w_bf, pool_scale, w_out_bf, norm_w, rw_split, rb, seq, tile0):
    t = ret.shape[0]
    tm = TOKEN_TILE
    hist_blocks = tm // POOL_HISTORY
    tile = lambda width: pl.BlockSpec((tm, width), lambda i: (i, 0))
    const2 = lambda shape: pl.BlockSpec(shape, lambda i: (0, 0))
    const3 = lambda shape: pl.BlockSpec(shape, lambda i: (0, 0, 0))
    return pl.pallas_call(
        functools.partial(_mix_router_kernel, seq),
        grid=(t // tm,),
        in_specs=[
            pl.BlockSpec((tm, D_MODEL), lambda i: (i + tile0, 0)), tile(D_RET), tile(D_POOL),
            pl.BlockSpec((POOL_HISTORY, D_POOL), lambda i: (jnp.maximum(i * hist_blocks - 1, 0), 0)),
            const3((len(POOL_WINDOWS), POOL_BLOCK, POOL_BLOCK + POOL_HISTORY)),
            const3((len(POOL_WINDOWS), POOL_GROUP_DIM, POOL_GROUP_DIM)),
            const2((1, D_POOL)), const2((D_MODEL, D_MODEL)), const2((1, D_MODEL)),
            const2((D_MODEL, 2 * LANES)), const2((1, LANES)),
        ],
        out_specs=[tile(D_MODEL), tile(D_MODEL // 2), tile(LANES), tile(LANES), const2((1, LANES))],
        out_shape=[
            jax.ShapeDtypeStruct((t, D_MODEL), F32),
            jax.ShapeDtypeStruct((t, D_MODEL // 2), U32),
            jax.ShapeDtypeStruct((t, LANES), F32),
            jax.ShapeDtypeStruct((t, LANES), F32),
            jax.ShapeDtypeStruct((1, LANES), F32),
        ],
        scratch_shapes=[pltpu.VMEM((tm + POOL_HISTORY, D_POOL), BF16)],
        compiler_params=_params(("arbitrary",)),
        name="mix_router",
    )(x2, ret, u, u, _pool_bands(), pool_w_bf, pool_scale, w_out_bf, norm_w, rw_split, rb)


def _positions_kernel(code_ref, off_ref, tri_ref, pos_ref, carry_ref):
    @pl.when(pl.program_id(0) == 0)
    def _():
        carry_ref[...] = jnp.zeros_like(carry_ref)

    code = code_ref[...]
    tm = code.shape[0]
    sel = (code > 0).astype(BF16)
    carry = carry_ref[...]
    rank = _dot(tri_ref[...], sel) + (carry + off_ref[...])
    carry_ref[...] = carry + jnp.sum(sel.astype(F32), axis=0, keepdims=True)
    lane = lax.broadcasted_iota(jnp.int32, (tm, LANES), 1)
    pos = jnp.zeros((tm, LANES), F32)
    for k in range(TOP_K):
        pk = jnp.sum(jnp.where(code == float(k + 1), rank, 0.0), axis=-1, keepdims=True)
        pos = jnp.where(lane == k, pk, pos)
    pos_ref[...] = pos.T[0:POS_ROWS, :].astype(jnp.int32)


def _positions(code, offsets):
    t = code.shape[0]
    tm = POSITIONS_TILE
    tri = jnp.asarray(np.tril(np.ones((tm, tm), np.float32), -1), dtype=BF16)
    return pl.pallas_call(
        _positions_kernel,
        grid=(t // tm,),
        in_specs=[pl.BlockSpec((tm, LANES), lambda i: (i, 0)),
                  pl.BlockSpec((1, LANES), lambda i: (0, 0)),
                  pl.BlockSpec((tm, tm), lambda i: (0, 0))],
        out_specs=pl.BlockSpec((POS_ROWS, tm), lambda i: (0, i)),
        out_shape=jax.ShapeDtypeStruct((POS_ROWS, t), jnp.int32),
        scratch_shapes=[pltpu.VMEM((1, LANES), F32)],
        compiler_params=_params(("arbitrary",)),
        name="positions",
    )(code, offsets, tri)


def _sc_mesh():
    return plsc.VectorSubcoreMesh(core_axis_name="core", subcore_axis_name="subcore")


def _sc_workers():
    info = plsc.get_sparse_core_info()
    return info.num_cores, info.num_cores * info.num_subcores


def _sc_gather(src, idx):
    n = idx.shape[0]
    d = src.shape[1]
    w = SC_WINDOW // 2
    num_cores, workers = _sc_workers()
    per = n // workers
    pairs = per // (2 * w)
    assert per * workers == n and pairs * 2 * w == per

    @functools.partial(
        pl.kernel, out_type=jax.ShapeDtypeStruct((n, d), src.dtype), mesh=_sc_mesh(),
        scratch_types=[pltpu.VMEM((w,), jnp.int32), pltpu.VMEM((w,), jnp.int32),
                       pltpu.VMEM((w, d), src.dtype), pltpu.VMEM((w, d), src.dtype),
                       pltpu.SemaphoreType.DMA, pltpu.SemaphoreType.DMA,
                       pltpu.SemaphoreType.DMA, pltpu.SemaphoreType.DMA],
        name="sc_gather")
    def gather(src_hbm, idx_hbm, out_hbm, idx0, idx1, rows0, rows1, g0, g1, w0, w1):
        first = (lax.axis_index("subcore") * num_cores + lax.axis_index("core")) * per

        def start_gather(win, idx_v, rows_v, sem):
            pltpu.sync_copy(idx_hbm.at[pl.ds(first + win * w, w)], idx_v)
            pltpu.async_copy(src_hbm.at[idx_v], rows_v, sem)

        def wait_gather(idx_v, rows_v, sem):
            pltpu.make_async_copy(src_hbm.at[idx_v], rows_v, sem).wait()

        def start_write(win, rows_v, sem):
            pltpu.async_copy(rows_v, out_hbm.at[pl.ds(first + win * w, w)], sem)

        def wait_write(rows_v, sem):
            pltpu.make_async_copy(rows_v, out_hbm.at[pl.ds(first, w)], sem).wait()

        start_gather(0, idx0, rows0, g0)

        @pl.loop(0, pairs)
        def _(j):
            even = 2 * j

            @pl.when(j > 0)
            def _():
                wait_write(rows1, w1)

            start_gather(even + 1, idx1, rows1, g1)
            wait_gather(idx0, rows0, g0)
            start_write(even, rows0, w0)
            wait_write(rows0, w0)

            @pl.when(j + 1 < pairs)
            def _():
                start_gather(even + 2, idx0, rows0, g0)

            wait_gather(idx1, rows1, g1)
            start_write(even + 1, rows1, w1)

        wait_write(rows1, w1)

    return gather(src, idx)


def _sc_scatter(src, idx):
    t, d = src.shape
    w = SC_WINDOW
    num_cores, workers = _sc_workers()
    per = t // workers
    assert idx.shape[0] >= TOP_K and idx.shape[1] == t and per * workers == t and per % w == 0

    @functools.partial(
        pl.kernel, out_type=jax.ShapeDtypeStruct((TOP_K * t, d), src.dtype), mesh=_sc_mesh(),
        scratch_types=[pltpu.VMEM((TOP_K, w), jnp.int32), pltpu.VMEM((w, d), src.dtype),
                       pltpu.SemaphoreType.DMA],
        name="sc_scatter")
    def scatter(src_hbm, idx_hbm, out_hbm, idx_v, rows_v, sem):
        first = (lax.axis_index("subcore") * num_cores + lax.axis_index("core")) * per

        @pl.loop(0, per // w)
        def _(j):
            base = first + j * w
            for k in range(TOP_K):
                pltpu.sync_copy(idx_hbm.at[k, pl.ds(base, w)], idx_v.at[k])
            pltpu.sync_copy(src_hbm.at[pl.ds(base, w)], rows_v)
            copies = [pltpu.async_copy(rows_v, out_hbm.at[idx_v.at[k]], sem) for k in range(TOP_K)]
            for c in copies:
                c.wait()

    return scatter(src, idx)


def _experts_kernel(items_ref, xs_ref, wgu_hbm, bg_ref, bu_ref, wdn_hbm, bd_ref, perm_ref, y_ref,
                    wgu_ref, wdn_ref, wg_ref, wu_ref, wd_ref, sems):
    i = pl.program_id(0)
    tm = xs_ref.shape[0]

    def weight_copies():
        e = items_ref[I_FETCH, i]
        return (pltpu.make_async_copy(wgu_hbm.at[e], wgu_ref, sems.at[0]),
                pltpu.make_async_copy(wdn_hbm.at[e], wdn_ref, sems.at[1]))

    @pl.when(items_ref[I_START, i] == 1)
    def _():
        for copy in weight_copies():
            copy.start()

    @pl.when(items_ref[I_PREP, i] == 1)
    def _():
        for copy in weight_copies():
            copy.wait()
        ps = items_ref[I_PREP_SLOT, i]
        perm = perm_ref[...]
        pair = 2 * LANES
        for c in range(2 * D_MODEL // pair):
            sel = _dot(wgu_ref[:, c * pair:(c + 1) * pair].astype(BF16), perm).astype(BF16)
            wg_ref[ps, :, c * LANES:(c + 1) * LANES] = sel[:, :LANES]
            wu_ref[ps, :, c * LANES:(c + 1) * LANES] = sel[:, LANES:]
        wd_ref[ps] = wdn_ref[...].astype(BF16)

    def expert_mlp(r0, r1):
        n = r1 - r0
        s = items_ref[I_SLOT, i]
        x_lo, x_hi = _unpack_bf16_pair(xs_ref[r0:r1, :])
        x = jnp.concatenate([x_lo.astype(BF16), x_hi.astype(BF16)], axis=1)
        acts = []
        chunks = [slice(c * EXPERT_COL_CHUNK, (c + 1) * EXPERT_COL_CHUNK)
                  for c in range(D_MODEL // EXPERT_COL_CHUNK)]
        for cs in chunks:
            gate = jnp.minimum(_dot(x, wg_ref[s, :, cs]) + bg_ref[:, cs], SWIGLU_LIMIT)
            up = jnp.clip(_dot(x, wu_ref[s, :, cs]) + bu_ref[:, cs], -SWIGLU_LIMIT, SWIGLU_LIMIT)
            acts.append(((up + 1.0) * (gate * _sigmoid(SWIGLU_ALPHA * gate))).astype(BF16))
        rows = items_ref[I_TILE, i] * tm + r0 + lax.broadcasted_iota(jnp.int32, (n, 1), 0)
        mine = (rows >= items_ref[I_LO, i]) & (rows < items_ref[I_HI, i])

        half, quarter = D_MODEL // 2, D_MODEL // 4
        for h in range(2):
            hs = slice(h * half, (h + 1) * half)
            yh = bd_ref[:, hs]
            for cs, act in zip(chunks, acts):
                yh = yh + _dot(act, wd_ref[s, cs, hs])
            cols = slice(h * quarter, (h + 1) * quarter)
            packed = _pack_bf16_pair(yh[:, :quarter], yh[:, quarter:])
            y_ref[r0:r1, cols] = jnp.where(mine, packed, y_ref[r0:r1, cols])

    @pl.when(items_ref[I_FIRST, i] == 1)
    def _():
        y_ref[...] = jnp.zeros(y_ref.shape, U32)

    for mode, (r0, r1) in ((MODE_FULL, (0, tm)), (MODE_LOWER, (0, tm // 2)), (MODE_UPPER, (tm // 2, tm))):
        pl.when(items_ref[I_MODE, i] == mode)(functools.partial(expert_mlp, r0, r1))


(I_TILE, I_EXPERT, I_LO, I_HI, I_FIRST, I_MODE, I_FETCH, I_START, I_PREP, I_PREP_SLOT, I_SLOT) = range(11)
ITEM_FIELDS = 16
(MODE_SKIP, MODE_FULL, MODE_LOWER, MODE_UPPER) = range(4)


def _plan_kernel(tm, n_tiles, cnt_ref, off_ref, items_ref):
    n = items_ref.shape[1]
    cnt_r = cnt_ref[...]
    sub = lax.broadcasted_iota(jnp.int32, (LANES, LANES), 0).astype(F32)
    lan = lax.broadcasted_iota(jnp.int32, (LANES, LANES), 1).astype(F32)
    big = float(4 * LANES)

    def col(row):
        return jnp.sum(jnp.where(lan == sub, row, 0.0), axis=1, keepdims=True)

    def row(column):
        return jnp.sum(jnp.where(lan == sub, column, 0.0), axis=0, keepdims=True)

    def prefix(r):
        return jnp.sum(jnp.where(lan <= sub, r, 0.0), axis=1, keepdims=True)

    cnt = col(cnt_r)
    ends = prefix(cnt_r)
    starts = ends - cnt
    off_ref[...] = row(starts)
    used = cnt > 0.0
    inv_tm = 1.0 / tm
    first_tile = jnp.floor(starts * inv_tm)
    last_tile = jnp.floor(jnp.maximum(ends - 1.0, 0.0) * inv_tm)
    per = jnp.where(used, last_tile - first_tile + 1.0, 0.0)
    item_end = prefix(row(per))
    item_start = item_end - per
    used_r = row(used.astype(F32))
    ordinal = prefix(used_r) - 1.0
    nxt = jnp.min(jnp.where((lan > sub) & (used_r > 0.0), lan, big), axis=1, keepdims=True)
    has_next = nxt < big
    e_col = sub[:, 0:1]
    nxt = jnp.where(has_next, nxt, e_col)
    e_first = jnp.min(jnp.where(used_r > 0.0, lan[0:1, :], big), axis=1, keepdims=True)
    e_last = jnp.max(jnp.where(used_r > 0.0, lan[0:1, :], -1.0), axis=1, keepdims=True)
    total = jnp.max(item_end, axis=0, keepdims=True)

    item = lax.broadcasted_iota(jnp.int32, (LANES, n), 1).astype(F32) - 1.0
    mine = ((item >= item_start) & (item < item_end)).astype(F32)
    pick = lambda column: jnp.sum(mine * column, axis=0, keepdims=True)
    item_r = item[0:1, :]
    valid = pick(jnp.ones_like(cnt))
    lead = item_r < 0.0
    past = item_r >= total
    new_tile = 1.0 - pick(((item == item_start) & (starts - first_tile * tm > 0.0)).astype(F32))

    def put(field, value):
        items_ref[field:field + 1, :] = value.astype(jnp.int32)

    tile = pick(first_tile - item_start) + item_r
    lo = pick(starts)
    hi = pick(ends)
    needs_lower = lo < tile * tm + 0.5 * tm
    needs_upper = hi > tile * tm + 0.5 * tm
    mode = jnp.where(needs_lower & needs_upper, float(MODE_FULL),
                     jnp.where(needs_lower, float(MODE_LOWER), float(MODE_UPPER)))
    put(I_TILE, jnp.where(past, float(n_tiles - 1), jnp.where(lead, 0.0, tile)))
    put(I_EXPERT, jnp.where(past, e_last, jnp.where(lead, e_first, pick(e_col))))
    put(I_LO, lo)
    put(I_HI, hi)
    put(I_FIRST, valid * new_tile)
    put(I_MODE, valid * mode)
    put(I_FETCH, jnp.where(past, e_last, jnp.where(lead, e_first, pick(nxt))))
    put(I_START, jnp.where(lead, 1.0, pick(((item == item_start) & has_next).astype(F32))))
    put(I_PREP, jnp.where(lead, 1.0, pick(((item == item_end - 1.0) & has_next).astype(F32))))
    put(I_PREP_SLOT, pick(ordinal + 1.0 - 2.0 * jnp.floor((ordinal + 1.0) * 0.5)))
    put(I_SLOT, pick(ordinal - 2.0 * jnp.floor(ordinal * 0.5)))
    for field in range(I_SLOT + 1, ITEM_FIELDS):
        put(field, jnp.zeros_like(valid))


def _plan(counts, n_rows, tm):
    n_tiles = n_rows // tm
    n_items = n_tiles + N_EXPERTS
    width = -(-n_items // LANES) * LANES
    offsets, items = pl.pallas_call(
        functools.partial(_plan_kernel, float(tm), n_tiles),
        out_shape=[jax.ShapeDtypeStruct((1, LANES), F32),
                   jax.ShapeDtypeStruct((ITEM_FIELDS, width), jnp.int32)],
        name="plan",
    )(counts)
    return offsets, items, n_items


def _experts(xs, items, n_items, w_gate_up, bg, bu, w_down, bd):
    n_rows = xs.shape[0]
    tm = EXPERT_TILE
    half = D_MODEL // 2
    j = np.arange(2 * LANES)
    perm = np.zeros((2 * LANES, 2 * LANES), np.float32)
    perm[j, np.where(j % 2 == 0, j // 2, LANES + j // 2)] = 1.0
    row_tile = pl.BlockSpec((tm, half), lambda i, items: (items[I_TILE, i], 0))
    by_expert = lambda *shape: pl.BlockSpec((None,) + shape, lambda i, items: (items[I_EXPERT, i], 0, 0))
    in_hbm = pl.BlockSpec(memory_space=pl.ANY)
    return pl.pallas_call(
        _experts_kernel,
        grid_spec=pltpu.PrefetchScalarGridSpec(
            num_scalar_prefetch=1,
            grid=(n_items,),
            in_specs=[row_tile, in_hbm, by_expert(1, D_MODEL), by_expert(1, D_MODEL), in_hbm,
                      by_expert(1, D_MODEL),
                      pl.BlockSpec((2 * LANES, 2 * LANES), lambda i, items: (0, 0))],
            out_specs=row_tile,
            scratch_shapes=[pltpu.VMEM((D_MODEL, 2 * D_MODEL), F32), pltpu.VMEM((D_MODEL, D_MODEL), F32)]
            + [pltpu.VMEM((2, D_MODEL, D_MODEL), BF16)] * 3 + [pltpu.SemaphoreType.DMA((2,))],
        ),
        out_shape=jax.ShapeDtypeStruct((n_rows, half), U32),
        compiler_params=_params(("arbitrary",)),
        name="experts",
    )(items, xs, w_gate_up, bg, bu, w_down, bd, jnp.asarray(perm, dtype=BF16))


def _tail_kernel(h1_ref, yu_ref, gates_ref, p_ref, nple_ref, wg_ref, wp_ref, nfin_ref, *rest):
    out_ref = rest[-1]
    gates = gates_ref[...]
    lo = None
    hi = None
    for k in range(TOP_K):
        gk = gates[:, k:k + 1]
        yl, yh = _unpack_bf16_pair(yu_ref[k])
        lo = gk * yl if lo is None else lo + gk * yl
        hi = gk * yh if hi is None else hi + gk * yh
    quarter = D_MODEL // 4
    h2 = h1_ref[...] + jnp.concatenate([lo[:, :quarter], hi[:, :quarter], lo[:, quarter:], hi[:, quarter:]],
                                       axis=1)
    hn = _rms(h2, nple_ref[...]).astype(BF16)
    gate = _sigmoid(_dot(hn, wg_ref[...]))
    h3 = h2 + gate * _dot(p_ref[...].astype(BF16), wp_ref[...])
    out_ref[...] = _rms(h3, nfin_ref[...])


def _tail(h1, yu, gates, p2, norm_ple_w, ple_gate_bf, ple_proj_bf, final_norm_w, local0, tile0, result):
    t = yu.shape[1]
    tm = TOKEN_TILE
    half = D_MODEL // 2
    tile = lambda width: pl.BlockSpec((tm, width), lambda i: (i + local0, 0))
    shifted = lambda width: pl.BlockSpec((tm, width), lambda i: (i + tile0 + local0, 0))
    const2 = lambda shape: pl.BlockSpec(shape, lambda i: (0, 0))
    in_specs = [tile(D_MODEL), pl.BlockSpec((TOP_K, tm, half), lambda i: (0, i, 0)),
                tile(LANES), shifted(PLE_DIM), const2((1, D_MODEL)),
                const2((D_MODEL, D_MODEL)), const2((PLE_DIM, D_MODEL)), const2((1, D_MODEL))]
    args = [h1, yu, gates, p2, norm_ple_w, ple_gate_bf, ple_proj_bf, final_norm_w]
    aliases = {}
    if result is not None:
        in_specs.append(pl.BlockSpec(memory_space=pl.ANY))
        args.append(result)
        aliases = {len(args) - 1: 0}
    return pl.pallas_call(
        _tail_kernel,
        grid=(t // tm,),
        in_specs=in_specs,
        out_specs=shifted(D_MODEL),
        out_shape=jax.ShapeDtypeStruct((p2.shape[0], D_MODEL), F32),
        input_output_aliases=aliases,
        compiler_params=_params(("parallel",)),
        name="tail",
    )(*args)


def kernel(x, p, positions, w_in, w_out, ret_gn_w, pool_w, pool_scale, norm_mix_w, norm_moe_w, router_w, router_b, expert_w_gate_up, expert_b_gate_up, expert_w_down, expert_b_down, norm_ple_w, ple_gate_w, ple_proj_w, final_norm_w):
    batch, seq, d = x.shape
    depth = w_in.shape[0]
    assert depth == 1 and d == D_MODEL and seq % TOKEN_TILE == 0
    assert seq % (RET_CHUNK * RET_STEP_CHUNKS) == 0
    groups = BATCH_GROUPS if batch % BATCH_GROUPS == 0 else 1
    t = batch * seq
    tg = t // groups
    n_rows = tg * TOP_K
    assert n_rows % EXPERT_TILE == 0
    row = lambda a: a.reshape(1, -1).astype(F32)
    l = 0

    x2 = x.reshape(t, d)
    pos2 = positions.reshape(t, 1)
    p2 = p[l].reshape(t, PLE_DIM)
    w_in_bf = w_in[l].astype(BF16)
    rw = jnp.pad(router_w[l].astype(F32), ((0, 0), (0, LANES - N_EXPERTS)))
    rw_hi = rw.astype(BF16)
    rw_lo = (rw - rw_hi.astype(F32)).astype(BF16)
    rw_split = jnp.concatenate([rw_hi, rw_lo], axis=1)
    rb = jnp.pad(router_b[l].astype(F32), (0, LANES - N_EXPERTS), constant_values=NEG_BIG).reshape(1, LANES)
    bgu = expert_b_gate_up[l].reshape(N_EXPERTS, 1, D_MODEL, 2).astype(F32)
    bdn = expert_b_down[l].reshape(N_EXPERTS, 1, D_MODEL).astype(F32)
    pool_w_bf, w_out_bf = pool_w[l].astype(BF16), w_out[l].astype(BF16)
    ple_gate_bf, ple_proj_bf = ple_gate_w[l].astype(BF16), ple_proj_w[l].astype(BF16)

    out = None
    for gi in range(groups):
        tile0 = gi * (tg // TOKEN_TILE)
        ret, u = _retention(x2, pos2, row(norm_mix_w[l]), w_in_bf, row(ret_gn_w[l]),
                            gi * (tg // (RET_CHUNK * RET_STEP_CHUNKS)), batch // groups, seq)
        h1, hn_packed, code, gates, counts = _mix_router(
            x2, ret, u, pool_w_bf, row(pool_scale[l]), w_out_bf, row(norm_moe_w[l]), rw_split, rb,
            seq, tile0)

        offsets, items, n_items = _plan(counts, n_rows, EXPERT_TILE)
        pos = _positions(code, offsets)
        xs = _sc_scatter(hn_packed, pos)
        y = _experts(xs, items, n_items, expert_w_gate_up[l], bgu[..., 0], bgu[..., 1],
                     expert_w_down[l], bdn)

        spans = LAST_GROUP_SPANS if gi == groups - 1 else 1
        span = tg // spans
        for si in range(spans):
            idx = pos[:TOP_K, si * span:(si + 1) * span].reshape(TOP_K * span)
            yu = _sc_gather(y, idx).reshape(TOP_K, span, d // 2)
            out = _tail(h1, yu, gates, p2, row(norm_ple_w[l]), ple_gate_bf, ple_proj_bf,
                        row(final_norm_w), si * (span // TOKEN_TILE), tile0, out)
    return out.reshape(batch, seq, d)
```

```python
import functools
import math

import numpy as np
import jax
import jax.numpy as jnp
from jax import lax
from jax.experimental import pallas as pl
from jax.experimental.pallas import tpu as pltpu
from jax.experimental.pallas import tpu_sc as plsc

D_MODEL = 1024
D_RET = 512
D_POOL = 512
RET_HEADS = 8
RET_HEAD_DIM = 64
HEAD_PAIRS = RET_HEADS // 2
ROPE_BASE = 10000.0
POOL_WINDOWS = (2, 4, 8, 16)
POOL_GROUP_DIM = 128
POOL_HISTORY = 16
POOL_BLOCK = 128
D_IN_PROJ = 4 * D_RET + D_POOL
N_EXPERTS = 32
TOP_K = 4
SWIGLU_LIMIT = 7.0
SWIGLU_ALPHA = 1.702
PLE_DIM = 256
NORM_EPS = 1e-5
GN_EPS = 1e-5

LANES = 128
NEG_BIG = -1e30

TOKEN_TILE = 512
POSITIONS_TILE = 1024
POS_ROWS = 8
RET_CHUNK = 256
RET_STEP_CHUNKS = 4
EXPERT_TILE = 512
EXPERT_COL_CHUNK = 512
SC_WINDOW = 128
GROUP_BATCHES = (6, 2)
LAST_GROUP_SPANS = 2
VMEM_LIMIT = 56 * 1024 * 1024

F32 = jnp.float32
BF16 = jnp.bfloat16
U32 = jnp.uint32


def _params(semantics):
    return pltpu.CompilerParams(dimension_semantics=semantics, vmem_limit_bytes=VMEM_LIMIT)


def _dot(a, b):
    return jnp.dot(a, b, preferred_element_type=F32)


def _dot_nt(a, b):
    return lax.dot_general(a, b, (((1,), (1,)), ((), ())), preferred_element_type=F32)


def _dot_tn(a, b):
    return lax.dot_general(a, b, (((0,), (0,)), ((), ())), preferred_element_type=F32)


def _rms(x, w):
    ms = jnp.mean(x * x, axis=-1, keepdims=True)
    return x * lax.rsqrt(ms + NORM_EPS) * w


def _sigmoid(z):
    return 1.0 / (1.0 + jnp.exp(-z))


def _pack_bf16_pair(lo, hi):
    lo_bits = pltpu.bitcast(lo.astype(BF16).astype(F32), U32) >> 16
    hi_bits = pltpu.bitcast(hi.astype(BF16).astype(F32), U32) & jnp.uint32(0xFFFF0000)
    return lo_bits | hi_bits


def _unpack_bf16_pair(packed):
    lo = pltpu.bitcast(packed << 16, F32)
    hi = pltpu.bitcast(packed & jnp.uint32(0xFFFF0000), F32)
    return lo, hi


def _project_rows(x, pos, nw_ref, w_ref, freq_ref, qkvg_ref, u_ref, rows):
    hn = _rms(x, nw_ref[...]).astype(BF16)
    ang = pos.astype(F32) * freq_ref[0:1, :]
    cos = jnp.cos(ang)
    sin = jnp.sin(ang)
    slab = 2 * LANES
    cos_t = jnp.concatenate([cos, cos], axis=1)
    sin_up = jnp.concatenate([sin * freq_ref[1:2, :]] * 2, axis=1)
    sin_dn = jnp.concatenate([sin * freq_ref[2:3, :]] * 2, axis=1)
    half = RET_HEAD_DIM // 2

    for s in range(2 * D_RET // slab):
        sl = slice(s * slab, (s + 1) * slab)
        v = _dot(hn, w_ref[:, sl])
        v = v * cos_t + pltpu.roll(v, slab - half, 1) * sin_up + pltpu.roll(v, half, 1) * sin_dn
        if s >= D_RET // slab:
            v = v * (RET_HEAD_DIM ** -0.5)
        qkvg_ref[rows, sl] = v.astype(BF16)
    for s in range(2 * D_RET // 512, 4 * D_RET // 512):
        sl = slice(s * 512, (s + 1) * 512)
        qkvg_ref[rows, sl] = _dot(hn, w_ref[:, sl]).astype(BF16)
    u_ref[rows, :] = _dot(hn, w_ref[:, 4 * D_RET:]).astype(BF16)


def _rope_table():
    j = np.arange(LANES)
    half = RET_HEAD_DIM // 2
    inv_freq = ROPE_BASE ** (-(np.arange(half, dtype=np.float32)) / half)
    freq = np.zeros((8, LANES), np.float32)
    freq[0] = inv_freq[j % half]
    freq[1] = np.where(j % RET_HEAD_DIM < half, -1.0, 0.0)
    freq[2] = np.where(j % RET_HEAD_DIM >= half, 1.0, 0.0)
    return jnp.asarray(freq)


def _retention_kernel(x_ref, pos_ref, nw_ref, w_ref, freq_ref, dec_ref, xi_ref, zeta_ref, cd_ref, bd_ref,
                      m64_ref, eye_ref, gnw_ref, out_ref, u_ref, state_ref, qkvg_ref):
    @pl.when(pl.program_id(1) == 0)
    def _():
        state_ref[...] = jnp.zeros_like(state_ref)

    c = dec_ref.shape[1]
    n_chunks = x_ref.shape[0] // c
    lane = lax.broadcasted_iota(jnp.int32, (1, LANES), 1)
    m64 = m64_ref[...]
    bd = bd_ref[...]
    eye = eye_ref[...]

    def group_mean(v):
        hi = v.astype(BF16)
        lo = (v - hi.astype(F32)).astype(BF16)
        return _dot(jnp.concatenate([hi, lo], axis=1), m64)

    for ci in range(n_chunks):
        rows = slice(ci * c, (ci + 1) * c)
        _project_rows(x_ref[rows, :], pos_ref[rows, :], nw_ref, w_ref, freq_ref, qkvg_ref, u_ref, rows)
        for p in range(HEAD_PAIRS):
            sl = slice(p * LANES, (p + 1) * LANES)
            qp = qkvg_ref[rows, sl]
            kp = qkvg_ref[rows, D_RET + p * LANES:D_RET + (p + 1) * LANES]
            vp = qkvg_ref[rows, 2 * D_RET + p * LANES:2 * D_RET + (p + 1) * LANES]
            y = None
            for hh in range(2):
                in_head = (lane >= RET_HEAD_DIM) == bool(hh)
                qm = jnp.where(in_head, qp, jnp.zeros_like(qp))
                vm = jnp.where(in_head, vp, jnp.zeros_like(vp))
                scores = _dot_nt(qm, kp) * dec_ref[2 * p + hh]
                part = _dot(scores.astype(BF16), vm)
                y = part if y is None else y + part
            st = state_ref[p]
            y = y + _dot((qp.astype(F32) * xi_ref[p]).astype(BF16), st.astype(BF16))
            kz = (kp.astype(F32) * zeta_ref[p]).astype(BF16)
            kz_t = _dot_nt(eye, kz).astype(BF16)
            state_ref[p] = cd_ref[p] * st + _dot(kz_t, vp) * bd
            mu = group_mean(y)
            var = group_mean(y * y) - mu * mu
            yn = (y - mu) * lax.rsqrt(var + GN_EPS) * gnw_ref[:, sl]
            g = qkvg_ref[rows, 3 * D_RET + p * LANES:3 * D_RET + (p + 1) * LANES].astype(F32)
            out_ref[rows, sl] = (yn * g * _sigmoid(g)).astype(BF16)


def _retention_tables(c):
    h = np.arange(RET_HEADS, dtype=np.float64)
    log_gamma = np.log1p(-np.power(2.0, -5.0 - h))
    idx = np.arange(c, dtype=np.float64)
    rel = idx[:, None] - idx[None, :]
    dec = np.where(rel >= 0, np.exp(np.where(rel >= 0, rel, 0.0)[None] * log_gamma[:, None, None]), 0.0)
    lane_head = np.arange(LANES) // RET_HEAD_DIM
    xi = np.zeros((HEAD_PAIRS, c, LANES))
    zeta = np.zeros((HEAD_PAIRS, c, LANES))
    cd = np.zeros((HEAD_PAIRS, LANES, LANES))
    same = lane_head[:, None] == lane_head[None, :]
    for p in range(HEAD_PAIRS):
        lg = log_gamma[2 * p + lane_head]
        xi[p] = np.exp((idx + 1.0)[:, None] * lg[None, :])
        zeta[p] = np.exp((c - 1 - idx)[:, None] * lg[None, :])
        cd[p] = np.where(same, np.exp(c * lg)[:, None], 0.0)
    bd = same.astype(np.float32)
    m64 = np.concatenate([same, same], axis=0).astype(np.float32) / RET_HEAD_DIM
    f = lambda a: jnp.asarray(a, dtype=F32)
    return (f(dec), f(xi), f(zeta), f(cd), f(bd), jnp.asarray(m64, dtype=BF16),
            jnp.asarray(np.eye(LANES, dtype=np.float32), dtype=BF16))


def _retention(x2, pos2, norm_w, w_in_bf, gn_w, block0, batch, seq):
    c = RET_CHUNK
    rows = RET_STEP_CHUNKS * c
    n = seq // rows
    t = batch * seq
    dec, xi, zeta, cd, bd, m64, eye = _retention_tables(c)
    const3 = lambda shape: pl.BlockSpec(shape, lambda b, i: (0, 0, 0))
    const2 = lambda shape: pl.BlockSpec(shape, lambda b, i: (0, 0))
    out_tile = pl.BlockSpec((rows, D_RET), lambda b, i: (b * n + i, 0))
    return pl.pallas_call(
        _retention_kernel,
        grid=(batch, n),
        in_specs=[pl.BlockSpec((rows, D_MODEL), lambda b, i: (block0 + b * n + i, 0)),
                  pl.BlockSpec((rows, 1), lambda b, i: (block0 + b * n + i, 0)),
                  const2((1, D_MODEL)), const2((D_MODEL, D_IN_PROJ)), const2((8, LANES)),
                  const3((RET_HEADS, c, c)), const3((HEAD_PAIRS, c, LANES)),
                  const3((HEAD_PAIRS, c, LANES)), const3((HEAD_PAIRS, LANES, LANES)),
                  const2((LANES, LANES)), const2((2 * LANES, LANES)), const2((LANES, LANES)),
                  const2((1, D_RET))],
        out_specs=[out_tile, out_tile],
        out_shape=[jax.ShapeDtypeStruct((t, D_RET), BF16), jax.ShapeDtypeStruct((t, D_POOL), BF16)],
        scratch_shapes=[pltpu.VMEM((HEAD_PAIRS, LANES, LANES), F32), pltpu.VMEM((rows, 4 * D_RET), BF16)],
        compiler_params=_params(("arbitrary", "arbitrary")),
        name="retention",
    )(x2, pos2, norm_w, w_in_bf, _rope_table(), dec, xi, zeta, cd, bd, m64, eye, gn_w)


def _mix_router_kernel(seq, x_ref, ret_ref, u_ref, uprev_ref, band_ref, pw_ref, ps_ref, wout_ref,
                       nw_ref, rw_ref, rb_ref,
                       h1_ref, hnp_ref, code_ref, gates_ref, cnt_ref, uext_ref):
    i = pl.program_id(0)
    tm = x_ref.shape[0]
    t0 = lax.rem(i * tm, seq)

    @pl.when(i == 0)
    def _():
        cnt_ref[...] = jnp.zeros_like(cnt_ref)

    prev = uprev_ref[...]
    uext_ref[0:POOL_HISTORY, :] = jnp.where(t0 == 0, jnp.zeros_like(prev), prev)
    uext_ref[POOL_HISTORY:, :] = u_ref[...]

    row = lax.broadcasted_iota(jnp.int32, (tm, 1), 0)
    t_seq = (t0 + row + 1).astype(F32)
    mixed = []
    for gi, w in enumerate(POOL_WINDOWS):
        sl = slice(gi * POOL_GROUP_DIM, (gi + 1) * POOL_GROUP_DIM)
        wsum = jnp.concatenate(
            [_dot(band_ref[gi], uext_ref[r0:r0 + POOL_BLOCK + POOL_HISTORY, sl])
             for r0 in range(0, tm, POOL_BLOCK)], axis=0)
        count = jnp.minimum(t_seq, float(w))
        pooled = wsum / count - u_ref[:, sl].astype(F32)
        mixed.append(_dot(pooled.astype(BF16), pw_ref[gi]))
    pool = (jnp.concatenate(mixed, axis=1) * ps_ref[...]).astype(BF16)

    h1 = (x_ref[...] + _dot(ret_ref[...], wout_ref[0:D_RET, :])
          + _dot(pool, wout_ref[D_RET:, :]))
    h1_ref[...] = h1
    hn = _rms(h1, nw_ref[...])
    hn_hi = hn.astype(BF16)
    half = D_MODEL // 2
    hnp_ref[...] = _pack_bf16_pair(hn[:, :half], hn[:, half:])

    hn_lo = (hn - hn_hi.astype(F32)).astype(BF16)
    both = _dot(hn_hi, rw_ref[...])
    logits = (both[:, :LANES] + both[:, LANES:] + _dot(hn_lo, rw_ref[:, :LANES])
              + rb_ref[...])

    lane = lax.broadcasted_iota(jnp.int32, (tm, LANES), 1).astype(F32)
    code = jnp.zeros((tm, LANES), F32)
    vals = []
    work = logits
    for k in range(TOP_K):
        m = jnp.max(work, axis=-1, keepdims=True)
        idx = jnp.min(jnp.where(work == m, lane, float(LANES)), axis=-1, keepdims=True)
        chosen = lane == idx
        code = jnp.where(chosen, float(k + 1), code)
        work = jnp.where(chosen, -jnp.inf, work)
        vals.append(m)
    exps = [jnp.exp(v - vals[0]) for v in vals]
    denom = exps[0] + exps[1] + exps[2] + exps[3]
    gates = jnp.zeros((tm, LANES), F32)
    for k in range(TOP_K):
        gates = jnp.where(lane == float(k), exps[k] / denom, gates)
    code_ref[...] = code
    gates_ref[...] = gates
    cnt_ref[...] += jnp.sum((code > 0).astype(F32), axis=0, keepdims=True)


def _pool_bands():
    r = np.arange(POOL_BLOCK)[:, None]
    s = np.arange(POOL_BLOCK + POOL_HISTORY)[None, :] - POOL_HISTORY
    bands = [((s <= r) & (s > r - w)) for w in POOL_WINDOWS]
    return jnp.asarray(np.stack(bands).astype(np.float32), dtype=BF16)


def _mix_router(x2, ret, u, pool_w_bf, pool_scale, w_out_bf, norm_w, rw_split, rb, seq, tile0):
    t = ret.shape[0]
    tm = TOKEN_TILE
    hist_blocks = tm // POOL_HISTORY
    tile = lambda width: pl.BlockSpec((tm, width), lambda i: (i, 0))
    const2 = lambda shape: pl.BlockSpec(shape, lambda i: (0, 0))
    const3 = lambda shape: pl.BlockSpec(shape, lambda i: (0, 0, 0))
    return pl.pallas_call(
        functools.partial(_mix_router_kernel, seq),
        grid=(t // tm,),
        in_specs=[
            pl.BlockSpec((tm, D_MODEL), lambda i: (i + tile0, 0)), tile(D_RET), tile(D_POOL),
            pl.BlockSpec((POOL_HISTORY, D_POOL), lambda i: (jnp.maximum(i * hist_blocks - 1, 0), 0)),
            const3((len(POOL_WINDOWS), POOL_BLOCK, POOL_BLOCK + POOL_HISTORY)),
            const3((len(POOL_WINDOWS), POOL_GROUP_DIM, POOL_GROUP_DIM)),
            const2((1, D_POOL)), const2((D_MODEL, D_MODEL)), const2((1, D_MODEL)),
            const2((D_MODEL, 2 * LANES)), const2((1, LANES)),
        ],
        out_specs=[tile(D_MODEL), tile(D_MODEL // 2), tile(LANES), tile(LANES), const2((1, LANES))],
        out_shape=[
            jax.ShapeDtypeStruct((t, D_MODEL), F32),
            jax.ShapeDtypeStruct((t, D_MODEL // 2), U32),
            jax.ShapeDtypeStruct((t, LANES), F32),
            jax.ShapeDtypeStruct((t, LANES), F32),
            jax.ShapeDtypeStruct((1, LANES), F32),
        ],
        scratch_shapes=[pltpu.VMEM((tm + POOL_HISTORY, D_POOL), BF16)],
        compiler_params=_params(("arbitrary",)),
        name="mix_router",
    )(x2, ret, u, u, _pool_bands(), pool_w_bf, pool_scale, w_out_bf, norm_w, rw_split, rb)


def _positions_kernel(code_ref, off_ref, tri_ref, pos_ref, carry_ref):
    @pl.when(pl.program_id(0) == 0)
    def _():
        carry_ref[...] = jnp.zeros_like(carry_ref)

    code = code_ref[...]
    tm = code.shape[0]
    sel = (code > 0).astype(BF16)
    carry = carry_ref[...]
    rank = _dot(tri_ref[...], sel) + (carry + off_ref[...])
    carry_ref[...] = carry + jnp.sum(sel.astype(F32), axis=0, keepdims=True)
    lane = lax.broadcasted_iota(jnp.int32, (tm, LANES), 1)
    pos = jnp.zeros((tm, LANES), F32)
    for k in range(TOP_K):
        pk = jnp.sum(jnp.where(code == float(k + 1), rank, 0.0), axis=-1, keepdims=True)
        pos = jnp.where(lane == k, pk, pos)
    pos_ref[...] = pos.T[0:POS_ROWS, :].astype(jnp.int32)


def _positions(code, offsets):
    t = code.shape[0]
    tm = POSITIONS_TILE
    tri = jnp.asarray(np.tril(np.ones((tm, tm), np.float32), -1), dtype=BF16)
    return pl.pallas_call(
        _positions_kernel,
        grid=(t // tm,),
        in_specs=[pl.BlockSpec((tm, LANES), lambda i: (i, 0)),
                  pl.BlockSpec((1, LANES), lambda i: (0, 0)),
                  pl.BlockSpec((tm, tm), lambda i: (0, 0))],
        out_specs=pl.BlockSpec((POS_ROWS, tm), lambda i: (0, i)),
        out_shape=jax.ShapeDtypeStruct((POS_ROWS, t), jnp.int32),
        scratch_shapes=[pltpu.VMEM((1, LANES), F32)],
        compiler_params=_params(("arbitrary",)),
        name="positions",
    )(code, offsets, tri)


def _sc_mesh():
    return plsc.VectorSubcoreMesh(core_axis_name="core", subcore_axis_name="subcore")


def _sc_workers():
    info = plsc.get_sparse_core_info()
    return info.num_cores, info.num_cores * info.num_subcores


def _sc_gather(src, idx):
    n = idx.shape[0]
    d = src.shape[1]
    w = SC_WINDOW // 2
    num_cores, workers = _sc_workers()
    per = n // workers
    pairs = per // (2 * w)
    assert per * workers == n and pairs * 2 * w == per

    @functools.partial(
        pl.kernel, out_type=jax.ShapeDtypeStruct((n, d), src.dtype), mesh=_sc_mesh(),
        scratch_types=[pltpu.VMEM((w,), jnp.int32), pltpu.VMEM((w,), jnp.int32),
                       pltpu.VMEM((w, d), src.dtype), pltpu.VMEM((w, d), src.dtype),
                       pltpu.SemaphoreType.DMA, pltpu.SemaphoreType.DMA,
                       pltpu.SemaphoreType.DMA, pltpu.SemaphoreType.DMA],
        name="sc_gather")
    def gather(src_hbm, idx_hbm, out_hbm, idx0, idx1, rows0, rows1, g0, g1, w0, w1):
        first = (lax.axis_index("subcore") * num_cores + lax.axis_index("core")) * per

        def start_gather(win, idx_v, rows_v, sem):
            pltpu.sync_copy(idx_hbm.at[pl.ds(first + win * w, w)], idx_v)
            pltpu.async_copy(src_hbm.at[idx_v], rows_v, sem)

        def wait_gather(idx_v, rows_v, sem):
            pltpu.make_async_copy(src_hbm.at[idx_v], rows_v, sem).wait()

        def start_write(win, rows_v, sem):
            pltpu.async_copy(rows_v, out_hbm.at[pl.ds(first + win * w, w)], sem)

        def wait_write(rows_v, sem):
            pltpu.make_async_copy(rows_v, out_hbm.at[pl.ds(first, w)], sem).wait()

        start_gather(0, idx0, rows0, g0)

        @pl.loop(0, pairs)
        def _(j):
            even = 2 * j

            @pl.when(j > 0)
            def _():
                wait_write(rows1, w1)

            start_gather(even + 1, idx1, rows1, g1)
            wait_gather(idx0, rows0, g0)
            start_write(even, rows0, w0)
            wait_write(rows0, w0)

            @pl.when(j + 1 < pairs)
            def _():
                start_gather(even + 2, idx0, rows0, g0)

            wait_gather(idx1, rows1, g1)
            start_write(even + 1, rows1, w1)

        wait_write(rows1, w1)

    return gather(src, idx)


def _sc_scatter(src, idx):
    t, d = src.shape
    w = SC_WINDOW
    num_cores, workers = _sc_workers()
    per = t // workers
    assert idx.shape[0] >= TOP_K and idx.shape[1] == t and per * workers == t and per % w == 0

    @functools.partial(
        pl.kernel, out_type=jax.ShapeDtypeStruct((TOP_K * t, d), src.dtype), mesh=_sc_mesh(),
        scratch_types=[pltpu.VMEM((TOP_K, w), jnp.int32), pltpu.VMEM((w, d), src.dtype),
                       pltpu.SemaphoreType.DMA],
        name="sc_scatter")
    def scatter(src_hbm, idx_hbm, out_hbm, idx_v, rows_v, sem):
        first = (lax.axis_index("subcore") * num_cores + lax.axis_index("core")) * per

        @pl.loop(0, per // w)
        def _(j):
            base = first + j * w
            for k in range(TOP_K):
                pltpu.sync_copy(idx_hbm.at[k, pl.ds(base, w)], idx_v.at[k])
            pltpu.sync_copy(src_hbm.at[pl.ds(base, w)], rows_v)
            copies = [pltpu.async_copy(rows_v, out_hbm.at[idx_v.at[k]], sem) for k in range(TOP_K)]
            for c in copies:
                c.wait()

    return scatter(src, idx)


def _experts_kernel(items_ref, xs_ref, wgu_hbm, bg_ref, bu_ref, wdn_hbm, bd_ref, perm_ref, y_ref,
                    wgu_ref, wdn_ref, wg_ref, wu_ref, wd_ref, sems):
    i = pl.program_id(0)
    tm = xs_ref.shape[0]

    def weight_copies():
        e = items_ref[I_FETCH, i]
        return (pltpu.make_async_copy(wgu_hbm.at[e], wgu_ref, sems.at[0]),
                pltpu.make_async_copy(wdn_hbm.at[e], wdn_ref, sems.at[1]))

    @pl.when(items_ref[I_START, i] == 1)
    def _():
        for copy in weight_copies():
            copy.start()

    @pl.when(items_ref[I_PREP, i] == 1)
    def _():
        for copy in weight_copies():
            copy.wait()
        ps = items_ref[I_PREP_SLOT, i]
        perm = perm_ref[...]
        pair = 2 * LANES
        for c in range(2 * D_MODEL // pair):
            sel = _dot(wgu_ref[:, c * pair:(c + 1) * pair].astype(BF16), perm).astype(BF16)
            wg_ref[ps, :, c * LANES:(c + 1) * LANES] = sel[:, :LANES]
            wu_ref[ps, :, c * LANES:(c + 1) * LANES] = sel[:, LANES:]
        wd_ref[ps] = wdn_ref[...].astype(BF16)

    def expert_mlp(r0, r1):
        n = r1 - r0
        s = items_ref[I_SLOT, i]
        x_lo, x_hi = _unpack_bf16_pair(xs_ref[r0:r1, :])
        x = jnp.concatenate([x_lo.astype(BF16), x_hi.astype(BF16)], axis=1)
        acts = []
        chunks = [slice(c * EXPERT_COL_CHUNK, (c + 1) * EXPERT_COL_CHUNK)
                  for c in range(D_MODEL // EXPERT_COL_CHUNK)]
        for cs in chunks:
            gate = jnp.minimum(_dot(x, wg_ref[s, :, cs]) + bg_ref[:, cs], SWIGLU_LIMIT)
            up = jnp.clip(_dot(x, wu_ref[s, :, cs]) + bu_ref[:, cs], -SWIGLU_LIMIT, SWIGLU_LIMIT)
            acts.append(((up + 1.0) * (gate * _sigmoid(SWIGLU_ALPHA * gate))).astype(BF16))
        rows = items_ref[I_TILE, i] * tm + r0 + lax.broadcasted_iota(jnp.int32, (n, 1), 0)
        mine = (rows >= items_ref[I_LO, i]) & (rows < items_ref[I_HI, i])

        half, quarter = D_MODEL // 2, D_MODEL // 4
        for h in range(2):
            hs = slice(h * half, (h + 1) * half)
            yh = bd_ref[:, hs]
            for cs, act in zip(chunks, acts):
                yh = yh + _dot(act, wd_ref[s, cs, hs])
            cols = slice(h * quarter, (h + 1) * quarter)
            packed = _pack_bf16_pair(yh[:, :quarter], yh[:, quarter:])
            y_ref[r0:r1, cols] = jnp.where(mine, packed, y_ref[r0:r1, cols])

    @pl.when(items_ref[I_FIRST, i] == 1)
    def _():
        y_ref[...] = jnp.zeros(y_ref.shape, U32)

    for mode, (r0, r1) in ((MODE_FULL, (0, tm)), (MODE_LOWER, (0, tm // 2)), (MODE_UPPER, (tm // 2, tm))):
        pl.when(items_ref[I_MODE, i] == mode)(functools.partial(expert_mlp, r0, r1))


(I_TILE, I_EXPERT, I_LO, I_HI, I_FIRST, I_MODE, I_FETCH, I_START, I_PREP, I_PREP_SLOT, I_SLOT) = range(11)
ITEM_FIELDS = 16
(MODE_SKIP, MODE_FULL, MODE_LOWER, MODE_UPPER) = range(4)


def _plan_kernel(tm, n_tiles, cnt_ref, off_ref, items_ref):
    n = items_ref.shape[1]
    cnt_r = cnt_ref[...]
    sub = lax.broadcasted_iota(jnp.int32, (LANES, LANES), 0).astype(F32)
    lan = lax.broadcasted_iota(jnp.int32, (LANES, LANES), 1).astype(F32)
    big = float(4 * LANES)

    def col(row):
        return jnp.sum(jnp.where(lan == sub, row, 0.0), axis=1, keepdims=True)

    def row(column):
        return jnp.sum(jnp.where(lan == sub, column, 0.0), axis=0, keepdims=True)

    def prefix(r):
        return jnp.sum(jnp.where(lan <= sub, r, 0.0), axis=1, keepdims=True)

    cnt = col(cnt_r)
    ends = prefix(cnt_r)
    starts = ends - cnt
    off_ref[...] = row(starts)
    used = cnt > 0.0
    inv_tm = 1.0 / tm
    first_tile = jnp.floor(starts * inv_tm)
    last_tile = jnp.floor(jnp.maximum(ends - 1.0, 0.0) * inv_tm)
    per = jnp.where(used, last_tile - first_tile + 1.0, 0.0)
    item_end = prefix(row(per))
    item_start = item_end - per
    used_r = row(used.astype(F32))
    ordinal = prefix(used_r) - 1.0
    nxt = jnp.min(jnp.where((lan > sub) & (used_r > 0.0), lan, big), axis=1, keepdims=True)
    has_next = nxt < big
    e_col = sub[:, 0:1]
    nxt = jnp.where(has_next, nxt, e_col)
    e_first = jnp.min(jnp.where(used_r > 0.0, lan[0:1, :], big), axis=1, keepdims=True)
    e_last = jnp.max(jnp.where(used_r > 0.0, lan[0:1, :], -1.0), axis=1, keepdims=True)
    total = jnp.max(item_end, axis=0, keepdims=True)

    item = lax.broadcasted_iota(jnp.int32, (LANES, n), 1).astype(F32) - 1.0
    mine = ((item >= item_start) & (item < item_end)).astype(F32)
    pick = lambda column: jnp.sum(mine * column, axis=0, keepdims=True)
    item_r = item[0:1, :]
    valid = pick(jnp.ones_like(cnt))
    lead = item_r < 0.0
    past = item_r >= total
    new_tile = 1.0 - pick(((item == item_start) & (starts - first_tile * tm > 0.0)).astype(F32))

    def put(field, value):
        items_ref[field:field + 1, :] = value.astype(jnp.int32)

    tile = pick(first_tile - item_start) + item_r
    lo = pick(starts)
    hi = pick(ends)
    needs_lower = lo < tile * tm + 0.5 * tm
    needs_upper = hi > tile * tm + 0.5 * tm
    mode = jnp.where(needs_lower & needs_upper, float(MODE_FULL),
                     jnp.where(needs_lower, float(MODE_LOWER), float(MODE_UPPER)))
    put(I_TILE, jnp.where(past, float(n_tiles - 1), jnp.where(lead, 0.0, tile)))
    put(I_EXPERT, jnp.where(past, e_last, jnp.where(lead, e_first, pick(e_col))))
    put(I_LO, lo)
    put(I_HI, hi)
    put(I_FIRST, valid * new_tile)
    put(I_MODE, valid * mode)
    put(I_FETCH, jnp.where(past, e_last, jnp.where(lead, e_first, pick(nxt))))
    put(I_START, jnp.where(lead, 1.0, pick(((item == item_start) & has_next).astype(F32))))
    put(I_PREP, jnp.where(lead, 1.0, pick(((item == item_end - 1.0) & has_next).astype(F32))))
    put(I_PREP_SLOT, pick(ordinal + 1.0 - 2.0 * jnp.floor((ordinal + 1.0) * 0.5)))
    put(I_SLOT, pick(ordinal - 2.0 * jnp.floor(ordinal * 0.5)))
    for field in range(I_SLOT + 1, ITEM_FIELDS):
        put(field, jnp.zeros_like(valid))


def _plan(counts, n_rows, tm):
    n_tiles = n_rows // tm
    n_items = n_tiles + N_EXPERTS
    width = -(-n_items // LANES) * LANES
    offsets, items = pl.pallas_call(
        functools.partial(_plan_kernel, float(tm), n_tiles),
        out_shape=[jax.ShapeDtypeStruct((1, LANES), F32),
                   jax.ShapeDtypeStruct((ITEM_FIELDS, width), jnp.int32)],
        name="plan",
    )(counts)
    return offsets, items, n_items


def _experts(xs, items, n_items, w_gate_up, bg, bu, w_down, bd):
    n_rows = xs.shape[0]
    tm = EXPERT_TILE
    half = D_MODEL // 2
    j = np.arange(2 * LANES)
    perm = np.zeros((2 * LANES, 2 * LANES), np.float32)
    perm[j, np.where(j % 2 == 0, j // 2, LANES + j // 2)] = 1.0
    row_tile = pl.BlockSpec((tm, half), lambda i, items: (items[I_TILE, i], 0))
    by_expert = lambda *shape: pl.BlockSpec((None,) + shape, lambda i, items: (items[I_EXPERT, i], 0, 0))
    in_hbm = pl.BlockSpec(memory_space=pl.ANY)
    return pl.pallas_call(
        _experts_kernel,
        grid_spec=pltpu.PrefetchScalarGridSpec(
            num_scalar_prefetch=1,
            grid=(n_items,),
            in_specs=[row_tile, in_hbm, by_expert(1, D_MODEL), by_expert(1, D_MODEL), in_hbm,
                      by_expert(1, D_MODEL),
                      pl.BlockSpec((2 * LANES, 2 * LANES), lambda i, items: (0, 0))],
            out_specs=row_tile,
            scratch_shapes=[pltpu.VMEM((D_MODEL, 2 * D_MODEL), F32), pltpu.VMEM((D_MODEL, D_MODEL), F32)]
            + [pltpu.VMEM((2, D_MODEL, D_MODEL), BF16)] * 3 + [pltpu.SemaphoreType.DMA((2,))],
        ),
        out_shape=jax.ShapeDtypeStruct((n_rows, half), U32),
        compiler_params=_params(("arbitrary",)),
        name="experts",
    )(items, xs, w_gate_up, bg, bu, w_down, bd, jnp.asarray(perm, dtype=BF16))


def _tail_kernel(h1_ref, yu_ref, gates_ref, p_ref, nple_ref, wg_ref, wp_ref, nfin_ref, *rest):
    out_ref = rest[-1]
    gates = gates_ref[...]
    lo = None
    hi = None
    for k in range(TOP_K):
        gk = gates[:, k:k + 1]
        yl, yh = _unpack_bf16_pair(yu_ref[k])
        lo = gk * yl if lo is None else lo + gk * yl
        hi = gk * yh if hi is None else hi + gk * yh
    quarter = D_MODEL // 4
    h2 = h1_ref[...] + jnp.concatenate([lo[:, :quarter], hi[:, :quarter], lo[:, quarter:], hi[:, quarter:]],
                                       axis=1)
    hn = _rms(h2, nple_ref[...]).astype(BF16)
    gate = _sigmoid(_dot(hn, wg_ref[...]))
    h3 = h2 + gate * _dot(p_ref[...].astype(BF16), wp_ref[...])
    out_ref[...] = _rms(h3, nfin_ref[...])


def _tail(h1, yu, gates, p2, norm_ple_w, ple_gate_bf, ple_proj_bf, final_norm_w, local0, tile0, result):
    t = yu.shape[1]
    tm = TOKEN_TILE
    half = D_MODEL // 2
    tile = lambda width: pl.BlockSpec((tm, width), lambda i: (i + local0, 0))
    shifted = lambda width: pl.BlockSpec((tm, width), lambda i: (i + tile0 + local0, 0))
    const2 = lambda shape: pl.BlockSpec(shape, lambda i: (0, 0))
    in_specs = [tile(D_MODEL), pl.BlockSpec((TOP_K, tm, half), lambda i: (0, i, 0)),
                tile(LANES), shifted(PLE_DIM), const2((1, D_MODEL)),
                const2((D_MODEL, D_MODEL)), const2((PLE_DIM, D_MODEL)), const2((1, D_MODEL))]
    args = [h1, yu, gates, p2, norm_ple_w, ple_gate_bf, ple_proj_bf, final_norm_w]
    aliases = {}
    if result is not None:
        in_specs.append(pl.BlockSpec(memory_space=pl.ANY))
        args.append(result)
        aliases = {len(args) - 1: 0}
    return pl.pallas_call(
        _tail_kernel,
        grid=(t // tm,),
        in_specs=in_specs,
        out_specs=shifted(D_MODEL),
        out_shape=jax.ShapeDtypeStruct((p2.shape[0], D_MODEL), F32),
        input_output_aliases=aliases,
        compiler_params=_params(("parallel",)),
        name="tail",
    )(*args)


def kernel(x, p, positions, w_in, w_out, ret_gn_w, pool_w, pool_scale, norm_mix_w, norm_moe_w, router_w, router_b, expert_w_gate_up, expert_b_gate_up, expert_w_down, expert_b_down, norm_ple_w, ple_gate_w, ple_proj_w, final_norm_w):
    batch, seq, d = x.shape
    depth = w_in.shape[0]
    assert depth == 1 and d == D_MODEL and seq % TOKEN_TILE == 0
    assert seq % (RET_CHUNK * RET_STEP_CHUNKS) == 0
    group_batches = GROUP_BATCHES if sum(GROUP_BATCHES) == batch else (batch,)
    t = batch * seq
    row = lambda a: a.reshape(1, -1).astype(F32)
    l = 0

    x2 = x.reshape(t, d)
    pos2 = positions.reshape(t, 1)
    p2 = p[l].reshape(t, PLE_DIM)
    w_in_bf = w_in[l].astype(BF16)
    rw = jnp.pad(router_w[l].astype(F32), ((0, 0), (0, LANES - N_EXPERTS)))
    rw_hi = rw.astype(BF16)
    rw_lo = (rw - rw_hi.astype(F32)).astype(BF16)
    rw_split = jnp.concatenate([rw_hi, rw_lo], axis=1)
    rb = jnp.pad(router_b[l].astype(F32), (0, LANES - N_EXPERTS), constant_values=NEG_BIG).reshape(1, LANES)
    bgu = expert_b_gate_up[l].reshape(N_EXPERTS, 1, D_MODEL, 2).astype(F32)
    bdn = expert_b_down[l].reshape(N_EXPERTS, 1, D_MODEL).astype(F32)
    pool_w_bf, w_out_bf = pool_w[l].astype(BF16), w_out[l].astype(BF16)
    ple_gate_bf, ple_proj_bf = ple_gate_w[l].astype(BF16), ple_proj_w[l].astype(BF16)

    out = None
    first_batch = 0
    for gi, nb in enumerate(group_batches):
        tg = nb * seq
        n_rows = tg * TOP_K
        assert n_rows % EXPERT_TILE == 0
        tile0 = first_batch * seq // TOKEN_TILE
        ret, u = _retention(x2, pos2, row(norm_mix_w[l]), w_in_bf, row(ret_gn_w[l]),
                            first_batch * seq // (RET_CHUNK * RET_STEP_CHUNKS), nb, seq)
        first_batch += nb
        h1, hn_packed, code, gates, counts = _mix_router(
            x2, ret, u, pool_w_bf, row(pool_scale[l]), w_out_bf, row(norm_moe_w[l]), rw_split, rb,
            seq, tile0)

        offsets, items, n_items = _plan(counts, n_rows, EXPERT_TILE)
        pos = _positions(code, offsets)
        xs = _sc_scatter(hn_packed, pos)
        y = _experts(xs, items, n_items, expert_w_gate_up[l], bgu[..., 0], bgu[..., 1],
                     expert_w_down[l], bdn)

        spans = LAST_GROUP_SPANS if gi == len(group_batches) - 1 else 1
        span = tg // spans
        for si in range(spans):
            idx = pos[:TOP_K, si * span:(si + 1) * span].reshape(TOP_K * span)
            yu = _sc_gather(y, idx).reshape(TOP_K, span, d // 2)
            out = _tail(h1, yu, gates, p2, row(norm_ple_w[l]), ple_gate_bf, ple_proj_bf,
                        row(final_norm_w), si * (span // TOKEN_TILE), tile0, out)
    return out.reshape(batch, seq, d)
```

```python
import functools

import numpy as np
import jax
import jax.numpy as jnp
from jax import lax
from jax.experimental import pallas as pl
from jax.experimental.pallas import tpu as pltpu
from jax.experimental.pallas import tpu_sc as plsc

D_MODEL = 1024
D_RET = 512
D_POOL = 512
RET_HEADS = 8
RET_HEAD_DIM = 64
HEAD_PAIRS = RET_HEADS // 2
ROPE_BASE = 10000.0
POOL_WINDOWS = (2, 4, 8, 16)
POOL_GROUP_DIM = 128
POOL_HISTORY = 16
POOL_BLOCK = 128
D_IN_PROJ = 4 * D_RET + D_POOL
N_EXPERTS = 32
TOP_K = 4
SWIGLU_LIMIT = 7.0
SWIGLU_ALPHA = 1.702
PLE_DIM = 256
NORM_EPS = 1e-5
GN_EPS = 1e-5

LANES = 128
NEG_BIG = -1e30

TOKEN_TILE = 512
POSITIONS_TILE = 1024
POS_ROWS = 8
RET_CHUNK = 256
RET_STEP_CHUNKS = 4
EXPERT_TILE = 512
EXPERT_COL_CHUNK = 512
SC_WINDOW = 128
GROUP_BATCHES = (6, 2)
LAST_GROUP_SPANS = 2
VMEM_LIMIT = 56 * 1024 * 1024

F32 = jnp.float32
BF16 = jnp.bfloat16
U32 = jnp.uint32


def _params(semantics):
    return pltpu.CompilerParams(dimension_semantics=semantics, vmem_limit_bytes=VMEM_LIMIT)


def _dot(a, b):
    return jnp.dot(a, b, preferred_element_type=F32)


def _dot_nt(a, b):
    return lax.dot_general(a, b, (((1,), (1,)), ((), ())), preferred_element_type=F32)


def _rms(x, w):
    ms = jnp.mean(x * x, axis=-1, keepdims=True)
    return x * lax.rsqrt(ms + NORM_EPS) * w


def _sigmoid(z):
    return 1.0 / (1.0 + jnp.exp(-z))


def _pack_bf16_pair(lo, hi):
    lo_bits = pltpu.bitcast(lo.astype(BF16).astype(F32), U32) >> 16
    hi_bits = pltpu.bitcast(hi.astype(BF16).astype(F32), U32) & jnp.uint32(0xFFFF0000)
    return lo_bits | hi_bits


def _unpack_bf16_pair(packed):
    lo = pltpu.bitcast(packed << 16, F32)
    hi = pltpu.bitcast(packed & jnp.uint32(0xFFFF0000), F32)
    return lo, hi


def _project_rows(x, pos, nw_ref, w_ref, freq_ref, qkvg_ref, u_ref, rows):
    hn = _rms(x, nw_ref[...]).astype(BF16)
    ang = pos.astype(F32) * freq_ref[0:1, :]
    cos = jnp.cos(ang)
    sin = jnp.sin(ang)
    slab = 2 * LANES
    cos_t = jnp.concatenate([cos, cos], axis=1)
    sin_up = jnp.concatenate([sin * freq_ref[1:2, :]] * 2, axis=1)
    sin_dn = jnp.concatenate([sin * freq_ref[2:3, :]] * 2, axis=1)
    half = RET_HEAD_DIM // 2

    for s in range(2 * D_RET // slab):
        sl = slice(s * slab, (s + 1) * slab)
        v = _dot(hn, w_ref[:, sl])
        v = v * cos_t + pltpu.roll(v, slab - half, 1) * sin_up + pltpu.roll(v, half, 1) * sin_dn
        if s >= D_RET // slab:
            v = v * (RET_HEAD_DIM ** -0.5)
        qkvg_ref[rows, sl] = v.astype(BF16)
    for s in range(2 * D_RET // 512, 4 * D_RET // 512):
        sl = slice(s * 512, (s + 1) * 512)
        qkvg_ref[rows, sl] = _dot(hn, w_ref[:, sl]).astype(BF16)
    u_ref[rows, :] = _dot(hn, w_ref[:, 4 * D_RET:]).astype(BF16)


def _rope_table():
    j = np.arange(LANES)
    half = RET_HEAD_DIM // 2
    inv_freq = ROPE_BASE ** (-(np.arange(half, dtype=np.float32)) / half)
    freq = np.zeros((8, LANES), np.float32)
    freq[0] = inv_freq[j % half]
    freq[1] = np.where(j % RET_HEAD_DIM < half, -1.0, 0.0)
    freq[2] = np.where(j % RET_HEAD_DIM >= half, 1.0, 0.0)
    return jnp.asarray(freq)


def _retention_kernel(x_ref, pos_ref, nw_ref, w_ref, freq_ref, dec_ref, xi_ref, zeta_ref, cd_ref, bd_ref,
                      m64_ref, eye_ref, gnw_ref, out_ref, u_ref, state_ref, qkvg_ref):
    @pl.when(pl.program_id(1) == 0)
    def _():
        state_ref[...] = jnp.zeros_like(state_ref)

    c = dec_ref.shape[1]
    n_chunks = x_ref.shape[0] // c
    lane = lax.broadcasted_iota(jnp.int32, (1, LANES), 1)
    m64 = m64_ref[...]
    bd = bd_ref[...]
    eye = eye_ref[...]

    def group_mean(v):
        hi = v.astype(BF16)
        lo = (v - hi.astype(F32)).astype(BF16)
        return _dot(jnp.concatenate([hi, lo], axis=1), m64)

    for ci in range(n_chunks):
        rows = slice(ci * c, (ci + 1) * c)
        _project_rows(x_ref[rows, :], pos_ref[rows, :], nw_ref, w_ref, freq_ref, qkvg_ref, u_ref, rows)
        for p in range(HEAD_PAIRS):
            sl = slice(p * LANES, (p + 1) * LANES)
            qp = qkvg_ref[rows, sl]
            kp = qkvg_ref[rows, D_RET + p * LANES:D_RET + (p + 1) * LANES]
            vp = qkvg_ref[rows, 2 * D_RET + p * LANES:2 * D_RET + (p + 1) * LANES]
            y = None
            for hh in range(2):
                in_head = (lane >= RET_HEAD_DIM) == bool(hh)
                qm = jnp.where(in_head, qp, jnp.zeros_like(qp))
                vm = jnp.where(in_head, vp, jnp.zeros_like(vp))
                scores = _dot_nt(qm, kp) * dec_ref[2 * p + hh]
                part = _dot(scores.astype(BF16), vm)
                y = part if y is None else y + part
            st = state_ref[p]
            y = y + _dot((qp.astype(F32) * xi_ref[p]).astype(BF16), st.astype(BF16))
            kz = (kp.astype(F32) * zeta_ref[p]).astype(BF16)
            kz_t = _dot_nt(eye, kz).astype(BF16)
            state_ref[p] = cd_ref[p] * st + _dot(kz_t, vp) * bd
            mu = group_mean(y)
            var = group_mean(y * y) - mu * mu
            yn = (y - mu) * lax.rsqrt(var + GN_EPS) * gnw_ref[:, sl]
            g = qkvg_ref[rows, 3 * D_RET + p * LANES:3 * D_RET + (p + 1) * LANES].astype(F32)
            out_ref[rows, sl] = (yn * g * _sigmoid(g)).astype(BF16)


def _retention_tables(c):
    h = np.arange(RET_HEADS, dtype=np.float64)
    log_gamma = np.log1p(-np.power(2.0, -5.0 - h))
    idx = np.arange(c, dtype=np.float64)
    rel = idx[:, None] - idx[None, :]
    dec = np.where(rel >= 0, np.exp(np.where(rel >= 0, rel, 0.0)[None] * log_gamma[:, None, None]), 0.0)
    lane_head = np.arange(LANES) // RET_HEAD_DIM
    xi = np.zeros((HEAD_PAIRS, c, LANES))
    zeta = np.zeros((HEAD_PAIRS, c, LANES))
    cd = np.zeros((HEAD_PAIRS, LANES, LANES))
    same = lane_head[:, None] == lane_head[None, :]
    for p in range(HEAD_PAIRS):
        lg = log_gamma[2 * p + lane_head]
        xi[p] = np.exp((idx + 1.0)[:, None] * lg[None, :])
        zeta[p] = np.exp((c - 1 - idx)[:, None] * lg[None, :])
        cd[p] = np.where(same, np.exp(c * lg)[:, None], 0.0)
    bd = same.astype(np.float32)
    m64 = np.concatenate([same, same], axis=0).astype(np.float32) / RET_HEAD_DIM
    f = lambda a: jnp.asarray(a, dtype=F32)
    return (f(dec), f(xi), f(zeta), f(cd), f(bd), jnp.asarray(m64, dtype=BF16),
            jnp.asarray(np.eye(LANES, dtype=np.float32), dtype=BF16))


def _retention(x2, pos2, norm_w, w_in_bf, gn_w, block0, batch, seq):
    c = RET_CHUNK
    rows = RET_STEP_CHUNKS * c
    n = seq // rows
    t = batch * seq
    dec, xi, zeta, cd, bd, m64, eye = _retention_tables(c)
    const3 = lambda shape: pl.BlockSpec(shape, lambda b, i: (0, 0, 0))
    const2 = lambda shape: pl.BlockSpec(shape, lambda b, i: (0, 0))
    out_tile = pl.BlockSpec((rows, D_RET), lambda b, i: (b * n + i, 0))
    return pl.pallas_call(
        _retention_kernel,
        grid=(batch, n),
        in_specs=[pl.BlockSpec((rows, D_MODEL), lambda b, i: (block0 + b * n + i, 0)),
                  pl.BlockSpec((rows, 1), lambda b, i: (block0 + b * n + i, 0)),
                  const2((1, D_MODEL)), const2((D_MODEL, D_IN_PROJ)), const2((8, LANES)),
                  const3((RET_HEADS, c, c)), const3((HEAD_PAIRS, c, LANES)),
                  const3((HEAD_PAIRS, c, LANES)), const3((HEAD_PAIRS, LANES, LANES)),
                  const2((LANES, LANES)), const2((2 * LANES, LANES)), const2((LANES, LANES)),
                  const2((1, D_RET))],
        out_specs=[out_tile, out_tile],
        out_shape=[jax.ShapeDtypeStruct((t, D_RET), BF16), jax.ShapeDtypeStruct((t, D_POOL), BF16)],
        scratch_shapes=[pltpu.VMEM((HEAD_PAIRS, LANES, LANES), F32), pltpu.VMEM((rows, 4 * D_RET), BF16)],
        compiler_params=_params(("arbitrary", "arbitrary")),
        name="retention",
    )(x2, pos2, norm_w, w_in_bf, _rope_table(), dec, xi, zeta, cd, bd, m64, eye, gn_w)


def _mix_router_kernel(seq, x_ref, ret_ref, u_ref, uprev_ref, band_ref, pw_ref, ps_ref, wout_ref,
                       nw_ref, rw_ref, rb_ref,
                       h1_ref, hnp_ref, code_ref, gates_ref, cnt_ref, uext_ref):
    i = pl.program_id(0)
    tm = x_ref.shape[0]
    t0 = lax.rem(i * tm, seq)

    @pl.when(i == 0)
    def _():
        cnt_ref[...] = jnp.zeros_like(cnt_ref)

    prev = uprev_ref[...]
    uext_ref[0:POOL_HISTORY, :] = jnp.where(t0 == 0, jnp.zeros_like(prev), prev)
    uext_ref[POOL_HISTORY:, :] = u_ref[...]

    row = lax.broadcasted_iota(jnp.int32, (tm, 1), 0)
    t_seq = (t0 + row + 1).astype(F32)
    mixed = []
    for gi, w in enumerate(POOL_WINDOWS):
        sl = slice(gi * POOL_GROUP_DIM, (gi + 1) * POOL_GROUP_DIM)
        wsum = jnp.concatenate(
            [_dot(band_ref[gi], uext_ref[r0:r0 + POOL_BLOCK + POOL_HISTORY, sl])
             for r0 in range(0, tm, POOL_BLOCK)], axis=0)
        count = jnp.minimum(t_seq, float(w))
        pooled = wsum / count - u_ref[:, sl].astype(F32)
        mixed.append(_dot(pooled.astype(BF16), pw_ref[gi]))
    pool = (jnp.concatenate(mixed, axis=1) * ps_ref[...]).astype(BF16)

    h1 = (x_ref[...] + _dot(ret_ref[...], wout_ref[0:D_RET, :])
          + _dot(pool, wout_ref[D_RET:, :]))
    h1_ref[...] = h1
    hn = _rms(h1, nw_ref[...])
    hn_hi = hn.astype(BF16)
    half = D_MODEL // 2
    hnp_ref[...] = _pack_bf16_pair(hn[:, :half], hn[:, half:])

    hn_lo = (hn - hn_hi.astype(F32)).astype(BF16)
    both = _dot(hn_hi, rw_ref[...])
    logits = (both[:, :LANES] + both[:, LANES:] + _dot(hn_lo, rw_ref[:, :LANES])
              + rb_ref[...])

    lane = lax.broadcasted_iota(jnp.int32, (tm, LANES), 1).astype(F32)
    code = jnp.zeros((tm, LANES), F32)
    vals = []
    work = logits
    for k in range(TOP_K):
        m = jnp.max(work, axis=-1, keepdims=True)
        idx = jnp.min(jnp.where(work == m, lane, float(LANES)), axis=-1, keepdims=True)
        chosen = lane == idx
        code = jnp.where(chosen, float(k + 1), code)
        work = jnp.where(chosen, -jnp.inf, work)
        vals.append(m)
    exps = [jnp.exp(v - vals[0]) for v in vals]
    denom = exps[0] + exps[1] + exps[2] + exps[3]
    gates = jnp.zeros((tm, LANES), F32)
    for k in range(TOP_K):
        gates = jnp.where(lane == float(k), exps[k] / denom, gates)
    code_ref[...] = code
    gates_ref[...] = gates
    cnt_ref[...] += jnp.sum((code > 0).astype(F32), axis=0, keepdims=True)


def _pool_bands():
    r = np.arange(POOL_BLOCK)[:, None]
    s = np.arange(POOL_BLOCK + POOL_HISTORY)[None, :] - POOL_HISTORY
    bands = [((s <= r) & (s > r - w)) for w in POOL_WINDOWS]
    return jnp.asarray(np.stack(bands).astype(np.float32), dtype=BF16)


def _mix_router(x2, ret, u, pool_w_bf, pool_scale, w_out_bf, norm_w, rw_split, rb, seq, tile0):
    t = ret.shape[0]
    tm = TOKEN_TILE
    hist_blocks = tm // POOL_HISTORY
    tile = lambda width: pl.BlockSpec((tm, width), lambda i: (i, 0))
    const2 = lambda shape: pl.BlockSpec(shape, lambda i: (0, 0))
    const3 = lambda shape: pl.BlockSpec(shape, lambda i: (0, 0, 0))
    return pl.pallas_call(
        functools.partial(_mix_router_kernel, seq),
        grid=(t // tm,),
        in_specs=[
            pl.BlockSpec((tm, D_MODEL), lambda i: (i + tile0, 0)), tile(D_RET), tile(D_POOL),
            pl.BlockSpec((POOL_HISTORY, D_POOL), lambda i: (jnp.maximum(i * hist_blocks - 1, 0), 0)),
            const3((len(POOL_WINDOWS), POOL_BLOCK, POOL_BLOCK + POOL_HISTORY)),
            const3((len(POOL_WINDOWS), POOL_GROUP_DIM, POOL_GROUP_DIM)),
            const2((1, D_POOL)), const2((D_MODEL, D_MODEL)), const2((1, D_MODEL)),
            const2((D_MODEL, 2 * LANES)), const2((1, LANES)),
        ],
        out_specs=[tile(D_MODEL), tile(D_MODEL // 2), tile(LANES), tile(LANES), const2((1, LANES))],
        out_shape=[
            jax.ShapeDtypeStruct((t, D_MODEL), F32),
            jax.ShapeDtypeStruct((t, D_MODEL // 2), U32),
            jax.ShapeDtypeStruct((t, LANES), F32),
            jax.ShapeDtypeStruct((t, LANES), F32),
            jax.ShapeDtypeStruct((1, LANES), F32),
        ],
        scratch_shapes=[pltpu.VMEM((tm + POOL_HISTORY, D_POOL), BF16)],
        compiler_params=_params(("arbitrary",)),
        name="mix_router",
    )(x2, ret, u, u, _pool_bands(), pool_w_bf, pool_scale, w_out_bf, norm_w, rw_split, rb)


def _positions_kernel(code_ref, off_ref, tri_ref, pos_ref, carry_ref):
    @pl.when(pl.program_id(0) == 0)
    def _():
        carry_ref[...] = jnp.zeros_like(carry_ref)

    code = code_ref[...]
    tm = code.shape[0]
    sel = (code > 0).astype(BF16)
    carry = carry_ref[...]
    rank = _dot(tri_ref[...], sel) + (carry + off_ref[...])
    carry_ref[...] = carry + jnp.sum(sel.astype(F32), axis=0, keepdims=True)
    lane = lax.broadcasted_iota(jnp.int32, (tm, LANES), 1)
    pos = jnp.zeros((tm, LANES), F32)
    for k in range(TOP_K):
        pk = jnp.sum(jnp.where(code == float(k + 1), rank, 0.0), axis=-1, keepdims=True)
        pos = jnp.where(lane == k, pk, pos)
    pos_ref[...] = pos.T[0:POS_ROWS, :].astype(jnp.int32)


def _positions(code, offsets):
    t = code.shape[0]
    tm = POSITIONS_TILE
    tri = jnp.asarray(np.tril(np.ones((tm, tm), np.float32), -1), dtype=BF16)
    return pl.pallas_call(
        _positions_kernel,
        grid=(t // tm,),
        in_specs=[pl.BlockSpec((tm, LANES), lambda i: (i, 0)),
                  pl.BlockSpec((1, LANES), lambda i: (0, 0)),
                  pl.BlockSpec((tm, tm), lambda i: (0, 0))],
        out_specs=pl.BlockSpec((POS_ROWS, tm), lambda i: (0, i)),
        out_shape=jax.ShapeDtypeStruct((POS_ROWS, t), jnp.int32),
        scratch_shapes=[pltpu.VMEM((1, LANES), F32)],
        compiler_params=_params(("arbitrary",)),
        name="positions",
    )(code, offsets, tri)


def _sc_mesh():
    return plsc.VectorSubcoreMesh(core_axis_name="core", subcore_axis_name="subcore")


def _sc_workers():
    info = plsc.get_sparse_core_info()
    return info.num_cores, info.num_cores * info.num_subcores


def _sc_gather(src, idx):
    n = idx.shape[0]
    d = src.shape[1]
    w = SC_WINDOW // 2
    num_cores, workers = _sc_workers()
    per = n // workers
    pairs = per // (2 * w)
    assert per * workers == n and pairs * 2 * w == per

    @functools.partial(
        pl.kernel, out_type=jax.ShapeDtypeStruct((n, d), src.dtype), mesh=_sc_mesh(),
        scratch_types=[pltpu.VMEM((w,), jnp.int32), pltpu.VMEM((w,), jnp.int32),
                       pltpu.VMEM((w, d), src.dtype), pltpu.VMEM((w, d), src.dtype),
                       pltpu.SemaphoreType.DMA, pltpu.SemaphoreType.DMA,
                       pltpu.SemaphoreType.DMA, pltpu.SemaphoreType.DMA],
        name="sc_gather")
    def gather(src_hbm, idx_hbm, out_hbm, idx0, idx1, rows0, rows1, g0, g1, w0, w1):
        first = (lax.axis_index("subcore") * num_cores + lax.axis_index("core")) * per

        def start_gather(win, idx_v, rows_v, sem):
            pltpu.sync_copy(idx_hbm.at[pl.ds(first + win * w, w)], idx_v)
            pltpu.async_copy(src_hbm.at[idx_v], rows_v, sem)

        def wait_gather(idx_v, rows_v, sem):
            pltpu.make_async_copy(src_hbm.at[idx_v], rows_v, sem).wait()

        def start_write(win, rows_v, sem):
            pltpu.async_copy(rows_v, out_hbm.at[pl.ds(first + win * w, w)], sem)

        def wait_write(rows_v, sem):
            pltpu.make_async_copy(rows_v, out_hbm.at[pl.ds(first, w)], sem).wait()

        start_gather(0, idx0, rows0, g0)

        @pl.loop(0, pairs)
        def _(j):
            even = 2 * j

            @pl.when(j > 0)
            def _():
                wait_write(rows1, w1)

            start_gather(even + 1, idx1, rows1, g1)
            wait_gather(idx0, rows0, g0)
            start_write(even, rows0, w0)
            wait_write(rows0, w0)

            @pl.when(j + 1 < pairs)
            def _():
                start_gather(even + 2, idx0, rows0, g0)

            wait_gather(idx1, rows1, g1)
            start_write(even + 1, rows1, w1)

        wait_write(rows1, w1)

    return gather(src, idx)


def _sc_scatter(src, idx):
    t, d = src.shape
    w = SC_WINDOW
    num_cores, workers = _sc_workers()
    per = t // workers
    assert idx.shape[0] >= TOP_K and idx.shape[1] == t and per * workers == t and per % w == 0

    @functools.partial(
        pl.kernel, out_type=jax.ShapeDtypeStruct((TOP_K * t, d), src.dtype), mesh=_sc_mesh(),
        scratch_types=[pltpu.VMEM((TOP_K, w), jnp.int32), pltpu.VMEM((w, d), src.dtype),
                       pltpu.SemaphoreType.DMA],
        name="sc_scatter")
    def scatter(src_hbm, idx_hbm, out_hbm, idx_v, rows_v, sem):
        first = (lax.axis_index("subcore") * num_cores + lax.axis_index("core")) * per

        @pl.loop(0, per // w)
        def _(j):
            base = first + j * w
            for k in range(TOP_K):
                pltpu.sync_copy(idx_hbm.at[k, pl.ds(base, w)], idx_v.at[k])
            pltpu.sync_copy(src_hbm.at[pl.ds(base, w)], rows_v)
            copies = [pltpu.async_copy(rows_v, out_hbm.at[idx_v.at[k]], sem) for k in range(TOP_K)]
            for c in copies:
                c.wait()

    return scatter(src, idx)


def _experts_kernel(items_ref, xs_ref, wgu_hbm, bg_ref, bu_ref, wdn_hbm, bd_ref, perm_ref, y_ref,
                    wgu_ref, wdn_ref, wg_ref, wu_ref, wd_ref, sems):
    i = pl.program_id(0)
    tm = xs_ref.shape[0]

    def weight_copies():
        e = items_ref[I_FETCH, i]
        return (pltpu.make_async_copy(wgu_hbm.at[e], wgu_ref, sems.at[0]),
                pltpu.make_async_copy(wdn_hbm.at[e], wdn_ref, sems.at[1]))

    @pl.when(items_ref[I_START, i] == 1)
    def _():
        for copy in weight_copies():
            copy.start()

    @pl.when(items_ref[I_PREP, i] == 1)
    def _():
        for copy in weight_copies():
            copy.wait()
        ps = items_ref[I_PREP_SLOT, i]
        perm = perm_ref[...]
        pair = 2 * LANES
        for c in range(2 * D_MODEL // pair):
            sel = _dot(wgu_ref[:, c * pair:(c + 1) * pair].astype(BF16), perm).astype(BF16)
            wg_ref[ps, :, c * LANES:(c + 1) * LANES] = sel[:, :LANES]
            wu_ref[ps, :, c * LANES:(c + 1) * LANES] = sel[:, LANES:]
        wd_ref[ps] = wdn_ref[...].astype(BF16)

    def expert_mlp(r0, r1):
        n = r1 - r0
        s = items_ref[I_SLOT, i]
        x_lo, x_hi = _unpack_bf16_pair(xs_ref[r0:r1, :])
        x = jnp.concatenate([x_lo.astype(BF16), x_hi.astype(BF16)], axis=1)
        acts = []
        chunks = [slice(c * EXPERT_COL_CHUNK, (c + 1) * EXPERT_COL_CHUNK)
                  for c in range(D_MODEL // EXPERT_COL_CHUNK)]
        for cs in chunks:
            gate = jnp.minimum(_dot(x, wg_ref[s, :, cs]) + bg_ref[:, cs], SWIGLU_LIMIT)
            up = jnp.clip(_dot(x, wu_ref[s, :, cs]) + bu_ref[:, cs], -SWIGLU_LIMIT, SWIGLU_LIMIT)
            acts.append(((up + 1.0) * (gate * _sigmoid(SWIGLU_ALPHA * gate))).astype(BF16))
        rows = items_ref[I_TILE, i] * tm + r0 + lax.broadcasted_iota(jnp.int32, (n, 1), 0)
        mine = (rows >= items_ref[I_LO, i]) & (rows < items_ref[I_HI, i])

        half, quarter = D_MODEL // 2, D_MODEL // 4
        for h in range(2):
            hs = slice(h * half, (h + 1) * half)
            yh = bd_ref[:, hs]
            for cs, act in zip(chunks, acts):
                yh = yh + _dot(act, wd_ref[s, cs, hs])
            cols = slice(h * quarter, (h + 1) * quarter)
            packed = _pack_bf16_pair(yh[:, :quarter], yh[:, quarter:])
            y_ref[r0:r1, cols] = jnp.where(mine, packed, y_ref[r0:r1, cols])

    @pl.when(items_ref[I_FIRST, i] == 1)
    def _():
        y_ref[...] = jnp.zeros(y_ref.shape, U32)

    block = tm // ROW_BLOCKS
    for m in range(1, ROW_BLOCKS + 1):
        pl.when(items_ref[I_MODE, i] == m)(functools.partial(expert_mlp, 0, m * block))
    for m in range(1, ROW_BLOCKS):
        pl.when(items_ref[I_MODE, i] == ROW_BLOCKS + m)(functools.partial(expert_mlp, m * block, tm))


(I_TILE, I_EXPERT, I_LO, I_HI, I_FIRST, I_MODE, I_FETCH, I_START, I_PREP, I_PREP_SLOT, I_SLOT) = range(11)
ITEM_FIELDS = 16
ROW_BLOCKS = 4


def _plan_kernel(tm, n_tiles, cnt_ref, off_ref, items_ref):
    n = items_ref.shape[1]
    cnt_r = cnt_ref[...]
    sub = lax.broadcasted_iota(jnp.int32, (LANES, LANES), 0).astype(F32)
    lan = lax.broadcasted_iota(jnp.int32, (LANES, LANES), 1).astype(F32)
    big = float(4 * LANES)

    def col(row):
        return jnp.sum(jnp.where(lan == sub, row, 0.0), axis=1, keepdims=True)

    def row(column):
        return jnp.sum(jnp.where(lan == sub, column, 0.0), axis=0, keepdims=True)

    def prefix(r):
        return jnp.sum(jnp.where(lan <= sub, r, 0.0), axis=1, keepdims=True)

    cnt = col(cnt_r)
    ends = prefix(cnt_r)
    starts = ends - cnt
    off_ref[...] = row(starts)
    used = cnt > 0.0
    inv_tm = 1.0 / tm
    first_tile = jnp.floor(starts * inv_tm)
    last_tile = jnp.floor(jnp.maximum(ends - 1.0, 0.0) * inv_tm)
    per = jnp.where(used, last_tile - first_tile + 1.0, 0.0)
    item_end = prefix(row(per))
    item_start = item_end - per
    used_r = row(used.astype(F32))
    ordinal = prefix(used_r) - 1.0
    nxt = jnp.min(jnp.where((lan > sub) & (used_r > 0.0), lan, big), axis=1, keepdims=True)
    has_next = nxt < big
    e_col = sub[:, 0:1]
    nxt = jnp.where(has_next, nxt, e_col)
    e_first = jnp.min(jnp.where(used_r > 0.0, lan[0:1, :], big), axis=1, keepdims=True)
    e_last = jnp.max(jnp.where(used_r > 0.0, lan[0:1, :], -1.0), axis=1, keepdims=True)
    total = jnp.max(item_end, axis=0, keepdims=True)

    item = lax.broadcasted_iota(jnp.int32, (LANES, n), 1).astype(F32) - 1.0
    mine = ((item >= item_start) & (item < item_end)).astype(F32)
    pick = lambda column: jnp.sum(mine * column, axis=0, keepdims=True)
    item_r = item[0:1, :]
    valid = pick(jnp.ones_like(cnt))
    lead = item_r < 0.0
    past = item_r >= total
    new_tile = 1.0 - pick(((item == item_start) & (starts - first_tile * tm > 0.0)).astype(F32))

    def put(field, value):
        items_ref[field:field + 1, :] = value.astype(jnp.int32)

    tile = pick(first_tile - item_start) + item_r
    lo = pick(starts)
    hi = pick(ends)
    inv_block = ROW_BLOCKS * inv_tm
    first_block = jnp.floor(jnp.maximum(lo - tile * tm, 0.0) * inv_block)
    end_block = jnp.floor((jnp.minimum(hi - tile * tm, tm) + (tm / ROW_BLOCKS - 1.0)) * inv_block)
    mode = jnp.where(first_block == 0.0, end_block,
                     jnp.where(end_block == float(ROW_BLOCKS), ROW_BLOCKS + first_block, float(ROW_BLOCKS)))
    put(I_TILE, jnp.where(past, float(n_tiles - 1), jnp.where(lead, 0.0, tile)))
    put(I_EXPERT, jnp.where(past, e_last, jnp.where(lead, e_first, pick(e_col))))
    put(I_LO, lo)
    put(I_HI, hi)
    put(I_FIRST, valid * new_tile)
    put(I_MODE, valid * mode)
    put(I_FETCH, jnp.where(past, e_last, jnp.where(lead, e_first, pick(nxt))))
    put(I_START, jnp.where(lead, 1.0, pick(((item == item_start) & has_next).astype(F32))))
    put(I_PREP, jnp.where(lead, 1.0, pick(((item == item_end - 1.0) & has_next).astype(F32))))
    put(I_PREP_SLOT, pick(ordinal + 1.0 - 2.0 * jnp.floor((ordinal + 1.0) * 0.5)))
    put(I_SLOT, pick(ordinal - 2.0 * jnp.floor(ordinal * 0.5)))
    for field in range(I_SLOT + 1, ITEM_FIELDS):
        put(field, jnp.zeros_like(valid))


def _plan(counts, n_rows, tm):
    n_tiles = n_rows // tm
    n_items = n_tiles + N_EXPERTS
    width = -(-n_items // LANES) * LANES
    offsets, items = pl.pallas_call(
        functools.partial(_plan_kernel, float(tm), n_tiles),
        out_shape=[jax.ShapeDtypeStruct((1, LANES), F32),
                   jax.ShapeDtypeStruct((ITEM_FIELDS, width), jnp.int32)],
        name="plan",
    )(counts)
    return offsets, items, n_items


def _experts(xs, items, n_items, w_gate_up, bg, bu, w_down, bd):
    n_rows = xs.shape[0]
    tm = EXPERT_TILE
    half = D_MODEL // 2
    j = np.arange(2 * LANES)
    perm = np.zeros((2 * LANES, 2 * LANES), np.float32)
    perm[j, np.where(j % 2 == 0, j // 2, LANES + j // 2)] = 1.0
    row_tile = pl.BlockSpec((tm, half), lambda i, items: (items[I_TILE, i], 0))
    by_expert = lambda *shape: pl.BlockSpec((None,) + shape, lambda i, items: (items[I_EXPERT, i], 0, 0))
    in_hbm = pl.BlockSpec(memory_space=pl.ANY)
    return pl.pallas_call(
        _experts_kernel,
        grid_spec=pltpu.PrefetchScalarGridSpec(
            num_scalar_prefetch=1,
            grid=(n_items,),
            in_specs=[row_tile, in_hbm, by_expert(1, D_MODEL), by_expert(1, D_MODEL), in_hbm,
                      by_expert(1, D_MODEL),
                      pl.BlockSpec((2 * LANES, 2 * LANES), lambda i, items: (0, 0))],
            out_specs=row_tile,
            scratch_shapes=[pltpu.VMEM((D_MODEL, 2 * D_MODEL), F32), pltpu.VMEM((D_MODEL, D_MODEL), F32)]
            + [pltpu.VMEM((2, D_MODEL, D_MODEL), BF16)] * 3 + [pltpu.SemaphoreType.DMA((2,))],
        ),
        out_shape=jax.ShapeDtypeStruct((n_rows, half), U32),
        compiler_params=_params(("arbitrary",)),
        name="experts",
    )(items, xs, w_gate_up, bg, bu, w_down, bd, jnp.asarray(perm, dtype=BF16))


def _tail_kernel(h1_ref, yu_ref, gates_ref, p_ref, nple_ref, wg_ref, wp_ref, nfin_ref, *rest):
    out_ref = rest[-1]
    gates = gates_ref[...]
    lo = None
    hi = None
    for k in range(TOP_K):
        gk = gates[:, k:k + 1]
        yl, yh = _unpack_bf16_pair(yu_ref[k])
        lo = gk * yl if lo is None else lo + gk * yl
        hi = gk * yh if hi is None else hi + gk * yh
    quarter = D_MODEL // 4
    h2 = h1_ref[...] + jnp.concatenate([lo[:, :quarter], hi[:, :quarter], lo[:, quarter:], hi[:, quarter:]],
                                       axis=1)
    hn = _rms(h2, nple_ref[...]).astype(BF16)
    gate = _sigmoid(_dot(hn, wg_ref[...]))
    h3 = h2 + gate * _dot(p_ref[...].astype(BF16), wp_ref[...])
    out_ref[...] = _rms(h3, nfin_ref[...])


def _tail(h1, yu, gates, p2, norm_ple_w, ple_gate_bf, ple_proj_bf, final_norm_w, local0, tile0, result):
    t = yu.shape[1]
    tm = TOKEN_TILE
    half = D_MODEL // 2
    tile = lambda width: pl.BlockSpec((tm, width), lambda i: (i + local0, 0))
    shifted = lambda width: pl.BlockSpec((tm, width), lambda i: (i + tile0 + local0, 0))
    const2 = lambda shape: pl.BlockSpec(shape, lambda i: (0, 0))
    in_specs = [tile(D_MODEL), pl.BlockSpec((TOP_K, tm, half), lambda i: (0, i, 0)),
                tile(LANES), shifted(PLE_DIM), const2((1, D_MODEL)),
                const2((D_MODEL, D_MODEL)), const2((PLE_DIM, D_MODEL)), const2((1, D_MODEL))]
    args = [h1, yu, gates, p2, norm_ple_w, ple_gate_bf, ple_proj_bf, final_norm_w]
    aliases = {}
    if result is not None:
        in_specs.append(pl.BlockSpec(memory_space=pl.ANY))
        args.append(result)
        aliases = {len(args) - 1: 0}
    return pl.pallas_call(
        _tail_kernel,
        grid=(t // tm,),
        in_specs=in_specs,
        out_specs=shifted(D_MODEL),
        out_shape=jax.ShapeDtypeStruct((p2.shape[0], D_MODEL), F32),
        input_output_aliases=aliases,
        compiler_params=_params(("parallel",)),
        name="tail",
    )(*args)


def kernel(x, p, positions, w_in, w_out, ret_gn_w, pool_w, pool_scale, norm_mix_w, norm_moe_w, router_w, router_b, expert_w_gate_up, expert_b_gate_up, expert_w_down, expert_b_down, norm_ple_w, ple_gate_w, ple_proj_w, final_norm_w):
    batch, seq, d = x.shape
    depth = w_in.shape[0]
    assert depth == 1 and d == D_MODEL and seq % TOKEN_TILE == 0
    assert seq % (RET_CHUNK * RET_STEP_CHUNKS) == 0
    group_batches = GROUP_BATCHES if sum(GROUP_BATCHES) == batch else (batch,)
    t = batch * seq
    row = lambda a: a.reshape(1, -1).astype(F32)
    l = 0

    x2 = x.reshape(t, d)
    pos2 = positions.reshape(t, 1)
    p2 = p[l].reshape(t, PLE_DIM)
    w_in_bf = w_in[l].astype(BF16)
    rw = jnp.pad(router_w[l].astype(F32), ((0, 0), (0, LANES - N_EXPERTS)))
    rw_hi = rw.astype(BF16)
    rw_lo = (rw - rw_hi.astype(F32)).astype(BF16)
    rw_split = jnp.concatenate([rw_hi, rw_lo], axis=1)
    rb = jnp.pad(router_b[l].astype(F32), (0, LANES - N_EXPERTS), constant_values=NEG_BIG).reshape(1, LANES)
    bgu = expert_b_gate_up[l].reshape(N_EXPERTS, 1, D_MODEL, 2).astype(F32)
    bdn = expert_b_down[l].reshape(N_EXPERTS, 1, D_MODEL).astype(F32)
    pool_w_bf, w_out_bf = pool_w[l].astype(BF16), w_out[l].astype(BF16)
    ple_gate_bf, ple_proj_bf = ple_gate_w[l].astype(BF16), ple_proj_w[l].astype(BF16)

    out = None
    first_batch = 0
    for gi, nb in enumerate(group_batches):
        tg = nb * seq
        n_rows = tg * TOP_K
        assert n_rows % EXPERT_TILE == 0
        tile0 = first_batch * seq // TOKEN_TILE
        ret, u = _retention(x2, pos2, row(norm_mix_w[l]), w_in_bf, row(ret_gn_w[l]),
                            first_batch * seq // (RET_CHUNK * RET_STEP_CHUNKS), nb, seq)
        first_batch += nb
        h1, hn_packed, code, gates, counts = _mix_router(
            x2, ret, u, pool_w_bf, row(pool_scale[l]), w_out_bf, row(norm_moe_w[l]), rw_split, rb,
            seq, tile0)

        offsets, items, n_items = _plan(counts, n_rows, EXPERT_TILE)
        pos = _positions(code, offsets)
        xs = _sc_scatter(hn_packed, pos)
        y = _experts(xs, items, n_items, expert_w_gate_up[l], bgu[..., 0], bgu[..., 1],
                     expert_w_down[l], bdn)

        spans = LAST_GROUP_SPANS if gi == len(group_batches) - 1 else 1
        span = tg // spans
        for si in range(spans):
            idx = pos[:TOP_K, si * span:(si + 1) * span].reshape(TOP_K * span)
            yu = _sc_gather(y, idx).reshape(TOP_K, span, d // 2)
            out = _tail(h1, yu, gates, p2, row(norm_ple_w[l]), ple_gate_bf, ple_proj_bf,
                        row(final_norm_w), si * (span // TOKEN_TILE), tile0, out)
    return out.reshape(batch, seq, d)
```

```python
import functools

import numpy as np
import jax
import jax.numpy as jnp
from jax import lax
from jax.experimental import pallas as pl
from jax.experimental.pallas import tpu as pltpu
from jax.experimental.pallas import tpu_sc as plsc

D_MODEL = 1024
D_RET = 512
D_POOL = 512
RET_HEADS = 8
RET_HEAD_DIM = 64
HEAD_PAIRS = RET_HEADS // 2
ROPE_BASE = 10000.0
POOL_WINDOWS = (2, 4, 8, 16)
POOL_GROUP_DIM = 128
POOL_HISTORY = 16
POOL_BLOCK = 128
D_IN_PROJ = 4 * D_RET + D_POOL
N_EXPERTS = 32
TOP_K = 4
SWIGLU_LIMIT = 7.0
SWIGLU_ALPHA = 1.702
PLE_DIM = 256
NORM_EPS = 1e-5
GN_EPS = 1e-5

LANES = 128
NEG_BIG = -1e30

TOKEN_TILE = 512
POSITIONS_TILE = 1024
POS_ROWS = 8
RET_CHUNK = 256
RET_STEP_CHUNKS = 4
EXPERT_TILE = 512
EXPERT_COL_CHUNK = 512
SC_WINDOW = 128
GROUP_BATCHES = (6, 2)
LAST_GROUP_SPANS = 2
VMEM_LIMIT = 56 * 1024 * 1024

F32 = jnp.float32
BF16 = jnp.bfloat16
U32 = jnp.uint32


def _params(semantics):
    return pltpu.CompilerParams(dimension_semantics=semantics, vmem_limit_bytes=VMEM_LIMIT)


def _dot(a, b):
    return jnp.dot(a, b, preferred_element_type=F32)


def _dot_nt(a, b):
    return lax.dot_general(a, b, (((1,), (1,)), ((), ())), preferred_element_type=F32)


def _rms(x, w):
    ms = jnp.mean(x * x, axis=-1, keepdims=True)
    return x * lax.rsqrt(ms + NORM_EPS) * w


def _sigmoid(z):
    return 1.0 / (1.0 + jnp.exp(-z))


def _pack_bf16_pair(lo, hi):
    lo_bits = pltpu.bitcast(lo.astype(BF16).astype(F32), U32) >> 16
    hi_bits = pltpu.bitcast(hi.astype(BF16).astype(F32), U32) & jnp.uint32(0xFFFF0000)
    return lo_bits | hi_bits


def _unpack_bf16_pair(packed):
    lo = pltpu.bitcast(packed << 16, F32)
    hi = pltpu.bitcast(packed & jnp.uint32(0xFFFF0000), F32)
    return lo, hi


def _project_rows(x, pos, nw_ref, w_ref, freq_ref, qkvg_ref, u_ref, rows):
    hn = _rms(x, nw_ref[...]).astype(BF16)
    ang = pos.astype(F32) * freq_ref[0:1, :]
    cos = jnp.cos(ang)
    sin = jnp.sin(ang)
    slab = 2 * LANES
    cos_t = jnp.concatenate([cos, cos], axis=1)
    sin_up = jnp.concatenate([sin * freq_ref[1:2, :]] * 2, axis=1)
    sin_dn = jnp.concatenate([sin * freq_ref[2:3, :]] * 2, axis=1)
    half = RET_HEAD_DIM // 2

    for s in range(2 * D_RET // slab):
        sl = slice(s * slab, (s + 1) * slab)
        v = _dot(hn, w_ref[:, sl])
        v = v * cos_t + pltpu.roll(v, slab - half, 1) * sin_up + pltpu.roll(v, half, 1) * sin_dn
        if s >= D_RET // slab:
            v = v * (RET_HEAD_DIM ** -0.5)
        qkvg_ref[rows, sl] = v.astype(BF16)
    for s in range(2 * D_RET // 512, 4 * D_RET // 512):
        sl = slice(s * 512, (s + 1) * 512)
        qkvg_ref[rows, sl] = _dot(hn, w_ref[:, sl]).astype(BF16)
    u_ref[rows, :] = _dot(hn, w_ref[:, 4 * D_RET:]).astype(BF16)


def _rope_table():
    j = np.arange(LANES)
    half = RET_HEAD_DIM // 2
    inv_freq = ROPE_BASE ** (-(np.arange(half, dtype=np.float32)) / half)
    freq = np.zeros((8, LANES), np.float32)
    freq[0] = inv_freq[j % half]
    freq[1] = np.where(j % RET_HEAD_DIM < half, -1.0, 0.0)
    freq[2] = np.where(j % RET_HEAD_DIM >= half, 1.0, 0.0)
    return jnp.asarray(freq)


def _retention_kernel(x_ref, pos_ref, nw_ref, w_ref, freq_ref, dec_ref, xi_ref, zeta_ref, cd_ref, bd_ref,
                      m64_ref, eye_ref, gnw_ref, out_ref, u_ref, state_ref, qkvg_ref):
    @pl.when(pl.program_id(1) == 0)
    def _():
        state_ref[...] = jnp.zeros_like(state_ref)

    c = dec_ref.shape[1]
    n_chunks = x_ref.shape[0] // c
    lane = lax.broadcasted_iota(jnp.int32, (1, LANES), 1)
    m64 = m64_ref[...]
    bd = bd_ref[...]
    eye = eye_ref[...]

    def group_mean(v):
        hi = v.astype(BF16)
        lo = (v - hi.astype(F32)).astype(BF16)
        return _dot(jnp.concatenate([hi, lo], axis=1), m64)

    for ci in range(n_chunks):
        rows = slice(ci * c, (ci + 1) * c)
        _project_rows(x_ref[rows, :], pos_ref[rows, :], nw_ref, w_ref, freq_ref, qkvg_ref, u_ref, rows)
        for p in range(HEAD_PAIRS):
            sl = slice(p * LANES, (p + 1) * LANES)
            qp = qkvg_ref[rows, sl]
            kp = qkvg_ref[rows, D_RET + p * LANES:D_RET + (p + 1) * LANES]
            vp = qkvg_ref[rows, 2 * D_RET + p * LANES:2 * D_RET + (p + 1) * LANES]
            y = None
            for hh in range(2):
                in_head = (lane >= RET_HEAD_DIM) == bool(hh)
                qm = jnp.where(in_head, qp, jnp.zeros_like(qp))
                vm = jnp.where(in_head, vp, jnp.zeros_like(vp))
                scores = _dot_nt(qm, kp) * dec_ref[2 * p + hh]
                part = _dot(scores.astype(BF16), vm)
                y = part if y is None else y + part
            st = state_ref[p]
            y = y + _dot((qp.astype(F32) * xi_ref[p]).astype(BF16), st.astype(BF16))
            kz = (kp.astype(F32) * zeta_ref[p]).astype(BF16)
            kz_t = _dot_nt(eye, kz).astype(BF16)
            state_ref[p] = cd_ref[p] * st + _dot(kz_t, vp) * bd
            mu = group_mean(y)
            var = group_mean(y * y) - mu * mu
            yn = (y - mu) * lax.rsqrt(var + GN_EPS) * gnw_ref[:, sl]
            g = qkvg_ref[rows, 3 * D_RET + p * LANES:3 * D_RET + (p + 1) * LANES].astype(F32)
            out_ref[rows, sl] = (yn * g * _sigmoid(g)).astype(BF16)


def _retention_tables(c):
    h = np.arange(RET_HEADS, dtype=np.float64)
    log_gamma = np.log1p(-np.power(2.0, -5.0 - h))
    idx = np.arange(c, dtype=np.float64)
    rel = idx[:, None] - idx[None, :]
    dec = np.where(rel >= 0, np.exp(np.where(rel >= 0, rel, 0.0)[None] * log_gamma[:, None, None]), 0.0)
    lane_head = np.arange(LANES) // RET_HEAD_DIM
    xi = np.zeros((HEAD_PAIRS, c, LANES))
    zeta = np.zeros((HEAD_PAIRS, c, LANES))
    cd = np.zeros((HEAD_PAIRS, LANES, LANES))
    same = lane_head[:, None] == lane_head[None, :]
    for p in range(HEAD_PAIRS):
        lg = log_gamma[2 * p + lane_head]
        xi[p] = np.exp((idx + 1.0)[:, None] * lg[None, :])
        zeta[p] = np.exp((c - 1 - idx)[:, None] * lg[None, :])
        cd[p] = np.where(same, np.exp(c * lg)[:, None], 0.0)
    bd = same.astype(np.float32)
    m64 = np.concatenate([same, same], axis=0).astype(np.float32) / RET_HEAD_DIM
    f = lambda a: jnp.asarray(a, dtype=F32)
    return (f(dec), f(xi), f(zeta), f(cd), f(bd), jnp.asarray(m64, dtype=BF16),
            jnp.asarray(np.eye(LANES, dtype=np.float32), dtype=BF16))


def _retention(x2, pos2, norm_w, w_in_bf, gn_w, block0, batch, seq):
    c = RET_CHUNK
    rows = RET_STEP_CHUNKS * c
    n = seq // rows
    t = batch * seq
    dec, xi, zeta, cd, bd, m64, eye = _retention_tables(c)
    const3 = lambda shape: pl.BlockSpec(shape, lambda b, i: (0, 0, 0))
    const2 = lambda shape: pl.BlockSpec(shape, lambda b, i: (0, 0))
    out_tile = pl.BlockSpec((rows, D_RET), lambda b, i: (b * n + i, 0))
    return pl.pallas_call(
        _retention_kernel,
        grid=(batch, n),
        in_specs=[pl.BlockSpec((rows, D_MODEL), lambda b, i: (block0 + b * n + i, 0)),
                  pl.BlockSpec((rows, 1), lambda b, i: (block0 + b * n + i, 0)),
                  const2((1, D_MODEL)), const2((D_MODEL, D_IN_PROJ)), const2((8, LANES)),
                  const3((RET_HEADS, c, c)), const3((HEAD_PAIRS, c, LANES)),
                  const3((HEAD_PAIRS, c, LANES)), const3((HEAD_PAIRS, LANES, LANES)),
                  const2((LANES, LANES)), const2((2 * LANES, LANES)), const2((LANES, LANES)),
                  const2((1, D_RET))],
        out_specs=[out_tile, out_tile],
        out_shape=[jax.ShapeDtypeStruct((t, D_RET), BF16), jax.ShapeDtypeStruct((t, D_POOL), BF16)],
        scratch_shapes=[pltpu.VMEM((HEAD_PAIRS, LANES, LANES), F32), pltpu.VMEM((rows, 4 * D_RET), BF16)],
        compiler_params=_params(("arbitrary", "arbitrary")),
        name="retention",
    )(x2, pos2, norm_w, w_in_bf, _rope_table(), dec, xi, zeta, cd, bd, m64, eye, gn_w)


def _mix_router_kernel(seq, x_ref, ret_ref, u_ref, uprev_ref, band_ref, pw_ref, ps_ref, wout_ref,
                       nw_ref, rw_ref, rb_ref,
                       h1_ref, hnp_ref, code_ref, gates_ref, cnt_ref, uext_ref):
    i = pl.program_id(0)
    tm = x_ref.shape[0]
    t0 = lax.rem(i * tm, seq)

    @pl.when(i == 0)
    def _():
        cnt_ref[...] = jnp.zeros_like(cnt_ref)

    prev = uprev_ref[...]
    uext_ref[0:POOL_HISTORY, :] = jnp.where(t0 == 0, jnp.zeros_like(prev), prev)
    uext_ref[POOL_HISTORY:, :] = u_ref[...]

    row = lax.broadcasted_iota(jnp.int32, (tm, 1), 0)
    t_seq = (t0 + row + 1).astype(F32)
    mixed = []
    for gi, w in enumerate(POOL_WINDOWS):
        sl = slice(gi * POOL_GROUP_DIM, (gi + 1) * POOL_GROUP_DIM)
        wsum = jnp.concatenate(
            [_dot(band_ref[gi], uext_ref[r0:r0 + POOL_BLOCK + POOL_HISTORY, sl])
             for r0 in range(0, tm, POOL_BLOCK)], axis=0)
        count = jnp.minimum(t_seq, float(w))
        pooled = wsum / count - u_ref[:, sl].astype(F32)
        mixed.append(_dot(pooled.astype(BF16), pw_ref[gi]))
    pool = (jnp.concatenate(mixed, axis=1) * ps_ref[...]).astype(BF16)

    h1 = (x_ref[...] + _dot(ret_ref[...], wout_ref[0:D_RET, :])
          + _dot(pool, wout_ref[D_RET:, :]))
    h1_ref[...] = h1
    hn = _rms(h1, nw_ref[...])
    hn_hi = hn.astype(BF16)
    half = D_MODEL // 2
    hnp_ref[...] = _pack_bf16_pair(hn[:, :half], hn[:, half:])

    hn_lo = (hn - hn_hi.astype(F32)).astype(BF16)
    both = _dot(hn_hi, rw_ref[...])
    logits = (both[:, :LANES] + both[:, LANES:] + _dot(hn_lo, rw_ref[:, :LANES])
              + rb_ref[...])

    lane = lax.broadcasted_iota(jnp.int32, (tm, LANES), 1).astype(F32)
    code = jnp.zeros((tm, LANES), F32)
    vals = []
    work = logits
    for k in range(TOP_K):
        m = jnp.max(work, axis=-1, keepdims=True)
        idx = jnp.min(jnp.where(work == m, lane, float(LANES)), axis=-1, keepdims=True)
        chosen = lane == idx
        code = jnp.where(chosen, float(k + 1), code)
        work = jnp.where(chosen, -jnp.inf, work)
        vals.append(m)
    exps = [jnp.exp(v - vals[0]) for v in vals]
    denom = exps[0] + exps[1] + exps[2] + exps[3]
    gates = jnp.zeros((tm, LANES), F32)
    for k in range(TOP_K):
        gates = jnp.where(lane == float(k), exps[k] / denom, gates)
    code_ref[...] = code
    gates_ref[...] = gates
    cnt_ref[...] += jnp.sum((code > 0).astype(F32), axis=0, keepdims=True)


def _pool_bands():
    r = np.arange(POOL_BLOCK)[:, None]
    s = np.arange(POOL_BLOCK + POOL_HISTORY)[None, :] - POOL_HISTORY
    bands = [((s <= r) & (s > r - w)) for w in POOL_WINDOWS]
    return jnp.asarray(np.stack(bands).astype(np.float32), dtype=BF16)


def _mix_router(x2, ret, u, pool_w_bf, pool_scale, w_out_bf, norm_w, rw_split, rb, seq, tile0):
    t = ret.shape[0]
    tm = TOKEN_TILE
    hist_blocks = tm // POOL_HISTORY
    tile = lambda width: pl.BlockSpec((tm, width), lambda i: (i, 0))
    const2 = lambda shape: pl.BlockSpec(shape, lambda i: (0, 0))
    const3 = lambda shape: pl.BlockSpec(shape, lambda i: (0, 0, 0))
    return pl.pallas_call(
        functools.partial(_mix_router_kernel, seq),
        grid=(t // tm,),
        in_specs=[
            pl.BlockSpec((tm, D_MODEL), lambda i: (i + tile0, 0)), tile(D_RET), tile(D_POOL),
            pl.BlockSpec((POOL_HISTORY, D_POOL), lambda i: (jnp.maximum(i * hist_blocks - 1, 0), 0)),
            const3((len(POOL_WINDOWS), POOL_BLOCK, POOL_BLOCK + POOL_HISTORY)),
            const3((len(POOL_WINDOWS), POOL_GROUP_DIM, POOL_GROUP_DIM)),
            const2((1, D_POOL)), const2((D_MODEL, D_MODEL)), const2((1, D_MODEL)),
            const2((D_MODEL, 2 * LANES)), const2((1, LANES)),
        ],
        out_specs=[tile(D_MODEL), tile(D_MODEL // 2), tile(LANES), tile(LANES), const2((1, LANES))],
        out_shape=[
            jax.ShapeDtypeStruct((t, D_MODEL), F32),
            jax.ShapeDtypeStruct((t, D_MODEL // 2), U32),
            jax.ShapeDtypeStruct((t, LANES), F32),
            jax.ShapeDtypeStruct((t, LANES), F32),
            jax.ShapeDtypeStruct((1, LANES), F32),
        ],
        scratch_shapes=[pltpu.VMEM((tm + POOL_HISTORY, D_POOL), BF16)],
        compiler_params=_params(("arbitrary",)),
        name="mix_router",
    )(x2, ret, u, u, _pool_bands(), pool_w_bf, pool_scale, w_out_bf, norm_w, rw_split, rb)


def _positions_kernel(code_ref, off_ref, tri_ref, pos_ref, carry_ref):
    @pl.when(pl.program_id(0) == 0)
    def _():
        carry_ref[...] = jnp.zeros_like(carry_ref)

    code = code_ref[...]
    tm = code.shape[0]
    sel = (code > 0).astype(BF16)
    carry = carry_ref[...]
    rank = _dot(tri_ref[...], sel) + (carry + off_ref[...])
    carry_ref[...] = carry + jnp.sum(sel.astype(F32), axis=0, keepdims=True)
    lane = lax.broadcasted_iota(jnp.int32, (tm, LANES), 1)
    pos = jnp.zeros((tm, LANES), F32)
    for k in range(TOP_K):
        pk = jnp.sum(jnp.where(code == float(k + 1), rank, 0.0), axis=-1, keepdims=True)
        pos = jnp.where(lane == k, pk, pos)
    pos_ref[...] = pos.T[0:POS_ROWS, :].astype(jnp.int32)


def _positions(code, offsets):
    t = code.shape[0]
    tm = POSITIONS_TILE
    tri = jnp.asarray(np.tril(np.ones((tm, tm), np.float32), -1), dtype=BF16)
    return pl.pallas_call(
        _positions_kernel,
        grid=(t // tm,),
        in_specs=[pl.BlockSpec((tm, LANES), lambda i: (i, 0)),
                  pl.BlockSpec((1, LANES), lambda i: (0, 0)),
                  pl.BlockSpec((tm, tm), lambda i: (0, 0))],
        out_specs=pl.BlockSpec((POS_ROWS, tm), lambda i: (0, i)),
        out_shape=jax.ShapeDtypeStruct((POS_ROWS, t), jnp.int32),
        scratch_shapes=[pltpu.VMEM((1, LANES), F32)],
        compiler_params=_params(("arbitrary",)),
        name="positions",
    )(code, offsets, tri)


def _sc_mesh():
    return plsc.VectorSubcoreMesh(core_axis_name="core", subcore_axis_name="subcore")


def _sc_workers():
    info = plsc.get_sparse_core_info()
    return info.num_cores, info.num_cores * info.num_subcores


def _sc_gather(src, idx):
    n = idx.shape[0]
    d = src.shape[1]
    w = SC_WINDOW // 2
    num_cores, workers = _sc_workers()
    per = n // workers
    pairs = per // (2 * w)
    assert per * workers == n and pairs * 2 * w == per

    @functools.partial(
        pl.kernel, out_type=jax.ShapeDtypeStruct((n, d), src.dtype), mesh=_sc_mesh(),
        scratch_types=[pltpu.VMEM((w,), jnp.int32), pltpu.VMEM((w,), jnp.int32),
                       pltpu.VMEM((w, d), src.dtype), pltpu.VMEM((w, d), src.dtype),
                       pltpu.SemaphoreType.DMA, pltpu.SemaphoreType.DMA,
                       pltpu.SemaphoreType.DMA, pltpu.SemaphoreType.DMA],
        name="sc_gather")
    def gather(src_hbm, idx_hbm, out_hbm, idx0, idx1, rows0, rows1, g0, g1, w0, w1):
        first = (lax.axis_index("subcore") * num_cores + lax.axis_index("core")) * per

        def start_gather(win, idx_v, rows_v, sem):
            pltpu.sync_copy(idx_hbm.at[pl.ds(first + win * w, w)], idx_v)
            pltpu.async_copy(src_hbm.at[idx_v], rows_v, sem)

        def wait_gather(idx_v, rows_v, sem):
            pltpu.make_async_copy(src_hbm.at[idx_v], rows_v, sem).wait()

        def start_write(win, rows_v, sem):
            pltpu.async_copy(rows_v, out_hbm.at[pl.ds(first + win * w, w)], sem)

        def wait_write(rows_v, sem):
            pltpu.make_async_copy(rows_v, out_hbm.at[pl.ds(first, w)], sem).wait()

        start_gather(0, idx0, rows0, g0)

        @pl.loop(0, pairs)
        def _(j):
            even = 2 * j

            @pl.when(j > 0)
            def _():
                wait_write(rows1, w1)

            start_gather(even + 1, idx1, rows1, g1)
            wait_gather(idx0, rows0, g0)
            start_write(even, rows0, w0)
            wait_write(rows0, w0)

            @pl.when(j + 1 < pairs)
            def _():
                start_gather(even + 2, idx0, rows0, g0)

            wait_gather(idx1, rows1, g1)
            start_write(even + 1, rows1, w1)

        wait_write(rows1, w1)

    return gather(src, idx)


def _sc_scatter(src, idx):
    t, d = src.shape
    w = SC_WINDOW
    num_cores, workers = _sc_workers()
    per = t // workers
    assert idx.shape[0] >= TOP_K and idx.shape[1] == t and per * workers == t and per % w == 0

    @functools.partial(
        pl.kernel, out_type=jax.ShapeDtypeStruct((TOP_K * t, d), src.dtype), mesh=_sc_mesh(),
        scratch_types=[pltpu.VMEM((TOP_K, w), jnp.int32), pltpu.VMEM((w, d), src.dtype),
                       pltpu.SemaphoreType.DMA],
        name="sc_scatter")
    def scatter(src_hbm, idx_hbm, out_hbm, idx_v, rows_v, sem):
        first = (lax.axis_index("subcore") * num_cores + lax.axis_index("core")) * per

        @pl.loop(0, per // w)
        def _(j):
            base = first + j * w
            for k in range(TOP_K):
                pltpu.sync_copy(idx_hbm.at[k, pl.ds(base, w)], idx_v.at[k])
            pltpu.sync_copy(src_hbm.at[pl.ds(base, w)], rows_v)
            copies = [pltpu.async_copy(rows_v, out_hbm.at[idx_v.at[k]], sem) for k in range(TOP_K)]
            for c in copies:
                c.wait()

    return scatter(src, idx)


def _experts_kernel(items_ref, xs_ref, wgu_hbm, bg_ref, bu_ref, wdn_hbm, bd_ref, perm_ref, y_ref,
                    wgu_ref, wdn_ref, wg_ref, wu_ref, wd_ref, sems):
    i = pl.program_id(0)
    tm = xs_ref.shape[0]

    def weight_copies():
        e = items_ref[I_FETCH, i]
        return (pltpu.make_async_copy(wgu_hbm.at[e], wgu_ref, sems.at[0]),
                pltpu.make_async_copy(wdn_hbm.at[e], wdn_ref, sems.at[1]))

    @pl.when(items_ref[I_START, i] == 1)
    def _():
        for copy in weight_copies():
            copy.start()

    @pl.when(items_ref[I_PREP, i] == 1)
    def _():
        for copy in weight_copies():
            copy.wait()
        ps = items_ref[I_PREP_SLOT, i]
        perm = perm_ref[...]
        pair = 2 * LANES
        for c in range(2 * D_MODEL // pair):
            sel = _dot(wgu_ref[:, c * pair:(c + 1) * pair].astype(BF16), perm).astype(BF16)
            wg_ref[ps, :, c * LANES:(c + 1) * LANES] = sel[:, :LANES]
            wu_ref[ps, :, c * LANES:(c + 1) * LANES] = sel[:, LANES:]
        wd_ref[ps] = wdn_ref[...].astype(BF16)

    def expert_mlp(r0, r1):
        n = r1 - r0
        s = items_ref[I_SLOT, i]
        x_lo, x_hi = _unpack_bf16_pair(xs_ref[r0:r1, :])
        x = jnp.concatenate([x_lo.astype(BF16), x_hi.astype(BF16)], axis=1)
        acts = []
        chunks = [slice(c * EXPERT_COL_CHUNK, (c + 1) * EXPERT_COL_CHUNK)
                  for c in range(D_MODEL // EXPERT_COL_CHUNK)]
        for cs in chunks:
            gate = jnp.minimum(_dot(x, wg_ref[s, :, cs]) + bg_ref[:, cs], SWIGLU_LIMIT)
            up = jnp.clip(_dot(x, wu_ref[s, :, cs]) + bu_ref[:, cs], -SWIGLU_LIMIT, SWIGLU_LIMIT)
            acts.append(((up + 1.0) * (gate * _sigmoid(SWIGLU_ALPHA * gate))).astype(BF16))
        rows = items_ref[I_TILE, i] * tm + r0 + lax.broadcasted_iota(jnp.int32, (n, 1), 0)
        mine = (rows >= items_ref[I_LO, i]) & (rows < items_ref[I_HI, i])

        half, quarter = D_MODEL // 2, D_MODEL // 4
        for h in range(2):
            hs = slice(h * half, (h + 1) * half)
            yh = bd_ref[:, hs]
            for cs, act in zip(chunks, acts):
                yh = yh + _dot(act, wd_ref[s, cs, hs])
            cols = slice(h * quarter, (h + 1) * quarter)
            packed = _pack_bf16_pair(yh[:, :quarter], yh[:, quarter:])
            y_ref[r0:r1, cols] = jnp.where(mine, packed, y_ref[r0:r1, cols])

    @pl.when(items_ref[I_FIRST, i] == 1)
    def _():
        y_ref[...] = jnp.zeros(y_ref.shape, U32)

    block = tm // ROW_BLOCKS
    for m in range(1, ROW_BLOCKS + 1):
        pl.when(items_ref[I_MODE, i] == m)(functools.partial(expert_mlp, 0, m * block))
    for m in range(1, ROW_BLOCKS):
        pl.when(items_ref[I_MODE, i] == ROW_BLOCKS + m)(functools.partial(expert_mlp, m * block, tm))


(I_TILE, I_EXPERT, I_LO, I_HI, I_FIRST, I_MODE, I_FETCH, I_START, I_PREP, I_PREP_SLOT, I_SLOT) = range(11)
ITEM_FIELDS = 16
ROW_BLOCKS = 8


def _plan_kernel(tm, n_tiles, cnt_ref, off_ref, items_ref):
    n = items_ref.shape[1]
    cnt_r = cnt_ref[...]
    sub = lax.broadcasted_iota(jnp.int32, (LANES, LANES), 0).astype(F32)
    lan = lax.broadcasted_iota(jnp.int32, (LANES, LANES), 1).astype(F32)
    big = float(4 * LANES)

    def col(row):
        return jnp.sum(jnp.where(lan == sub, row, 0.0), axis=1, keepdims=True)

    def row(column):
        return jnp.sum(jnp.where(lan == sub, column, 0.0), axis=0, keepdims=True)

    def prefix(r):
        return jnp.sum(jnp.where(lan <= sub, r, 0.0), axis=1, keepdims=True)

    cnt = col(cnt_r)
    ends = prefix(cnt_r)
    starts = ends - cnt
    off_ref[...] = row(starts)
    used = cnt > 0.0
    inv_tm = 1.0 / tm
    first_tile = jnp.floor(starts * inv_tm)
    last_tile = jnp.floor(jnp.maximum(ends - 1.0, 0.0) * inv_tm)
    per = jnp.where(used, last_tile - first_tile + 1.0, 0.0)
    item_end = prefix(row(per))
    item_start = item_end - per
    used_r = row(used.astype(F32))
    ordinal = prefix(used_r) - 1.0
    nxt = jnp.min(jnp.where((lan > sub) & (used_r > 0.0), lan, big), axis=1, keepdims=True)
    has_next = nxt < big
    e_col = sub[:, 0:1]
    nxt = jnp.where(has_next, nxt, e_col)
    e_first = jnp.min(jnp.where(used_r > 0.0, lan[0:1, :], big), axis=1, keepdims=True)
    e_last = jnp.max(jnp.where(used_r > 0.0, lan[0:1, :], -1.0), axis=1, keepdims=True)
    total = jnp.max(item_end, axis=0, keepdims=True)

    item = lax.broadcasted_iota(jnp.int32, (LANES, n), 1).astype(F32) - 1.0
    mine = ((item >= item_start) & (item < item_end)).astype(F32)
    pick = lambda column: jnp.sum(mine * column, axis=0, keepdims=True)
    item_r = item[0:1, :]
    valid = pick(jnp.ones_like(cnt))
    lead = item_r < 0.0
    past = item_r >= total
    new_tile = 1.0 - pick(((item == item_start) & (starts - first_tile * tm > 0.0)).astype(F32))

    def put(field, value):
        items_ref[field:field + 1, :] = value.astype(jnp.int32)

    tile = pick(first_tile - item_start) + item_r
    lo = pick(starts)
    hi = pick(ends)
    inv_block = ROW_BLOCKS * inv_tm
    first_block = jnp.floor(jnp.maximum(lo - tile * tm, 0.0) * inv_block)
    end_block = jnp.floor((jnp.minimum(hi - tile * tm, tm) + (tm / ROW_BLOCKS - 1.0)) * inv_block)
    mode = jnp.where(first_block == 0.0, end_block,
                     jnp.where(end_block == float(ROW_BLOCKS), ROW_BLOCKS + first_block, float(ROW_BLOCKS)))
    put(I_TILE, jnp.where(past, float(n_tiles - 1), jnp.where(lead, 0.0, tile)))
    put(I_EXPERT, jnp.where(past, e_last, jnp.where(lead, e_first, pick(e_col))))
    put(I_LO, lo)
    put(I_HI, hi)
    put(I_FIRST, valid * new_tile)
    put(I_MODE, valid * mode)
    put(I_FETCH, jnp.where(past, e_last, jnp.where(lead, e_first, pick(nxt))))
    put(I_START, jnp.where(lead, 1.0, pick(((item == item_start) & has_next).astype(F32))))
    put(I_PREP, jnp.where(lead, 1.0, pick(((item == item_end - 1.0) & has_next).astype(F32))))
    put(I_PREP_SLOT, pick(ordinal + 1.0 - 2.0 * jnp.floor((ordinal + 1.0) * 0.5)))
    put(I_SLOT, pick(ordinal - 2.0 * jnp.floor(ordinal * 0.5)))
    for field in range(I_SLOT + 1, ITEM_FIELDS):
        put(field, jnp.zeros_like(valid))


def _plan(counts, n_rows, tm):
    n_tiles = n_rows // tm
    n_items = n_tiles + N_EXPERTS
    width = -(-n_items // LANES) * LANES
    offsets, items = pl.pallas_call(
        functools.partial(_plan_kernel, float(tm), n_tiles),
        out_shape=[jax.ShapeDtypeStruct((1, LANES), F32),
                   jax.ShapeDtypeStruct((ITEM_FIELDS, width), jnp.int32)],
        name="plan",
    )(counts)
    return offsets, items, n_items


def _experts(xs, items, n_items, w_gate_up, bg, bu, w_down, bd):
    n_rows = xs.shape[0]
    tm = EXPERT_TILE
    half = D_MODEL // 2
    j = np.arange(2 * LANES)
    perm = np.zeros((2 * LANES, 2 * LANES), np.float32)
    perm[j, np.where(j % 2 == 0, j // 2, LANES + j // 2)] = 1.0
    row_tile = pl.BlockSpec((tm, half), lambda i, items: (items[I_TILE, i], 0))
    by_expert = lambda *shape: pl.BlockSpec((None,) + shape, lambda i, items: (items[I_EXPERT, i], 0, 0))
    in_hbm = pl.BlockSpec(memory_space=pl.ANY)
    return pl.pallas_call(
        _experts_kernel,
        grid_spec=pltpu.PrefetchScalarGridSpec(
            num_scalar_prefetch=1,
            grid=(n_items,),
            in_specs=[row_tile, in_hbm, by_expert(1, D_MODEL), by_expert(1, D_MODEL), in_hbm,
                      by_expert(1, D_MODEL),
                      pl.BlockSpec((2 * LANES, 2 * LANES), lambda i, items: (0, 0))],
            out_specs=row_tile,
            scratch_shapes=[pltpu.VMEM((D_MODEL, 2 * D_MODEL), F32), pltpu.VMEM((D_MODEL, D_MODEL), F32)]
            + [pltpu.VMEM((2, D_MODEL, D_MODEL), BF16)] * 3 + [pltpu.SemaphoreType.DMA((2,))],
        ),
        out_shape=jax.ShapeDtypeStruct((n_rows, half), U32),
        compiler_params=_params(("arbitrary",)),
        name="experts",
    )(items, xs, w_gate_up, bg, bu, w_down, bd, jnp.asarray(perm, dtype=BF16))


def _tail_kernel(h1_ref, yu_ref, gates_ref, p_ref, nple_ref, wg_ref, wp_ref, nfin_ref, *rest):
    out_ref = rest[-1]
    gates = gates_ref[...]
    lo = None
    hi = None
    for k in range(TOP_K):
        gk = gates[:, k:k + 1]
        yl, yh = _unpack_bf16_pair(yu_ref[k])
        lo = gk * yl if lo is None else lo + gk * yl
        hi = gk * yh if hi is None else hi + gk * yh
    quarter = D_MODEL // 4
    h2 = h1_ref[...] + jnp.concatenate([lo[:, :quarter], hi[:, :quarter], lo[:, quarter:], hi[:, quarter:]],
                                       axis=1)
    hn = _rms(h2, nple_ref[...]).astype(BF16)
    gate = _sigmoid(_dot(hn, wg_ref[...]))
    h3 = h2 + gate * _dot(p_ref[...].astype(BF16), wp_ref[...])
    out_ref[...] = _rms(h3, nfin_ref[...])


def _tail(h1, yu, gates, p2, norm_ple_w, ple_gate_bf, ple_proj_bf, final_norm_w, local0, tile0, result):
    t = yu.shape[1]
    tm = TOKEN_TILE
    half = D_MODEL // 2
    tile = lambda width: pl.BlockSpec((tm, width), lambda i: (i + local0, 0))
    shifted = lambda width: pl.BlockSpec((tm, width), lambda i: (i + tile0 + local0, 0))
    const2 = lambda shape: pl.BlockSpec(shape, lambda i: (0, 0))
    in_specs = [tile(D_MODEL), pl.BlockSpec((TOP_K, tm, half), lambda i: (0, i, 0)),
                tile(LANES), shifted(PLE_DIM), const2((1, D_MODEL)),
                const2((D_MODEL, D_MODEL)), const2((PLE_DIM, D_MODEL)), const2((1, D_MODEL))]
    args = [h1, yu, gates, p2, norm_ple_w, ple_gate_bf, ple_proj_bf, final_norm_w]
    aliases = {}
    if result is not None:
        in_specs.append(pl.BlockSpec(memory_space=pl.ANY))
        args.append(result)
        aliases = {len(args) - 1: 0}
    return pl.pallas_call(
        _tail_kernel,
        grid=(t // tm,),
        in_specs=in_specs,
        out_specs=shifted(D_MODEL),
        out_shape=jax.ShapeDtypeStruct((p2.shape[0], D_MODEL), F32),
        input_output_aliases=aliases,
        compiler_params=_params(("parallel",)),
        name="tail",
    )(*args)


def kernel(x, p, positions, w_in, w_out, ret_gn_w, pool_w, pool_scale, norm_mix_w, norm_moe_w, router_w, router_b, expert_w_gate_up, expert_b_gate_up, expert_w_down, expert_b_down, norm_ple_w, ple_gate_w, ple_proj_w, final_norm_w):
    batch, seq, d = x.shape
    depth = w_in.shape[0]
    assert depth == 1 and d == D_MODEL and seq % TOKEN_TILE == 0
    assert seq % (RET_CHUNK * RET_STEP_CHUNKS) == 0
    group_batches = GROUP_BATCHES if sum(GROUP_BATCHES) == batch else (batch,)
    t = batch * seq
    row = lambda a: a.reshape(1, -1).astype(F32)
    l = 0

    x2 = x.reshape(t, d)
    pos2 = positions.reshape(t, 1)
    p2 = p[l].reshape(t, PLE_DIM)
    w_in_bf = w_in[l].astype(BF16)
    rw = jnp.pad(router_w[l].astype(F32), ((0, 0), (0, LANES - N_EXPERTS)))
    rw_hi = rw.astype(BF16)
    rw_lo = (rw - rw_hi.astype(F32)).astype(BF16)
    rw_split = jnp.concatenate([rw_hi, rw_lo], axis=1)
    rb = jnp.pad(router_b[l].astype(F32), (0, LANES - N_EXPERTS), constant_values=NEG_BIG).reshape(1, LANES)
    bgu = expert_b_gate_up[l].reshape(N_EXPERTS, 1, D_MODEL, 2).astype(F32)
    bdn = expert_b_down[l].reshape(N_EXPERTS, 1, D_MODEL).astype(F32)
    pool_w_bf, w_out_bf = pool_w[l].astype(BF16), w_out[l].astype(BF16)
    ple_gate_bf, ple_proj_bf = ple_gate_w[l].astype(BF16), ple_proj_w[l].astype(BF16)

    out = None
    first_batch = 0
    for gi, nb in enumerate(group_batches):
        tg = nb * seq
        n_rows = tg * TOP_K
        assert n_rows % EXPERT_TILE == 0
        tile0 = first_batch * seq // TOKEN_TILE
        ret, u = _retention(x2, pos2, row(norm_mix_w[l]), w_in_bf, row(ret_gn_w[l]),
                            first_batch * seq // (RET_CHUNK * RET_STEP_CHUNKS), nb, seq)
        first_batch += nb
        h1, hn_packed, code, gates, counts = _mix_router(
            x2, ret, u, pool_w_bf, row(pool_scale[l]), w_out_bf, row(norm_moe_w[l]), rw_split, rb,
            seq, tile0)

        offsets, items, n_items = _plan(counts, n_rows, EXPERT_TILE)
        pos = _positions(code, offsets)
        xs = _sc_scatter(hn_packed, pos)
        y = _experts(xs, items, n_items, expert_w_gate_up[l], bgu[..., 0], bgu[..., 1],
                     expert_w_down[l], bdn)

        spans = LAST_GROUP_SPANS if gi == len(group_batches) - 1 else 1
        span = tg // spans
        for si in range(spans):
            idx = pos[:TOP_K, si * span:(si + 1) * span].reshape(TOP_K * span)
            yu = _sc_gather(y, idx).reshape(TOP_K, span, d // 2)
            out = _tail(h1, yu, gates, p2, row(norm_ple_w[l]), ple_gate_bf, ple_proj_bf,
                        row(final_norm_w), si * (span // TOKEN_TILE), tile0, out)
    return out.reshape(batch, seq, d)
```

```python
import functools

import numpy as np
import jax
import jax.numpy as jnp
from jax import lax
from jax.experimental import pallas as pl
from jax.experimental.pallas import tpu as pltpu
from jax.experimental.pallas import tpu_sc as plsc

D_MODEL = 1024
D_RET = 512
D_POOL = 512
RET_HEADS = 8
RET_HEAD_DIM = 64
HEAD_PAIRS = RET_HEADS // 2
ROPE_BASE = 10000.0
POOL_WINDOWS = (2, 4, 8, 16)
POOL_GROUP_DIM = 128
POOL_HISTORY = 16
POOL_BLOCK = 128
D_IN_PROJ = 4 * D_RET + D_POOL
N_EXPERTS = 32
TOP_K = 4
SWIGLU_LIMIT = 7.0
SWIGLU_ALPHA = 1.702
PLE_DIM = 256
NORM_EPS = 1e-5
GN_EPS = 1e-5

LANES = 128
NEG_BIG = -1e30

TOKEN_TILE = 512
POSITIONS_TILE = 1024
TAIL_ROWS = 256
POS_ROWS = 8
RET_CHUNK = 256
RET_STEP_CHUNKS = 4
EXPERT_TILE = 512
EXPERT_COL_CHUNK = 512
SC_WINDOW = 128
GROUP_BATCHES = (6, 2)
LAST_GROUP_SPANS = 2
VMEM_LIMIT = 56 * 1024 * 1024

F32 = jnp.float32
BF16 = jnp.bfloat16
U32 = jnp.uint32


def _params(semantics):
    return pltpu.CompilerParams(dimension_semantics=semantics, vmem_limit_bytes=VMEM_LIMIT)


def _dot(a, b):
    return jnp.dot(a, b, preferred_element_type=F32)


def _dot_nt(a, b):
    return lax.dot_general(a, b, (((1,), (1,)), ((), ())), preferred_element_type=F32)


def _rms(x, w):
    ms = jnp.mean(x * x, axis=-1, keepdims=True)
    return x * lax.rsqrt(ms + NORM_EPS) * w


def _sigmoid(z):
    return 1.0 / (1.0 + jnp.exp(-z))


def _pack_bf16_pair(lo, hi):
    lo_bits = pltpu.bitcast(lo.astype(BF16).astype(F32), U32) >> 16
    hi_bits = pltpu.bitcast(hi.astype(BF16).astype(F32), U32) & jnp.uint32(0xFFFF0000)
    return lo_bits | hi_bits


def _unpack_bf16_pair(packed):
    lo = pltpu.bitcast(packed << 16, F32)
    hi = pltpu.bitcast(packed & jnp.uint32(0xFFFF0000), F32)
    return lo, hi


def _project_rows(x, pos, nw_ref, w_ref, freq_ref, qkvg_ref, u_ref, rows):
    hn = _rms(x, nw_ref[...]).astype(BF16)
    ang = pos.astype(F32) * freq_ref[0:1, :]
    cos = jnp.cos(ang)
    sin = jnp.sin(ang)
    slab = 2 * LANES
    cos_t = jnp.concatenate([cos, cos], axis=1)
    sin_up = jnp.concatenate([sin * freq_ref[1:2, :]] * 2, axis=1)
    sin_dn = jnp.concatenate([sin * freq_ref[2:3, :]] * 2, axis=1)
    half = RET_HEAD_DIM // 2

    for s in range(2 * D_RET // slab):
        sl = slice(s * slab, (s + 1) * slab)
        v = _dot(hn, w_ref[:, sl])
        v = v * cos_t + pltpu.roll(v, slab - half, 1) * sin_up + pltpu.roll(v, half, 1) * sin_dn
        if s >= D_RET // slab:
            v = v * (RET_HEAD_DIM ** -0.5)
        qkvg_ref[rows, sl] = v.astype(BF16)
    for s in range(2 * D_RET // 512, 4 * D_RET // 512):
        sl = slice(s * 512, (s + 1) * 512)
        qkvg_ref[rows, sl] = _dot(hn, w_ref[:, sl]).astype(BF16)
    u_ref[rows, :] = _dot(hn, w_ref[:, 4 * D_RET:]).astype(BF16)


def _rope_table():
    j = np.arange(LANES)
    half = RET_HEAD_DIM // 2
    inv_freq = ROPE_BASE ** (-(np.arange(half, dtype=np.float32)) / half)
    freq = np.zeros((8, LANES), np.float32)
    freq[0] = inv_freq[j % half]
    freq[1] = np.where(j % RET_HEAD_DIM < half, -1.0, 0.0)
    freq[2] = np.where(j % RET_HEAD_DIM >= half, 1.0, 0.0)
    return jnp.asarray(freq)


def _retention_kernel(x_ref, pos_ref, nw_ref, w_ref, freq_ref, dec_ref, xi_ref, zeta_ref, cd_ref, bd_ref,
                      m64_ref, eye_ref, gnw_ref, out_ref, u_ref, state_ref, qkvg_ref):
    @pl.when(pl.program_id(1) == 0)
    def _():
        state_ref[...] = jnp.zeros_like(state_ref)

    c = dec_ref.shape[1]
    n_chunks = x_ref.shape[0] // c
    lane = lax.broadcasted_iota(jnp.int32, (1, LANES), 1)
    m64 = m64_ref[...]
    bd = bd_ref[...]
    eye = eye_ref[...]

    def group_mean(v):
        hi = v.astype(BF16)
        lo = (v - hi.astype(F32)).astype(BF16)
        return _dot(jnp.concatenate([hi, lo], axis=1), m64)

    for ci in range(n_chunks):
        rows = slice(ci * c, (ci + 1) * c)
        _project_rows(x_ref[rows, :], pos_ref[rows, :], nw_ref, w_ref, freq_ref, qkvg_ref, u_ref, rows)
        for p in range(HEAD_PAIRS):
            sl = slice(p * LANES, (p + 1) * LANES)
            qp = qkvg_ref[rows, sl]
            kp = qkvg_ref[rows, D_RET + p * LANES:D_RET + (p + 1) * LANES]
            vp = qkvg_ref[rows, 2 * D_RET + p * LANES:2 * D_RET + (p + 1) * LANES]
            y = None
            for hh in range(2):
                in_head = (lane >= RET_HEAD_DIM) == bool(hh)
                qm = jnp.where(in_head, qp, jnp.zeros_like(qp))
                vm = jnp.where(in_head, vp, jnp.zeros_like(vp))
                scores = _dot_nt(qm, kp) * dec_ref[2 * p + hh]
                part = _dot(scores.astype(BF16), vm)
                y = part if y is None else y + part
            st = state_ref[p]
            y = y + _dot((qp.astype(F32) * xi_ref[p]).astype(BF16), st.astype(BF16))
            kz = (kp.astype(F32) * zeta_ref[p]).astype(BF16)
            kz_t = _dot_nt(eye, kz).astype(BF16)
            state_ref[p] = cd_ref[p] * st + _dot(kz_t, vp) * bd
            mu = group_mean(y)
            var = group_mean(y * y) - mu * mu
            yn = (y - mu) * lax.rsqrt(var + GN_EPS) * gnw_ref[:, sl]
            g = qkvg_ref[rows, 3 * D_RET + p * LANES:3 * D_RET + (p + 1) * LANES].astype(F32)
            out_ref[rows, sl] = (yn * g * _sigmoid(g)).astype(BF16)


def _retention_tables(c):
    h = np.arange(RET_HEADS, dtype=np.float64)
    log_gamma = np.log1p(-np.power(2.0, -5.0 - h))
    idx = np.arange(c, dtype=np.float64)
    rel = idx[:, None] - idx[None, :]
    dec = np.where(rel >= 0, np.exp(np.where(rel >= 0, rel, 0.0)[None] * log_gamma[:, None, None]), 0.0)
    lane_head = np.arange(LANES) // RET_HEAD_DIM
    xi = np.zeros((HEAD_PAIRS, c, LANES))
    zeta = np.zeros((HEAD_PAIRS, c, LANES))
    cd = np.zeros((HEAD_PAIRS, LANES, LANES))
    same = lane_head[:, None] == lane_head[None, :]
    for p in range(HEAD_PAIRS):
        lg = log_gamma[2 * p + lane_head]
        xi[p] = np.exp((idx + 1.0)[:, None] * lg[None, :])
        zeta[p] = np.exp((c - 1 - idx)[:, None] * lg[None, :])
        cd[p] = np.where(same, np.exp(c * lg)[:, None], 0.0)
    bd = same.astype(np.float32)
    m64 = np.concatenate([same, same], axis=0).astype(np.float32) / RET_HEAD_DIM
    f = lambda a: jnp.asarray(a, dtype=F32)
    return (f(dec), f(xi), f(zeta), f(cd), f(bd), jnp.asarray(m64, dtype=BF16),
            jnp.asarray(np.eye(LANES, dtype=np.float32), dtype=BF16))


def _retention(x2, pos2, norm_w, w_in_bf, gn_w, block0, batch, seq):
    c = RET_CHUNK
    rows = RET_STEP_CHUNKS * c
    n = seq // rows
    t = batch * seq
    dec, xi, zeta, cd, bd, m64, eye = _retention_tables(c)
    const3 = lambda shape: pl.BlockSpec(shape, lambda b, i: (0, 0, 0))
    const2 = lambda shape: pl.BlockSpec(shape, lambda b, i: (0, 0))
    out_tile = pl.BlockSpec((rows, D_RET), lambda b, i: (b * n + i, 0))
    return pl.pallas_call(
        _retention_kernel,
        grid=(batch, n),
        in_specs=[pl.BlockSpec((rows, D_MODEL), lambda b, i: (block0 + b * n + i, 0)),
                  pl.BlockSpec((rows, 1), lambda b, i: (block0 + b * n + i, 0)),
                  const2((1, D_MODEL)), const2((D_MODEL, D_IN_PROJ)), const2((8, LANES)),
                  const3((RET_HEADS, c, c)), const3((HEAD_PAIRS, c, LANES)),
                  const3((HEAD_PAIRS, c, LANES)), const3((HEAD_PAIRS, LANES, LANES)),
                  const2((LANES, LANES)), const2((2 * LANES, LANES)), const2((LANES, LANES)),
                  const2((1, D_RET))],
        out_specs=[out_tile, out_tile],
        out_shape=[jax.ShapeDtypeStruct((t, D_RET), BF16), jax.ShapeDtypeStruct((t, D_POOL), BF16)],
        scratch_shapes=[pltpu.VMEM((HEAD_PAIRS, LANES, LANES), F32), pltpu.VMEM((rows, 4 * D_RET), BF16)],
        compiler_params=_params(("arbitrary", "arbitrary")),
        name="retention",
    )(x2, pos2, norm_w, w_in_bf, _rope_table(), dec, xi, zeta, cd, bd, m64, eye, gn_w)


def _mix_router_kernel(seq, x_ref, ret_ref, u_ref, uprev_ref, band_ref, pw_ref, ps_ref, wout_ref,
                       nw_ref, rw_ref, rb_ref,
                       h1_ref, hnp_ref, code_ref, gates_ref, cnt_ref, uext_ref):
    i = pl.program_id(0)
    tm = x_ref.shape[0]
    t0 = lax.rem(i * tm, seq)

    @pl.when(i == 0)
    def _():
        cnt_ref[...] = jnp.zeros_like(cnt_ref)

    prev = uprev_ref[...]
    uext_ref[0:POOL_HISTORY, :] = jnp.where(t0 == 0, jnp.zeros_like(prev), prev)
    uext_ref[POOL_HISTORY:, :] = u_ref[...]

    row = lax.broadcasted_iota(jnp.int32, (tm, 1), 0)
    t_seq = (t0 + row + 1).astype(F32)
    mixed = []
    for gi, w in enumerate(POOL_WINDOWS):
        sl = slice(gi * POOL_GROUP_DIM, (gi + 1) * POOL_GROUP_DIM)
        wsum = jnp.concatenate(
            [_dot(band_ref[gi], uext_ref[r0:r0 + POOL_BLOCK + POOL_HISTORY, sl])
             for r0 in range(0, tm, POOL_BLOCK)], axis=0)
        count = jnp.minimum(t_seq, float(w))
        pooled = wsum / count - u_ref[:, sl].astype(F32)
        mixed.append(_dot(pooled.astype(BF16), pw_ref[gi]))
    pool = (jnp.concatenate(mixed, axis=1) * ps_ref[...]).astype(BF16)

    h1 = (x_ref[...] + _dot(ret_ref[...], wout_ref[0:D_RET, :])
          + _dot(pool, wout_ref[D_RET:, :]))
    h1_ref[...] = h1
    hn = _rms(h1, nw_ref[...])
    hn_hi = hn.astype(BF16)
    half = D_MODEL // 2
    hnp_ref[...] = _pack_bf16_pair(hn[:, :half], hn[:, half:])

    hn_lo = (hn - hn_hi.astype(F32)).astype(BF16)
    both = _dot(hn_hi, rw_ref[...])
    logits = (both[:, :LANES] + both[:, LANES:] + _dot(hn_lo, rw_ref[:, :LANES])
              + rb_ref[...])

    lane = lax.broadcasted_iota(jnp.int32, (tm, LANES), 1).astype(F32)
    code = jnp.zeros((tm, LANES), F32)
    vals = []
    work = logits
    for k in range(TOP_K):
        m = jnp.max(work, axis=-1, keepdims=True)
        idx = jnp.min(jnp.where(work == m, lane, float(LANES)), axis=-1, keepdims=True)
        chosen = lane == idx
        code = jnp.where(chosen, float(k + 1), code)
        work = jnp.where(chosen, -jnp.inf, work)
        vals.append(m)
    exps = [jnp.exp(v - vals[0]) for v in vals]
    denom = exps[0] + exps[1] + exps[2] + exps[3]
    gates = jnp.zeros((tm, LANES), F32)
    for k in range(TOP_K):
        gates = jnp.where(lane == float(k), exps[k] / denom, gates)
    code_ref[...] = code
    gates_ref[...] = gates
    cnt_ref[...] += jnp.sum((code > 0).astype(F32), axis=0, keepdims=True)


def _pool_bands():
    r = np.arange(POOL_BLOCK)[:, None]
    s = np.arange(POOL_BLOCK + POOL_HISTORY)[None, :] - POOL_HISTORY
    bands = [((s <= r) & (s > r - w)) for w in POOL_WINDOWS]
    return jnp.asarray(np.stack(bands).astype(np.float32), dtype=BF16)


def _mix_router(x2, ret, u, pool_w_bf, pool_scale, w_out_bf, norm_w, rw_split, rb, seq, tile0):
    t = ret.shape[0]
    tm = TOKEN_TILE
    hist_blocks = tm // POOL_HISTORY
    tile = lambda width: pl.BlockSpec((tm, width), lambda i: (i, 0))
    const2 = lambda shape: pl.BlockSpec(shape, lambda i: (0, 0))
    const3 = lambda shape: pl.BlockSpec(shape, lambda i: (0, 0, 0))
    return pl.pallas_call(
        functools.partial(_mix_router_kernel, seq),
        grid=(t // tm,),
        in_specs=[
            pl.BlockSpec((tm, D_MODEL), lambda i: (i + tile0, 0)), tile(D_RET), tile(D_POOL),
            pl.BlockSpec((POOL_HISTORY, D_POOL), lambda i: (jnp.maximum(i * hist_blocks - 1, 0), 0)),
            const3((len(POOL_WINDOWS), POOL_BLOCK, POOL_BLOCK + POOL_HISTORY)),
            const3((len(POOL_WINDOWS), POOL_GROUP_DIM, POOL_GROUP_DIM)),
            const2((1, D_POOL)), const2((D_MODEL, D_MODEL)), const2((1, D_MODEL)),
            const2((D_MODEL, 2 * LANES)), const2((1, LANES)),
        ],
        out_specs=[tile(D_MODEL), tile(D_MODEL // 2), tile(LANES), tile(LANES), const2((1, LANES))],
        out_shape=[
            jax.ShapeDtypeStruct((t, D_MODEL), F32),
            jax.ShapeDtypeStruct((t, D_MODEL // 2), U32),
            jax.ShapeDtypeStruct((t, LANES), F32),
            jax.ShapeDtypeStruct((t, LANES), F32),
            jax.ShapeDtypeStruct((1, LANES), F32),
        ],
        scratch_shapes=[pltpu.VMEM((tm + POOL_HISTORY, D_POOL), BF16)],
        compiler_params=_params(("arbitrary",)),
        name="mix_router",
    )(x2, ret, u, u, _pool_bands(), pool_w_bf, pool_scale, w_out_bf, norm_w, rw_split, rb)


def _positions_kernel(code_ref, off_ref, tri_ref, pos_ref, carry_ref):
    @pl.when(pl.program_id(0) == 0)
    def _():
        carry_ref[...] = jnp.zeros_like(carry_ref)

    code = code_ref[...]
    tm = code.shape[0]
    sel = (code > 0).astype(BF16)
    carry = carry_ref[...]
    rank = _dot(tri_ref[...], sel) + (carry + off_ref[...])
    carry_ref[...] = carry + jnp.sum(sel.astype(F32), axis=0, keepdims=True)
    lane = lax.broadcasted_iota(jnp.int32, (tm, LANES), 1)
    pos = jnp.zeros((tm, LANES), F32)
    for k in range(TOP_K):
        pk = jnp.sum(jnp.where(code == float(k + 1), rank, 0.0), axis=-1, keepdims=True)
        pos = jnp.where(lane == k, pk, pos)
    pos_ref[...] = pos.T[0:POS_ROWS, :].astype(jnp.int32)


def _positions(code, offsets):
    t = code.shape[0]
    tm = POSITIONS_TILE
    tri = jnp.asarray(np.tril(np.ones((tm, tm), np.float32), -1), dtype=BF16)
    return pl.pallas_call(
        _positions_kernel,
        grid=(t // tm,),
        in_specs=[pl.BlockSpec((tm, LANES), lambda i: (i, 0)),
                  pl.BlockSpec((1, LANES), lambda i: (0, 0)),
                  pl.BlockSpec((tm, tm), lambda i: (0, 0))],
        out_specs=pl.BlockSpec((POS_ROWS, tm), lambda i: (0, i)),
        out_shape=jax.ShapeDtypeStruct((POS_ROWS, t), jnp.int32),
        scratch_shapes=[pltpu.VMEM((1, LANES), F32)],
        compiler_params=_params(("arbitrary",)),
        name="positions",
    )(code, offsets, tri)


def _sc_mesh():
    return plsc.VectorSubcoreMesh(core_axis_name="core", subcore_axis_name="subcore")


def _sc_workers():
    info = plsc.get_sparse_core_info()
    return info.num_cores, info.num_cores * info.num_subcores


def _sc_gather(src, idx):
    n = idx.shape[0]
    d = src.shape[1]
    w = SC_WINDOW // 2
    num_cores, workers = _sc_workers()
    per = n // workers
    pairs = per // (2 * w)
    assert per * workers == n and pairs * 2 * w == per

    @functools.partial(
        pl.kernel, out_type=jax.ShapeDtypeStruct((n, d), src.dtype), mesh=_sc_mesh(),
        scratch_types=[pltpu.VMEM((w,), jnp.int32), pltpu.VMEM((w,), jnp.int32),
                       pltpu.VMEM((w, d), src.dtype), pltpu.VMEM((w, d), src.dtype),
                       pltpu.SemaphoreType.DMA, pltpu.SemaphoreType.DMA,
                       pltpu.SemaphoreType.DMA, pltpu.SemaphoreType.DMA],
        name="sc_gather")
    def gather(src_hbm, idx_hbm, out_hbm, idx0, idx1, rows0, rows1, g0, g1, w0, w1):
        first = (lax.axis_index("subcore") * num_cores + lax.axis_index("core")) * per

        def start_gather(win, idx_v, rows_v, sem):
            pltpu.sync_copy(idx_hbm.at[pl.ds(first + win * w, w)], idx_v)
            pltpu.async_copy(src_hbm.at[idx_v], rows_v, sem)

        def wait_gather(idx_v, rows_v, sem):
            pltpu.make_async_copy(src_hbm.at[idx_v], rows_v, sem).wait()

        def start_write(win, rows_v, sem):
            pltpu.async_copy(rows_v, out_hbm.at[pl.ds(first + win * w, w)], sem)

        def wait_write(rows_v, sem):
            pltpu.make_async_copy(rows_v, out_hbm.at[pl.ds(first, w)], sem).wait()

        start_gather(0, idx0, rows0, g0)

        @pl.loop(0, pairs)
        def _(j):
            even = 2 * j

            @pl.when(j > 0)
            def _():
                wait_write(rows1, w1)

            start_gather(even + 1, idx1, rows1, g1)
            wait_gather(idx0, rows0, g0)
            start_write(even, rows0, w0)
            wait_write(rows0, w0)

            @pl.when(j + 1 < pairs)
            def _():
                start_gather(even + 2, idx0, rows0, g0)

            wait_gather(idx1, rows1, g1)
            start_write(even + 1, rows1, w1)

        wait_write(rows1, w1)

    return gather(src, idx)


def _sc_scatter(src, idx):
    t, d = src.shape
    w = SC_WINDOW
    num_cores, workers = _sc_workers()
    per = t // workers
    assert idx.shape[0] >= TOP_K and idx.shape[1] == t and per * workers == t and per % w == 0

    @functools.partial(
        pl.kernel, out_type=jax.ShapeDtypeStruct((TOP_K * t, d), src.dtype), mesh=_sc_mesh(),
        scratch_types=[pltpu.VMEM((TOP_K, w), jnp.int32), pltpu.VMEM((w, d), src.dtype),
                       pltpu.SemaphoreType.DMA],
        name="sc_scatter")
    def scatter(src_hbm, idx_hbm, out_hbm, idx_v, rows_v, sem):
        first = (lax.axis_index("subcore") * num_cores + lax.axis_index("core")) * per

        @pl.loop(0, per // w)
        def _(j):
            base = first + j * w
            for k in range(TOP_K):
                pltpu.sync_copy(idx_hbm.at[k, pl.ds(base, w)], idx_v.at[k])
            pltpu.sync_copy(src_hbm.at[pl.ds(base, w)], rows_v)
            copies = [pltpu.async_copy(rows_v, out_hbm.at[idx_v.at[k]], sem) for k in range(TOP_K)]
            for c in copies:
                c.wait()

    return scatter(src, idx)


def _experts_kernel(items_ref, xs_ref, wgu_hbm, bg_ref, bu_ref, wdn_hbm, bd_ref, perm_ref, y_ref,
                    wgu_ref, wdn_ref, wg_ref, wu_ref, wd_ref, sems):
    i = pl.program_id(0)
    tm = xs_ref.shape[0]

    def weight_copies():
        e = items_ref[I_FETCH, i]
        return (pltpu.make_async_copy(wgu_hbm.at[e], wgu_ref, sems.at[0]),
                pltpu.make_async_copy(wdn_hbm.at[e], wdn_ref, sems.at[1]))

    @pl.when(items_ref[I_START, i] == 1)
    def _():
        for copy in weight_copies():
            copy.start()

    @pl.when(items_ref[I_PREP, i] == 1)
    def _():
        for copy in weight_copies():
            copy.wait()
        ps = items_ref[I_PREP_SLOT, i]
        perm = perm_ref[...]
        pair = 2 * LANES
        for c in range(2 * D_MODEL // pair):
            sel = _dot(wgu_ref[:, c * pair:(c + 1) * pair].astype(BF16), perm).astype(BF16)
            wg_ref[ps, :, c * LANES:(c + 1) * LANES] = sel[:, :LANES]
            wu_ref[ps, :, c * LANES:(c + 1) * LANES] = sel[:, LANES:]
        wd_ref[ps] = wdn_ref[...].astype(BF16)

    def expert_mlp(r0, r1):
        n = r1 - r0
        s = items_ref[I_SLOT, i]
        x_lo, x_hi = _unpack_bf16_pair(xs_ref[r0:r1, :])
        x = jnp.concatenate([x_lo.astype(BF16), x_hi.astype(BF16)], axis=1)
        acts = []
        chunks = [slice(c * EXPERT_COL_CHUNK, (c + 1) * EXPERT_COL_CHUNK)
                  for c in range(D_MODEL // EXPERT_COL_CHUNK)]
        for cs in chunks:
            gate = jnp.minimum(_dot(x, wg_ref[s, :, cs]) + bg_ref[:, cs], SWIGLU_LIMIT)
            up = jnp.clip(_dot(x, wu_ref[s, :, cs]) + bu_ref[:, cs], -SWIGLU_LIMIT, SWIGLU_LIMIT)
            acts.append(((up + 1.0) * (gate * _sigmoid(SWIGLU_ALPHA * gate))).astype(BF16))
        rows = items_ref[I_TILE, i] * tm + r0 + lax.broadcasted_iota(jnp.int32, (n, 1), 0)
        mine = (rows >= items_ref[I_LO, i]) & (rows < items_ref[I_HI, i])

        half, quarter = D_MODEL // 2, D_MODEL // 4
        for h in range(2):
            hs = slice(h * half, (h + 1) * half)
            yh = bd_ref[:, hs]
            for cs, act in zip(chunks, acts):
                yh = yh + _dot(act, wd_ref[s, cs, hs])
            cols = slice(h * quarter, (h + 1) * quarter)
            packed = _pack_bf16_pair(yh[:, :quarter], yh[:, quarter:])
            y_ref[r0:r1, cols] = jnp.where(mine, packed, y_ref[r0:r1, cols])

    @pl.when(items_ref[I_FIRST, i] == 1)
    def _():
        y_ref[...] = jnp.zeros(y_ref.shape, U32)

    block = tm // ROW_BLOCKS
    for m in range(1, ROW_BLOCKS + 1):
        pl.when(items_ref[I_MODE, i] == m)(functools.partial(expert_mlp, 0, m * block))
    for m in range(1, ROW_BLOCKS):
        pl.when(items_ref[I_MODE, i] == ROW_BLOCKS + m)(functools.partial(expert_mlp, m * block, tm))


(I_TILE, I_EXPERT, I_LO, I_HI, I_FIRST, I_MODE, I_FETCH, I_START, I_PREP, I_PREP_SLOT, I_SLOT) = range(11)
ITEM_FIELDS = 16
ROW_BLOCKS = 4


def _plan_kernel(tm, n_tiles, cnt_ref, off_ref, items_ref):
    n = items_ref.shape[1]
    cnt_r = cnt_ref[...]
    sub = lax.broadcasted_iota(jnp.int32, (LANES, LANES), 0).astype(F32)
    lan = lax.broadcasted_iota(jnp.int32, (LANES, LANES), 1).astype(F32)
    big = float(4 * LANES)

    def col(row):
        return jnp.sum(jnp.where(lan == sub, row, 0.0), axis=1, keepdims=True)

    def row(column):
        return jnp.sum(jnp.where(lan == sub, column, 0.0), axis=0, keepdims=True)

    def prefix(r):
        return jnp.sum(jnp.where(lan <= sub, r, 0.0), axis=1, keepdims=True)

    cnt = col(cnt_r)
    ends = prefix(cnt_r)
    starts = ends - cnt
    off_ref[...] = row(starts)
    used = cnt > 0.0
    inv_tm = 1.0 / tm
    first_tile = jnp.floor(starts * inv_tm)
    last_tile = jnp.floor(jnp.maximum(ends - 1.0, 0.0) * inv_tm)
    per = jnp.where(used, last_tile - first_tile + 1.0, 0.0)
    item_end = prefix(row(per))
    item_start = item_end - per
    used_r = row(used.astype(F32))
    ordinal = prefix(used_r) - 1.0
    nxt = jnp.min(jnp.where((lan > sub) & (used_r > 0.0), lan, big), axis=1, keepdims=True)
    has_next = nxt < big
    e_col = sub[:, 0:1]
    nxt = jnp.where(has_next, nxt, e_col)
    e_first = jnp.min(jnp.where(used_r > 0.0, lan[0:1, :], big), axis=1, keepdims=True)
    e_last = jnp.max(jnp.where(used_r > 0.0, lan[0:1, :], -1.0), axis=1, keepdims=True)
    total = jnp.max(item_end, axis=0, keepdims=True)

    item = lax.broadcasted_iota(jnp.int32, (LANES, n), 1).astype(F32) - 1.0
    mine = ((item >= item_start) & (item < item_end)).astype(F32)
    pick = lambda column: jnp.sum(mine * column, axis=0, keepdims=True)
    item_r = item[0:1, :]
    valid = pick(jnp.ones_like(cnt))
    lead = item_r < 0.0
    past = item_r >= total
    new_tile = 1.0 - pick(((item == item_start) & (starts - first_tile * tm > 0.0)).astype(F32))

    def put(field, value):
        items_ref[field:field + 1, :] = value.astype(jnp.int32)

    tile = pick(first_tile - item_start) + item_r
    lo = pick(starts)
    hi = pick(ends)
    inv_block = ROW_BLOCKS * inv_tm
    first_block = jnp.floor(jnp.maximum(lo - tile * tm, 0.0) * inv_block)
    end_block = jnp.floor((jnp.minimum(hi - tile * tm, tm) + (tm / ROW_BLOCKS - 1.0)) * inv_block)
    mode = jnp.where(first_block == 0.0, end_block,
                     jnp.where(end_block == float(ROW_BLOCKS), ROW_BLOCKS + first_block, float(ROW_BLOCKS)))
    put(I_TILE, jnp.where(past, float(n_tiles - 1), jnp.where(lead, 0.0, tile)))
    put(I_EXPERT, jnp.where(past, e_last, jnp.where(lead, e_first, pick(e_col))))
    put(I_LO, lo)
    put(I_HI, hi)
    put(I_FIRST, valid * new_tile)
    put(I_MODE, valid * mode)
    put(I_FETCH, jnp.where(past, e_last, jnp.where(lead, e_first, pick(nxt))))
    put(I_START, jnp.where(lead, 1.0, pick(((item == item_start) & has_next).astype(F32))))
    put(I_PREP, jnp.where(lead, 1.0, pick(((item == item_end - 1.0) & has_next).astype(F32))))
    put(I_PREP_SLOT, pick(ordinal + 1.0 - 2.0 * jnp.floor((ordinal + 1.0) * 0.5)))
    put(I_SLOT, pick(ordinal - 2.0 * jnp.floor(ordinal * 0.5)))
    for field in range(I_SLOT + 1, ITEM_FIELDS):
        put(field, jnp.zeros_like(valid))


def _plan(counts, n_rows, tm):
    n_tiles = n_rows // tm
    n_items = n_tiles + N_EXPERTS
    width = -(-n_items // LANES) * LANES
    offsets, items = pl.pallas_call(
        functools.partial(_plan_kernel, float(tm), n_tiles),
        out_shape=[jax.ShapeDtypeStruct((1, LANES), F32),
                   jax.ShapeDtypeStruct((ITEM_FIELDS, width), jnp.int32)],
        name="plan",
    )(counts)
    return offsets, items, n_items


def _experts(xs, items, n_items, w_gate_up, bg, bu, w_down, bd):
    n_rows = xs.shape[0]
    tm = EXPERT_TILE
    half = D_MODEL // 2
    j = np.arange(2 * LANES)
    perm = np.zeros((2 * LANES, 2 * LANES), np.float32)
    perm[j, np.where(j % 2 == 0, j // 2, LANES + j // 2)] = 1.0
    row_tile = pl.BlockSpec((tm, half), lambda i, items: (items[I_TILE, i], 0))
    by_expert = lambda *shape: pl.BlockSpec((None,) + shape, lambda i, items: (items[I_EXPERT, i], 0, 0))
    in_hbm = pl.BlockSpec(memory_space=pl.ANY)
    return pl.pallas_call(
        _experts_kernel,
        grid_spec=pltpu.PrefetchScalarGridSpec(
            num_scalar_prefetch=1,
            grid=(n_items,),
            in_specs=[row_tile, in_hbm, by_expert(1, D_MODEL), by_expert(1, D_MODEL), in_hbm,
                      by_expert(1, D_MODEL),
                      pl.BlockSpec((2 * LANES, 2 * LANES), lambda i, items: (0, 0))],
            out_specs=row_tile,
            scratch_shapes=[pltpu.VMEM((D_MODEL, 2 * D_MODEL), F32), pltpu.VMEM((D_MODEL, D_MODEL), F32)]
            + [pltpu.VMEM((2, D_MODEL, D_MODEL), BF16)] * 3 + [pltpu.SemaphoreType.DMA((2,))],
        ),
        out_shape=jax.ShapeDtypeStruct((n_rows, half), U32),
        compiler_params=_params(("arbitrary",)),
        name="experts",
    )(items, xs, w_gate_up, bg, bu, w_down, bd, jnp.asarray(perm, dtype=BF16))


def _tail_kernel(h1_ref, yu_ref, gates_ref, p_ref, nple_ref, wg_ref, wp_ref, nfin_ref, *rest):
    out_ref = rest[-1]
    quarter = D_MODEL // 4
    for r in range(0, h1_ref.shape[0], TAIL_ROWS):
        rows = slice(r, r + TAIL_ROWS)
        gates = gates_ref[rows, :]
        lo = None
        hi = None
        for k in range(TOP_K):
            gk = gates[:, k:k + 1]
            yl, yh = _unpack_bf16_pair(yu_ref[k, rows, :])
            lo = gk * yl if lo is None else lo + gk * yl
            hi = gk * yh if hi is None else hi + gk * yh
        moe = jnp.concatenate([lo[:, :quarter], hi[:, :quarter], lo[:, quarter:], hi[:, quarter:]], axis=1)
        h2 = h1_ref[rows, :] + moe
        hn = _rms(h2, nple_ref[...]).astype(BF16)
        gate = _sigmoid(_dot(hn, wg_ref[...]))
        h3 = h2 + gate * _dot(p_ref[rows, :].astype(BF16), wp_ref[...])
        out_ref[rows, :] = _rms(h3, nfin_ref[...])


def _tail(h1, yu, gates, p2, norm_ple_w, ple_gate_bf, ple_proj_bf, final_norm_w, local0, tile0, result):
    t = yu.shape[1]
    tm = TOKEN_TILE
    half = D_MODEL // 2
    tile = lambda width: pl.BlockSpec((tm, width), lambda i: (i + local0, 0))
    shifted = lambda width: pl.BlockSpec((tm, width), lambda i: (i + tile0 + local0, 0))
    const2 = lambda shape: pl.BlockSpec(shape, lambda i: (0, 0))
    in_specs = [tile(D_MODEL), pl.BlockSpec((TOP_K, tm, half), lambda i: (0, i, 0)),
                tile(LANES), shifted(PLE_DIM), const2((1, D_MODEL)),
                const2((D_MODEL, D_MODEL)), const2((PLE_DIM, D_MODEL)), const2((1, D_MODEL))]
    args = [h1, yu, gates, p2, norm_ple_w, ple_gate_bf, ple_proj_bf, final_norm_w]
    aliases = {}
    if result is not None:
        in_specs.append(pl.BlockSpec(memory_space=pl.ANY))
        args.append(result)
        aliases = {len(args) - 1: 0}
    return pl.pallas_call(
        _tail_kernel,
        grid=(t // tm,),
        in_specs=in_specs,
        out_specs=shifted(D_MODEL),
        out_shape=jax.ShapeDtypeStruct((p2.shape[0], D_MODEL), F32),
        input_output_aliases=aliases,
        compiler_params=_params(("parallel",)),
        name="tail",
    )(*args)


def kernel(x, p, positions, w_in, w_out, ret_gn_w, pool_w, pool_scale, norm_mix_w, norm_moe_w, router_w, router_b, expert_w_gate_up, expert_b_gate_up, expert_w_down, expert_b_down, norm_ple_w, ple_gate_w, ple_proj_w, final_norm_w):
    batch, seq, d = x.shape
    depth = w_in.shape[0]
    assert depth == 1 and d == D_MODEL and seq % TOKEN_TILE == 0
    assert seq % (RET_CHUNK * RET_STEP_CHUNKS) == 0
    group_batches = GROUP_BATCHES if sum(GROUP_BATCHES) == batch else (batch,)
    t = batch * seq
    row = lambda a: a.reshape(1, -1).astype(F32)
    l = 0

    x2 = x.reshape(t, d)
    pos2 = positions.reshape(t, 1)
    p2 = p[l].reshape(t, PLE_DIM)
    w_in_bf = w_in[l].astype(BF16)
    rw = jnp.pad(router_w[l].astype(F32), ((0, 0), (0, LANES - N_EXPERTS)))
    rw_hi = rw.astype(BF16)
    rw_lo = (rw - rw_hi.astype(F32)).astype(BF16)
    rw_split = jnp.concatenate([rw_hi, rw_lo], axis=1)
    rb = jnp.pad(router_b[l].astype(F32), (0, LANES - N_EXPERTS), constant_values=NEG_BIG).reshape(1, LANES)
    bgu = expert_b_gate_up[l].reshape(N_EXPERTS, 1, D_MODEL, 2).astype(F32)
    bdn = expert_b_down[l].reshape(N_EXPERTS, 1, D_MODEL).astype(F32)
    pool_w_bf, w_out_bf = pool_w[l].astype(BF16), w_out[l].astype(BF16)
    ple_gate_bf, ple_proj_bf = ple_gate_w[l].astype(BF16), ple_proj_w[l].astype(BF16)

    out = None
    first_batch = 0
    for gi, nb in enumerate(group_batches):
        tg = nb * seq
        n_rows = tg * TOP_K
        assert n_rows % EXPERT_TILE == 0
        tile0 = first_batch * seq // TOKEN_TILE
        ret, u = _retention(x2, pos2, row(norm_mix_w[l]), w_in_bf, row(ret_gn_w[l]),
                            first_batch * seq // (RET_CHUNK * RET_STEP_CHUNKS), nb, seq)
        first_batch += nb
        h1, hn_packed, code, gates, counts = _mix_router(
            x2, ret, u, pool_w_bf, row(pool_scale[l]), w_out_bf, row(norm_moe_w[l]), rw_split, rb,
            seq, tile0)

        offsets, items, n_items = _plan(counts, n_rows, EXPERT_TILE)
        pos = _positions(code, offsets)
        xs = _sc_scatter(hn_packed, pos)
        y = _experts(xs, items, n_items, expert_w_gate_up[l], bgu[..., 0], bgu[..., 1],
                     expert_w_down[l], bdn)

        spans = LAST_GROUP_SPANS if gi == len(group_batches) - 1 else 1
        span = tg // spans
        for si in range(spans):
            idx = pos[:TOP_K, si * span:(si + 1) * span].reshape(TOP_K * span)
            yu = _sc_gather(y, idx).reshape(TOP_K, span, d // 2)
            out = _tail(h1, yu, gates, p2, row(norm_ple_w[l]), ple_gate_bf, ple_proj_bf,
                        row(final_norm_w), si * (span // TOKEN_TILE), tile0, out)
    return out.reshape(batch, seq, d)
```

```python
import functools

import numpy as np
import jax
import jax.numpy as jnp
from jax import lax
from jax.experimental import pallas as pl
from jax.experimental.pallas import tpu as pltpu
from jax.experimental.pallas import tpu_sc as plsc

D_MODEL = 1024
D_RET = 512
D_POOL = 512
RET_HEADS = 8
RET_HEAD_DIM = 64
HEAD_PAIRS = RET_HEADS // 2
ROPE_BASE = 10000.0
POOL_WINDOWS = (2, 4, 8, 16)
POOL_GROUP_DIM = 128
POOL_HISTORY = 16
POOL_BLOCK = 128
D_IN_PROJ = 4 * D_RET + D_POOL
N_EXPERTS = 32
TOP_K = 4
SWIGLU_LIMIT = 7.0
SWIGLU_ALPHA = 1.702
PLE_DIM = 256
NORM_EPS = 1e-5
GN_EPS = 1e-5

LANES = 128
NEG_BIG = -1e30

TOKEN_TILE = 512
POSITIONS_TILE = 1024
TAIL_ROWS = 256
POS_ROWS = 8
RET_CHUNK = 256
RET_STEP_CHUNKS = 4
EXPERT_TILE = 512
EXPERT_COL_CHUNK = 512
SC_WINDOW = 128
GROUP_BATCHES = (6, 2)
LAST_GROUP_SPANS = 2
VMEM_LIMIT = 56 * 1024 * 1024

F32 = jnp.float32
BF16 = jnp.bfloat16
U32 = jnp.uint32


def _params(semantics):
    return pltpu.CompilerParams(dimension_semantics=semantics, vmem_limit_bytes=VMEM_LIMIT)


def _dot(a, b):
    return jnp.dot(a, b, preferred_element_type=F32)


def _dot_nt(a, b):
    return lax.dot_general(a, b, (((1,), (1,)), ((), ())), preferred_element_type=F32)


def _rms(x, w):
    ms = jnp.mean(x * x, axis=-1, keepdims=True)
    return x * lax.rsqrt(ms + NORM_EPS) * w


def _sigmoid(z):
    return 1.0 / (1.0 + jnp.exp(-z))


def _pack_bf16_pair(lo, hi):
    lo_bits = pltpu.bitcast(lo.astype(BF16).astype(F32), U32) >> 16
    hi_bits = pltpu.bitcast(hi.astype(BF16).astype(F32), U32) & jnp.uint32(0xFFFF0000)
    return lo_bits | hi_bits


def _unpack_bf16_pair(packed):
    lo = pltpu.bitcast(packed << 16, F32)
    hi = pltpu.bitcast(packed & jnp.uint32(0xFFFF0000), F32)
    return lo, hi


def _project_rows(x, pos, nw_ref, w_ref, freq_ref, qkvg_ref, u_ref, rows):
    hn = _rms(x, nw_ref[...]).astype(BF16)
    ang = pos.astype(F32) * freq_ref[0:1, :]
    cos = jnp.cos(ang)
    sin = jnp.sin(ang)
    slab = 2 * LANES
    cos_t = jnp.concatenate([cos, cos], axis=1)
    sin_up = jnp.concatenate([sin * freq_ref[1:2, :]] * 2, axis=1)
    sin_dn = jnp.concatenate([sin * freq_ref[2:3, :]] * 2, axis=1)
    half = RET_HEAD_DIM // 2

    for s in range(2 * D_RET // slab):
        sl = slice(s * slab, (s + 1) * slab)
        v = _dot(hn, w_ref[:, sl])
        v = v * cos_t + pltpu.roll(v, slab - half, 1) * sin_up + pltpu.roll(v, half, 1) * sin_dn
        if s >= D_RET // slab:
            v = v * (RET_HEAD_DIM ** -0.5)
        qkvg_ref[rows, sl] = v.astype(BF16)
    for s in range(2 * D_RET // 512, 4 * D_RET // 512):
        sl = slice(s * 512, (s + 1) * 512)
        qkvg_ref[rows, sl] = _dot(hn, w_ref[:, sl]).astype(BF16)
    u_ref[rows, :] = _dot(hn, w_ref[:, 4 * D_RET:]).astype(BF16)


def _rope_table():
    j = np.arange(LANES)
    half = RET_HEAD_DIM // 2
    inv_freq = ROPE_BASE ** (-(np.arange(half, dtype=np.float32)) / half)
    freq = np.zeros((8, LANES), np.float32)
    freq[0] = inv_freq[j % half]
    freq[1] = np.where(j % RET_HEAD_DIM < half, -1.0, 0.0)
    freq[2] = np.where(j % RET_HEAD_DIM >= half, 1.0, 0.0)
    return jnp.asarray(freq)


def _retention_kernel(x_ref, pos_ref, nw_ref, w_ref, freq_ref, dec_ref, xi_ref, zeta_ref, cd_ref, bd_ref,
                      m64_ref, eye_ref, gnw_ref, out_ref, u_ref, state_ref, qkvg_ref):
    @pl.when(pl.program_id(1) == 0)
    def _():
        state_ref[...] = jnp.zeros_like(state_ref)

    c = dec_ref.shape[1]
    n_chunks = x_ref.shape[0] // c
    lane = lax.broadcasted_iota(jnp.int32, (1, LANES), 1)
    m64 = m64_ref[...]
    bd = bd_ref[...]
    eye = eye_ref[...]

    def group_mean(v):
        hi = v.astype(BF16)
        lo = (v - hi.astype(F32)).astype(BF16)
        return _dot(jnp.concatenate([hi, lo], axis=1), m64)

    @pl.loop(0, n_chunks)
    def _(ci):
        rows = pl.ds(pl.multiple_of(ci * c, c), c)
        _project_rows(x_ref[rows, :], pos_ref[rows, :], nw_ref, w_ref, freq_ref, qkvg_ref, u_ref, rows)
        for p in range(HEAD_PAIRS):
            sl = slice(p * LANES, (p + 1) * LANES)
            qp = qkvg_ref[rows, sl]
            kp = qkvg_ref[rows, D_RET + p * LANES:D_RET + (p + 1) * LANES]
            vp = qkvg_ref[rows, 2 * D_RET + p * LANES:2 * D_RET + (p + 1) * LANES]
            y = None
            for hh in range(2):
                in_head = (lane >= RET_HEAD_DIM) == bool(hh)
                qm = jnp.where(in_head, qp, jnp.zeros_like(qp))
                vm = jnp.where(in_head, vp, jnp.zeros_like(vp))
                scores = _dot_nt(qm, kp) * dec_ref[2 * p + hh]
                part = _dot(scores.astype(BF16), vm)
                y = part if y is None else y + part
            st = state_ref[p]
            y = y + _dot((qp.astype(F32) * xi_ref[p]).astype(BF16), st.astype(BF16))
            kz = (kp.astype(F32) * zeta_ref[p]).astype(BF16)
            kz_t = _dot_nt(eye, kz).astype(BF16)
            state_ref[p] = cd_ref[p] * st + _dot(kz_t, vp) * bd
            mu = group_mean(y)
            var = group_mean(y * y) - mu * mu
            yn = (y - mu) * lax.rsqrt(var + GN_EPS) * gnw_ref[:, sl]
            g = qkvg_ref[rows, 3 * D_RET + p * LANES:3 * D_RET + (p + 1) * LANES].astype(F32)
            out_ref[rows, sl] = (yn * g * _sigmoid(g)).astype(BF16)


def _retention_tables(c):
    h = np.arange(RET_HEADS, dtype=np.float64)
    log_gamma = np.log1p(-np.power(2.0, -5.0 - h))
    idx = np.arange(c, dtype=np.float64)
    rel = idx[:, None] - idx[None, :]
    dec = np.where(rel >= 0, np.exp(np.where(rel >= 0, rel, 0.0)[None] * log_gamma[:, None, None]), 0.0)
    lane_head = np.arange(LANES) // RET_HEAD_DIM
    xi = np.zeros((HEAD_PAIRS, c, LANES))
    zeta = np.zeros((HEAD_PAIRS, c, LANES))
    cd = np.zeros((HEAD_PAIRS, LANES, LANES))
    same = lane_head[:, None] == lane_head[None, :]
    for p in range(HEAD_PAIRS):
        lg = log_gamma[2 * p + lane_head]
        xi[p] = np.exp((idx + 1.0)[:, None] * lg[None, :])
        zeta[p] = np.exp((c - 1 - idx)[:, None] * lg[None, :])
        cd[p] = np.where(same, np.exp(c * lg)[:, None], 0.0)
    bd = same.astype(np.float32)
    m64 = np.concatenate([same, same], axis=0).astype(np.float32) / RET_HEAD_DIM
    f = lambda a: jnp.asarray(a, dtype=F32)
    return (f(dec), f(xi), f(zeta), f(cd), f(bd), jnp.asarray(m64, dtype=BF16),
            jnp.asarray(np.eye(LANES, dtype=np.float32), dtype=BF16))


def _retention(x2, pos2, norm_w, w_in_bf, gn_w, block0, batch, seq):
    c = RET_CHUNK
    rows = RET_STEP_CHUNKS * c
    n = seq // rows
    t = batch * seq
    dec, xi, zeta, cd, bd, m64, eye = _retention_tables(c)
    const3 = lambda shape: pl.BlockSpec(shape, lambda b, i: (0, 0, 0))
    const2 = lambda shape: pl.BlockSpec(shape, lambda b, i: (0, 0))
    out_tile = pl.BlockSpec((rows, D_RET), lambda b, i: (b * n + i, 0))
    return pl.pallas_call(
        _retention_kernel,
        grid=(batch, n),
        in_specs=[pl.BlockSpec((rows, D_MODEL), lambda b, i: (block0 + b * n + i, 0)),
                  pl.BlockSpec((rows, 1), lambda b, i: (block0 + b * n + i, 0)),
                  const2((1, D_MODEL)), const2((D_MODEL, D_IN_PROJ)), const2((8, LANES)),
                  const3((RET_HEADS, c, c)), const3((HEAD_PAIRS, c, LANES)),
                  const3((HEAD_PAIRS, c, LANES)), const3((HEAD_PAIRS, LANES, LANES)),
                  const2((LANES, LANES)), const2((2 * LANES, LANES)), const2((LANES, LANES)),
                  const2((1, D_RET))],
        out_specs=[out_tile, out_tile],
        out_shape=[jax.ShapeDtypeStruct((t, D_RET), BF16), jax.ShapeDtypeStruct((t, D_POOL), BF16)],
        scratch_shapes=[pltpu.VMEM((HEAD_PAIRS, LANES, LANES), F32), pltpu.VMEM((rows, 4 * D_RET), BF16)],
        compiler_params=_params(("arbitrary", "arbitrary")),
        name="retention",
    )(x2, pos2, norm_w, w_in_bf, _rope_table(), dec, xi, zeta, cd, bd, m64, eye, gn_w)


def _mix_router_kernel(seq, x_ref, ret_ref, u_ref, uprev_ref, band_ref, pw_ref, ps_ref, wout_ref,
                       nw_ref, rw_ref, rb_ref,
                       h1_ref, hnp_ref, code_ref, gates_ref, cnt_ref, uext_ref):
    i = pl.program_id(0)
    tm = x_ref.shape[0]
    t0 = lax.rem(i * tm, seq)

    @pl.when(i == 0)
    def _():
        cnt_ref[...] = jnp.zeros_like(cnt_ref)

    prev = uprev_ref[...]
    uext_ref[0:POOL_HISTORY, :] = jnp.where(t0 == 0, jnp.zeros_like(prev), prev)
    uext_ref[POOL_HISTORY:, :] = u_ref[...]

    row = lax.broadcasted_iota(jnp.int32, (tm, 1), 0)
    t_seq = (t0 + row + 1).astype(F32)
    mixed = []
    for gi, w in enumerate(POOL_WINDOWS):
        sl = slice(gi * POOL_GROUP_DIM, (gi + 1) * POOL_GROUP_DIM)
        wsum = jnp.concatenate(
            [_dot(band_ref[gi], uext_ref[r0:r0 + POOL_BLOCK + POOL_HISTORY, sl])
             for r0 in range(0, tm, POOL_BLOCK)], axis=0)
        count = jnp.minimum(t_seq, float(w))
        pooled = wsum / count - u_ref[:, sl].astype(F32)
        mixed.append(_dot(pooled.astype(BF16), pw_ref[gi]))
    pool = (jnp.concatenate(mixed, axis=1) * ps_ref[...]).astype(BF16)

    h1 = (x_ref[...] + _dot(ret_ref[...], wout_ref[0:D_RET, :])
          + _dot(pool, wout_ref[D_RET:, :]))
    h1_ref[...] = h1
    hn = _rms(h1, nw_ref[...])
    hn_hi = hn.astype(BF16)
    half = D_MODEL // 2
    hnp_ref[...] = _pack_bf16_pair(hn[:, :half], hn[:, half:])

    hn_lo = (hn - hn_hi.astype(F32)).astype(BF16)
    both = _dot(hn_hi, rw_ref[...])
    logits = (both[:, :LANES] + both[:, LANES:] + _dot(hn_lo, rw_ref[:, :LANES])
              + rb_ref[...])

    lane = lax.broadcasted_iota(jnp.int32, (tm, LANES), 1).astype(F32)
    code = jnp.zeros((tm, LANES), F32)
    vals = []
    work = logits
    for k in range(TOP_K):
        m = jnp.max(work, axis=-1, keepdims=True)
        idx = jnp.min(jnp.where(work == m, lane, float(LANES)), axis=-1, keepdims=True)
        chosen = lane == idx
        code = jnp.where(chosen, float(k + 1), code)
        work = jnp.where(chosen, -jnp.inf, work)
        vals.append(m)
    exps = [jnp.exp(v - vals[0]) for v in vals]
    denom = exps[0] + exps[1] + exps[2] + exps[3]
    gates = jnp.zeros((tm, LANES), F32)
    for k in range(TOP_K):
        gates = jnp.where(lane == float(k), exps[k] / denom, gates)
    code_ref[...] = code
    gates_ref[...] = gates
    cnt_ref[...] += jnp.sum((code > 0).astype(F32), axis=0, keepdims=True)


def _pool_bands():
    r = np.arange(POOL_BLOCK)[:, None]
    s = np.arange(POOL_BLOCK + POOL_HISTORY)[None, :] - POOL_HISTORY
    bands = [((s <= r) & (s > r - w)) for w in POOL_WINDOWS]
    return jnp.asarray(np.stack(bands).astype(np.float32), dtype=BF16)


def _mix_router(x2, ret, u, pool_w_bf, pool_scale, w_out_bf, norm_w, rw_split, rb, seq, tile0):
    t = ret.shape[0]
    tm = TOKEN_TILE
    hist_blocks = tm // POOL_HISTORY
    tile = lambda width: pl.BlockSpec((tm, width), lambda i: (i, 0))
    const2 = lambda shape: pl.BlockSpec(shape, lambda i: (0, 0))
    const3 = lambda shape: pl.BlockSpec(shape, lambda i: (0, 0, 0))
    return pl.pallas_call(
        functools.partial(_mix_router_kernel, seq),
        grid=(t // tm,),
        in_specs=[
            pl.BlockSpec((tm, D_MODEL), lambda i: (i + tile0, 0)), tile(D_RET), tile(D_POOL),
            pl.BlockSpec((POOL_HISTORY, D_POOL), lambda i: (jnp.maximum(i * hist_blocks - 1, 0), 0)),
            const3((len(POOL_WINDOWS), POOL_BLOCK, POOL_BLOCK + POOL_HISTORY)),
            const3((len(POOL_WINDOWS), POOL_GROUP_DIM, POOL_GROUP_DIM)),
            const2((1, D_POOL)), const2((D_MODEL, D_MODEL)), const2((1, D_MODEL)),
            const2((D_MODEL, 2 * LANES)), const2((1, LANES)),
        ],
        out_specs=[tile(D_MODEL), tile(D_MODEL // 2), tile(LANES), tile(LANES), const2((1, LANES))],
        out_shape=[
            jax.ShapeDtypeStruct((t, D_MODEL), F32),
            jax.ShapeDtypeStruct((t, D_MODEL // 2), U32),
            jax.ShapeDtypeStruct((t, LANES), F32),
            jax.ShapeDtypeStruct((t, LANES), F32),
            jax.ShapeDtypeStruct((1, LANES), F32),
        ],
        scratch_shapes=[pltpu.VMEM((tm + POOL_HISTORY, D_POOL), BF16)],
        compiler_params=_params(("arbitrary",)),
        name="mix_router",
    )(x2, ret, u, u, _pool_bands(), pool_w_bf, pool_scale, w_out_bf, norm_w, rw_split, rb)


def _positions_kernel(code_ref, off_ref, tri_ref, pos_ref, carry_ref):
    @pl.when(pl.program_id(0) == 0)
    def _():
        carry_ref[...] = jnp.zeros_like(carry_ref)

    code = code_ref[...]
    tm = code.shape[0]
    sel = (code > 0).astype(BF16)
    carry = carry_ref[...]
    rank = _dot(tri_ref[...], sel) + (carry + off_ref[...])
    carry_ref[...] = carry + jnp.sum(sel.astype(F32), axis=0, keepdims=True)
    lane = lax.broadcasted_iota(jnp.int32, (tm, LANES), 1)
    pos = jnp.zeros((tm, LANES), F32)
    for k in range(TOP_K):
        pk = jnp.sum(jnp.where(code == float(k + 1), rank, 0.0), axis=-1, keepdims=True)
        pos = jnp.where(lane == k, pk, pos)
    pos_ref[...] = pos.T[0:POS_ROWS, :].astype(jnp.int32)


def _positions(code, offsets):
    t = code.shape[0]
    tm = POSITIONS_TILE
    tri = jnp.asarray(np.tril(np.ones((tm, tm), np.float32), -1), dtype=BF16)
    return pl.pallas_call(
        _positions_kernel,
        grid=(t // tm,),
        in_specs=[pl.BlockSpec((tm, LANES), lambda i: (i, 0)),
                  pl.BlockSpec((1, LANES), lambda i: (0, 0)),
                  pl.BlockSpec((tm, tm), lambda i: (0, 0))],
        out_specs=pl.BlockSpec((POS_ROWS, tm), lambda i: (0, i)),
        out_shape=jax.ShapeDtypeStruct((POS_ROWS, t), jnp.int32),
        scratch_shapes=[pltpu.VMEM((1, LANES), F32)],
        compiler_params=_params(("arbitrary",)),
        name="positions",
    )(code, offsets, tri)


def _sc_mesh():
    return plsc.VectorSubcoreMesh(core_axis_name="core", subcore_axis_name="subcore")


def _sc_workers():
    info = plsc.get_sparse_core_info()
    return info.num_cores, info.num_cores * info.num_subcores


def _sc_gather(src, idx):
    n = idx.shape[0]
    d = src.shape[1]
    w = SC_WINDOW // 2
    num_cores, workers = _sc_workers()
    per = n // workers
    pairs = per // (2 * w)
    assert per * workers == n and pairs * 2 * w == per

    @functools.partial(
        pl.kernel, out_type=jax.ShapeDtypeStruct((n, d), src.dtype), mesh=_sc_mesh(),
        scratch_types=[pltpu.VMEM((w,), jnp.int32), pltpu.VMEM((w,), jnp.int32),
                       pltpu.VMEM((w, d), src.dtype), pltpu.VMEM((w, d), src.dtype),
                       pltpu.SemaphoreType.DMA, pltpu.SemaphoreType.DMA,
                       pltpu.SemaphoreType.DMA, pltpu.SemaphoreType.DMA],
        name="sc_gather")
    def gather(src_hbm, idx_hbm, out_hbm, idx0, idx1, rows0, rows1, g0, g1, w0, w1):
        first = (lax.axis_index("subcore") * num_cores + lax.axis_index("core")) * per

        def start_gather(win, idx_v, rows_v, sem):
            pltpu.sync_copy(idx_hbm.at[pl.ds(first + win * w, w)], idx_v)
            pltpu.async_copy(src_hbm.at[idx_v], rows_v, sem)

        def wait_gather(idx_v, rows_v, sem):
            pltpu.make_async_copy(src_hbm.at[idx_v], rows_v, sem).wait()

        def start_write(win, rows_v, sem):
            pltpu.async_copy(rows_v, out_hbm.at[pl.ds(first + win * w, w)], sem)

        def wait_write(rows_v, sem):
            pltpu.make_async_copy(rows_v, out_hbm.at[pl.ds(first, w)], sem).wait()

        start_gather(0, idx0, rows0, g0)

        @pl.loop(0, pairs)
        def _(j):
            even = 2 * j

            @pl.when(j > 0)
            def _():
                wait_write(rows1, w1)

            start_gather(even + 1, idx1, rows1, g1)
            wait_gather(idx0, rows0, g0)
            start_write(even, rows0, w0)
            wait_write(rows0, w0)

            @pl.when(j + 1 < pairs)
            def _():
                start_gather(even + 2, idx0, rows0, g0)

            wait_gather(idx1, rows1, g1)
            start_write(even + 1, rows1, w1)

        wait_write(rows1, w1)

    return gather(src, idx)


def _sc_scatter(src, idx):
    t, d = src.shape
    w = SC_WINDOW
    num_cores, workers = _sc_workers()
    per = t // workers
    assert idx.shape[0] >= TOP_K and idx.shape[1] == t and per * workers == t and per % w == 0

    @functools.partial(
        pl.kernel, out_type=jax.ShapeDtypeStruct((TOP_K * t, d), src.dtype), mesh=_sc_mesh(),
        scratch_types=[pltpu.VMEM((TOP_K, w), jnp.int32), pltpu.VMEM((w, d), src.dtype),
                       pltpu.SemaphoreType.DMA],
        name="sc_scatter")
    def scatter(src_hbm, idx_hbm, out_hbm, idx_v, rows_v, sem):
        first = (lax.axis_index("subcore") * num_cores + lax.axis_index("core")) * per

        @pl.loop(0, per // w)
        def _(j):
            base = first + j * w
            for k in range(TOP_K):
                pltpu.sync_copy(idx_hbm.at[k, pl.ds(base, w)], idx_v.at[k])
            pltpu.sync_copy(src_hbm.at[pl.ds(base, w)], rows_v)
            copies = [pltpu.async_copy(rows_v, out_hbm.at[idx_v.at[k]], sem) for k in range(TOP_K)]
            for c in copies:
                c.wait()

    return scatter(src, idx)


def _experts_kernel(items_ref, xs_ref, wgu_hbm, bg_ref, bu_ref, wdn_hbm, bd_ref, perm_ref, y_ref,
                    wgu_ref, wdn_ref, wg_ref, wu_ref, wd_ref, sems):
    i = pl.program_id(0)
    tm = xs_ref.shape[0]

    def weight_copies():
        e = items_ref[I_FETCH, i]
        return (pltpu.make_async_copy(wgu_hbm.at[e], wgu_ref, sems.at[0]),
                pltpu.make_async_copy(wdn_hbm.at[e], wdn_ref, sems.at[1]))

    @pl.when(items_ref[I_START, i] == 1)
    def _():
        for copy in weight_copies():
            copy.start()

    @pl.when(items_ref[I_PREP, i] == 1)
    def _():
        for copy in weight_copies():
            copy.wait()
        ps = items_ref[I_PREP_SLOT, i]
        perm = perm_ref[...]
        pair = 2 * LANES
        for c in range(2 * D_MODEL // pair):
            sel = _dot(wgu_ref[:, c * pair:(c + 1) * pair].astype(BF16), perm).astype(BF16)
            wg_ref[ps, :, c * LANES:(c + 1) * LANES] = sel[:, :LANES]
            wu_ref[ps, :, c * LANES:(c + 1) * LANES] = sel[:, LANES:]
        wd_ref[ps] = wdn_ref[...].astype(BF16)

    def expert_mlp(r0, r1):
        n = r1 - r0
        s = items_ref[I_SLOT, i]
        x_lo, x_hi = _unpack_bf16_pair(xs_ref[r0:r1, :])
        x = jnp.concatenate([x_lo.astype(BF16), x_hi.astype(BF16)], axis=1)
        acts = []
        chunks = [slice(c * EXPERT_COL_CHUNK, (c + 1) * EXPERT_COL_CHUNK)
                  for c in range(D_MODEL // EXPERT_COL_CHUNK)]
        for cs in chunks:
            gate = jnp.minimum(_dot(x, wg_ref[s, :, cs]) + bg_ref[:, cs], SWIGLU_LIMIT)
            up = jnp.clip(_dot(x, wu_ref[s, :, cs]) + bu_ref[:, cs], -SWIGLU_LIMIT, SWIGLU_LIMIT)
            acts.append(((up + 1.0) * (gate * _sigmoid(SWIGLU_ALPHA * gate))).astype(BF16))
        rows = items_ref[I_TILE, i] * tm + r0 + lax.broadcasted_iota(jnp.int32, (n, 1), 0)
        mine = (rows >= items_ref[I_LO, i]) & (rows < items_ref[I_HI, i])

        half, quarter = D_MODEL // 2, D_MODEL // 4
        for h in range(2):
            hs = slice(h * half, (h + 1) * half)
            yh = bd_ref[:, hs]
            for cs, act in zip(chunks, acts):
                yh = yh + _dot(act, wd_ref[s, cs, hs])
            cols = slice(h * quarter, (h + 1) * quarter)
            packed = _pack_bf16_pair(yh[:, :quarter], yh[:, quarter:])
            y_ref[r0:r1, cols] = jnp.where(mine, packed, y_ref[r0:r1, cols])

    @pl.when(items_ref[I_FIRST, i] == 1)
    def _():
        y_ref[...] = jnp.zeros(y_ref.shape, U32)

    block = tm // ROW_BLOCKS
    for m in range(1, ROW_BLOCKS + 1):
        pl.when(items_ref[I_MODE, i] == m)(functools.partial(expert_mlp, 0, m * block))
    for m in range(1, ROW_BLOCKS):
        pl.when(items_ref[I_MODE, i] == ROW_BLOCKS + m)(functools.partial(expert_mlp, m * block, tm))


(I_TILE, I_EXPERT, I_LO, I_HI, I_FIRST, I_MODE, I_FETCH, I_START, I_PREP, I_PREP_SLOT, I_SLOT) = range(11)
ITEM_FIELDS = 16
ROW_BLOCKS = 4


def _plan_kernel(tm, n_tiles, cnt_ref, off_ref, items_ref):
    n = items_ref.shape[1]
    cnt_r = cnt_ref[...]
    sub = lax.broadcasted_iota(jnp.int32, (LANES, LANES), 0).astype(F32)
    lan = lax.broadcasted_iota(jnp.int32, (LANES, LANES), 1).astype(F32)
    big = float(4 * LANES)

    def col(row):
        return jnp.sum(jnp.where(lan == sub, row, 0.0), axis=1, keepdims=True)

    def row(column):
        return jnp.sum(jnp.where(lan == sub, column, 0.0), axis=0, keepdims=True)

    def prefix(r):
        return jnp.sum(jnp.where(lan <= sub, r, 0.0), axis=1, keepdims=True)

    cnt = col(cnt_r)
    ends = prefix(cnt_r)
    starts = ends - cnt
    off_ref[...] = row(starts)
    used = cnt > 0.0
    inv_tm = 1.0 / tm
    first_tile = jnp.floor(starts * inv_tm)
    last_tile = jnp.floor(jnp.maximum(ends - 1.0, 0.0) * inv_tm)
    per = jnp.where(used, last_tile - first_tile + 1.0, 0.0)
    item_end = prefix(row(per))
    item_start = item_end - per
    used_r = row(used.astype(F32))
    ordinal = prefix(used_r) - 1.0
    nxt = jnp.min(jnp.where((lan > sub) & (used_r > 0.0), lan, big), axis=1, keepdims=True)
    has_next = nxt < big
    e_col = sub[:, 0:1]
    nxt = jnp.where(has_next, nxt, e_col)
    e_first = jnp.min(jnp.where(used_r > 0.0, lan[0:1, :], big), axis=1, keepdims=True)
    e_last = jnp.max(jnp.where(used_r > 0.0, lan[0:1, :], -1.0), axis=1, keepdims=True)
    total = jnp.max(item_end, axis=0, keepdims=True)

    item = lax.broadcasted_iota(jnp.int32, (LANES, n), 1).astype(F32) - 1.0
    mine = ((item >= item_start) & (item < item_end)).astype(F32)
    pick = lambda column: jnp.sum(mine * column, axis=0, keepdims=True)
    item_r = item[0:1, :]
    valid = pick(jnp.ones_like(cnt))
    lead = item_r < 0.0
    past = item_r >= total
    new_tile = 1.0 - pick(((item == item_start) & (starts - first_tile * tm > 0.0)).astype(F32))

    def put(field, value):
        items_ref[field:field + 1, :] = value.astype(jnp.int32)

    tile = pick(first_tile - item_start) + item_r
    lo = pick(starts)
    hi = pick(ends)
    inv_block = ROW_BLOCKS * inv_tm
    first_block = jnp.floor(jnp.maximum(lo - tile * tm, 0.0) * inv_block)
    end_block = jnp.floor((jnp.minimum(hi - tile * tm, tm) + (tm / ROW_BLOCKS - 1.0)) * inv_block)
    mode = jnp.where(first_block == 0.0, end_block,
                     jnp.where(end_block == float(ROW_BLOCKS), ROW_BLOCKS + first_block, float(ROW_BLOCKS)))
    put(I_TILE, jnp.where(past, float(n_tiles - 1), jnp.where(lead, 0.0, tile)))
    put(I_EXPERT, jnp.where(past, e_last, jnp.where(lead, e_first, pick(e_col))))
    put(I_LO, lo)
    put(I_HI, hi)
    put(I_FIRST, valid * new_tile)
    put(I_MODE, valid * mode)
    put(I_FETCH, jnp.where(past, e_last, jnp.where(lead, e_first, pick(nxt))))
    put(I_START, jnp.where(lead, 1.0, pick(((item == item_start) & has_next).astype(F32))))
    put(I_PREP, jnp.where(lead, 1.0, pick(((item == item_end - 1.0) & has_next).astype(F32))))
    put(I_PREP_SLOT, pick(ordinal + 1.0 - 2.0 * jnp.floor((ordinal + 1.0) * 0.5)))
    put(I_SLOT, pick(ordinal - 2.0 * jnp.floor(ordinal * 0.5)))
    for field in range(I_SLOT + 1, ITEM_FIELDS):
        put(field, jnp.zeros_like(valid))


def _plan(counts, n_rows, tm):
    n_tiles = n_rows // tm
    n_items = n_tiles + N_EXPERTS
    width = -(-n_items // LANES) * LANES
    offsets, items = pl.pallas_call(
        functools.partial(_plan_kernel, float(tm), n_tiles),
        out_shape=[jax.ShapeDtypeStruct((1, LANES), F32),
                   jax.ShapeDtypeStruct((ITEM_FIELDS, width), jnp.int32)],
        name="plan",
    )(counts)
    return offsets, items, n_items


def _experts(xs, items, n_items, w_gate_up, bg, bu, w_down, bd):
    n_rows = xs.shape[0]
    tm = EXPERT_TILE
    half = D_MODEL // 2
    j = np.arange(2 * LANES)
    perm = np.zeros((2 * LANES, 2 * LANES), np.float32)
    perm[j, np.where(j % 2 == 0, j // 2, LANES + j // 2)] = 1.0
    row_tile = pl.BlockSpec((tm, half), lambda i, items: (items[I_TILE, i], 0))
    by_expert = lambda *shape: pl.BlockSpec((None,) + shape, lambda i, items: (items[I_EXPERT, i], 0, 0))
    in_hbm = pl.BlockSpec(memory_space=pl.ANY)
    return pl.pallas_call(
        _experts_kernel,
        grid_spec=pltpu.PrefetchScalarGridSpec(
            num_scalar_prefetch=1,
            grid=(n_items,),
            in_specs=[row_tile, in_hbm, by_expert(1, D_MODEL), by_expert(1, D_MODEL), in_hbm,
                      by_expert(1, D_MODEL),
                      pl.BlockSpec((2 * LANES, 2 * LANES), lambda i, items: (0, 0))],
            out_specs=row_tile,
            scratch_shapes=[pltpu.VMEM((D_MODEL, 2 * D_MODEL), F32), pltpu.VMEM((D_MODEL, D_MODEL), F32)]
            + [pltpu.VMEM((2, D_MODEL, D_MODEL), BF16)] * 3 + [pltpu.SemaphoreType.DMA((2,))],
        ),
        out_shape=jax.ShapeDtypeStruct((n_rows, half), U32),
        compiler_params=_params(("arbitrary",)),
        name="experts",
    )(items, xs, w_gate_up, bg, bu, w_down, bd, jnp.asarray(perm, dtype=BF16))


def _tail_kernel(h1_ref, yu_ref, gates_ref, p_ref, nple_ref, wg_ref, wp_ref, nfin_ref, *rest):
    out_ref = rest[-1]
    quarter = D_MODEL // 4
    for r in range(0, h1_ref.shape[0], TAIL_ROWS):
        rows = slice(r, r + TAIL_ROWS)
        gates = gates_ref[rows, :]
        lo = None
        hi = None
        for k in range(TOP_K):
            gk = gates[:, k:k + 1]
            yl, yh = _unpack_bf16_pair(yu_ref[k, rows, :])
            lo = gk * yl if lo is None else lo + gk * yl
            hi = gk * yh if hi is None else hi + gk * yh
        moe = jnp.concatenate([lo[:, :quarter], hi[:, :quarter], lo[:, quarter:], hi[:, quarter:]], axis=1)
        h2 = h1_ref[rows, :] + moe
        hn = _rms(h2, nple_ref[...]).astype(BF16)
        gate = _sigmoid(_dot(hn, wg_ref[...]))
        h3 = h2 + gate * _dot(p_ref[rows, :].astype(BF16), wp_ref[...])
        out_ref[rows, :] = _rms(h3, nfin_ref[...])


def _tail(h1, yu, gates, p2, norm_ple_w, ple_gate_bf, ple_proj_bf, final_norm_w, local0, tile0, result):
    t = yu.shape[1]
    tm = TOKEN_TILE
    half = D_MODEL // 2
    tile = lambda width: pl.BlockSpec((tm, width), lambda i: (i + local0, 0))
    shifted = lambda width: pl.BlockSpec((tm, width), lambda i: (i + tile0 + local0, 0))
    const2 = lambda shape: pl.BlockSpec(shape, lambda i: (0, 0))
    in_specs = [tile(D_MODEL), pl.BlockSpec((TOP_K, tm, half), lambda i: (0, i, 0)),
                tile(LANES), shifted(PLE_DIM), const2((1, D_MODEL)),
                const2((D_MODEL, D_MODEL)), const2((PLE_DIM, D_MODEL)), const2((1, D_MODEL))]
    args = [h1, yu, gates, p2, norm_ple_w, ple_gate_bf, ple_proj_bf, final_norm_w]
    aliases = {}
    if result is not None:
        in_specs.append(pl.BlockSpec(memory_space=pl.ANY))
        args.append(result)
        aliases = {len(args) - 1: 0}
    return pl.pallas_call(
        _tail_kernel,
        grid=(t // tm,),
        in_specs=in_specs,
        out_specs=shifted(D_MODEL),
        out_shape=jax.ShapeDtypeStruct((p2.shape[0], D_MODEL), F32),
        input_output_aliases=aliases,
        compiler_params=_params(("parallel",)),
        name="tail",
    )(*args)


def kernel(x, p, positions, w_in, w_out, ret_gn_w, pool_w, pool_scale, norm_mix_w, norm_moe_w, router_w, router_b, expert_w_gate_up, expert_b_gate_up, expert_w_down, expert_b_down, norm_ple_w, ple_gate_w, ple_proj_w, final_norm_w):
    batch, seq, d = x.shape
    depth = w_in.shape[0]
    assert depth == 1 and d == D_MODEL and seq % TOKEN_TILE == 0
    assert seq % (RET_CHUNK * RET_STEP_CHUNKS) == 0
    group_batches = GROUP_BATCHES if sum(GROUP_BATCHES) == batch else (batch,)
    t = batch * seq
    row = lambda a: a.reshape(1, -1).astype(F32)
    l = 0

    x2 = x.reshape(t, d)
    pos2 = positions.reshape(t, 1)
    p2 = p[l].reshape(t, PLE_DIM)
    w_in_bf = w_in[l].astype(BF16)
    rw = jnp.pad(router_w[l].astype(F32), ((0, 0), (0, LANES - N_EXPERTS)))
    rw_hi = rw.astype(BF16)
    rw_lo = (rw - rw_hi.astype(F32)).astype(BF16)
    rw_split = jnp.concatenate([rw_hi, rw_lo], axis=1)
    rb = jnp.pad(router_b[l].astype(F32), (0, LANES - N_EXPERTS), constant_values=NEG_BIG).reshape(1, LANES)
    bgu = expert_b_gate_up[l].reshape(N_EXPERTS, 1, D_MODEL, 2).astype(F32)
    bdn = expert_b_down[l].reshape(N_EXPERTS, 1, D_MODEL).astype(F32)
    pool_w_bf, w_out_bf = pool_w[l].astype(BF16), w_out[l].astype(BF16)
    ple_gate_bf, ple_proj_bf = ple_gate_w[l].astype(BF16), ple_proj_w[l].astype(BF16)

    out = None
    first_batch = 0
    for gi, nb in enumerate(group_batches):
        tg = nb * seq
        n_rows = tg * TOP_K
        assert n_rows % EXPERT_TILE == 0
        tile0 = first_batch * seq // TOKEN_TILE
        ret, u = _retention(x2, pos2, row(norm_mix_w[l]), w_in_bf, row(ret_gn_w[l]),
                            first_batch * seq // (RET_CHUNK * RET_STEP_CHUNKS), nb, seq)
        first_batch += nb
        h1, hn_packed, code, gates, counts = _mix_router(
            x2, ret, u, pool_w_bf, row(pool_scale[l]), w_out_bf, row(norm_moe_w[l]), rw_split, rb,
            seq, tile0)

        offsets, items, n_items = _plan(counts, n_rows, EXPERT_TILE)
        pos = _positions(code, offsets)
        xs = _sc_scatter(hn_packed, pos)
        y = _experts(xs, items, n_items, expert_w_gate_up[l], bgu[..., 0], bgu[..., 1],
                     expert_w_down[l], bdn)

        spans = LAST_GROUP_SPANS if gi == len(group_batches) - 1 else 1
        span = tg // spans
        for si in range(spans):
            idx = pos[:TOP_K, si * span:(si + 1) * span].reshape(TOP_K * span)
            yu = _sc_gather(y, idx).reshape(TOP_K, span, d // 2)
            out = _tail(h1, yu, gates, p2, row(norm_ple_w[l]), ple_gate_bf, ple_proj_bf,
                        row(final_norm_w), si * (span // TOKEN_TILE), tile0, out)
    return out.reshape(batch, seq, d)
```

```python
import functools

import numpy as np
import jax
import jax.numpy as jnp
from jax import lax
from jax.experimental import pallas as pl
from jax.experimental.pallas import tpu as pltpu
from jax.experimental.pallas import tpu_sc as plsc

D_MODEL = 1024
D_RET = 512
D_POOL = 512
RET_HEADS = 8
RET_HEAD_DIM = 64
HEAD_PAIRS = RET_HEADS // 2
ROPE_BASE = 10000.0
POOL_WINDOWS = (2, 4, 8, 16)
POOL_GROUP_DIM = 128
POOL_HISTORY = 16
POOL_BLOCK = 128
D_IN_PROJ = 4 * D_RET + D_POOL
N_EXPERTS = 32
TOP_K = 4
SWIGLU_LIMIT = 7.0
SWIGLU_ALPHA = 1.702
PLE_DIM = 256
NORM_EPS = 1e-5
GN_EPS = 1e-5

LANES = 128
NEG_BIG = -1e30

TOKEN_TILE = 512
POSITIONS_TILE = 1024
TAIL_ROWS = 256
POS_ROWS = 8
RET_CHUNK = 256
RET_STEP_CHUNKS = 4
EXPERT_TILE = 512
EXPERT_COL_CHUNK = 512
SC_WINDOW = 128
GROUP_BATCHES = (5, 3)
LAST_GROUP_SPANS = 2
VMEM_LIMIT = 56 * 1024 * 1024

F32 = jnp.float32
BF16 = jnp.bfloat16
U32 = jnp.uint32


def _params(semantics):
    return pltpu.CompilerParams(dimension_semantics=semantics, vmem_limit_bytes=VMEM_LIMIT)


def _dot(a, b):
    return jnp.dot(a, b, preferred_element_type=F32)


def _dot_nt(a, b):
    return lax.dot_general(a, b, (((1,), (1,)), ((), ())), preferred_element_type=F32)


def _rms(x, w):
    ms = jnp.mean(x * x, axis=-1, keepdims=True)
    return x * lax.rsqrt(ms + NORM_EPS) * w


def _sigmoid(z):
    return 1.0 / (1.0 + jnp.exp(-z))


def _pack_bf16_pair(lo, hi):
    lo_bits = pltpu.bitcast(lo.astype(BF16).astype(F32), U32) >> 16
    hi_bits = pltpu.bitcast(hi.astype(BF16).astype(F32), U32) & jnp.uint32(0xFFFF0000)
    return lo_bits | hi_bits


def _unpack_bf16_pair(packed):
    lo = pltpu.bitcast(packed << 16, F32)
    hi = pltpu.bitcast(packed & jnp.uint32(0xFFFF0000), F32)
    return lo, hi


def _project_rows(x, pos, nw_ref, w_ref, freq_ref, qkvg_ref, u_ref, rows):
    hn = _rms(x, nw_ref[...]).astype(BF16)
    ang = pos.astype(F32) * freq_ref[0:1, :]
    cos = jnp.cos(ang)
    sin = jnp.sin(ang)
    slab = 2 * LANES
    cos_t = jnp.concatenate([cos, cos], axis=1)
    sin_up = jnp.concatenate([sin * freq_ref[1:2, :]] * 2, axis=1)
    sin_dn = jnp.concatenate([sin * freq_ref[2:3, :]] * 2, axis=1)
    half = RET_HEAD_DIM // 2

    for s in range(2 * D_RET // slab):
        sl = slice(s * slab, (s + 1) * slab)
        v = _dot(hn, w_ref[:, sl])
        v = v * cos_t + pltpu.roll(v, slab - half, 1) * sin_up + pltpu.roll(v, half, 1) * sin_dn
        if s >= D_RET // slab:
            v = v * (RET_HEAD_DIM ** -0.5)
        qkvg_ref[rows, sl] = v.astype(BF16)
    for s in range(2 * D_RET // 512, 4 * D_RET // 512):
        sl = slice(s * 512, (s + 1) * 512)
        qkvg_ref[rows, sl] = _dot(hn, w_ref[:, sl]).astype(BF16)
    u_ref[rows, :] = _dot(hn, w_ref[:, 4 * D_RET:]).astype(BF16)


def _rope_table():
    j = np.arange(LANES)
    half = RET_HEAD_DIM // 2
    inv_freq = ROPE_BASE ** (-(np.arange(half, dtype=np.float32)) / half)
    freq = np.zeros((8, LANES), np.float32)
    freq[0] = inv_freq[j % half]
    freq[1] = np.where(j % RET_HEAD_DIM < half, -1.0, 0.0)
    freq[2] = np.where(j % RET_HEAD_DIM >= half, 1.0, 0.0)
    return jnp.asarray(freq)


def _retention_kernel(x_ref, pos_ref, nw_ref, w_ref, freq_ref, dec_ref, xi_ref, zeta_ref, cd_ref, bd_ref,
                      m64_ref, eye_ref, gnw_ref, out_ref, u_ref, state_ref, qkvg_ref):
    @pl.when(pl.program_id(1) == 0)
    def _():
        state_ref[...] = jnp.zeros_like(state_ref)

    c = dec_ref.shape[1]
    n_chunks = x_ref.shape[0] // c
    lane = lax.broadcasted_iota(jnp.int32, (1, LANES), 1)
    m64 = m64_ref[...]
    bd = bd_ref[...]
    eye = eye_ref[...]

    def group_mean(v):
        hi = v.astype(BF16)
        lo = (v - hi.astype(F32)).astype(BF16)
        return _dot(jnp.concatenate([hi, lo], axis=1), m64)

    for ci in range(n_chunks):
        rows = slice(ci * c, (ci + 1) * c)
        _project_rows(x_ref[rows, :], pos_ref[rows, :], nw_ref, w_ref, freq_ref, qkvg_ref, u_ref, rows)
        for p in range(HEAD_PAIRS):
            sl = slice(p * LANES, (p + 1) * LANES)
            qp = qkvg_ref[rows, sl]
            kp = qkvg_ref[rows, D_RET + p * LANES:D_RET + (p + 1) * LANES]
            vp = qkvg_ref[rows, 2 * D_RET + p * LANES:2 * D_RET + (p + 1) * LANES]
            y = None
            for hh in range(2):
                in_head = (lane >= RET_HEAD_DIM) == bool(hh)
                qm = jnp.where(in_head, qp, jnp.zeros_like(qp))
                vm = jnp.where(in_head, vp, jnp.zeros_like(vp))
                scores = _dot_nt(qm, kp) * dec_ref[2 * p + hh]
                part = _dot(scores.astype(BF16), vm)
                y = part if y is None else y + part
            st = state_ref[p]
            y = y + _dot((qp.astype(F32) * xi_ref[p]).astype(BF16), st.astype(BF16))
            kz = (kp.astype(F32) * zeta_ref[p]).astype(BF16)
            kz_t = _dot_nt(eye, kz).astype(BF16)
            state_ref[p] = cd_ref[p] * st + _dot(kz_t, vp) * bd
            mu = group_mean(y)
            var = group_mean(y * y) - mu * mu
            yn = (y - mu) * lax.rsqrt(var + GN_EPS) * gnw_ref[:, sl]
            g = qkvg_ref[rows, 3 * D_RET + p * LANES:3 * D_RET + (p + 1) * LANES].astype(F32)
            out_ref[rows, sl] = (yn * g * _sigmoid(g)).astype(BF16)


def _retention_tables(c):
    h = np.arange(RET_HEADS, dtype=np.float64)
    log_gamma = np.log1p(-np.power(2.0, -5.0 - h))
    idx = np.arange(c, dtype=np.float64)
    rel = idx[:, None] - idx[None, :]
    dec = np.where(rel >= 0, np.exp(np.where(rel >= 0, rel, 0.0)[None] * log_gamma[:, None, None]), 0.0)
    lane_head = np.arange(LANES) // RET_HEAD_DIM
    xi = np.zeros((HEAD_PAIRS, c, LANES))
    zeta = np.zeros((HEAD_PAIRS, c, LANES))
    cd = np.zeros((HEAD_PAIRS, LANES, LANES))
    same = lane_head[:, None] == lane_head[None, :]
    for p in range(HEAD_PAIRS):
        lg = log_gamma[2 * p + lane_head]
        xi[p] = np.exp((idx + 1.0)[:, None] * lg[None, :])
        zeta[p] = np.exp((c - 1 - idx)[:, None] * lg[None, :])
        cd[p] = np.where(same, np.exp(c * lg)[:, None], 0.0)
    bd = same.astype(np.float32)
    m64 = np.concatenate([same, same], axis=0).astype(np.float32) / RET_HEAD_DIM
    f = lambda a: jnp.asarray(a, dtype=F32)
    return (f(dec), f(xi), f(zeta), f(cd), f(bd), jnp.asarray(m64, dtype=BF16),
            jnp.asarray(np.eye(LANES, dtype=np.float32), dtype=BF16))


def _retention(x2, pos2, norm_w, w_in_bf, gn_w, block0, batch, seq):
    c = RET_CHUNK
    rows = RET_STEP_CHUNKS * c
    n = seq // rows
    t = batch * seq
    dec, xi, zeta, cd, bd, m64, eye = _retention_tables(c)
    const3 = lambda shape: pl.BlockSpec(shape, lambda b, i: (0, 0, 0))
    const2 = lambda shape: pl.BlockSpec(shape, lambda b, i: (0, 0))
    out_tile = pl.BlockSpec((rows, D_RET), lambda b, i: (b * n + i, 0))
    return pl.pallas_call(
        _retention_kernel,
        grid=(batch, n),
        in_specs=[pl.BlockSpec((rows, D_MODEL), lambda b, i: (block0 + b * n + i, 0)),
                  pl.BlockSpec((rows, 1), lambda b, i: (block0 + b * n + i, 0)),
                  const2((1, D_MODEL)), const2((D_MODEL, D_IN_PROJ)), const2((8, LANES)),
                  const3((RET_HEADS, c, c)), const3((HEAD_PAIRS, c, LANES)),
                  const3((HEAD_PAIRS, c, LANES)), const3((HEAD_PAIRS, LANES, LANES)),
                  const2((LANES, LANES)), const2((2 * LANES, LANES)), const2((LANES, LANES)),
                  const2((1, D_RET))],
        out_specs=[out_tile, out_tile],
        out_shape=[jax.ShapeDtypeStruct((t, D_RET), BF16), jax.ShapeDtypeStruct((t, D_POOL), BF16)],
        scratch_shapes=[pltpu.VMEM((HEAD_PAIRS, LANES, LANES), F32), pltpu.VMEM((rows, 4 * D_RET), BF16)],
        compiler_params=_params(("arbitrary", "arbitrary")),
        name="retention",
    )(x2, pos2, norm_w, w_in_bf, _rope_table(), dec, xi, zeta, cd, bd, m64, eye, gn_w)


def _mix_router_kernel(seq, x_ref, ret_ref, u_ref, uprev_ref, band_ref, pw_ref, ps_ref, wout_ref,
                       nw_ref, rw_ref, rb_ref,
                       h1_ref, hnp_ref, code_ref, gates_ref, cnt_ref, uext_ref):
    i = pl.program_id(0)
    tm = x_ref.shape[0]
    t0 = lax.rem(i * tm, seq)

    @pl.when(i == 0)
    def _():
        cnt_ref[...] = jnp.zeros_like(cnt_ref)

    prev = uprev_ref[...]
    uext_ref[0:POOL_HISTORY, :] = jnp.where(t0 == 0, jnp.zeros_like(prev), prev)
    uext_ref[POOL_HISTORY:, :] = u_ref[...]

    row = lax.broadcasted_iota(jnp.int32, (tm, 1), 0)
    t_seq = (t0 + row + 1).astype(F32)
    mixed = []
    for gi, w in enumerate(POOL_WINDOWS):
        sl = slice(gi * POOL_GROUP_DIM, (gi + 1) * POOL_GROUP_DIM)
        wsum = jnp.concatenate(
            [_dot(band_ref[gi], uext_ref[r0:r0 + POOL_BLOCK + POOL_HISTORY, sl])
             for r0 in range(0, tm, POOL_BLOCK)], axis=0)
        count = jnp.minimum(t_seq, float(w))
        pooled = wsum / count - u_ref[:, sl].astype(F32)
        mixed.append(_dot(pooled.astype(BF16), pw_ref[gi]))
    pool = (jnp.concatenate(mixed, axis=1) * ps_ref[...]).astype(BF16)

    h1 = (x_ref[...] + _dot(ret_ref[...], wout_ref[0:D_RET, :])
          + _dot(pool, wout_ref[D_RET:, :]))
    h1_ref[...] = h1
    hn = _rms(h1, nw_ref[...])
    hn_hi = hn.astype(BF16)
    half = D_MODEL // 2
    hnp_ref[...] = _pack_bf16_pair(hn[:, :half], hn[:, half:])

    hn_lo = (hn - hn_hi.astype(F32)).astype(BF16)
    both = _dot(hn_hi, rw_ref[...])
    logits = (both[:, :LANES] + both[:, LANES:] + _dot(hn_lo, rw_ref[:, :LANES])
              + rb_ref[...])

    lane = lax.broadcasted_iota(jnp.int32, (tm, LANES), 1).astype(F32)
    code = jnp.zeros((tm, LANES), F32)
    vals = []
    work = logits
    for k in range(TOP_K):
        m = jnp.max(work, axis=-1, keepdims=True)
        idx = jnp.min(jnp.where(work == m, lane, float(LANES)), axis=-1, keepdims=True)
        chosen = lane == idx
        code = jnp.where(chosen, float(k + 1), code)
        work = jnp.where(chosen, -jnp.inf, work)
        vals.append(m)
    exps = [jnp.exp(v - vals[0]) for v in vals]
    denom = exps[0] + exps[1] + exps[2] + exps[3]
    gates = jnp.zeros((tm, LANES), F32)
    for k in range(TOP_K):
        gates = jnp.where(lane == float(k), exps[k] / denom, gates)
    code_ref[...] = code
    gates_ref[...] = gates
    cnt_ref[...] += jnp.sum((code > 0).astype(F32), axis=0, keepdims=True)


def _pool_bands():
    r = np.arange(POOL_BLOCK)[:, None]
    s = np.arange(POOL_BLOCK + POOL_HISTORY)[None, :] - POOL_HISTORY
    bands = [((s <= r) & (s > r - w)) for w in POOL_WINDOWS]
    return jnp.asarray(np.stack(bands).astype(np.float32), dtype=BF16)


def _mix_router(x2, ret, u, pool_w_bf, pool_scale, w_out_bf, norm_w, rw_split, rb, seq, tile0):
    t = ret.shape[0]
    tm = TOKEN_TILE
    hist_blocks = tm // POOL_HISTORY
    tile = lambda width: pl.BlockSpec((tm, width), lambda i: (i, 0))
    const2 = lambda shape: pl.BlockSpec(shape, lambda i: (0, 0))
    const3 = lambda shape: pl.BlockSpec(shape, lambda i: (0, 0, 0))
    return pl.pallas_call(
        functools.partial(_mix_router_kernel, seq),
        grid=(t // tm,),
        in_specs=[
            pl.BlockSpec((tm, D_MODEL), lambda i: (i + tile0, 0)), tile(D_RET), tile(D_POOL),
            pl.BlockSpec((POOL_HISTORY, D_POOL), lambda i: (jnp.maximum(i * hist_blocks - 1, 0), 0)),
            const3((len(POOL_WINDOWS), POOL_BLOCK, POOL_BLOCK + POOL_HISTORY)),
            const3((len(POOL_WINDOWS), POOL_GROUP_DIM, POOL_GROUP_DIM)),
            const2((1, D_POOL)), const2((D_MODEL, D_MODEL)), const2((1, D_MODEL)),
            const2((D_MODEL, 2 * LANES)), const2((1, LANES)),
        ],
        out_specs=[tile(D_MODEL), tile(D_MODEL // 2), tile(LANES), tile(LANES), const2((1, LANES))],
        out_shape=[
            jax.ShapeDtypeStruct((t, D_MODEL), F32),
            jax.ShapeDtypeStruct((t, D_MODEL // 2), U32),
            jax.ShapeDtypeStruct((t, LANES), F32),
            jax.ShapeDtypeStruct((t, LANES), F32),
            jax.ShapeDtypeStruct((1, LANES), F32),
        ],
        scratch_shapes=[pltpu.VMEM((tm + POOL_HISTORY, D_POOL), BF16)],
        compiler_params=_params(("arbitrary",)),
        name="mix_router",
    )(x2, ret, u, u, _pool_bands(), pool_w_bf, pool_scale, w_out_bf, norm_w, rw_split, rb)


def _positions_kernel(code_ref, off_ref, tri_ref, pos_ref, carry_ref):
    @pl.when(pl.program_id(0) == 0)
    def _():
        carry_ref[...] = jnp.zeros_like(carry_ref)

    code = code_ref[...]
    tm = code.shape[0]
    sel = (code > 0).astype(BF16)
    carry = carry_ref[...]
    rank = _dot(tri_ref[...], sel) + (carry + off_ref[...])
    carry_ref[...] = carry + jnp.sum(sel.astype(F32), axis=0, keepdims=True)
    lane = lax.broadcasted_iota(jnp.int32, (tm, LANES), 1)
    pos = jnp.zeros((tm, LANES), F32)
    for k in range(TOP_K):
        pk = jnp.sum(jnp.where(code == float(k + 1), rank, 0.0), axis=-1, keepdims=True)
        pos = jnp.where(lane == k, pk, pos)
    pos_ref[...] = pos.T[0:POS_ROWS, :].astype(jnp.int32)


def _positions(code, offsets):
    t = code.shape[0]
    tm = POSITIONS_TILE
    tri = jnp.asarray(np.tril(np.ones((tm, tm), np.float32), -1), dtype=BF16)
    return pl.pallas_call(
        _positions_kernel,
        grid=(t // tm,),
        in_specs=[pl.BlockSpec((tm, LANES), lambda i: (i, 0)),
                  pl.BlockSpec((1, LANES), lambda i: (0, 0)),
                  pl.BlockSpec((tm, tm), lambda i: (0, 0))],
        out_specs=pl.BlockSpec((POS_ROWS, tm), lambda i: (0, i)),
        out_shape=jax.ShapeDtypeStruct((POS_ROWS, t), jnp.int32),
        scratch_shapes=[pltpu.VMEM((1, LANES), F32)],
        compiler_params=_params(("arbitrary",)),
        name="positions",
    )(code, offsets, tri)


def _sc_mesh():
    return plsc.VectorSubcoreMesh(core_axis_name="core", subcore_axis_name="subcore")


def _sc_workers():
    info = plsc.get_sparse_core_info()
    return info.num_cores, info.num_cores * info.num_subcores


def _sc_gather(src, idx):
    n = idx.shape[0]
    d = src.shape[1]
    w = SC_WINDOW // 2
    num_cores, workers = _sc_workers()
    per = n // workers
    pairs = per // (2 * w)
    assert per * workers == n and pairs * 2 * w == per

    @functools.partial(
        pl.kernel, out_type=jax.ShapeDtypeStruct((n, d), src.dtype), mesh=_sc_mesh(),
        scratch_types=[pltpu.VMEM((w,), jnp.int32), pltpu.VMEM((w,), jnp.int32),
                       pltpu.VMEM((w, d), src.dtype), pltpu.VMEM((w, d), src.dtype),
                       pltpu.SemaphoreType.DMA, pltpu.SemaphoreType.DMA,
                       pltpu.SemaphoreType.DMA, pltpu.SemaphoreType.DMA],
        name="sc_gather")
    def gather(src_hbm, idx_hbm, out_hbm, idx0, idx1, rows0, rows1, g0, g1, w0, w1):
        first = (lax.axis_index("subcore") * num_cores + lax.axis_index("core")) * per

        def start_gather(win, idx_v, rows_v, sem):
            pltpu.sync_copy(idx_hbm.at[pl.ds(first + win * w, w)], idx_v)
            pltpu.async_copy(src_hbm.at[idx_v], rows_v, sem)

        def wait_gather(idx_v, rows_v, sem):
            pltpu.make_async_copy(src_hbm.at[idx_v], rows_v, sem).wait()

        def start_write(win, rows_v, sem):
            pltpu.async_copy(rows_v, out_hbm.at[pl.ds(first + win * w, w)], sem)

        def wait_write(rows_v, sem):
            pltpu.make_async_copy(rows_v, out_hbm.at[pl.ds(first, w)], sem).wait()

        start_gather(0, idx0, rows0, g0)

        @pl.loop(0, pairs)
        def _(j):
            even = 2 * j

            @pl.when(j > 0)
            def _():
                wait_write(rows1, w1)

            start_gather(even + 1, idx1, rows1, g1)
            wait_gather(idx0, rows0, g0)
            start_write(even, rows0, w0)
            wait_write(rows0, w0)

            @pl.when(j + 1 < pairs)
            def _():
                start_gather(even + 2, idx0, rows0, g0)

            wait_gather(idx1, rows1, g1)
            start_write(even + 1, rows1, w1)

        wait_write(rows1, w1)

    return gather(src, idx)


def _sc_scatter(src, idx):
    t, d = src.shape
    w = SC_WINDOW
    num_cores, workers = _sc_workers()
    per = t // workers
    assert idx.shape[0] >= TOP_K and idx.shape[1] == t and per * workers == t and per % w == 0

    @functools.partial(
        pl.kernel, out_type=jax.ShapeDtypeStruct((TOP_K * t, d), src.dtype), mesh=_sc_mesh(),
        scratch_types=[pltpu.VMEM((TOP_K, w), jnp.int32), pltpu.VMEM((w, d), src.dtype),
                       pltpu.SemaphoreType.DMA],
        name="sc_scatter")
    def scatter(src_hbm, idx_hbm, out_hbm, idx_v, rows_v, sem):
        first = (lax.axis_index("subcore") * num_cores + lax.axis_index("core")) * per

        @pl.loop(0, per // w)
        def _(j):
            base = first + j * w
            for k in range(TOP_K):
                pltpu.sync_copy(idx_hbm.at[k, pl.ds(base, w)], idx_v.at[k])
            pltpu.sync_copy(src_hbm.at[pl.ds(base, w)], rows_v)
            copies = [pltpu.async_copy(rows_v, out_hbm.at[idx_v.at[k]], sem) for k in range(TOP_K)]
            for c in copies:
                c.wait()

    return scatter(src, idx)


def _experts_kernel(items_ref, xs_ref, wgu_hbm, bg_ref, bu_ref, wdn_hbm, bd_ref, perm_ref, y_ref,
                    wgu_ref, wdn_ref, wg_ref, wu_ref, wd_ref, sems):
    i = pl.program_id(0)
    tm = xs_ref.shape[0]

    def weight_copies():
        e = items_ref[I_FETCH, i]
        return (pltpu.make_async_copy(wgu_hbm.at[e], wgu_ref, sems.at[0]),
                pltpu.make_async_copy(wdn_hbm.at[e], wdn_ref, sems.at[1]))

    @pl.when(items_ref[I_START, i] == 1)
    def _():
        for copy in weight_copies():
            copy.start()

    @pl.when(items_ref[I_PREP, i] == 1)
    def _():
        for copy in weight_copies():
            copy.wait()
        ps = items_ref[I_PREP_SLOT, i]
        perm = perm_ref[...]
        pair = 2 * LANES
        for c in range(2 * D_MODEL // pair):
            sel = _dot(wgu_ref[:, c * pair:(c + 1) * pair].astype(BF16), perm).astype(BF16)
            wg_ref[ps, :, c * LANES:(c + 1) * LANES] = sel[:, :LANES]
            wu_ref[ps, :, c * LANES:(c + 1) * LANES] = sel[:, LANES:]
        wd_ref[ps] = wdn_ref[...].astype(BF16)

    def expert_mlp(r0, r1):
        n = r1 - r0
        s = items_ref[I_SLOT, i]
        x_lo, x_hi = _unpack_bf16_pair(xs_ref[r0:r1, :])
        x = jnp.concatenate([x_lo.astype(BF16), x_hi.astype(BF16)], axis=1)
        acts = []
        chunks = [slice(c * EXPERT_COL_CHUNK, (c + 1) * EXPERT_COL_CHUNK)
                  for c in range(D_MODEL // EXPERT_COL_CHUNK)]
        for cs in chunks:
            gate = jnp.minimum(_dot(x, wg_ref[s, :, cs]) + bg_ref[:, cs], SWIGLU_LIMIT)
            up = jnp.clip(_dot(x, wu_ref[s, :, cs]) + bu_ref[:, cs], -SWIGLU_LIMIT, SWIGLU_LIMIT)
            acts.append(((up + 1.0) * (gate * _sigmoid(SWIGLU_ALPHA * gate))).astype(BF16))
        rows = items_ref[I_TILE, i] * tm + r0 + lax.broadcasted_iota(jnp.int32, (n, 1), 0)
        mine = (rows >= items_ref[I_LO, i]) & (rows < items_ref[I_HI, i])

        half, quarter = D_MODEL // 2, D_MODEL // 4
        for h in range(2):
            hs = slice(h * half, (h + 1) * half)
            yh = bd_ref[:, hs]
            for cs, act in zip(chunks, acts):
                yh = yh + _dot(act, wd_ref[s, cs, hs])
            cols = slice(h * quarter, (h + 1) * quarter)
            packed = _pack_bf16_pair(yh[:, :quarter], yh[:, quarter:])
            y_ref[r0:r1, cols] = jnp.where(mine, packed, y_ref[r0:r1, cols])

    @pl.when(items_ref[I_FIRST, i] == 1)
    def _():
        y_ref[...] = jnp.zeros(y_ref.shape, U32)

    block = tm // ROW_BLOCKS
    for m in range(1, ROW_BLOCKS + 1):
        pl.when(items_ref[I_MODE, i] == m)(functools.partial(expert_mlp, 0, m * block))
    for m in range(1, ROW_BLOCKS):
        pl.when(items_ref[I_MODE, i] == ROW_BLOCKS + m)(functools.partial(expert_mlp, m * block, tm))


(I_TILE, I_EXPERT, I_LO, I_HI, I_FIRST, I_MODE, I_FETCH, I_START, I_PREP, I_PREP_SLOT, I_SLOT) = range(11)
ITEM_FIELDS = 16
ROW_BLOCKS = 4


def _plan_kernel(tm, n_tiles, cnt_ref, off_ref, items_ref):
    n = items_ref.shape[1]
    cnt_r = cnt_ref[...]
    sub = lax.broadcasted_iota(jnp.int32, (LANES, LANES), 0).astype(F32)
    lan = lax.broadcasted_iota(jnp.int32, (LANES, LANES), 1).astype(F32)
    big = float(4 * LANES)

    def col(row):
        return jnp.sum(jnp.where(lan == sub, row, 0.0), axis=1, keepdims=True)

    def row(column):
        return jnp.sum(jnp.where(lan == sub, column, 0.0), axis=0, keepdims=True)

    def prefix(r):
        return jnp.sum(jnp.where(lan <= sub, r, 0.0), axis=1, keepdims=True)

    cnt = col(cnt_r)
    ends = prefix(cnt_r)
    starts = ends - cnt
    off_ref[...] = row(starts)
    used = cnt > 0.0
    inv_tm = 1.0 / tm
    first_tile = jnp.floor(starts * inv_tm)
    last_tile = jnp.floor(jnp.maximum(ends - 1.0, 0.0) * inv_tm)
    per = jnp.where(used, last_tile - first_tile + 1.0, 0.0)
    item_end = prefix(row(per))
    item_start = item_end - per
    used_r = row(used.astype(F32))
    ordinal = prefix(used_r) - 1.0
    nxt = jnp.min(jnp.where((lan > sub) & (used_r > 0.0), lan, big), axis=1, keepdims=True)
    has_next = nxt < big
    e_col = sub[:, 0:1]
    nxt = jnp.where(has_next, nxt, e_col)
    e_first = jnp.min(jnp.where(used_r > 0.0, lan[0:1, :], big), axis=1, keepdims=True)
    e_last = jnp.max(jnp.where(used_r > 0.0, lan[0:1, :], -1.0), axis=1, keepdims=True)
    total = jnp.max(item_end, axis=0, keepdims=True)

    item = lax.broadcasted_iota(jnp.int32, (LANES, n), 1).astype(F32) - 1.0
    mine = ((item >= item_start) & (item < item_end)).astype(F32)
    pick = lambda column: jnp.sum(mine * column, axis=0, keepdims=True)
    item_r = item[0:1, :]
    valid = pick(jnp.ones_like(cnt))
    lead = item_r < 0.0
    past = item_r >= total
    new_tile = 1.0 - pick(((item == item_start) & (starts - first_tile * tm > 0.0)).astype(F32))

    def put(field, value):
        items_ref[field:field + 1, :] = value.astype(jnp.int32)

    tile = pick(first_tile - item_start) + item_r
    lo = pick(starts)
    hi = pick(ends)
    inv_block = ROW_BLOCKS * inv_tm
    first_block = jnp.floor(jnp.maximum(lo - tile * tm, 0.0) * inv_block)
    end_block = jnp.floor((jnp.minimum(hi - tile * tm, tm) + (tm / ROW_BLOCKS - 1.0)) * inv_block)
    mode = jnp.where(first_block == 0.0, end_block,
                     jnp.where(end_block == float(ROW_BLOCKS), ROW_BLOCKS + first_block, float(ROW_BLOCKS)))
    put(I_TILE, jnp.where(past, float(n_tiles - 1), jnp.where(lead, 0.0, tile)))
    put(I_EXPERT, jnp.where(past, e_last, jnp.where(lead, e_first, pick(e_col))))
    put(I_LO, lo)
    put(I_HI, hi)
    put(I_FIRST, valid * new_tile)
    put(I_MODE, valid * mode)
    put(I_FETCH, jnp.where(past, e_last, jnp.where(lead, e_first, pick(nxt))))
    put(I_START, jnp.where(lead, 1.0, pick(((item == item_start) & has_next).astype(F32))))
    put(I_PREP, jnp.where(lead, 1.0, pick(((item == item_end - 1.0) & has_next).astype(F32))))
    put(I_PREP_SLOT, pick(ordinal + 1.0 - 2.0 * jnp.floor((ordinal + 1.0) * 0.5)))
    put(I_SLOT, pick(ordinal - 2.0 * jnp.floor(ordinal * 0.5)))
    for field in range(I_SLOT + 1, ITEM_FIELDS):
        put(field, jnp.zeros_like(valid))


def _plan(counts, n_rows, tm):
    n_tiles = n_rows // tm
    n_items = n_tiles + N_EXPERTS
    width = -(-n_items // LANES) * LANES
    offsets, items = pl.pallas_call(
        functools.partial(_plan_kernel, float(tm), n_tiles),
        out_shape=[jax.ShapeDtypeStruct((1, LANES), F32),
                   jax.ShapeDtypeStruct((ITEM_FIELDS, width), jnp.int32)],
        name="plan",
    )(counts)
    return offsets, items, n_items


def _experts(xs, items, n_items, w_gate_up, bg, bu, w_down, bd):
    n_rows = xs.shape[0]
    tm = EXPERT_TILE
    half = D_MODEL // 2
    j = np.arange(2 * LANES)
    perm = np.zeros((2 * LANES, 2 * LANES), np.float32)
    perm[j, np.where(j % 2 == 0, j // 2, LANES + j // 2)] = 1.0
    row_tile = pl.BlockSpec((tm, half), lambda i, items: (items[I_TILE, i], 0))
    by_expert = lambda *shape: pl.BlockSpec((None,) + shape, lambda i, items: (items[I_EXPERT, i], 0, 0))
    in_hbm = pl.BlockSpec(memory_space=pl.ANY)
    return pl.pallas_call(
        _experts_kernel,
        grid_spec=pltpu.PrefetchScalarGridSpec(
            num_scalar_prefetch=1,
            grid=(n_items,),
            in_specs=[row_tile, in_hbm, by_expert(1, D_MODEL), by_expert(1, D_MODEL), in_hbm,
                      by_expert(1, D_MODEL),
                      pl.BlockSpec((2 * LANES, 2 * LANES), lambda i, items: (0, 0))],
            out_specs=row_tile,
            scratch_shapes=[pltpu.VMEM((D_MODEL, 2 * D_MODEL), F32), pltpu.VMEM((D_MODEL, D_MODEL), F32)]
            + [pltpu.VMEM((2, D_MODEL, D_MODEL), BF16)] * 3 + [pltpu.SemaphoreType.DMA((2,))],
        ),
        out_shape=jax.ShapeDtypeStruct((n_rows, half), U32),
        compiler_params=_params(("arbitrary",)),
        name="experts",
    )(items, xs, w_gate_up, bg, bu, w_down, bd, jnp.asarray(perm, dtype=BF16))


def _tail_kernel(h1_ref, yu_ref, gates_ref, p_ref, nple_ref, wg_ref, wp_ref, nfin_ref, *rest):
    out_ref = rest[-1]
    quarter = D_MODEL // 4
    for r in range(0, h1_ref.shape[0], TAIL_ROWS):
        rows = slice(r, r + TAIL_ROWS)
        gates = gates_ref[rows, :]
        lo = None
        hi = None
        for k in range(TOP_K):
            gk = gates[:, k:k + 1]
            yl, yh = _unpack_bf16_pair(yu_ref[k, rows, :])
            lo = gk * yl if lo is None else lo + gk * yl
            hi = gk * yh if hi is None else hi + gk * yh
        moe = jnp.concatenate([lo[:, :quarter], hi[:, :quarter], lo[:, quarter:], hi[:, quarter:]], axis=1)
        h2 = h1_ref[rows, :] + moe
        hn = _rms(h2, nple_ref[...]).astype(BF16)
        gate = _sigmoid(_dot(hn, wg_ref[...]))
        h3 = h2 + gate * _dot(p_ref[rows, :].astype(BF16), wp_ref[...])
        out_ref[rows, :] = _rms(h3, nfin_ref[...])


def _tail(h1, yu, gates, p2, norm_ple_w, ple_gate_bf, ple_proj_bf, final_norm_w, local0, tile0, result):
    t = yu.shape[1]
    tm = TOKEN_TILE
    half = D_MODEL // 2
    tile = lambda width: pl.BlockSpec((tm, width), lambda i: (i + local0, 0))
    shifted = lambda width: pl.BlockSpec((tm, width), lambda i: (i + tile0 + local0, 0))
    const2 = lambda shape: pl.BlockSpec(shape, lambda i: (0, 0))
    in_specs = [tile(D_MODEL), pl.BlockSpec((TOP_K, tm, half), lambda i: (0, i, 0)),
                tile(LANES), shifted(PLE_DIM), const2((1, D_MODEL)),
                const2((D_MODEL, D_MODEL)), const2((PLE_DIM, D_MODEL)), const2((1, D_MODEL))]
    args = [h1, yu, gates, p2, norm_ple_w, ple_gate_bf, ple_proj_bf, final_norm_w]
    aliases = {}
    if result is not None:
        in_specs.append(pl.BlockSpec(memory_space=pl.ANY))
        args.append(result)
        aliases = {len(args) - 1: 0}
    return pl.pallas_call(
        _tail_kernel,
        grid=(t // tm,),
        in_specs=in_specs,
        out_specs=shifted(D_MODEL),
        out_shape=jax.ShapeDtypeStruct((p2.shape[0], D_MODEL), F32),
        input_output_aliases=aliases,
        compiler_params=_params(("parallel",)),
        name="tail",
    )(*args)


def kernel(x, p, positions, w_in, w_out, ret_gn_w, pool_w, pool_scale, norm_mix_w, norm_moe_w, router_w, router_b, expert_w_gate_up, expert_b_gate_up, expert_w_down, expert_b_down, norm_ple_w, ple_gate_w, ple_proj_w, final_norm_w):
    batch, seq, d = x.shape
    depth = w_in.shape[0]
    assert depth == 1 and d == D_MODEL and seq % TOKEN_TILE == 0
    assert seq % (RET_CHUNK * RET_STEP_CHUNKS) == 0
    group_batches = GROUP_BATCHES if sum(GROUP_BATCHES) == batch else (batch,)
    t = batch * seq
    row = lambda a: a.reshape(1, -1).astype(F32)
    l = 0

    x2 = x.reshape(t, d)
    pos2 = positions.reshape(t, 1)
    p2 = p[l].reshape(t, PLE_DIM)
    w_in_bf = w_in[l].astype(BF16)
    rw = jnp.pad(router_w[l].astype(F32), ((0, 0), (0, LANES - N_EXPERTS)))
    rw_hi = rw.astype(BF16)
    rw_lo = (rw - rw_hi.astype(F32)).astype(BF16)
    rw_split = jnp.concatenate([rw_hi, rw_lo], axis=1)
    rb = jnp.pad(router_b[l].astype(F32), (0, LANES - N_EXPERTS), constant_values=NEG_BIG).reshape(1, LANES)
    bgu = expert_b_gate_up[l].reshape(N_EXPERTS, 1, D_MODEL, 2).astype(F32)
    bdn = expert_b_down[l].reshape(N_EXPERTS, 1, D_MODEL).astype(F32)
    pool_w_bf, w_out_bf = pool_w[l].astype(BF16), w_out[l].astype(BF16)
    ple_gate_bf, ple_proj_bf = ple_gate_w[l].astype(BF16), ple_proj_w[l].astype(BF16)

    out = None
    first_batch = 0
    for gi, nb in enumerate(group_batches):
        tg = nb * seq
        n_rows = tg * TOP_K
        assert n_rows % EXPERT_TILE == 0
        tile0 = first_batch * seq // TOKEN_TILE
        ret, u = _retention(x2, pos2, row(norm_mix_w[l]), w_in_bf, row(ret_gn_w[l]),
                            first_batch * seq // (RET_CHUNK * RET_STEP_CHUNKS), nb, seq)
        first_batch += nb
        h1, hn_packed, code, gates, counts = _mix_router(
            x2, ret, u, pool_w_bf, row(pool_scale[l]), w_out_bf, row(norm_moe_w[l]), rw_split, rb,
            seq, tile0)

        offsets, items, n_items = _plan(counts, n_rows, EXPERT_TILE)
        pos = _positions(code, offsets)
        xs = _sc_scatter(hn_packed, pos)
        y = _experts(xs, items, n_items, expert_w_gate_up[l], bgu[..., 0], bgu[..., 1],
                     expert_w_down[l], bdn)

        spans = LAST_GROUP_SPANS if gi == len(group_batches) - 1 else 1
        span = tg // spans
        for si in range(spans):
            idx = pos[:TOP_K, si * span:(si + 1) * span].reshape(TOP_K * span)
            yu = _sc_gather(y, idx).reshape(TOP_K, span, d // 2)
            out = _tail(h1, yu, gates, p2, row(norm_ple_w[l]), ple_gate_bf, ple_proj_bf,
                        row(final_norm_w), si * (span // TOKEN_TILE), tile0, out)
    return out.reshape(batch, seq, d)
```

```python
import functools

import numpy as np
import jax
import jax.numpy as jnp
from jax import lax
from jax.experimental import pallas as pl
from jax.experimental.pallas import tpu as pltpu
from jax.experimental.pallas import tpu_sc as plsc

D_MODEL = 1024
D_RET = 512
D_POOL = 512
RET_HEADS = 8
RET_HEAD_DIM = 64
HEAD_PAIRS = RET_HEADS // 2
ROPE_BASE = 10000.0
POOL_WINDOWS = (2, 4, 8, 16)
POOL_GROUP_DIM = 128
POOL_HISTORY = 16
POOL_BLOCK = 128
D_IN_PROJ = 4 * D_RET + D_POOL
N_EXPERTS = 32
TOP_K = 4
SWIGLU_LIMIT = 7.0
SWIGLU_ALPHA = 1.702
PLE_DIM = 256
NORM_EPS = 1e-5
GN_EPS = 1e-5

LANES = 128
NEG_BIG = -1e30

TOKEN_TILE = 512
POSITIONS_TILE = 1024
TAIL_ROWS = 256
POS_ROWS = 8
RET_CHUNK = 256
RET_STEP_CHUNKS = 4
EXPERT_TILE = 512
EXPERT_COL_CHUNK = 512
SC_WINDOW = 128
GROUP_BATCHES = (5, 3)
LAST_GROUP_SPANS = 2
VMEM_LIMIT = 56 * 1024 * 1024

F32 = jnp.float32
BF16 = jnp.bfloat16
U32 = jnp.uint32


def _params(semantics):
    return pltpu.CompilerParams(dimension_semantics=semantics, vmem_limit_bytes=VMEM_LIMIT)


def _dot(a, b):
    return jnp.dot(a, b, preferred_element_type=F32)


def _dot_nt(a, b):
    return lax.dot_general(a, b, (((1,), (1,)), ((), ())), preferred_element_type=F32)


def _rms(x, w):
    ms = jnp.mean(x * x, axis=-1, keepdims=True)
    return x * lax.rsqrt(ms + NORM_EPS) * w


def _sigmoid(z):
    return 1.0 / (1.0 + jnp.exp(-z))


def _pack_bf16_pair(lo, hi):
    lo_bits = pltpu.bitcast(lo.astype(BF16).astype(F32), U32) >> 16
    hi_bits = pltpu.bitcast(hi.astype(BF16).astype(F32), U32) & jnp.uint32(0xFFFF0000)
    return lo_bits | hi_bits


def _unpack_bf16_pair(packed):
    lo = pltpu.bitcast(packed << 16, F32)
    hi = pltpu.bitcast(packed & jnp.uint32(0xFFFF0000), F32)
    return lo, hi


def _project_rows(x, pos, nw_ref, w_ref, freq_ref, qkvg_ref, u_ref, rows):
    hn = _rms(x, nw_ref[...]).astype(BF16)
    ang = pos.astype(F32) * freq_ref[0:1, :]
    cos = jnp.cos(ang)
    sin = jnp.sin(ang)
    slab = 2 * LANES
    cos_t = jnp.concatenate([cos, cos], axis=1)
    sin_up = jnp.concatenate([sin * freq_ref[1:2, :]] * 2, axis=1)
    sin_dn = jnp.concatenate([sin * freq_ref[2:3, :]] * 2, axis=1)
    half = RET_HEAD_DIM // 2

    for s in range(2 * D_RET // slab):
        sl = slice(s * slab, (s + 1) * slab)
        v = _dot(hn, w_ref[:, sl])
        v = v * cos_t + pltpu.roll(v, slab - half, 1) * sin_up + pltpu.roll(v, half, 1) * sin_dn
        if s >= D_RET // slab:
            v = v * (RET_HEAD_DIM ** -0.5)
        qkvg_ref[rows, sl] = v.astype(BF16)
    for s in range(2 * D_RET // 512, 4 * D_RET // 512):
        sl = slice(s * 512, (s + 1) * 512)
        qkvg_ref[rows, sl] = _dot(hn, w_ref[:, sl]).astype(BF16)
    u_ref[rows, :] = _dot(hn, w_ref[:, 4 * D_RET:]).astype(BF16)


def _rope_table():
    j = np.arange(LANES)
    half = RET_HEAD_DIM // 2
    inv_freq = ROPE_BASE ** (-(np.arange(half, dtype=np.float32)) / half)
    freq = np.zeros((8, LANES), np.float32)
    freq[0] = inv_freq[j % half]
    freq[1] = np.where(j % RET_HEAD_DIM < half, -1.0, 0.0)
    freq[2] = np.where(j % RET_HEAD_DIM >= half, 1.0, 0.0)
    return jnp.asarray(freq)


def _retention_kernel(x_ref, pos_ref, nw_ref, w_ref, freq_ref, dec_ref, xi_ref, zeta_ref, cd_ref, bd_ref,
                      m64_ref, eye_ref, gnw_ref, out_ref, u_ref, state_ref, qkvg_ref):
    @pl.when(pl.program_id(1) == 0)
    def _():
        state_ref[...] = jnp.zeros_like(state_ref)

    c = dec_ref.shape[1]
    n_chunks = x_ref.shape[0] // c
    lane = lax.broadcasted_iota(jnp.int32, (1, LANES), 1)
    m64 = m64_ref[...]
    bd = bd_ref[...]
    eye = eye_ref[...]

    def group_mean(v):
        hi = v.astype(BF16)
        lo = (v - hi.astype(F32)).astype(BF16)
        return _dot(jnp.concatenate([hi, lo], axis=1), m64)

    for ci in range(n_chunks):
        rows = slice(ci * c, (ci + 1) * c)
        _project_rows(x_ref[rows, :], pos_ref[rows, :], nw_ref, w_ref, freq_ref, qkvg_ref, u_ref, rows)
        for p in range(HEAD_PAIRS):
            sl = slice(p * LANES, (p + 1) * LANES)
            qp = qkvg_ref[rows, sl]
            kp = qkvg_ref[rows, D_RET + p * LANES:D_RET + (p + 1) * LANES]
            vp = qkvg_ref[rows, 2 * D_RET + p * LANES:2 * D_RET + (p + 1) * LANES]
            y = None
            for hh in range(2):
                in_head = (lane >= RET_HEAD_DIM) == bool(hh)
                qm = jnp.where(in_head, qp, jnp.zeros_like(qp))
                vm = jnp.where(in_head, vp, jnp.zeros_like(vp))
                scores = _dot_nt(qm, kp) * dec_ref[2 * p + hh]
                part = _dot(scores.astype(BF16), vm)
                y = part if y is None else y + part
            st = state_ref[p]
            y = y + _dot((qp.astype(F32) * xi_ref[p]).astype(BF16), st.astype(BF16))
            kz = (kp.astype(F32) * zeta_ref[p]).astype(BF16)
            kz_t = _dot_nt(eye, kz).astype(BF16)
            state_ref[p] = cd_ref[p] * st + _dot(kz_t, vp) * bd
            mu = group_mean(y)
            var = group_mean(y * y) - mu * mu
            yn = (y - mu) * lax.rsqrt(var + GN_EPS) * gnw_ref[:, sl]
            g = qkvg_ref[rows, 3 * D_RET + p * LANES:3 * D_RET + (p + 1) * LANES].astype(F32)
            out_ref[rows, sl] = (yn * g * _sigmoid(g)).astype(BF16)


def _retention_tables(c):
    h = np.arange(RET_HEADS, dtype=np.float64)
    log_gamma = np.log1p(-np.power(2.0, -5.0 - h))
    idx = np.arange(c, dtype=np.float64)
    rel = idx[:, None] - idx[None, :]
    dec = np.where(rel >= 0, np.exp(np.where(rel >= 0, rel, 0.0)[None] * log_gamma[:, None, None]), 0.0)
    lane_head = np.arange(LANES) // RET_HEAD_DIM
    xi = np.zeros((HEAD_PAIRS, c, LANES))
    zeta = np.zeros((HEAD_PAIRS, c, LANES))
    cd = np.zeros((HEAD_PAIRS, LANES, LANES))
    same = lane_head[:, None] == lane_head[None, :]
    for p in range(HEAD_PAIRS):
        lg = log_gamma[2 * p + lane_head]
        xi[p] = np.exp((idx + 1.0)[:, None] * lg[None, :])
        zeta[p] = np.exp((c - 1 - idx)[:, None] * lg[None, :])
        cd[p] = np.where(same, np.exp(c * lg)[:, None], 0.0)
    bd = same.astype(np.float32)
    m64 = np.concatenate([same, same], axis=0).astype(np.float32) / RET_HEAD_DIM
    f = lambda a: jnp.asarray(a, dtype=F32)
    return (f(dec), f(xi), f(zeta), f(cd), f(bd), jnp.asarray(m64, dtype=BF16),
            jnp.asarray(np.eye(LANES, dtype=np.float32), dtype=BF16))


def _retention(x2, pos2, norm_w, w_in_bf, gn_w, block0, batch, seq):
    c = RET_CHUNK
    rows = RET_STEP_CHUNKS * c
    n = seq // rows
    t = batch * seq
    dec, xi, zeta, cd, bd, m64, eye = _retention_tables(c)
    const3 = lambda shape: pl.BlockSpec(shape, lambda b, i: (0, 0, 0))
    const2 = lambda shape: pl.BlockSpec(shape, lambda b, i: (0, 0))
    out_tile = pl.BlockSpec((rows, D_RET), lambda b, i: (b * n + i, 0))
    return pl.pallas_call(
        _retention_kernel,
        grid=(batch, n),
        in_specs=[pl.BlockSpec((rows, D_MODEL), lambda b, i: (block0 + b * n + i, 0)),
                  pl.BlockSpec((rows, 1), lambda b, i: (block0 + b * n + i, 0)),
                  const2((1, D_MODEL)), const2((D_MODEL, D_IN_PROJ)), const2((8, LANES)),
                  const3((RET_HEADS, c, c)), const3((HEAD_PAIRS, c, LANES)),
                  const3((HEAD_PAIRS, c, LANES)), const3((HEAD_PAIRS, LANES, LANES)),
                  const2((LANES, LANES)), const2((2 * LANES, LANES)), const2((LANES, LANES)),
                  const2((1, D_RET))],
        out_specs=[out_tile, out_tile],
        out_shape=[jax.ShapeDtypeStruct((t, D_RET), BF16), jax.ShapeDtypeStruct((t, D_POOL), BF16)],
        scratch_shapes=[pltpu.VMEM((HEAD_PAIRS, LANES, LANES), F32), pltpu.VMEM((rows, 4 * D_RET), BF16)],
        compiler_params=_params(("arbitrary", "arbitrary")),
        name="retention",
    )(x2, pos2, norm_w, w_in_bf, _rope_table(), dec, xi, zeta, cd, bd, m64, eye, gn_w)


def _mix_router_kernel(seq, n_tiles, x_ref, ret_ref, u_ref, uprev_ref, band_ref, pw_ref, ps_ref, wout_ref,
                       nw_ref, rw_ref, rb_ref,
                       h1_ref, hnp_ref, code_ref, gates_ref, cnt_ref, uext_ref, logits_ref):
    i = pl.program_id(0)
    tm = x_ref.shape[0]
    tile = jnp.minimum(i, n_tiles - 1)
    t0 = lax.rem(tile * tm, seq)

    @pl.when(i == 0)
    def _():
        cnt_ref[...] = jnp.zeros_like(cnt_ref)
        logits_ref[...] = jnp.zeros_like(logits_ref)

    lane = lax.broadcasted_iota(jnp.int32, (tm, LANES), 1).astype(F32)
    code = jnp.zeros((tm, LANES), F32)
    vals = []
    work = logits_ref[...]
    for k in range(TOP_K):
        m = jnp.max(work, axis=-1, keepdims=True)
        idx = jnp.min(jnp.where(work == m, lane, float(LANES)), axis=-1, keepdims=True)
        chosen = lane == idx
        code = jnp.where(chosen, float(k + 1), code)
        work = jnp.where(chosen, -jnp.inf, work)
        vals.append(m)
    exps = [jnp.exp(v - vals[0]) for v in vals]
    denom = exps[0] + exps[1] + exps[2] + exps[3]
    gates = jnp.zeros((tm, LANES), F32)
    for k in range(TOP_K):
        gates = jnp.where(lane == float(k), exps[k] / denom, gates)
    code_ref[...] = code
    gates_ref[...] = gates
    routed = jnp.where(i > 0, 1.0, 0.0)
    cnt_ref[...] += routed * jnp.sum((code > 0).astype(F32), axis=0, keepdims=True)

    prev = uprev_ref[...]
    uext_ref[0:POOL_HISTORY, :] = jnp.where(t0 == 0, jnp.zeros_like(prev), prev)
    uext_ref[POOL_HISTORY:, :] = u_ref[...]

    row = lax.broadcasted_iota(jnp.int32, (tm, 1), 0)
    t_seq = (t0 + row + 1).astype(F32)
    mixed = []
    for gi, w in enumerate(POOL_WINDOWS):
        sl = slice(gi * POOL_GROUP_DIM, (gi + 1) * POOL_GROUP_DIM)
        wsum = jnp.concatenate(
            [_dot(band_ref[gi], uext_ref[r0:r0 + POOL_BLOCK + POOL_HISTORY, sl])
             for r0 in range(0, tm, POOL_BLOCK)], axis=0)
        count = jnp.minimum(t_seq, float(w))
        pooled = wsum / count - u_ref[:, sl].astype(F32)
        mixed.append(_dot(pooled.astype(BF16), pw_ref[gi]))
    pool = (jnp.concatenate(mixed, axis=1) * ps_ref[...]).astype(BF16)

    h1 = (x_ref[...] + _dot(ret_ref[...], wout_ref[0:D_RET, :])
          + _dot(pool, wout_ref[D_RET:, :]))
    h1_ref[...] = h1
    hn = _rms(h1, nw_ref[...])
    hn_hi = hn.astype(BF16)
    half = D_MODEL // 2
    hnp_ref[...] = _pack_bf16_pair(hn[:, :half], hn[:, half:])

    hn_lo = (hn - hn_hi.astype(F32)).astype(BF16)
    both = _dot(hn_hi, rw_ref[...])
    logits_ref[...] = (both[:, :LANES] + both[:, LANES:] + _dot(hn_lo, rw_ref[:, :LANES])
                       + rb_ref[...])


def _pool_bands():
    r = np.arange(POOL_BLOCK)[:, None]
    s = np.arange(POOL_BLOCK + POOL_HISTORY)[None, :] - POOL_HISTORY
    bands = [((s <= r) & (s > r - w)) for w in POOL_WINDOWS]
    return jnp.asarray(np.stack(bands).astype(np.float32), dtype=BF16)


def _mix_router(x2, ret, u, pool_w_bf, pool_scale, w_out_bf, norm_w, rw_split, rb, seq, tile0):
    t = ret.shape[0]
    tm = TOKEN_TILE
    hist_blocks = tm // POOL_HISTORY
    n = t // tm
    mixed = lambda i: jnp.minimum(i, n - 1)
    routed = lambda i: jnp.maximum(i - 1, 0)
    tile = lambda width: pl.BlockSpec((tm, width), lambda i: (mixed(i), 0))
    late = lambda width: pl.BlockSpec((tm, width), lambda i: (routed(i), 0))
    const2 = lambda shape: pl.BlockSpec(shape, lambda i: (0, 0))
    const3 = lambda shape: pl.BlockSpec(shape, lambda i: (0, 0, 0))
    return pl.pallas_call(
        functools.partial(_mix_router_kernel, seq, n),
        grid=(n + 1,),
        in_specs=[
            pl.BlockSpec((tm, D_MODEL), lambda i: (mixed(i) + tile0, 0)), tile(D_RET), tile(D_POOL),
            pl.BlockSpec((POOL_HISTORY, D_POOL), lambda i: (jnp.maximum(mixed(i) * hist_blocks - 1, 0), 0)),
            const3((len(POOL_WINDOWS), POOL_BLOCK, POOL_BLOCK + POOL_HISTORY)),
            const3((len(POOL_WINDOWS), POOL_GROUP_DIM, POOL_GROUP_DIM)),
            const2((1, D_POOL)), const2((D_MODEL, D_MODEL)), const2((1, D_MODEL)),
            const2((D_MODEL, 2 * LANES)), const2((1, LANES)),
        ],
        out_specs=[tile(D_MODEL), tile(D_MODEL // 2), late(LANES), late(LANES), const2((1, LANES))],
        out_shape=[
            jax.ShapeDtypeStruct((t, D_MODEL), F32),
            jax.ShapeDtypeStruct((t, D_MODEL // 2), U32),
            jax.ShapeDtypeStruct((t, LANES), F32),
            jax.ShapeDtypeStruct((t, LANES), F32),
            jax.ShapeDtypeStruct((1, LANES), F32),
        ],
        scratch_shapes=[pltpu.VMEM((tm + POOL_HISTORY, D_POOL), BF16), pltpu.VMEM((tm, LANES), F32)],
        compiler_params=_params(("arbitrary",)),
        name="mix_router",
    )(x2, ret, u, u, _pool_bands(), pool_w_bf, pool_scale, w_out_bf, norm_w, rw_split, rb)


def _positions_kernel(code_ref, off_ref, tri_ref, pos_ref, carry_ref):
    @pl.when(pl.program_id(0) == 0)
    def _():
        carry_ref[...] = jnp.zeros_like(carry_ref)

    code = code_ref[...]
    tm = code.shape[0]
    sel = (code > 0).astype(BF16)
    carry = carry_ref[...]
    rank = _dot(tri_ref[...], sel) + (carry + off_ref[...])
    carry_ref[...] = carry + jnp.sum(sel.astype(F32), axis=0, keepdims=True)
    lane = lax.broadcasted_iota(jnp.int32, (tm, LANES), 1)
    pos = jnp.zeros((tm, LANES), F32)
    for k in range(TOP_K):
        pk = jnp.sum(jnp.where(code == float(k + 1), rank, 0.0), axis=-1, keepdims=True)
        pos = jnp.where(lane == k, pk, pos)
    pos_ref[...] = pos.T[0:POS_ROWS, :].astype(jnp.int32)


def _positions(code, offsets):
    t = code.shape[0]
    tm = POSITIONS_TILE
    tri = jnp.asarray(np.tril(np.ones((tm, tm), np.float32), -1), dtype=BF16)
    return pl.pallas_call(
        _positions_kernel,
        grid=(t // tm,),
        in_specs=[pl.BlockSpec((tm, LANES), lambda i: (i, 0)),
                  pl.BlockSpec((1, LANES), lambda i: (0, 0)),
                  pl.BlockSpec((tm, tm), lambda i: (0, 0))],
        out_specs=pl.BlockSpec((POS_ROWS, tm), lambda i: (0, i)),
        out_shape=jax.ShapeDtypeStruct((POS_ROWS, t), jnp.int32),
        scratch_shapes=[pltpu.VMEM((1, LANES), F32)],
        compiler_params=_params(("arbitrary",)),
        name="positions",
    )(code, offsets, tri)


def _sc_mesh():
    return plsc.VectorSubcoreMesh(core_axis_name="core", subcore_axis_name="subcore")


def _sc_workers():
    info = plsc.get_sparse_core_info()
    return info.num_cores, info.num_cores * info.num_subcores


def _sc_gather(src, idx):
    n = idx.shape[0]
    d = src.shape[1]
    w = SC_WINDOW // 2
    num_cores, workers = _sc_workers()
    per = n // workers
    pairs = per // (2 * w)
    assert per * workers == n and pairs * 2 * w == per

    @functools.partial(
        pl.kernel, out_type=jax.ShapeDtypeStruct((n, d), src.dtype), mesh=_sc_mesh(),
        scratch_types=[pltpu.VMEM((w,), jnp.int32), pltpu.VMEM((w,), jnp.int32),
                       pltpu.VMEM((w, d), src.dtype), pltpu.VMEM((w, d), src.dtype),
                       pltpu.SemaphoreType.DMA, pltpu.SemaphoreType.DMA,
                       pltpu.SemaphoreType.DMA, pltpu.SemaphoreType.DMA],
        name="sc_gather")
    def gather(src_hbm, idx_hbm, out_hbm, idx0, idx1, rows0, rows1, g0, g1, w0, w1):
        first = (lax.axis_index("subcore") * num_cores + lax.axis_index("core")) * per

        def start_gather(win, idx_v, rows_v, sem):
            pltpu.sync_copy(idx_hbm.at[pl.ds(first + win * w, w)], idx_v)
            pltpu.async_copy(src_hbm.at[idx_v], rows_v, sem)

        def wait_gather(idx_v, rows_v, sem):
            pltpu.make_async_copy(src_hbm.at[idx_v], rows_v, sem).wait()

        def start_write(win, rows_v, sem):
            pltpu.async_copy(rows_v, out_hbm.at[pl.ds(first + win * w, w)], sem)

        def wait_write(rows_v, sem):
            pltpu.make_async_copy(rows_v, out_hbm.at[pl.ds(first, w)], sem).wait()

        start_gather(0, idx0, rows0, g0)

        @pl.loop(0, pairs)
        def _(j):
            even = 2 * j

            @pl.when(j > 0)
            def _():
                wait_write(rows1, w1)

            start_gather(even + 1, idx1, rows1, g1)
            wait_gather(idx0, rows0, g0)
            start_write(even, rows0, w0)
            wait_write(rows0, w0)

            @pl.when(j + 1 < pairs)
            def _():
                start_gather(even + 2, idx0, rows0, g0)

            wait_gather(idx1, rows1, g1)
            start_write(even + 1, rows1, w1)

        wait_write(rows1, w1)

    return gather(src, idx)


def _sc_scatter(src, idx):
    t, d = src.shape
    w = SC_WINDOW
    num_cores, workers = _sc_workers()
    per = t // workers
    assert idx.shape[0] >= TOP_K and idx.shape[1] == t and per * workers == t and per % w == 0

    @functools.partial(
        pl.kernel, out_type=jax.ShapeDtypeStruct((TOP_K * t, d), src.dtype), mesh=_sc_mesh(),
        scratch_types=[pltpu.VMEM((TOP_K, w), jnp.int32), pltpu.VMEM((w, d), src.dtype),
                       pltpu.SemaphoreType.DMA],
        name="sc_scatter")
    def scatter(src_hbm, idx_hbm, out_hbm, idx_v, rows_v, sem):
        first = (lax.axis_index("subcore") * num_cores + lax.axis_index("core")) * per

        @pl.loop(0, per // w)
        def _(j):
            base = first + j * w
            for k in range(TOP_K):
                pltpu.sync_copy(idx_hbm.at[k, pl.ds(base, w)], idx_v.at[k])
            pltpu.sync_copy(src_hbm.at[pl.ds(base, w)], rows_v)
            copies = [pltpu.async_copy(rows_v, out_hbm.at[idx_v.at[k]], sem) for k in range(TOP_K)]
            for c in copies:
                c.wait()

    return scatter(src, idx)


def _experts_kernel(items_ref, xs_ref, wgu_hbm, bg_ref, bu_ref, wdn_hbm, bd_ref, perm_ref, y_ref,
                    wgu_ref, wdn_ref, wg_ref, wu_ref, wd_ref, sems):
    i = pl.program_id(0)
    tm = xs_ref.shape[0]

    def weight_copies():
        e = items_ref[I_FETCH, i]
        return (pltpu.make_async_copy(wgu_hbm.at[e], wgu_ref, sems.at[0]),
                pltpu.make_async_copy(wdn_hbm.at[e], wdn_ref, sems.at[1]))

    @pl.when(items_ref[I_START, i] == 1)
    def _():
        for copy in weight_copies():
            copy.start()

    @pl.when(items_ref[I_PREP, i] == 1)
    def _():
        for copy in weight_copies():
            copy.wait()
        ps = items_ref[I_PREP_SLOT, i]
        perm = perm_ref[...]
        pair = 2 * LANES
        for c in range(2 * D_MODEL // pair):
            sel = _dot(wgu_ref[:, c * pair:(c + 1) * pair].astype(BF16), perm).astype(BF16)
            wg_ref[ps, :, c * LANES:(c + 1) * LANES] = sel[:, :LANES]
            wu_ref[ps, :, c * LANES:(c + 1) * LANES] = sel[:, LANES:]
        wd_ref[ps] = wdn_ref[...].astype(BF16)

    def expert_mlp(r0, r1):
        n = r1 - r0
        s = items_ref[I_SLOT, i]
        x_lo, x_hi = _unpack_bf16_pair(xs_ref[r0:r1, :])
        x = jnp.concatenate([x_lo.astype(BF16), x_hi.astype(BF16)], axis=1)
        acts = []
        chunks = [slice(c * EXPERT_COL_CHUNK, (c + 1) * EXPERT_COL_CHUNK)
                  for c in range(D_MODEL // EXPERT_COL_CHUNK)]
        for cs in chunks:
            gate = jnp.minimum(_dot(x, wg_ref[s, :, cs]) + bg_ref[:, cs], SWIGLU_LIMIT)
            up = jnp.clip(_dot(x, wu_ref[s, :, cs]) + bu_ref[:, cs], -SWIGLU_LIMIT, SWIGLU_LIMIT)
            acts.append(((up + 1.0) * (gate * _sigmoid(SWIGLU_ALPHA * gate))).astype(BF16))
        rows = items_ref[I_TILE, i] * tm + r0 + lax.broadcasted_iota(jnp.int32, (n, 1), 0)
        mine = (rows >= items_ref[I_LO, i]) & (rows < items_ref[I_HI, i])

        half, quarter = D_MODEL // 2, D_MODEL // 4
        for h in range(2):
            hs = slice(h * half, (h + 1) * half)
            yh = bd_ref[:, hs]
            for cs, act in zip(chunks, acts):
                yh = yh + _dot(act, wd_ref[s, cs, hs])
            cols = slice(h * quarter, (h + 1) * quarter)
            packed = _pack_bf16_pair(yh[:, :quarter], yh[:, quarter:])
            y_ref[r0:r1, cols] = jnp.where(mine, packed, y_ref[r0:r1, cols])

    @pl.when(items_ref[I_FIRST, i] == 1)
    def _():
        y_ref[...] = jnp.zeros(y_ref.shape, U32)

    block = tm // ROW_BLOCKS
    for m in range(1, ROW_BLOCKS + 1):
        pl.when(items_ref[I_MODE, i] == m)(functools.partial(expert_mlp, 0, m * block))
    for m in range(1, ROW_BLOCKS):
        pl.when(items_ref[I_MODE, i] == ROW_BLOCKS + m)(functools.partial(expert_mlp, m * block, tm))


(I_TILE, I_EXPERT, I_LO, I_HI, I_FIRST, I_MODE, I_FETCH, I_START, I_PREP, I_PREP_SLOT, I_SLOT) = range(11)
ITEM_FIELDS = 16
ROW_BLOCKS = 4


def _plan_kernel(tm, n_tiles, cnt_ref, off_ref, items_ref):
    n = items_ref.shape[1]
    cnt_r = cnt_ref[...]
    sub = lax.broadcasted_iota(jnp.int32, (LANES, LANES), 0).astype(F32)
    lan = lax.broadcasted_iota(jnp.int32, (LANES, LANES), 1).astype(F32)
    big = float(4 * LANES)

    def col(row):
        return jnp.sum(jnp.where(lan == sub, row, 0.0), axis=1, keepdims=True)

    def row(column):
        return jnp.sum(jnp.where(lan == sub, column, 0.0), axis=0, keepdims=True)

    def prefix(r):
        return jnp.sum(jnp.where(lan <= sub, r, 0.0), axis=1, keepdims=True)

    cnt = col(cnt_r)
    ends = prefix(cnt_r)
    starts = ends - cnt
    off_ref[...] = row(starts)
    used = cnt > 0.0
    inv_tm = 1.0 / tm
    first_tile = jnp.floor(starts * inv_tm)
    last_tile = jnp.floor(jnp.maximum(ends - 1.0, 0.0) * inv_tm)
    per = jnp.where(used, last_tile - first_tile + 1.0, 0.0)
    item_end = prefix(row(per))
    item_start = item_end - per
    used_r = row(used.astype(F32))
    ordinal = prefix(used_r) - 1.0
    nxt = jnp.min(jnp.where((lan > sub) & (used_r > 0.0), lan, big), axis=1, keepdims=True)
    has_next = nxt < big
    e_col = sub[:, 0:1]
    nxt = jnp.where(has_next, nxt, e_col)
    e_first = jnp.min(jnp.where(used_r > 0.0, lan[0:1, :], big), axis=1, keepdims=True)
    e_last = jnp.max(jnp.where(used_r > 0.0, lan[0:1, :], -1.0), axis=1, keepdims=True)
    total = jnp.max(item_end, axis=0, keepdims=True)

    item = lax.broadcasted_iota(jnp.int32, (LANES, n), 1).astype(F32) - 1.0
    mine = ((item >= item_start) & (item < item_end)).astype(F32)
    pick = lambda column: jnp.sum(mine * column, axis=0, keepdims=True)
    item_r = item[0:1, :]
    valid = pick(jnp.ones_like(cnt))
    lead = item_r < 0.0
    past = item_r >= total
    new_tile = 1.0 - pick(((item == item_start) & (starts - first_tile * tm > 0.0)).astype(F32))

    def put(field, value):
        items_ref[field:field + 1, :] = value.astype(jnp.int32)

    tile = pick(first_tile - item_start) + item_r
    lo = pick(starts)
    hi = pick(ends)
    inv_block = ROW_BLOCKS * inv_tm
    first_block = jnp.floor(jnp.maximum(lo - tile * tm, 0.0) * inv_block)
    end_block = jnp.floor((jnp.minimum(hi - tile * tm, tm) + (tm / ROW_BLOCKS - 1.0)) * inv_block)
    mode = jnp.where(first_block == 0.0, end_block,
                     jnp.where(end_block == float(ROW_BLOCKS), ROW_BLOCKS + first_block, float(ROW_BLOCKS)))
    put(I_TILE, jnp.where(past, float(n_tiles - 1), jnp.where(lead, 0.0, tile)))
    put(I_EXPERT, jnp.where(past, e_last, jnp.where(lead, e_first, pick(e_col))))
    put(I_LO, lo)
    put(I_HI, hi)
    put(I_FIRST, valid * new_tile)
    put(I_MODE, valid * mode)
    put(I_FETCH, jnp.where(past, e_last, jnp.where(lead, e_first, pick(nxt))))
    put(I_START, jnp.where(lead, 1.0, pick(((item == item_start) & has_next).astype(F32))))
    put(I_PREP, jnp.where(lead, 1.0, pick(((item == item_end - 1.0) & has_next).astype(F32))))
    put(I_PREP_SLOT, pick(ordinal + 1.0 - 2.0 * jnp.floor((ordinal + 1.0) * 0.5)))
    put(I_SLOT, pick(ordinal - 2.0 * jnp.floor(ordinal * 0.5)))
    for field in range(I_SLOT + 1, ITEM_FIELDS):
        put(field, jnp.zeros_like(valid))


def _plan(counts, n_rows, tm):
    n_tiles = n_rows // tm
    n_items = n_tiles + N_EXPERTS
    width = -(-n_items // LANES) * LANES
    offsets, items = pl.pallas_call(
        functools.partial(_plan_kernel, float(tm), n_tiles),
        out_shape=[jax.ShapeDtypeStruct((1, LANES), F32),
                   jax.ShapeDtypeStruct((ITEM_FIELDS, width), jnp.int32)],
        name="plan",
    )(counts)
    return offsets, items, n_items


def _experts(xs, items, n_items, w_gate_up, bg, bu, w_down, bd):
    n_rows = xs.shape[0]
    tm = EXPERT_TILE
    half = D_MODEL // 2
    j = np.arange(2 * LANES)
    perm = np.zeros((2 * LANES, 2 * LANES), np.float32)
    perm[j, np.where(j % 2 == 0, j // 2, LANES + j // 2)] = 1.0
    row_tile = pl.BlockSpec((tm, half), lambda i, items: (items[I_TILE, i], 0))
    by_expert = lambda *shape: pl.BlockSpec((None,) + shape, lambda i, items: (items[I_EXPERT, i], 0, 0))
    in_hbm = pl.BlockSpec(memory_space=pl.ANY)
    return pl.pallas_call(
        _experts_kernel,
        grid_spec=pltpu.PrefetchScalarGridSpec(
            num_scalar_prefetch=1,
            grid=(n_items,),
            in_specs=[row_tile, in_hbm, by_expert(1, D_MODEL), by_expert(1, D_MODEL), in_hbm,
                      by_expert(1, D_MODEL),
                      pl.BlockSpec((2 * LANES, 2 * LANES), lambda i, items: (0, 0))],
            out_specs=row_tile,
            scratch_shapes=[pltpu.VMEM((D_MODEL, 2 * D_MODEL), F32), pltpu.VMEM((D_MODEL, D_MODEL), F32)]
            + [pltpu.VMEM((2, D_MODEL, D_MODEL), BF16)] * 3 + [pltpu.SemaphoreType.DMA((2,))],
        ),
        out_shape=jax.ShapeDtypeStruct((n_rows, half), U32),
        compiler_params=_params(("arbitrary",)),
        name="experts",
    )(items, xs, w_gate_up, bg, bu, w_down, bd, jnp.asarray(perm, dtype=BF16))


def _tail_kernel(h1_ref, yu_ref, gates_ref, p_ref, nple_ref, wg_ref, wp_ref, nfin_ref, *rest):
    out_ref = rest[-1]
    quarter = D_MODEL // 4
    for r in range(0, h1_ref.shape[0], TAIL_ROWS):
        rows = slice(r, r + TAIL_ROWS)
        gates = gates_ref[rows, :]
        lo = None
        hi = None
        for k in range(TOP_K):
            gk = gates[:, k:k + 1]
            yl, yh = _unpack_bf16_pair(yu_ref[k, rows, :])
            lo = gk * yl if lo is None else lo + gk * yl
            hi = gk * yh if hi is None else hi + gk * yh
        moe = jnp.concatenate([lo[:, :quarter], hi[:, :quarter], lo[:, quarter:], hi[:, quarter:]], axis=1)
        h2 = h1_ref[rows, :] + moe
        hn = _rms(h2, nple_ref[...]).astype(BF16)
        gate = _sigmoid(_dot(hn, wg_ref[...]))
        h3 = h2 + gate * _dot(p_ref[rows, :].astype(BF16), wp_ref[...])
        out_ref[rows, :] = _rms(h3, nfin_ref[...])


def _tail(h1, yu, gates, p2, norm_ple_w, ple_gate_bf, ple_proj_bf, final_norm_w, local0, tile0, result):
    t = yu.shape[1]
    tm = TOKEN_TILE
    half = D_MODEL // 2
    tile = lambda width: pl.BlockSpec((tm, width), lambda i: (i + local0, 0))
    shifted = lambda width: pl.BlockSpec((tm, width), lambda i: (i + tile0 + local0, 0))
    const2 = lambda shape: pl.BlockSpec(shape, lambda i: (0, 0))
    in_specs = [tile(D_MODEL), pl.BlockSpec((TOP_K, tm, half), lambda i: (0, i, 0)),
                tile(LANES), shifted(PLE_DIM), const2((1, D_MODEL)),
                const2((D_MODEL, D_MODEL)), const2((PLE_DIM, D_MODEL)), const2((1, D_MODEL))]
    args = [h1, yu, gates, p2, norm_ple_w, ple_gate_bf, ple_proj_bf, final_norm_w]
    aliases = {}
    if result is not None:
        in_specs.append(pl.BlockSpec(memory_space=pl.ANY))
        args.append(result)
        aliases = {len(args) - 1: 0}
    return pl.pallas_call(
        _tail_kernel,
        grid=(t // tm,),
        in_specs=in_specs,
        out_specs=shifted(D_MODEL),
        out_shape=jax.ShapeDtypeStruct((p2.shape[0], D_MODEL), F32),
        input_output_aliases=aliases,
        compiler_params=_params(("parallel",)),
        name="tail",
    )(*args)


def kernel(x, p, positions, w_in, w_out, ret_gn_w, pool_w, pool_scale, norm_mix_w, norm_moe_w, router_w, router_b, expert_w_gate_up, expert_b_gate_up, expert_w_down, expert_b_down, norm_ple_w, ple_gate_w, ple_proj_w, final_norm_w):
    batch, seq, d = x.shape
    depth = w_in.shape[0]
    assert depth == 1 and d == D_MODEL and seq % TOKEN_TILE == 0
    assert seq % (RET_CHUNK * RET_STEP_CHUNKS) == 0
    group_batches = GROUP_BATCHES if sum(GROUP_BATCHES) == batch else (batch,)
    t = batch * seq
    row = lambda a: a.reshape(1, -1).astype(F32)
    l = 0

    x2 = x.reshape(t, d)
    pos2 = positions.reshape(t, 1)
    p2 = p[l].reshape(t, PLE_DIM)
    w_in_bf = w_in[l].astype(BF16)
    rw = jnp.pad(router_w[l].astype(F32), ((0, 0), (0, LANES - N_EXPERTS)))
    rw_hi = rw.astype(BF16)
    rw_lo = (rw - rw_hi.astype(F32)).astype(BF16)
    rw_split = jnp.concatenate([rw_hi, rw_lo], axis=1)
    rb = jnp.pad(router_b[l].astype(F32), (0, LANES - N_EXPERTS), constant_values=NEG_BIG).reshape(1, LANES)
    bgu = expert_b_gate_up[l].reshape(N_EXPERTS, 1, D_MODEL, 2).astype(F32)
    bdn = expert_b_down[l].reshape(N_EXPERTS, 1, D_MODEL).astype(F32)
    pool_w_bf, w_out_bf = pool_w[l].astype(BF16), w_out[l].astype(BF16)
    ple_gate_bf, ple_proj_bf = ple_gate_w[l].astype(BF16), ple_proj_w[l].astype(BF16)

    out = None
    first_batch = 0
    for gi, nb in enumerate(group_batches):
        tg = nb * seq
        n_rows = tg * TOP_K
        assert n_rows % EXPERT_TILE == 0
        tile0 = first_batch * seq // TOKEN_TILE
        ret, u = _retention(x2, pos2, row(norm_mix_w[l]), w_in_bf, row(ret_gn_w[l]),
                            first_batch * seq // (RET_CHUNK * RET_STEP_CHUNKS), nb, seq)
        first_batch += nb
        h1, hn_packed, code, gates, counts = _mix_router(
            x2, ret, u, pool_w_bf, row(pool_scale[l]), w_out_bf, row(norm_moe_w[l]), rw_split, rb,
            seq, tile0)

        offsets, items, n_items = _plan(counts, n_rows, EXPERT_TILE)
        pos = _positions(code, offsets)
        xs = _sc_scatter(hn_packed, pos)
        y = _experts(xs, items, n_items, expert_w_gate_up[l], bgu[..., 0], bgu[..., 1],
                     expert_w_down[l], bdn)

        spans = LAST_GROUP_SPANS if gi == len(group_batches) - 1 else 1
        span = tg // spans
        for si in range(spans):
            idx = pos[:TOP_K, si * span:(si + 1) * span].reshape(TOP_K * span)
            yu = _sc_gather(y, idx).reshape(TOP_K, span, d // 2)
            out = _tail(h1, yu, gates, p2, row(norm_ple_w[l]), ple_gate_bf, ple_proj_bf,
                        row(final_norm_w), si * (span // TOKEN_TILE), tile0, out)
    return out.reshape(batch, seq, d)
```

```python
import functools

import numpy as np
import jax
import jax.numpy as jnp
from jax import lax
from jax.experimental import pallas as pl
from jax.experimental.pallas import tpu as pltpu
from jax.experimental.pallas import tpu_sc as plsc

D_MODEL = 1024
D_RET = 512
D_POOL = 512
RET_HEADS = 8
RET_HEAD_DIM = 64
HEAD_PAIRS = RET_HEADS // 2
ROPE_BASE = 10000.0
POOL_WINDOWS = (2, 4, 8, 16)
POOL_GROUP_DIM = 128
POOL_HISTORY = 16
POOL_BLOCK = 128
D_IN_PROJ = 4 * D_RET + D_POOL
N_EXPERTS = 32
TOP_K = 4
SWIGLU_LIMIT = 7.0
SWIGLU_ALPHA = 1.702
PLE_DIM = 256
NORM_EPS = 1e-5
GN_EPS = 1e-5

LANES = 128
NEG_BIG = -1e30

TOKEN_TILE = 512
POSITIONS_TILE = 1024
TAIL_TILE = 1024
TAIL_ROWS = 256
POS_ROWS = 8
RET_CHUNK = 256
RET_STEP_CHUNKS = 4
EXPERT_TILE = 512
EXPERT_COL_CHUNK = 512
SC_WINDOW = 128
GROUP_BATCHES = (5, 3)
LAST_GROUP_SPANS = 2
VMEM_LIMIT = 56 * 1024 * 1024

F32 = jnp.float32
BF16 = jnp.bfloat16
U32 = jnp.uint32


def _params(semantics):
    return pltpu.CompilerParams(dimension_semantics=semantics, vmem_limit_bytes=VMEM_LIMIT)


def _dot(a, b):
    return jnp.dot(a, b, preferred_element_type=F32)


def _dot_nt(a, b):
    return lax.dot_general(a, b, (((1,), (1,)), ((), ())), preferred_element_type=F32)


def _rms(x, w):
    ms = jnp.mean(x * x, axis=-1, keepdims=True)
    return x * lax.rsqrt(ms + NORM_EPS) * w


def _sigmoid(z):
    return 1.0 / (1.0 + jnp.exp(-z))


def _pack_bf16_pair(lo, hi):
    lo_bits = pltpu.bitcast(lo.astype(BF16).astype(F32), U32) >> 16
    hi_bits = pltpu.bitcast(hi.astype(BF16).astype(F32), U32) & jnp.uint32(0xFFFF0000)
    return lo_bits | hi_bits


def _unpack_bf16_pair(packed):
    lo = pltpu.bitcast(packed << 16, F32)
    hi = pltpu.bitcast(packed & jnp.uint32(0xFFFF0000), F32)
    return lo, hi


def _project_rows(x, pos, nw_ref, w_ref, freq_ref, qkvg_ref, u_ref, rows):
    hn = _rms(x, nw_ref[...]).astype(BF16)
    ang = pos.astype(F32) * freq_ref[0:1, :]
    cos = jnp.cos(ang)
    sin = jnp.sin(ang)
    slab = 2 * LANES
    cos_t = jnp.concatenate([cos, cos], axis=1)
    sin_up = jnp.concatenate([sin * freq_ref[1:2, :]] * 2, axis=1)
    sin_dn = jnp.concatenate([sin * freq_ref[2:3, :]] * 2, axis=1)
    half = RET_HEAD_DIM // 2

    for s in range(2 * D_RET // slab):
        sl = slice(s * slab, (s + 1) * slab)
        v = _dot(hn, w_ref[:, sl])
        v = v * cos_t + pltpu.roll(v, slab - half, 1) * sin_up + pltpu.roll(v, half, 1) * sin_dn
        if s >= D_RET // slab:
            v = v * (RET_HEAD_DIM ** -0.5)
        qkvg_ref[rows, sl] = v.astype(BF16)
    for s in range(2 * D_RET // 512, 4 * D_RET // 512):
        sl = slice(s * 512, (s + 1) * 512)
        qkvg_ref[rows, sl] = _dot(hn, w_ref[:, sl]).astype(BF16)
    u_ref[rows, :] = _dot(hn, w_ref[:, 4 * D_RET:]).astype(BF16)


def _rope_table():
    j = np.arange(LANES)
    half = RET_HEAD_DIM // 2
    inv_freq = ROPE_BASE ** (-(np.arange(half, dtype=np.float32)) / half)
    freq = np.zeros((8, LANES), np.float32)
    freq[0] = inv_freq[j % half]
    freq[1] = np.where(j % RET_HEAD_DIM < half, -1.0, 0.0)
    freq[2] = np.where(j % RET_HEAD_DIM >= half, 1.0, 0.0)
    return jnp.asarray(freq)


def _retention_kernel(x_ref, pos_ref, nw_ref, w_ref, freq_ref, dec_ref, xi_ref, zeta_ref, cd_ref, bd_ref,
                      m64_ref, eye_ref, gnw_ref, out_ref, u_ref, state_ref, qkvg_ref):
    @pl.when(pl.program_id(1) == 0)
    def _():
        state_ref[...] = jnp.zeros_like(state_ref)

    c = dec_ref.shape[1]
    n_chunks = x_ref.shape[0] // c
    lane = lax.broadcasted_iota(jnp.int32, (1, LANES), 1)
    m64 = m64_ref[...]
    bd = bd_ref[...]
    eye = eye_ref[...]

    def group_mean(v):
        hi = v.astype(BF16)
        lo = (v - hi.astype(F32)).astype(BF16)
        return _dot(jnp.concatenate([hi, lo], axis=1), m64)

    for ci in range(n_chunks):
        rows = slice(ci * c, (ci + 1) * c)
        _project_rows(x_ref[rows, :], pos_ref[rows, :], nw_ref, w_ref, freq_ref, qkvg_ref, u_ref, rows)
        for p in range(HEAD_PAIRS):
            sl = slice(p * LANES, (p + 1) * LANES)
            qp = qkvg_ref[rows, sl]
            kp = qkvg_ref[rows, D_RET + p * LANES:D_RET + (p + 1) * LANES]
            vp = qkvg_ref[rows, 2 * D_RET + p * LANES:2 * D_RET + (p + 1) * LANES]
            y = None
            for hh in range(2):
                in_head = (lane >= RET_HEAD_DIM) == bool(hh)
                qm = jnp.where(in_head, qp, jnp.zeros_like(qp))
                vm = jnp.where(in_head, vp, jnp.zeros_like(vp))
                scores = _dot_nt(qm, kp) * dec_ref[2 * p + hh]
                part = _dot(scores.astype(BF16), vm)
                y = part if y is None else y + part
            st = state_ref[p]
            y = y + _dot((qp.astype(F32) * xi_ref[p]).astype(BF16), st.astype(BF16))
            kz = (kp.astype(F32) * zeta_ref[p]).astype(BF16)
            kz_t = _dot_nt(eye, kz).astype(BF16)
            state_ref[p] = cd_ref[p] * st + _dot(kz_t, vp) * bd
            mu = group_mean(y)
            var = group_mean(y * y) - mu * mu
            yn = (y - mu) * lax.rsqrt(var + GN_EPS) * gnw_ref[:, sl]
            g = qkvg_ref[rows, 3 * D_RET + p * LANES:3 * D_RET + (p + 1) * LANES].astype(F32)
            out_ref[rows, sl] = (yn * g * _sigmoid(g)).astype(BF16)


def _retention_tables(c):
    h = np.arange(RET_HEADS, dtype=np.float64)
    log_gamma = np.log1p(-np.power(2.0, -5.0 - h))
    idx = np.arange(c, dtype=np.float64)
    rel = idx[:, None] - idx[None, :]
    dec = np.where(rel >= 0, np.exp(np.where(rel >= 0, rel, 0.0)[None] * log_gamma[:, None, None]), 0.0)
    lane_head = np.arange(LANES) // RET_HEAD_DIM
    xi = np.zeros((HEAD_PAIRS, c, LANES))
    zeta = np.zeros((HEAD_PAIRS, c, LANES))
    cd = np.zeros((HEAD_PAIRS, LANES, LANES))
    same = lane_head[:, None] == lane_head[None, :]
    for p in range(HEAD_PAIRS):
        lg = log_gamma[2 * p + lane_head]
        xi[p] = np.exp((idx + 1.0)[:, None] * lg[None, :])
        zeta[p] = np.exp((c - 1 - idx)[:, None] * lg[None, :])
        cd[p] = np.where(same, np.exp(c * lg)[:, None], 0.0)
    bd = same.astype(np.float32)
    m64 = np.concatenate([same, same], axis=0).astype(np.float32) / RET_HEAD_DIM
    f = lambda a: jnp.asarray(a, dtype=F32)
    return (f(dec), f(xi), f(zeta), f(cd), f(bd), jnp.asarray(m64, dtype=BF16),
            jnp.asarray(np.eye(LANES, dtype=np.float32), dtype=BF16))


def _retention(x2, pos2, norm_w, w_in_bf, gn_w, block0, batch, seq):
    c = RET_CHUNK
    rows = RET_STEP_CHUNKS * c
    n = seq // rows
    t = batch * seq
    dec, xi, zeta, cd, bd, m64, eye = _retention_tables(c)
    const3 = lambda shape: pl.BlockSpec(shape, lambda b, i: (0, 0, 0))
    const2 = lambda shape: pl.BlockSpec(shape, lambda b, i: (0, 0))
    out_tile = pl.BlockSpec((rows, D_RET), lambda b, i: (b * n + i, 0))
    return pl.pallas_call(
        _retention_kernel,
        grid=(batch, n),
        in_specs=[pl.BlockSpec((rows, D_MODEL), lambda b, i: (block0 + b * n + i, 0)),
                  pl.BlockSpec((rows, 1), lambda b, i: (block0 + b * n + i, 0)),
                  const2((1, D_MODEL)), const2((D_MODEL, D_IN_PROJ)), const2((8, LANES)),
                  const3((RET_HEADS, c, c)), const3((HEAD_PAIRS, c, LANES)),
                  const3((HEAD_PAIRS, c, LANES)), const3((HEAD_PAIRS, LANES, LANES)),
                  const2((LANES, LANES)), const2((2 * LANES, LANES)), const2((LANES, LANES)),
                  const2((1, D_RET))],
        out_specs=[out_tile, out_tile],
        out_shape=[jax.ShapeDtypeStruct((t, D_RET), BF16), jax.ShapeDtypeStruct((t, D_POOL), BF16)],
        scratch_shapes=[pltpu.VMEM((HEAD_PAIRS, LANES, LANES), F32), pltpu.VMEM((rows, 4 * D_RET), BF16)],
        compiler_params=_params(("arbitrary", "arbitrary")),
        name="retention",
    )(x2, pos2, norm_w, w_in_bf, _rope_table(), dec, xi, zeta, cd, bd, m64, eye, gn_w)


def _mix_router_kernel(seq, n_tiles, x_ref, ret_ref, u_ref, uprev_ref, band_ref, pw_ref, ps_ref, wout_ref,
                       nw_ref, rw_ref, rb_ref,
                       h1_ref, hnp_ref, code_ref, gates_ref, cnt_ref, uext_ref, logits_ref):
    i = pl.program_id(0)
    tm = x_ref.shape[0]
    tile = jnp.minimum(i, n_tiles - 1)
    t0 = lax.rem(tile * tm, seq)

    @pl.when(i == 0)
    def _():
        cnt_ref[...] = jnp.zeros_like(cnt_ref)
        logits_ref[...] = jnp.zeros_like(logits_ref)

    lane = lax.broadcasted_iota(jnp.int32, (tm, LANES), 1).astype(F32)
    code = jnp.zeros((tm, LANES), F32)
    vals = []
    work = logits_ref[...]
    for k in range(TOP_K):
        m = jnp.max(work, axis=-1, keepdims=True)
        idx = jnp.min(jnp.where(work == m, lane, float(LANES)), axis=-1, keepdims=True)
        chosen = lane == idx
        code = jnp.where(chosen, float(k + 1), code)
        work = jnp.where(chosen, -jnp.inf, work)
        vals.append(m)
    exps = [jnp.exp(v - vals[0]) for v in vals]
    denom = exps[0] + exps[1] + exps[2] + exps[3]
    gates = jnp.zeros((tm, LANES), F32)
    for k in range(TOP_K):
        gates = jnp.where(lane == float(k), exps[k] / denom, gates)
    code_ref[...] = code
    gates_ref[...] = gates
    routed = jnp.where(i > 0, 1.0, 0.0)
    cnt_ref[...] += routed * jnp.sum((code > 0).astype(F32), axis=0, keepdims=True)

    prev = uprev_ref[...]
    uext_ref[0:POOL_HISTORY, :] = jnp.where(t0 == 0, jnp.zeros_like(prev), prev)
    uext_ref[POOL_HISTORY:, :] = u_ref[...]

    row = lax.broadcasted_iota(jnp.int32, (tm, 1), 0)
    t_seq = (t0 + row + 1).astype(F32)
    mixed = []
    for gi, w in enumerate(POOL_WINDOWS):
        sl = slice(gi * POOL_GROUP_DIM, (gi + 1) * POOL_GROUP_DIM)
        wsum = jnp.concatenate(
            [_dot(band_ref[gi], uext_ref[r0:r0 + POOL_BLOCK + POOL_HISTORY, sl])
             for r0 in range(0, tm, POOL_BLOCK)], axis=0)
        count = jnp.minimum(t_seq, float(w))
        pooled = wsum / count - u_ref[:, sl].astype(F32)
        mixed.append(_dot(pooled.astype(BF16), pw_ref[gi]))
    pool = (jnp.concatenate(mixed, axis=1) * ps_ref[...]).astype(BF16)

    h1 = (x_ref[...] + _dot(ret_ref[...], wout_ref[0:D_RET, :])
          + _dot(pool, wout_ref[D_RET:, :]))
    h1_ref[...] = h1
    hn = _rms(h1, nw_ref[...])
    hn_hi = hn.astype(BF16)
    half = D_MODEL // 2
    hnp_ref[...] = _pack_bf16_pair(hn[:, :half], hn[:, half:])

    hn_lo = (hn - hn_hi.astype(F32)).astype(BF16)
    both = _dot(hn_hi, rw_ref[...])
    logits_ref[...] = (both[:, :LANES] + both[:, LANES:] + _dot(hn_lo, rw_ref[:, :LANES])
                       + rb_ref[...])


def _pool_bands():
    r = np.arange(POOL_BLOCK)[:, None]
    s = np.arange(POOL_BLOCK + POOL_HISTORY)[None, :] - POOL_HISTORY
    bands = [((s <= r) & (s > r - w)) for w in POOL_WINDOWS]
    return jnp.asarray(np.stack(bands).astype(np.float32), dtype=BF16)


def _mix_router(x2, ret, u, pool_w_bf, pool_scale, w_out_bf, norm_w, rw_split, rb, seq, tile0):
    t = ret.shape[0]
    tm = TOKEN_TILE
    hist_blocks = tm // POOL_HISTORY
    n = t // tm
    mixed = lambda i: jnp.minimum(i, n - 1)
    routed = lambda i: jnp.maximum(i - 1, 0)
    tile = lambda width: pl.BlockSpec((tm, width), lambda i: (mixed(i), 0))
    late = lambda width: pl.BlockSpec((tm, width), lambda i: (routed(i), 0))
    const2 = lambda shape: pl.BlockSpec(shape, lambda i: (0, 0))
    const3 = lambda shape: pl.BlockSpec(shape, lambda i: (0, 0, 0))
    return pl.pallas_call(
        functools.partial(_mix_router_kernel, seq, n),
        grid=(n + 1,),
        in_specs=[
            pl.BlockSpec((tm, D_MODEL), lambda i: (mixed(i) + tile0, 0)), tile(D_RET), tile(D_POOL),
            pl.BlockSpec((POOL_HISTORY, D_POOL), lambda i: (jnp.maximum(mixed(i) * hist_blocks - 1, 0), 0)),
            const3((len(POOL_WINDOWS), POOL_BLOCK, POOL_BLOCK + POOL_HISTORY)),
            const3((len(POOL_WINDOWS), POOL_GROUP_DIM, POOL_GROUP_DIM)),
            const2((1, D_POOL)), const2((D_MODEL, D_MODEL)), const2((1, D_MODEL)),
            const2((D_MODEL, 2 * LANES)), const2((1, LANES)),
        ],
        out_specs=[tile(D_MODEL), tile(D_MODEL // 2), late(LANES), late(LANES), const2((1, LANES))],
        out_shape=[
            jax.ShapeDtypeStruct((t, D_MODEL), F32),
            jax.ShapeDtypeStruct((t, D_MODEL // 2), U32),
            jax.ShapeDtypeStruct((t, LANES), F32),
            jax.ShapeDtypeStruct((t, LANES), F32),
            jax.ShapeDtypeStruct((1, LANES), F32),
        ],
        scratch_shapes=[pltpu.VMEM((tm + POOL_HISTORY, D_POOL), BF16), pltpu.VMEM((tm, LANES), F32)],
        compiler_params=_params(("arbitrary",)),
        name="mix_router",
    )(x2, ret, u, u, _pool_bands(), pool_w_bf, pool_scale, w_out_bf, norm_w, rw_split, rb)


def _positions_kernel(code_ref, off_ref, tri_ref, pos_ref, carry_ref):
    @pl.when(pl.program_id(0) == 0)
    def _():
        carry_ref[...] = jnp.zeros_like(carry_ref)

    code = code_ref[...]
    tm = code.shape[0]
    sel = (code > 0).astype(BF16)
    carry = carry_ref[...]
    rank = _dot(tri_ref[...], sel) + (carry + off_ref[...])
    carry_ref[...] = carry + jnp.sum(sel.astype(F32), axis=0, keepdims=True)
    lane = lax.broadcasted_iota(jnp.int32, (tm, LANES), 1)
    pos = jnp.zeros((tm, LANES), F32)
    for k in range(TOP_K):
        pk = jnp.sum(jnp.where(code == float(k + 1), rank, 0.0), axis=-1, keepdims=True)
        pos = jnp.where(lane == k, pk, pos)
    pos_ref[...] = pos.T[0:POS_ROWS, :].astype(jnp.int32)


def _positions(code, offsets):
    t = code.shape[0]
    tm = POSITIONS_TILE
    tri = jnp.asarray(np.tril(np.ones((tm, tm), np.float32), -1), dtype=BF16)
    return pl.pallas_call(
        _positions_kernel,
        grid=(t // tm,),
        in_specs=[pl.BlockSpec((tm, LANES), lambda i: (i, 0)),
                  pl.BlockSpec((1, LANES), lambda i: (0, 0)),
                  pl.BlockSpec((tm, tm), lambda i: (0, 0))],
        out_specs=pl.BlockSpec((POS_ROWS, tm), lambda i: (0, i)),
        out_shape=jax.ShapeDtypeStruct((POS_ROWS, t), jnp.int32),
        scratch_shapes=[pltpu.VMEM((1, LANES), F32)],
        compiler_params=_params(("arbitrary",)),
        name="positions",
    )(code, offsets, tri)


def _sc_mesh():
    return plsc.VectorSubcoreMesh(core_axis_name="core", subcore_axis_name="subcore")


def _sc_workers():
    info = plsc.get_sparse_core_info()
    return info.num_cores, info.num_cores * info.num_subcores


def _sc_gather(src, idx):
    n = idx.shape[0]
    d = src.shape[1]
    w = SC_WINDOW // 2
    num_cores, workers = _sc_workers()
    per = n // workers
    pairs = per // (2 * w)
    assert per * workers == n and pairs * 2 * w == per

    @functools.partial(
        pl.kernel, out_type=jax.ShapeDtypeStruct((n, d), src.dtype), mesh=_sc_mesh(),
        scratch_types=[pltpu.VMEM((w,), jnp.int32), pltpu.VMEM((w,), jnp.int32),
                       pltpu.VMEM((w, d), src.dtype), pltpu.VMEM((w, d), src.dtype),
                       pltpu.SemaphoreType.DMA, pltpu.SemaphoreType.DMA,
                       pltpu.SemaphoreType.DMA, pltpu.SemaphoreType.DMA],
        name="sc_gather")
    def gather(src_hbm, idx_hbm, out_hbm, idx0, idx1, rows0, rows1, g0, g1, w0, w1):
        first = (lax.axis_index("subcore") * num_cores + lax.axis_index("core")) * per

        def start_gather(win, idx_v, rows_v, sem):
            pltpu.sync_copy(idx_hbm.at[pl.ds(first + win * w, w)], idx_v)
            pltpu.async_copy(src_hbm.at[idx_v], rows_v, sem)

        def wait_gather(idx_v, rows_v, sem):
            pltpu.make_async_copy(src_hbm.at[idx_v], rows_v, sem).wait()

        def start_write(win, rows_v, sem):
            pltpu.async_copy(rows_v, out_hbm.at[pl.ds(first + win * w, w)], sem)

        def wait_write(rows_v, sem):
            pltpu.make_async_copy(rows_v, out_hbm.at[pl.ds(first, w)], sem).wait()

        start_gather(0, idx0, rows0, g0)

        @pl.loop(0, pairs)
        def _(j):
            even = 2 * j

            @pl.when(j > 0)
            def _():
                wait_write(rows1, w1)

            start_gather(even + 1, idx1, rows1, g1)
            wait_gather(idx0, rows0, g0)
            start_write(even, rows0, w0)
            wait_write(rows0, w0)

            @pl.when(j + 1 < pairs)
            def _():
                start_gather(even + 2, idx0, rows0, g0)

            wait_gather(idx1, rows1, g1)
            start_write(even + 1, rows1, w1)

        wait_write(rows1, w1)

    return gather(src, idx)


def _sc_scatter(src, idx):
    t, d = src.shape
    w = SC_WINDOW
    num_cores, workers = _sc_workers()
    per = t // workers
    assert idx.shape[0] >= TOP_K and idx.shape[1] == t and per * workers == t and per % w == 0

    @functools.partial(
        pl.kernel, out_type=jax.ShapeDtypeStruct((TOP_K * t, d), src.dtype), mesh=_sc_mesh(),
        scratch_types=[pltpu.VMEM((TOP_K, w), jnp.int32), pltpu.VMEM((w, d), src.dtype),
                       pltpu.SemaphoreType.DMA],
        name="sc_scatter")
    def scatter(src_hbm, idx_hbm, out_hbm, idx_v, rows_v, sem):
        first = (lax.axis_index("subcore") * num_cores + lax.axis_index("core")) * per

        @pl.loop(0, per // w)
        def _(j):
            base = first + j * w
            for k in range(TOP_K):
                pltpu.sync_copy(idx_hbm.at[k, pl.ds(base, w)], idx_v.at[k])
            pltpu.sync_copy(src_hbm.at[pl.ds(base, w)], rows_v)
            copies = [pltpu.async_copy(rows_v, out_hbm.at[idx_v.at[k]], sem) for k in range(TOP_K)]
            for c in copies:
                c.wait()

    return scatter(src, idx)


def _experts_kernel(items_ref, xs_ref, wgu_hbm, bg_ref, bu_ref, wdn_hbm, bd_ref, perm_ref, y_ref,
                    wgu_ref, wdn_ref, wg_ref, wu_ref, wd_ref, sems):
    i = pl.program_id(0)
    tm = xs_ref.shape[0]

    def weight_copies():
        e = items_ref[I_FETCH, i]
        return (pltpu.make_async_copy(wgu_hbm.at[e], wgu_ref, sems.at[0]),
                pltpu.make_async_copy(wdn_hbm.at[e], wdn_ref, sems.at[1]))

    @pl.when(items_ref[I_START, i] == 1)
    def _():
        for copy in weight_copies():
            copy.start()

    @pl.when(items_ref[I_PREP, i] == 1)
    def _():
        for copy in weight_copies():
            copy.wait()
        ps = items_ref[I_PREP_SLOT, i]
        perm = perm_ref[...]
        pair = 2 * LANES
        for c in range(2 * D_MODEL // pair):
            sel = _dot(wgu_ref[:, c * pair:(c + 1) * pair].astype(BF16), perm).astype(BF16)
            wg_ref[ps, :, c * LANES:(c + 1) * LANES] = sel[:, :LANES]
            wu_ref[ps, :, c * LANES:(c + 1) * LANES] = sel[:, LANES:]
        wd_ref[ps] = wdn_ref[...].astype(BF16)

    def expert_mlp(r0, r1):
        n = r1 - r0
        s = items_ref[I_SLOT, i]
        x_lo, x_hi = _unpack_bf16_pair(xs_ref[r0:r1, :])
        x = jnp.concatenate([x_lo.astype(BF16), x_hi.astype(BF16)], axis=1)
        acts = []
        chunks = [slice(c * EXPERT_COL_CHUNK, (c + 1) * EXPERT_COL_CHUNK)
                  for c in range(D_MODEL // EXPERT_COL_CHUNK)]
        for cs in chunks:
            gate = jnp.minimum(_dot(x, wg_ref[s, :, cs]) + bg_ref[:, cs], SWIGLU_LIMIT)
            up = jnp.clip(_dot(x, wu_ref[s, :, cs]) + bu_ref[:, cs], -SWIGLU_LIMIT, SWIGLU_LIMIT)
            acts.append(((up + 1.0) * (gate * _sigmoid(SWIGLU_ALPHA * gate))).astype(BF16))
        rows = items_ref[I_TILE, i] * tm + r0 + lax.broadcasted_iota(jnp.int32, (n, 1), 0)
        mine = (rows >= items_ref[I_LO, i]) & (rows < items_ref[I_HI, i])

        half, quarter = D_MODEL // 2, D_MODEL // 4
        for h in range(2):
            hs = slice(h * half, (h + 1) * half)
            yh = bd_ref[:, hs]
            for cs, act in zip(chunks, acts):
                yh = yh + _dot(act, wd_ref[s, cs, hs])
            cols = slice(h * quarter, (h + 1) * quarter)
            packed = _pack_bf16_pair(yh[:, :quarter], yh[:, quarter:])
            y_ref[r0:r1, cols] = jnp.where(mine, packed, y_ref[r0:r1, cols])

    @pl.when(items_ref[I_FIRST, i] == 1)
    def _():
        y_ref[...] = jnp.zeros(y_ref.shape, U32)

    block = tm // ROW_BLOCKS
    for m in range(1, ROW_BLOCKS + 1):
        pl.when(items_ref[I_MODE, i] == m)(functools.partial(expert_mlp, 0, m * block))
    for m in range(1, ROW_BLOCKS):
        pl.when(items_ref[I_MODE, i] == ROW_BLOCKS + m)(functools.partial(expert_mlp, m * block, tm))


(I_TILE, I_EXPERT, I_LO, I_HI, I_FIRST, I_MODE, I_FETCH, I_START, I_PREP, I_PREP_SLOT, I_SLOT) = range(11)
ITEM_FIELDS = 16
ROW_BLOCKS = 4


def _plan_kernel(tm, n_tiles, cnt_ref, off_ref, items_ref):
    n = items_ref.shape[1]
    cnt_r = cnt_ref[...]
    sub = lax.broadcasted_iota(jnp.int32, (LANES, LANES), 0).astype(F32)
    lan = lax.broadcasted_iota(jnp.int32, (LANES, LANES), 1).astype(F32)
    big = float(4 * LANES)

    def col(row):
        return jnp.sum(jnp.where(lan == sub, row, 0.0), axis=1, keepdims=True)

    def row(column):
        return jnp.sum(jnp.where(lan == sub, column, 0.0), axis=0, keepdims=True)

    def prefix(r):
        return jnp.sum(jnp.where(lan <= sub, r, 0.0), axis=1, keepdims=True)

    cnt = col(cnt_r)
    ends = prefix(cnt_r)
    starts = ends - cnt
    off_ref[...] = row(starts)
    used = cnt > 0.0
    inv_tm = 1.0 / tm
    first_tile = jnp.floor(starts * inv_tm)
    last_tile = jnp.floor(jnp.maximum(ends - 1.0, 0.0) * inv_tm)
    per = jnp.where(used, last_tile - first_tile + 1.0, 0.0)
    item_end = prefix(row(per))
    item_start = item_end - per
    used_r = row(used.astype(F32))
    ordinal = prefix(used_r) - 1.0
    nxt = jnp.min(jnp.where((lan > sub) & (used_r > 0.0), lan, big), axis=1, keepdims=True)
    has_next = nxt < big
    e_col = sub[:, 0:1]
    nxt = jnp.where(has_next, nxt, e_col)
    e_first = jnp.min(jnp.where(used_r > 0.0, lan[0:1, :], big), axis=1, keepdims=True)
    e_last = jnp.max(jnp.where(used_r > 0.0, lan[0:1, :], -1.0), axis=1, keepdims=True)
    total = jnp.max(item_end, axis=0, keepdims=True)

    item = lax.broadcasted_iota(jnp.int32, (LANES, n), 1).astype(F32) - 1.0
    mine = ((item >= item_start) & (item < item_end)).astype(F32)
    pick = lambda column: jnp.sum(mine * column, axis=0, keepdims=True)
    item_r = item[0:1, :]
    valid = pick(jnp.ones_like(cnt))
    lead = item_r < 0.0
    past = item_r >= total
    new_tile = 1.0 - pick(((item == item_start) & (starts - first_tile * tm > 0.0)).astype(F32))

    def put(field, value):
        items_ref[field:field + 1, :] = value.astype(jnp.int32)

    tile = pick(first_tile - item_start) + item_r
    lo = pick(starts)
    hi = pick(ends)
    inv_block = ROW_BLOCKS * inv_tm
    first_block = jnp.floor(jnp.maximum(lo - tile * tm, 0.0) * inv_block)
    end_block = jnp.floor((jnp.minimum(hi - tile * tm, tm) + (tm / ROW_BLOCKS - 1.0)) * inv_block)
    mode = jnp.where(first_block == 0.0, end_block,
                     jnp.where(end_block == float(ROW_BLOCKS), ROW_BLOCKS + first_block, float(ROW_BLOCKS)))
    put(I_TILE, jnp.where(past, float(n_tiles - 1), jnp.where(lead, 0.0, tile)))
    put(I_EXPERT, jnp.where(past, e_last, jnp.where(lead, e_first, pick(e_col))))
    put(I_LO, lo)
    put(I_HI, hi)
    put(I_FIRST, valid * new_tile)
    put(I_MODE, valid * mode)
    put(I_FETCH, jnp.where(past, e_last, jnp.where(lead, e_first, pick(nxt))))
    put(I_START, jnp.where(lead, 1.0, pick(((item == item_start) & has_next).astype(F32))))
    put(I_PREP, jnp.where(lead, 1.0, pick(((item == item_end - 1.0) & has_next).astype(F32))))
    put(I_PREP_SLOT, pick(ordinal + 1.0 - 2.0 * jnp.floor((ordinal + 1.0) * 0.5)))
    put(I_SLOT, pick(ordinal - 2.0 * jnp.floor(ordinal * 0.5)))
    for field in range(I_SLOT + 1, ITEM_FIELDS):
        put(field, jnp.zeros_like(valid))


def _plan(counts, n_rows, tm):
    n_tiles = n_rows // tm
    n_items = n_tiles + N_EXPERTS
    width = -(-n_items // LANES) * LANES
    offsets, items = pl.pallas_call(
        functools.partial(_plan_kernel, float(tm), n_tiles),
        out_shape=[jax.ShapeDtypeStruct((1, LANES), F32),
                   jax.ShapeDtypeStruct((ITEM_FIELDS, width), jnp.int32)],
        name="plan",
    )(counts)
    return offsets, items, n_items


def _experts(xs, items, n_items, w_gate_up, bg, bu, w_down, bd):
    n_rows = xs.shape[0]
    tm = EXPERT_TILE
    half = D_MODEL // 2
    j = np.arange(2 * LANES)
    perm = np.zeros((2 * LANES, 2 * LANES), np.float32)
    perm[j, np.where(j % 2 == 0, j // 2, LANES + j // 2)] = 1.0
    row_tile = pl.BlockSpec((tm, half), lambda i, items: (items[I_TILE, i], 0))
    by_expert = lambda *shape: pl.BlockSpec((None,) + shape, lambda i, items: (items[I_EXPERT, i], 0, 0))
    in_hbm = pl.BlockSpec(memory_space=pl.ANY)
    return pl.pallas_call(
        _experts_kernel,
        grid_spec=pltpu.PrefetchScalarGridSpec(
            num_scalar_prefetch=1,
            grid=(n_items,),
            in_specs=[row_tile, in_hbm, by_expert(1, D_MODEL), by_expert(1, D_MODEL), in_hbm,
                      by_expert(1, D_MODEL),
                      pl.BlockSpec((2 * LANES, 2 * LANES), lambda i, items: (0, 0))],
            out_specs=row_tile,
            scratch_shapes=[pltpu.VMEM((D_MODEL, 2 * D_MODEL), F32), pltpu.VMEM((D_MODEL, D_MODEL), F32)]
            + [pltpu.VMEM((2, D_MODEL, D_MODEL), BF16)] * 3 + [pltpu.SemaphoreType.DMA((2,))],
        ),
        out_shape=jax.ShapeDtypeStruct((n_rows, half), U32),
        compiler_params=_params(("arbitrary",)),
        name="experts",
    )(items, xs, w_gate_up, bg, bu, w_down, bd, jnp.asarray(perm, dtype=BF16))


def _tail_kernel(h1_ref, yu_ref, gates_ref, p_ref, nple_ref, wg_ref, wp_ref, nfin_ref, *rest):
    out_ref = rest[-1]
    quarter = D_MODEL // 4
    for r in range(0, h1_ref.shape[0], TAIL_ROWS):
        rows = slice(r, r + TAIL_ROWS)
        gates = gates_ref[rows, :]
        lo = None
        hi = None
        for k in range(TOP_K):
            gk = gates[:, k:k + 1]
            yl, yh = _unpack_bf16_pair(yu_ref[k, rows, :])
            lo = gk * yl if lo is None else lo + gk * yl
            hi = gk * yh if hi is None else hi + gk * yh
        moe = jnp.concatenate([lo[:, :quarter], hi[:, :quarter], lo[:, quarter:], hi[:, quarter:]], axis=1)
        h2 = h1_ref[rows, :] + moe
        hn = _rms(h2, nple_ref[...]).astype(BF16)
        gate = _sigmoid(_dot(hn, wg_ref[...]))
        h3 = h2 + gate * _dot(p_ref[rows, :].astype(BF16), wp_ref[...])
        out_ref[rows, :] = _rms(h3, nfin_ref[...])


def _tail(h1, yu, gates, p2, norm_ple_w, ple_gate_bf, ple_proj_bf, final_norm_w, local0, tile0, result):
    t = yu.shape[1]
    tm = TAIL_TILE
    assert t % tm == 0
    half = D_MODEL // 2
    tile = lambda width: pl.BlockSpec((tm, width), lambda i: (i + local0, 0))
    shifted = lambda width: pl.BlockSpec((tm, width), lambda i: (i + tile0 + local0, 0))
    const2 = lambda shape: pl.BlockSpec(shape, lambda i: (0, 0))
    in_specs = [tile(D_MODEL), pl.BlockSpec((TOP_K, tm, half), lambda i: (0, i, 0)),
                tile(LANES), shifted(PLE_DIM), const2((1, D_MODEL)),
                const2((D_MODEL, D_MODEL)), const2((PLE_DIM, D_MODEL)), const2((1, D_MODEL))]
    args = [h1, yu, gates, p2, norm_ple_w, ple_gate_bf, ple_proj_bf, final_norm_w]
    aliases = {}
    if result is not None:
        in_specs.append(pl.BlockSpec(memory_space=pl.ANY))
        args.append(result)
        aliases = {len(args) - 1: 0}
    return pl.pallas_call(
        _tail_kernel,
        grid=(t // tm,),
        in_specs=in_specs,
        out_specs=shifted(D_MODEL),
        out_shape=jax.ShapeDtypeStruct((p2.shape[0], D_MODEL), F32),
        input_output_aliases=aliases,
        compiler_params=_params(("parallel",)),
        name="tail",
    )(*args)


def kernel(x, p, positions, w_in, w_out, ret_gn_w, pool_w, pool_scale, norm_mix_w, norm_moe_w, router_w, router_b, expert_w_gate_up, expert_b_gate_up, expert_w_down, expert_b_down, norm_ple_w, ple_gate_w, ple_proj_w, final_norm_w):
    batch, seq, d = x.shape
    depth = w_in.shape[0]
    assert depth == 1 and d == D_MODEL and seq % TOKEN_TILE == 0
    assert seq % (RET_CHUNK * RET_STEP_CHUNKS) == 0
    group_batches = GROUP_BATCHES if sum(GROUP_BATCHES) == batch else (batch,)
    t = batch * seq
    row = lambda a: a.reshape(1, -1).astype(F32)
    l = 0

    x2 = x.reshape(t, d)
    pos2 = positions.reshape(t, 1)
    p2 = p[l].reshape(t, PLE_DIM)
    w_in_bf = w_in[l].astype(BF16)
    rw = jnp.pad(router_w[l].astype(F32), ((0, 0), (0, LANES - N_EXPERTS)))
    rw_hi = rw.astype(BF16)
    rw_lo = (rw - rw_hi.astype(F32)).astype(BF16)
    rw_split = jnp.concatenate([rw_hi, rw_lo], axis=1)
    rb = jnp.pad(router_b[l].astype(F32), (0, LANES - N_EXPERTS), constant_values=NEG_BIG).reshape(1, LANES)
    bgu = expert_b_gate_up[l].reshape(N_EXPERTS, 1, D_MODEL, 2).astype(F32)
    bdn = expert_b_down[l].reshape(N_EXPERTS, 1, D_MODEL).astype(F32)
    pool_w_bf, w_out_bf = pool_w[l].astype(BF16), w_out[l].astype(BF16)
    ple_gate_bf, ple_proj_bf = ple_gate_w[l].astype(BF16), ple_proj_w[l].astype(BF16)

    out = None
    first_batch = 0
    for gi, nb in enumerate(group_batches):
        tg = nb * seq
        n_rows = tg * TOP_K
        assert n_rows % EXPERT_TILE == 0
        tile0 = first_batch * seq // TOKEN_TILE
        ret, u = _retention(x2, pos2, row(norm_mix_w[l]), w_in_bf, row(ret_gn_w[l]),
                            first_batch * seq // (RET_CHUNK * RET_STEP_CHUNKS), nb, seq)
        first_batch += nb
        h1, hn_packed, code, gates, counts = _mix_router(
            x2, ret, u, pool_w_bf, row(pool_scale[l]), w_out_bf, row(norm_moe_w[l]), rw_split, rb,
            seq, tile0)

        offsets, items, n_items = _plan(counts, n_rows, EXPERT_TILE)
        pos = _positions(code, offsets)
        xs = _sc_scatter(hn_packed, pos)
        y = _experts(xs, items, n_items, expert_w_gate_up[l], bgu[..., 0], bgu[..., 1],
                     expert_w_down[l], bdn)

        spans = LAST_GROUP_SPANS if gi == len(group_batches) - 1 else 1
        span = tg // spans
        for si in range(spans):
            idx = pos[:TOP_K, si * span:(si + 1) * span].reshape(TOP_K * span)
            yu = _sc_gather(y, idx).reshape(TOP_K, span, d // 2)
            out = _tail(h1, yu, gates, p2, row(norm_ple_w[l]), ple_gate_bf, ple_proj_bf,
                        row(final_norm_w), si * (span // TAIL_TILE), tile0 * TOKEN_TILE // TAIL_TILE, out)
    return out.reshape(batch, seq, d)
```

```python
import functools

import numpy as np
import jax
import jax.numpy as jnp
from jax import lax
from jax.experimental import pallas as pl
from jax.experimental.pallas import tpu as pltpu
from jax.experimental.pallas import tpu_sc as plsc

D_MODEL = 1024
D_RET = 512
D_POOL = 512
RET_HEADS = 8
RET_HEAD_DIM = 64
HEAD_PAIRS = RET_HEADS // 2
ROPE_BASE = 10000.0
POOL_WINDOWS = (2, 4, 8, 16)
POOL_GROUP_DIM = 128
POOL_HISTORY = 16
POOL_BLOCK = 128
D_IN_PROJ = 4 * D_RET + D_POOL
N_EXPERTS = 32
TOP_K = 4
SWIGLU_LIMIT = 7.0
SWIGLU_ALPHA = 1.702
PLE_DIM = 256
NORM_EPS = 1e-5
GN_EPS = 1e-5

LANES = 128
NEG_BIG = -1e30

TOKEN_TILE = 1024
POSITIONS_TILE = 1024
TAIL_TILE = 1024
TAIL_ROWS = 256
POS_ROWS = 8
RET_CHUNK = 256
RET_STEP_CHUNKS = 4
EXPERT_TILE = 512
EXPERT_COL_CHUNK = 512
SC_WINDOW = 128
GROUP_BATCHES = (5, 3)
LAST_GROUP_SPANS = 2
VMEM_LIMIT = 56 * 1024 * 1024

F32 = jnp.float32
BF16 = jnp.bfloat16
U32 = jnp.uint32


def _params(semantics):
    return pltpu.CompilerParams(dimension_semantics=semantics, vmem_limit_bytes=VMEM_LIMIT)


def _dot(a, b):
    return jnp.dot(a, b, preferred_element_type=F32)


def _dot_nt(a, b):
    return lax.dot_general(a, b, (((1,), (1,)), ((), ())), preferred_element_type=F32)


def _rms(x, w):
    ms = jnp.mean(x * x, axis=-1, keepdims=True)
    return x * lax.rsqrt(ms + NORM_EPS) * w


def _sigmoid(z):
    return 1.0 / (1.0 + jnp.exp(-z))


def _pack_bf16_pair(lo, hi):
    lo_bits = pltpu.bitcast(lo.astype(BF16).astype(F32), U32) >> 16
    hi_bits = pltpu.bitcast(hi.astype(BF16).astype(F32), U32) & jnp.uint32(0xFFFF0000)
    return lo_bits | hi_bits


def _unpack_bf16_pair(packed):
    lo = pltpu.bitcast(packed << 16, F32)
    hi = pltpu.bitcast(packed & jnp.uint32(0xFFFF0000), F32)
    return lo, hi


def _project_rows(x, pos, nw_ref, w_ref, freq_ref, qkvg_ref, u_ref, rows):
    hn = _rms(x, nw_ref[...]).astype(BF16)
    ang = pos.astype(F32) * freq_ref[0:1, :]
    cos = jnp.cos(ang)
    sin = jnp.sin(ang)
    slab = 2 * LANES
    cos_t = jnp.concatenate([cos, cos], axis=1)
    sin_up = jnp.concatenate([sin * freq_ref[1:2, :]] * 2, axis=1)
    sin_dn = jnp.concatenate([sin * freq_ref[2:3, :]] * 2, axis=1)
    half = RET_HEAD_DIM // 2

    for s in range(2 * D_RET // slab):
        sl = slice(s * slab, (s + 1) * slab)
        v = _dot(hn, w_ref[:, sl])
        v = v * cos_t + pltpu.roll(v, slab - half, 1) * sin_up + pltpu.roll(v, half, 1) * sin_dn
        if s >= D_RET // slab:
            v = v * (RET_HEAD_DIM ** -0.5)
        qkvg_ref[rows, sl] = v.astype(BF16)
    for s in range(2 * D_RET // 512, 4 * D_RET // 512):
        sl = slice(s * 512, (s + 1) * 512)
        qkvg_ref[rows, sl] = _dot(hn, w_ref[:, sl]).astype(BF16)
    u_ref[rows, :] = _dot(hn, w_ref[:, 4 * D_RET:]).astype(BF16)


def _rope_table():
    j = np.arange(LANES)
    half = RET_HEAD_DIM // 2
    inv_freq = ROPE_BASE ** (-(np.arange(half, dtype=np.float32)) / half)
    freq = np.zeros((8, LANES), np.float32)
    freq[0] = inv_freq[j % half]
    freq[1] = np.where(j % RET_HEAD_DIM < half, -1.0, 0.0)
    freq[2] = np.where(j % RET_HEAD_DIM >= half, 1.0, 0.0)
    return jnp.asarray(freq)


def _retention_kernel(x_ref, pos_ref, nw_ref, w_ref, freq_ref, dec_ref, xi_ref, zeta_ref, cd_ref, bd_ref,
                      m64_ref, eye_ref, gnw_ref, out_ref, u_ref, state_ref, qkvg_ref):
    @pl.when(pl.program_id(1) == 0)
    def _():
        state_ref[...] = jnp.zeros_like(state_ref)

    c = dec_ref.shape[1]
    n_chunks = x_ref.shape[0] // c
    lane = lax.broadcasted_iota(jnp.int32, (1, LANES), 1)
    m64 = m64_ref[...]
    bd = bd_ref[...]
    eye = eye_ref[...]

    def group_mean(v):
        hi = v.astype(BF16)
        lo = (v - hi.astype(F32)).astype(BF16)
        return _dot(jnp.concatenate([hi, lo], axis=1), m64)

    for ci in range(n_chunks):
        rows = slice(ci * c, (ci + 1) * c)
        _project_rows(x_ref[rows, :], pos_ref[rows, :], nw_ref, w_ref, freq_ref, qkvg_ref, u_ref, rows)
        for p in range(HEAD_PAIRS):
            sl = slice(p * LANES, (p + 1) * LANES)
            qp = qkvg_ref[rows, sl]
            kp = qkvg_ref[rows, D_RET + p * LANES:D_RET + (p + 1) * LANES]
            vp = qkvg_ref[rows, 2 * D_RET + p * LANES:2 * D_RET + (p + 1) * LANES]
            y = None
            for hh in range(2):
                in_head = (lane >= RET_HEAD_DIM) == bool(hh)
                qm = jnp.where(in_head, qp, jnp.zeros_like(qp))
                vm = jnp.where(in_head, vp, jnp.zeros_like(vp))
                scores = _dot_nt(qm, kp) * dec_ref[2 * p + hh]
                part = _dot(scores.astype(BF16), vm)
                y = part if y is None else y + part
            st = state_ref[p]
            y = y + _dot((qp.astype(F32) * xi_ref[p]).astype(BF16), st.astype(BF16))
            kz = (kp.astype(F32) * zeta_ref[p]).astype(BF16)
            kz_t = _dot_nt(eye, kz).astype(BF16)
            state_ref[p] = cd_ref[p] * st + _dot(kz_t, vp) * bd
            mu = group_mean(y)
            var = group_mean(y * y) - mu * mu
            yn = (y - mu) * lax.rsqrt(var + GN_EPS) * gnw_ref[:, sl]
            g = qkvg_ref[rows, 3 * D_RET + p * LANES:3 * D_RET + (p + 1) * LANES].astype(F32)
            out_ref[rows, sl] = (yn * g * _sigmoid(g)).astype(BF16)


def _retention_tables(c):
    h = np.arange(RET_HEADS, dtype=np.float64)
    log_gamma = np.log1p(-np.power(2.0, -5.0 - h))
    idx = np.arange(c, dtype=np.float64)
    rel = idx[:, None] - idx[None, :]
    dec = np.where(rel >= 0, np.exp(np.where(rel >= 0, rel, 0.0)[None] * log_gamma[:, None, None]), 0.0)
    lane_head = np.arange(LANES) // RET_HEAD_DIM
    xi = np.zeros((HEAD_PAIRS, c, LANES))
    zeta = np.zeros((HEAD_PAIRS, c, LANES))
    cd = np.zeros((HEAD_PAIRS, LANES, LANES))
    same = lane_head[:, None] == lane_head[None, :]
    for p in range(HEAD_PAIRS):
        lg = log_gamma[2 * p + lane_head]
        xi[p] = np.exp((idx + 1.0)[:, None] * lg[None, :])
        zeta[p] = np.exp((c - 1 - idx)[:, None] * lg[None, :])
        cd[p] = np.where(same, np.exp(c * lg)[:, None], 0.0)
    bd = same.astype(np.float32)
    m64 = np.concatenate([same, same], axis=0).astype(np.float32) / RET_HEAD_DIM
    f = lambda a: jnp.asarray(a, dtype=F32)
    return (f(dec), f(xi), f(zeta), f(cd), f(bd), jnp.asarray(m64, dtype=BF16),
            jnp.asarray(np.eye(LANES, dtype=np.float32), dtype=BF16))


def _retention(x2, pos2, norm_w, w_in_bf, gn_w, block0, batch, seq):
    c = RET_CHUNK
    rows = RET_STEP_CHUNKS * c
    n = seq // rows
    t = batch * seq
    dec, xi, zeta, cd, bd, m64, eye = _retention_tables(c)
    const3 = lambda shape: pl.BlockSpec(shape, lambda b, i: (0, 0, 0))
    const2 = lambda shape: pl.BlockSpec(shape, lambda b, i: (0, 0))
    out_tile = pl.BlockSpec((rows, D_RET), lambda b, i: (b * n + i, 0))
    return pl.pallas_call(
        _retention_kernel,
        grid=(batch, n),
        in_specs=[pl.BlockSpec((rows, D_MODEL), lambda b, i: (block0 + b * n + i, 0)),
                  pl.BlockSpec((rows, 1), lambda b, i: (block0 + b * n + i, 0)),
                  const2((1, D_MODEL)), const2((D_MODEL, D_IN_PROJ)), const2((8, LANES)),
                  const3((RET_HEADS, c, c)), const3((HEAD_PAIRS, c, LANES)),
                  const3((HEAD_PAIRS, c, LANES)), const3((HEAD_PAIRS, LANES, LANES)),
                  const2((LANES, LANES)), const2((2 * LANES, LANES)), const2((LANES, LANES)),
                  const2((1, D_RET))],
        out_specs=[out_tile, out_tile],
        out_shape=[jax.ShapeDtypeStruct((t, D_RET), BF16), jax.ShapeDtypeStruct((t, D_POOL), BF16)],
        scratch_shapes=[pltpu.VMEM((HEAD_PAIRS, LANES, LANES), F32), pltpu.VMEM((rows, 4 * D_RET), BF16)],
        compiler_params=_params(("arbitrary", "arbitrary")),
        name="retention",
    )(x2, pos2, norm_w, w_in_bf, _rope_table(), dec, xi, zeta, cd, bd, m64, eye, gn_w)


def _mix_router_kernel(seq, n_tiles, x_ref, ret_ref, u_ref, uprev_ref, band_ref, pw_ref, ps_ref, wout_ref,
                       nw_ref, rw_ref, rb_ref,
                       h1_ref, hnp_ref, code_ref, gates_ref, cnt_ref, uext_ref, logits_ref):
    i = pl.program_id(0)
    tm = x_ref.shape[0]
    tile = jnp.minimum(i, n_tiles - 1)
    t0 = lax.rem(tile * tm, seq)

    @pl.when(i == 0)
    def _():
        cnt_ref[...] = jnp.zeros_like(cnt_ref)
        logits_ref[...] = jnp.zeros_like(logits_ref)

    lane = lax.broadcasted_iota(jnp.int32, (tm, LANES), 1).astype(F32)
    code = jnp.zeros((tm, LANES), F32)
    vals = []
    work = logits_ref[...]
    for k in range(TOP_K):
        m = jnp.max(work, axis=-1, keepdims=True)
        idx = jnp.min(jnp.where(work == m, lane, float(LANES)), axis=-1, keepdims=True)
        chosen = lane == idx
        code = jnp.where(chosen, float(k + 1), code)
        work = jnp.where(chosen, -jnp.inf, work)
        vals.append(m)
    exps = [jnp.exp(v - vals[0]) for v in vals]
    denom = exps[0] + exps[1] + exps[2] + exps[3]
    gates = jnp.zeros((tm, LANES), F32)
    for k in range(TOP_K):
        gates = jnp.where(lane == float(k), exps[k] / denom, gates)
    code_ref[...] = code
    gates_ref[...] = gates
    routed = jnp.where(i > 0, 1.0, 0.0)
    cnt_ref[...] += routed * jnp.sum((code > 0).astype(F32), axis=0, keepdims=True)

    prev = uprev_ref[...]
    uext_ref[0:POOL_HISTORY, :] = jnp.where(t0 == 0, jnp.zeros_like(prev), prev)
    uext_ref[POOL_HISTORY:, :] = u_ref[...]

    row = lax.broadcasted_iota(jnp.int32, (tm, 1), 0)
    t_seq = (t0 + row + 1).astype(F32)
    mixed = []
    for gi, w in enumerate(POOL_WINDOWS):
        sl = slice(gi * POOL_GROUP_DIM, (gi + 1) * POOL_GROUP_DIM)
        wsum = jnp.concatenate(
            [_dot(band_ref[gi], uext_ref[r0:r0 + POOL_BLOCK + POOL_HISTORY, sl])
             for r0 in range(0, tm, POOL_BLOCK)], axis=0)
        count = jnp.minimum(t_seq, float(w))
        pooled = wsum / count - u_ref[:, sl].astype(F32)
        mixed.append(_dot(pooled.astype(BF16), pw_ref[gi]))
    pool = (jnp.concatenate(mixed, axis=1) * ps_ref[...]).astype(BF16)

    h1 = (x_ref[...] + _dot(ret_ref[...], wout_ref[0:D_RET, :])
          + _dot(pool, wout_ref[D_RET:, :]))
    h1_ref[...] = h1
    hn = _rms(h1, nw_ref[...])
    hn_hi = hn.astype(BF16)
    half = D_MODEL // 2
    hnp_ref[...] = _pack_bf16_pair(hn[:, :half], hn[:, half:])

    hn_lo = (hn - hn_hi.astype(F32)).astype(BF16)
    both = _dot(hn_hi, rw_ref[...])
    logits_ref[...] = (both[:, :LANES] + both[:, LANES:] + _dot(hn_lo, rw_ref[:, :LANES])
                       + rb_ref[...])


def _pool_bands():
    r = np.arange(POOL_BLOCK)[:, None]
    s = np.arange(POOL_BLOCK + POOL_HISTORY)[None, :] - POOL_HISTORY
    bands = [((s <= r) & (s > r - w)) for w in POOL_WINDOWS]
    return jnp.asarray(np.stack(bands).astype(np.float32), dtype=BF16)


def _mix_router(x2, ret, u, pool_w_bf, pool_scale, w_out_bf, norm_w, rw_split, rb, seq, tile0):
    t = ret.shape[0]
    tm = TOKEN_TILE
    hist_blocks = tm // POOL_HISTORY
    n = t // tm
    mixed = lambda i: jnp.minimum(i, n - 1)
    routed = lambda i: jnp.maximum(i - 1, 0)
    tile = lambda width: pl.BlockSpec((tm, width), lambda i: (mixed(i), 0))
    late = lambda width: pl.BlockSpec((tm, width), lambda i: (routed(i), 0))
    const2 = lambda shape: pl.BlockSpec(shape, lambda i: (0, 0))
    const3 = lambda shape: pl.BlockSpec(shape, lambda i: (0, 0, 0))
    return pl.pallas_call(
        functools.partial(_mix_router_kernel, seq, n),
        grid=(n + 1,),
        in_specs=[
            pl.BlockSpec((tm, D_MODEL), lambda i: (mixed(i) + tile0, 0)), tile(D_RET), tile(D_POOL),
            pl.BlockSpec((POOL_HISTORY, D_POOL), lambda i: (jnp.maximum(mixed(i) * hist_blocks - 1, 0), 0)),
            const3((len(POOL_WINDOWS), POOL_BLOCK, POOL_BLOCK + POOL_HISTORY)),
            const3((len(POOL_WINDOWS), POOL_GROUP_DIM, POOL_GROUP_DIM)),
            const2((1, D_POOL)), const2((D_MODEL, D_MODEL)), const2((1, D_MODEL)),
            const2((D_MODEL, 2 * LANES)), const2((1, LANES)),
        ],
        out_specs=[tile(D_MODEL), tile(D_MODEL // 2), late(LANES), late(LANES), const2((1, LANES))],
        out_shape=[
            jax.ShapeDtypeStruct((t, D_MODEL), F32),
            jax.ShapeDtypeStruct((t, D_MODEL // 2), U32),
            jax.ShapeDtypeStruct((t, LANES), F32),
            jax.ShapeDtypeStruct((t, LANES), F32),
            jax.ShapeDtypeStruct((1, LANES), F32),
        ],
        scratch_shapes=[pltpu.VMEM((tm + POOL_HISTORY, D_POOL), BF16), pltpu.VMEM((tm, LANES), F32)],
        compiler_params=_params(("arbitrary",)),
        name="mix_router",
    )(x2, ret, u, u, _pool_bands(), pool_w_bf, pool_scale, w_out_bf, norm_w, rw_split, rb)


def _positions_kernel(code_ref, off_ref, tri_ref, pos_ref, carry_ref):
    @pl.when(pl.program_id(0) == 0)
    def _():
        carry_ref[...] = jnp.zeros_like(carry_ref)

    code = code_ref[...]
    tm = code.shape[0]
    sel = (code > 0).astype(BF16)
    carry = carry_ref[...]
    rank = _dot(tri_ref[...], sel) + (carry + off_ref[...])
    carry_ref[...] = carry + jnp.sum(sel.astype(F32), axis=0, keepdims=True)
    lane = lax.broadcasted_iota(jnp.int32, (tm, LANES), 1)
    pos = jnp.zeros((tm, LANES), F32)
    for k in range(TOP_K):
        pk = jnp.sum(jnp.where(code == float(k + 1), rank, 0.0), axis=-1, keepdims=True)
        pos = jnp.where(lane == k, pk, pos)
    pos_ref[...] = pos.T[0:POS_ROWS, :].astype(jnp.int32)


def _positions(code, offsets):
    t = code.shape[0]
    tm = POSITIONS_TILE
    tri = jnp.asarray(np.tril(np.ones((tm, tm), np.float32), -1), dtype=BF16)
    return pl.pallas_call(
        _positions_kernel,
        grid=(t // tm,),
        in_specs=[pl.BlockSpec((tm, LANES), lambda i: (i, 0)),
                  pl.BlockSpec((1, LANES), lambda i: (0, 0)),
                  pl.BlockSpec((tm, tm), lambda i: (0, 0))],
        out_specs=pl.BlockSpec((POS_ROWS, tm), lambda i: (0, i)),
        out_shape=jax.ShapeDtypeStruct((POS_ROWS, t), jnp.int32),
        scratch_shapes=[pltpu.VMEM((1, LANES), F32)],
        compiler_params=_params(("arbitrary",)),
        name="positions",
    )(code, offsets, tri)


def _sc_mesh():
    return plsc.VectorSubcoreMesh(core_axis_name="core", subcore_axis_name="subcore")


def _sc_workers():
    info = plsc.get_sparse_core_info()
    return info.num_cores, info.num_cores * info.num_subcores


def _sc_gather(src, idx):
    n = idx.shape[0]
    d = src.shape[1]
    w = SC_WINDOW // 2
    num_cores, workers = _sc_workers()
    per = n // workers
    pairs = per // (2 * w)
    assert per * workers == n and pairs * 2 * w == per

    @functools.partial(
        pl.kernel, out_type=jax.ShapeDtypeStruct((n, d), src.dtype), mesh=_sc_mesh(),
        scratch_types=[pltpu.VMEM((w,), jnp.int32), pltpu.VMEM((w,), jnp.int32),
                       pltpu.VMEM((w, d), src.dtype), pltpu.VMEM((w, d), src.dtype),
                       pltpu.SemaphoreType.DMA, pltpu.SemaphoreType.DMA,
                       pltpu.SemaphoreType.DMA, pltpu.SemaphoreType.DMA],
        name="sc_gather")
    def gather(src_hbm, idx_hbm, out_hbm, idx0, idx1, rows0, rows1, g0, g1, w0, w1):
        first = (lax.axis_index("subcore") * num_cores + lax.axis_index("core")) * per

        def start_gather(win, idx_v, rows_v, sem):
            pltpu.sync_copy(idx_hbm.at[pl.ds(first + win * w, w)], idx_v)
            pltpu.async_copy(src_hbm.at[idx_v], rows_v, sem)

        def wait_gather(idx_v, rows_v, sem):
            pltpu.make_async_copy(src_hbm.at[idx_v], rows_v, sem).wait()

        def start_write(win, rows_v, sem):
            pltpu.async_copy(rows_v, out_hbm.at[pl.ds(first + win * w, w)], sem)

        def wait_write(rows_v, sem):
            pltpu.make_async_copy(rows_v, out_hbm.at[pl.ds(first, w)], sem).wait()

        start_gather(0, idx0, rows0, g0)

        @pl.loop(0, pairs)
        def _(j):
            even = 2 * j

            @pl.when(j > 0)
            def _():
                wait_write(rows1, w1)

            start_gather(even + 1, idx1, rows1, g1)
            wait_gather(idx0, rows0, g0)
            start_write(even, rows0, w0)
            wait_write(rows0, w0)

            @pl.when(j + 1 < pairs)
            def _():
                start_gather(even + 2, idx0, rows0, g0)

            wait_gather(idx1, rows1, g1)
            start_write(even + 1, rows1, w1)

        wait_write(rows1, w1)

    return gather(src, idx)


def _sc_scatter(src, idx):
    t, d = src.shape
    w = SC_WINDOW
    num_cores, workers = _sc_workers()
    per = t // workers
    assert idx.shape[0] >= TOP_K and idx.shape[1] == t and per * workers == t and per % w == 0

    @functools.partial(
        pl.kernel, out_type=jax.ShapeDtypeStruct((TOP_K * t, d), src.dtype), mesh=_sc_mesh(),
        scratch_types=[pltpu.VMEM((TOP_K, w), jnp.int32), pltpu.VMEM((w, d), src.dtype),
                       pltpu.SemaphoreType.DMA],
        name="sc_scatter")
    def scatter(src_hbm, idx_hbm, out_hbm, idx_v, rows_v, sem):
        first = (lax.axis_index("subcore") * num_cores + lax.axis_index("core")) * per

        @pl.loop(0, per // w)
        def _(j):
            base = first + j * w
            for k in range(TOP_K):
                pltpu.sync_copy(idx_hbm.at[k, pl.ds(base, w)], idx_v.at[k])
            pltpu.sync_copy(src_hbm.at[pl.ds(base, w)], rows_v)
            copies = [pltpu.async_copy(rows_v, out_hbm.at[idx_v.at[k]], sem) for k in range(TOP_K)]
            for c in copies:
                c.wait()

    return scatter(src, idx)


def _experts_kernel(items_ref, xs_ref, wgu_hbm, bg_ref, bu_ref, wdn_hbm, bd_ref, perm_ref, y_ref,
                    wgu_ref, wdn_ref, wg_ref, wu_ref, wd_ref, sems):
    i = pl.program_id(0)
    tm = xs_ref.shape[0]

    def weight_copies():
        e = items_ref[I_FETCH, i]
        return (pltpu.make_async_copy(wgu_hbm.at[e], wgu_ref, sems.at[0]),
                pltpu.make_async_copy(wdn_hbm.at[e], wdn_ref, sems.at[1]))

    @pl.when(items_ref[I_START, i] == 1)
    def _():
        for copy in weight_copies():
            copy.start()

    @pl.when(items_ref[I_PREP, i] == 1)
    def _():
        for copy in weight_copies():
            copy.wait()
        ps = items_ref[I_PREP_SLOT, i]
        perm = perm_ref[...]
        pair = 2 * LANES
        for c in range(2 * D_MODEL // pair):
            sel = _dot(wgu_ref[:, c * pair:(c + 1) * pair].astype(BF16), perm).astype(BF16)
            wg_ref[ps, :, c * LANES:(c + 1) * LANES] = sel[:, :LANES]
            wu_ref[ps, :, c * LANES:(c + 1) * LANES] = sel[:, LANES:]
        wd_ref[ps] = wdn_ref[...].astype(BF16)

    def expert_mlp(r0, r1):
        n = r1 - r0
        s = items_ref[I_SLOT, i]
        x_lo, x_hi = _unpack_bf16_pair(xs_ref[r0:r1, :])
        x = jnp.concatenate([x_lo.astype(BF16), x_hi.astype(BF16)], axis=1)
        acts = []
        chunks = [slice(c * EXPERT_COL_CHUNK, (c + 1) * EXPERT_COL_CHUNK)
                  for c in range(D_MODEL // EXPERT_COL_CHUNK)]
        for cs in chunks:
            gate = jnp.minimum(_dot(x, wg_ref[s, :, cs]) + bg_ref[:, cs], SWIGLU_LIMIT)
            up = jnp.clip(_dot(x, wu_ref[s, :, cs]) + bu_ref[:, cs], -SWIGLU_LIMIT, SWIGLU_LIMIT)
            acts.append(((up + 1.0) * (gate * _sigmoid(SWIGLU_ALPHA * gate))).astype(BF16))
        rows = items_ref[I_TILE, i] * tm + r0 + lax.broadcasted_iota(jnp.int32, (n, 1), 0)
        mine = (rows >= items_ref[I_LO, i]) & (rows < items_ref[I_HI, i])

        half, quarter = D_MODEL // 2, D_MODEL // 4
        for h in range(2):
            hs = slice(h * half, (h + 1) * half)
            yh = bd_ref[:, hs]
            for cs, act in zip(chunks, acts):
                yh = yh + _dot(act, wd_ref[s, cs, hs])
            cols = slice(h * quarter, (h + 1) * quarter)
            packed = _pack_bf16_pair(yh[:, :quarter], yh[:, quarter:])
            y_ref[r0:r1, cols] = jnp.where(mine, packed, y_ref[r0:r1, cols])

    @pl.when(items_ref[I_FIRST, i] == 1)
    def _():
        y_ref[...] = jnp.zeros(y_ref.shape, U32)

    block = tm // ROW_BLOCKS
    for m in range(1, ROW_BLOCKS + 1):
        pl.when(items_ref[I_MODE, i] == m)(functools.partial(expert_mlp, 0, m * block))
    for m in range(1, ROW_BLOCKS):
        pl.when(items_ref[I_MODE, i] == ROW_BLOCKS + m)(functools.partial(expert_mlp, m * block, tm))


(I_TILE, I_EXPERT, I_LO, I_HI, I_FIRST, I_MODE, I_FETCH, I_START, I_PREP, I_PREP_SLOT, I_SLOT) = range(11)
ITEM_FIELDS = 16
ROW_BLOCKS = 4


def _plan_kernel(tm, n_tiles, cnt_ref, off_ref, items_ref):
    n = items_ref.shape[1]
    cnt_r = cnt_ref[...]
    sub = lax.broadcasted_iota(jnp.int32, (LANES, LANES), 0).astype(F32)
    lan = lax.broadcasted_iota(jnp.int32, (LANES, LANES), 1).astype(F32)
    big = float(4 * LANES)

    def col(row):
        return jnp.sum(jnp.where(lan == sub, row, 0.0), axis=1, keepdims=True)

    def row(column):
        return jnp.sum(jnp.where(lan == sub, column, 0.0), axis=0, keepdims=True)

    def prefix(r):
        return jnp.sum(jnp.where(lan <= sub, r, 0.0), axis=1, keepdims=True)

    cnt = col(cnt_r)
    ends = prefix(cnt_r)
    starts = ends - cnt
    off_ref[...] = row(starts)
    used = cnt > 0.0
    inv_tm = 1.0 / tm
    first_tile = jnp.floor(starts * inv_tm)
    last_tile = jnp.floor(jnp.maximum(ends - 1.0, 0.0) * inv_tm)
    per = jnp.where(used, last_tile - first_tile + 1.0, 0.0)
    item_end = prefix(row(per))
    item_start = item_end - per
    used_r = row(used.astype(F32))
    ordinal = prefix(used_r) - 1.0
    nxt = jnp.min(jnp.where((lan > sub) & (used_r > 0.0), lan, big), axis=1, keepdims=True)
    has_next = nxt < big
    e_col = sub[:, 0:1]
    nxt = jnp.where(has_next, nxt, e_col)
    e_first = jnp.min(jnp.where(used_r > 0.0, lan[0:1, :], big), axis=1, keepdims=True)
    e_last = jnp.max(jnp.where(used_r > 0.0, lan[0:1, :], -1.0), axis=1, keepdims=True)
    total = jnp.max(item_end, axis=0, keepdims=True)

    item = lax.broadcasted_iota(jnp.int32, (LANES, n), 1).astype(F32) - 1.0
    mine = ((item >= item_start) & (item < item_end)).astype(F32)
    pick = lambda column: jnp.sum(mine * column, axis=0, keepdims=True)
    item_r = item[0:1, :]
    valid = pick(jnp.ones_like(cnt))
    lead = item_r < 0.0
    past = item_r >= total
    new_tile = 1.0 - pick(((item == item_start) & (starts - first_tile * tm > 0.0)).astype(F32))

    def put(field, value):
        items_ref[field:field + 1, :] = value.astype(jnp.int32)

    tile = pick(first_tile - item_start) + item_r
    lo = pick(starts)
    hi = pick(ends)
    inv_block = ROW_BLOCKS * inv_tm
    first_block = jnp.floor(jnp.maximum(lo - tile * tm, 0.0) * inv_block)
    end_block = jnp.floor((jnp.minimum(hi - tile * tm, tm) + (tm / ROW_BLOCKS - 1.0)) * inv_block)
    mode = jnp.where(first_block == 0.0, end_block,
                     jnp.where(end_block == float(ROW_BLOCKS), ROW_BLOCKS + first_block, float(ROW_BLOCKS)))
    put(I_TILE, jnp.where(past, float(n_tiles - 1), jnp.where(lead, 0.0, tile)))
    put(I_EXPERT, jnp.where(past, e_last, jnp.where(lead, e_first, pick(e_col))))
    put(I_LO, lo)
    put(I_HI, hi)
    put(I_FIRST, valid * new_tile)
    put(I_MODE, valid * mode)
    put(I_FETCH, jnp.where(past, e_last, jnp.where(lead, e_first, pick(nxt))))
    put(I_START, jnp.where(lead, 1.0, pick(((item == item_start) & has_next).astype(F32))))
    put(I_PREP, jnp.where(lead, 1.0, pick(((item == item_end - 1.0) & has_next).astype(F32))))
    put(I_PREP_SLOT, pick(ordinal + 1.0 - 2.0 * jnp.floor((ordinal + 1.0) * 0.5)))
    put(I_SLOT, pick(ordinal - 2.0 * jnp.floor(ordinal * 0.5)))
    for field in range(I_SLOT + 1, ITEM_FIELDS):
        put(field, jnp.zeros_like(valid))


def _plan(counts, n_rows, tm):
    n_tiles = n_rows // tm
    n_items = n_tiles + N_EXPERTS
    width = -(-n_items // LANES) * LANES
    offsets, items = pl.pallas_call(
        functools.partial(_plan_kernel, float(tm), n_tiles),
        out_shape=[jax.ShapeDtypeStruct((1, LANES), F32),
                   jax.ShapeDtypeStruct((ITEM_FIELDS, width), jnp.int32)],
        name="plan",
    )(counts)
    return offsets, items, n_items


def _experts(xs, items, n_items, w_gate_up, bg, bu, w_down, bd):
    n_rows = xs.shape[0]
    tm = EXPERT_TILE
    half = D_MODEL // 2
    j = np.arange(2 * LANES)
    perm = np.zeros((2 * LANES, 2 * LANES), np.float32)
    perm[j, np.where(j % 2 == 0, j // 2, LANES + j // 2)] = 1.0
    row_tile = pl.BlockSpec((tm, half), lambda i, items: (items[I_TILE, i], 0))
    by_expert = lambda *shape: pl.BlockSpec((None,) + shape, lambda i, items: (items[I_EXPERT, i], 0, 0))
    in_hbm = pl.BlockSpec(memory_space=pl.ANY)
    return pl.pallas_call(
        _experts_kernel,
        grid_spec=pltpu.PrefetchScalarGridSpec(
            num_scalar_prefetch=1,
            grid=(n_items,),
            in_specs=[row_tile, in_hbm, by_expert(1, D_MODEL), by_expert(1, D_MODEL), in_hbm,
                      by_expert(1, D_MODEL),
                      pl.BlockSpec((2 * LANES, 2 * LANES), lambda i, items: (0, 0))],
            out_specs=row_tile,
            scratch_shapes=[pltpu.VMEM((D_MODEL, 2 * D_MODEL), F32), pltpu.VMEM((D_MODEL, D_MODEL), F32)]
            + [pltpu.VMEM((2, D_MODEL, D_MODEL), BF16)] * 3 + [pltpu.SemaphoreType.DMA((2,))],
        ),
        out_shape=jax.ShapeDtypeStruct((n_rows, half), U32),
        compiler_params=_params(("arbitrary",)),
        name="experts",
    )(items, xs, w_gate_up, bg, bu, w_down, bd, jnp.asarray(perm, dtype=BF16))


def _tail_kernel(h1_ref, yu_ref, gates_ref, p_ref, nple_ref, wg_ref, wp_ref, nfin_ref, *rest):
    out_ref = rest[-1]
    quarter = D_MODEL // 4
    for r in range(0, h1_ref.shape[0], TAIL_ROWS):
        rows = slice(r, r + TAIL_ROWS)
        gates = gates_ref[rows, :]
        lo = None
        hi = None
        for k in range(TOP_K):
            gk = gates[:, k:k + 1]
            yl, yh = _unpack_bf16_pair(yu_ref[k, rows, :])
            lo = gk * yl if lo is None else lo + gk * yl
            hi = gk * yh if hi is None else hi + gk * yh
        moe = jnp.concatenate([lo[:, :quarter], hi[:, :quarter], lo[:, quarter:], hi[:, quarter:]], axis=1)
        h2 = h1_ref[rows, :] + moe
        hn = _rms(h2, nple_ref[...]).astype(BF16)
        gate = _sigmoid(_dot(hn, wg_ref[...]))
        h3 = h2 + gate * _dot(p_ref[rows, :].astype(BF16), wp_ref[...])
        out_ref[rows, :] = _rms(h3, nfin_ref[...])


def _tail(h1, yu, gates, p2, norm_ple_w, ple_gate_bf, ple_proj_bf, final_norm_w, local0, tile0, result):
    t = yu.shape[1]
    tm = TAIL_TILE
    assert t % tm == 0
    half = D_MODEL // 2
    tile = lambda width: pl.BlockSpec((tm, width), lambda i: (i + local0, 0))
    shifted = lambda width: pl.BlockSpec((tm, width), lambda i: (i + tile0 + local0, 0))
    const2 = lambda shape: pl.BlockSpec(shape, lambda i: (0, 0))
    in_specs = [tile(D_MODEL), pl.BlockSpec((TOP_K, tm, half), lambda i: (0, i, 0)),
                tile(LANES), shifted(PLE_DIM), const2((1, D_MODEL)),
                const2((D_MODEL, D_MODEL)), const2((PLE_DIM, D_MODEL)), const2((1, D_MODEL))]
    args = [h1, yu, gates, p2, norm_ple_w, ple_gate_bf, ple_proj_bf, final_norm_w]
    aliases = {}
    if result is not None:
        in_specs.append(pl.BlockSpec(memory_space=pl.ANY))
        args.append(result)
        aliases = {len(args) - 1: 0}
    return pl.pallas_call(
        _tail_kernel,
        grid=(t // tm,),
        in_specs=in_specs,
        out_specs=shifted(D_MODEL),
        out_shape=jax.ShapeDtypeStruct((p2.shape[0], D_MODEL), F32),
        input_output_aliases=aliases,
        compiler_params=_params(("parallel",)),
        name="tail",
    )(*args)


def kernel(x, p, positions, w_in, w_out, ret_gn_w, pool_w, pool_scale, norm_mix_w, norm_moe_w, router_w, router_b, expert_w_gate_up, expert_b_gate_up, expert_w_down, expert_b_down, norm_ple_w, ple_gate_w, ple_proj_w, final_norm_w):
    batch, seq, d = x.shape
    depth = w_in.shape[0]
    assert depth == 1 and d == D_MODEL and seq % TOKEN_TILE == 0
    assert seq % (RET_CHUNK * RET_STEP_CHUNKS) == 0
    group_batches = GROUP_BATCHES if sum(GROUP_BATCHES) == batch else (batch,)
    t = batch * seq
    row = lambda a: a.reshape(1, -1).astype(F32)
    l = 0

    x2 = x.reshape(t, d)
    pos2 = positions.reshape(t, 1)
    p2 = p[l].reshape(t, PLE_DIM)
    w_in_bf = w_in[l].astype(BF16)
    rw = jnp.pad(router_w[l].astype(F32), ((0, 0), (0, LANES - N_EXPERTS)))
    rw_hi = rw.astype(BF16)
    rw_lo = (rw - rw_hi.astype(F32)).astype(BF16)
    rw_split = jnp.concatenate([rw_hi, rw_lo], axis=1)
    rb = jnp.pad(router_b[l].astype(F32), (0, LANES - N_EXPERTS), constant_values=NEG_BIG).reshape(1, LANES)
    bgu = expert_b_gate_up[l].reshape(N_EXPERTS, 1, D_MODEL, 2).astype(F32)
    bdn = expert_b_down[l].reshape(N_EXPERTS, 1, D_MODEL).astype(F32)
    pool_w_bf, w_out_bf = pool_w[l].astype(BF16), w_out[l].astype(BF16)
    ple_gate_bf, ple_proj_bf = ple_gate_w[l].astype(BF16), ple_proj_w[l].astype(BF16)

    out = None
    first_batch = 0
    for gi, nb in enumerate(group_batches):
        tg = nb * seq
        n_rows = tg * TOP_K
        assert n_rows % EXPERT_TILE == 0
        tile0 = first_batch * seq // TOKEN_TILE
        ret, u = _retention(x2, pos2, row(norm_mix_w[l]), w_in_bf, row(ret_gn_w[l]),
                            first_batch * seq // (RET_CHUNK * RET_STEP_CHUNKS), nb, seq)
        first_batch += nb
        h1, hn_packed, code, gates, counts = _mix_router(
            x2, ret, u, pool_w_bf, row(pool_scale[l]), w_out_bf, row(norm_moe_w[l]), rw_split, rb,
            seq, tile0)

        offsets, items, n_items = _plan(counts, n_rows, EXPERT_TILE)
        pos = _positions(code, offsets)
        xs = _sc_scatter(hn_packed, pos)
        y = _experts(xs, items, n_items, expert_w_gate_up[l], bgu[..., 0], bgu[..., 1],
                     expert_w_down[l], bdn)

        spans = LAST_GROUP_SPANS if gi == len(group_batches) - 1 else 1
        span = tg // spans
        for si in range(spans):
            idx = pos[:TOP_K, si * span:(si + 1) * span].reshape(TOP_K * span)
            yu = _sc_gather(y, idx).reshape(TOP_K, span, d // 2)
            out = _tail(h1, yu, gates, p2, row(norm_ple_w[l]), ple_gate_bf, ple_proj_bf,
                        row(final_norm_w), si * (span // TAIL_TILE), tile0 * TOKEN_TILE // TAIL_TILE, out)
    return out.reshape(batch, seq, d)
```

```python
import functools

import numpy as np
import jax
import jax.numpy as jnp
from jax import lax
from jax.experimental import pallas as pl
from jax.experimental.pallas import tpu as pltpu
from jax.experimental.pallas import tpu_sc as plsc

D_MODEL = 1024
D_RET = 512
D_POOL = 512
RET_HEADS = 8
RET_HEAD_DIM = 64
HEAD_PAIRS = RET_HEADS // 2
ROPE_BASE = 10000.0
POOL_WINDOWS = (2, 4, 8, 16)
POOL_GROUP_DIM = 128
POOL_HISTORY = 16
POOL_BLOCK = 128
D_IN_PROJ = 4 * D_RET + D_POOL
N_EXPERTS = 32
TOP_K = 4
SWIGLU_LIMIT = 7.0
SWIGLU_ALPHA = 1.702
PLE_DIM = 256
NORM_EPS = 1e-5
GN_EPS = 1e-5

LANES = 128
NEG_BIG = -1e30

TOKEN_TILE = 512
POSITIONS_TILE = 1024
TAIL_TILE = 1024
TAIL_ROWS = 256
POS_ROWS = 8
RET_CHUNK = 256
RET_STEP_CHUNKS = 4
EXPERT_TILE = 512
EXPERT_COL_CHUNK = 512
SC_WINDOW = 128
GROUP_BATCHES = (5, 3)
LAST_GROUP_SPANS = 3
VMEM_LIMIT = 56 * 1024 * 1024

F32 = jnp.float32
BF16 = jnp.bfloat16
U32 = jnp.uint32


def _params(semantics):
    return pltpu.CompilerParams(dimension_semantics=semantics, vmem_limit_bytes=VMEM_LIMIT)


def _dot(a, b):
    return jnp.dot(a, b, preferred_element_type=F32)


def _dot_nt(a, b):
    return lax.dot_general(a, b, (((1,), (1,)), ((), ())), preferred_element_type=F32)


def _rms(x, w):
    ms = jnp.mean(x * x, axis=-1, keepdims=True)
    return x * lax.rsqrt(ms + NORM_EPS) * w


def _sigmoid(z):
    return 1.0 / (1.0 + jnp.exp(-z))


def _pack_bf16_pair(lo, hi):
    lo_bits = pltpu.bitcast(lo.astype(BF16).astype(F32), U32) >> 16
    hi_bits = pltpu.bitcast(hi.astype(BF16).astype(F32), U32) & jnp.uint32(0xFFFF0000)
    return lo_bits | hi_bits


def _unpack_bf16_pair(packed):
    lo = pltpu.bitcast(packed << 16, F32)
    hi = pltpu.bitcast(packed & jnp.uint32(0xFFFF0000), F32)
    return lo, hi


def _project_rows(x, pos, nw_ref, w_ref, freq_ref, qkvg_ref, u_ref, rows):
    hn = _rms(x, nw_ref[...]).astype(BF16)
    ang = pos.astype(F32) * freq_ref[0:1, :]
    cos = jnp.cos(ang)
    sin = jnp.sin(ang)
    slab = 2 * LANES
    cos_t = jnp.concatenate([cos, cos], axis=1)
    sin_up = jnp.concatenate([sin * freq_ref[1:2, :]] * 2, axis=1)
    sin_dn = jnp.concatenate([sin * freq_ref[2:3, :]] * 2, axis=1)
    half = RET_HEAD_DIM // 2

    for s in range(2 * D_RET // slab):
        sl = slice(s * slab, (s + 1) * slab)
        v = _dot(hn, w_ref[:, sl])
        v = v * cos_t + pltpu.roll(v, slab - half, 1) * sin_up + pltpu.roll(v, half, 1) * sin_dn
        if s >= D_RET // slab:
            v = v * (RET_HEAD_DIM ** -0.5)
        qkvg_ref[rows, sl] = v.astype(BF16)
    for s in range(2 * D_RET // 512, 4 * D_RET // 512):
        sl = slice(s * 512, (s + 1) * 512)
        qkvg_ref[rows, sl] = _dot(hn, w_ref[:, sl]).astype(BF16)
    u_ref[rows, :] = _dot(hn, w_ref[:, 4 * D_RET:]).astype(BF16)


def _rope_table():
    j = np.arange(LANES)
    half = RET_HEAD_DIM // 2
    inv_freq = ROPE_BASE ** (-(np.arange(half, dtype=np.float32)) / half)
    freq = np.zeros((8, LANES), np.float32)
    freq[0] = inv_freq[j % half]
    freq[1] = np.where(j % RET_HEAD_DIM < half, -1.0, 0.0)
    freq[2] = np.where(j % RET_HEAD_DIM >= half, 1.0, 0.0)
    return jnp.asarray(freq)


def _retention_kernel(x_ref, pos_ref, nw_ref, w_ref, freq_ref, dec_ref, xi_ref, zeta_ref, cd_ref, bd_ref,
                      m64_ref, eye_ref, gnw_ref, out_ref, u_ref, state_ref, qkvg_ref):
    @pl.when(pl.program_id(1) == 0)
    def _():
        state_ref[...] = jnp.zeros_like(state_ref)

    c = dec_ref.shape[1]
    n_chunks = x_ref.shape[0] // c
    lane = lax.broadcasted_iota(jnp.int32, (1, LANES), 1)
    m64 = m64_ref[...]
    bd = bd_ref[...]
    eye = eye_ref[...]

    def group_mean(v):
        hi = v.astype(BF16)
        lo = (v - hi.astype(F32)).astype(BF16)
        return _dot(jnp.concatenate([hi, lo], axis=1), m64)

    for ci in range(n_chunks):
        rows = slice(ci * c, (ci + 1) * c)
        _project_rows(x_ref[rows, :], pos_ref[rows, :], nw_ref, w_ref, freq_ref, qkvg_ref, u_ref, rows)
        for p in range(HEAD_PAIRS):
            sl = slice(p * LANES, (p + 1) * LANES)
            qp = qkvg_ref[rows, sl]
            kp = qkvg_ref[rows, D_RET + p * LANES:D_RET + (p + 1) * LANES]
            vp = qkvg_ref[rows, 2 * D_RET + p * LANES:2 * D_RET + (p + 1) * LANES]
            y = None
            for hh in range(2):
                in_head = (lane >= RET_HEAD_DIM) == bool(hh)
                qm = jnp.where(in_head, qp, jnp.zeros_like(qp))
                vm = jnp.where(in_head, vp, jnp.zeros_like(vp))
                scores = _dot_nt(qm, kp) * dec_ref[2 * p + hh]
                part = _dot(scores.astype(BF16), vm)
                y = part if y is None else y + part
            st = state_ref[p]
            y = y + _dot((qp.astype(F32) * xi_ref[p]).astype(BF16), st.astype(BF16))
            kz = (kp.astype(F32) * zeta_ref[p]).astype(BF16)
            kz_t = _dot_nt(eye, kz).astype(BF16)
            state_ref[p] = cd_ref[p] * st + _dot(kz_t, vp) * bd
            mu = group_mean(y)
            var = group_mean(y * y) - mu * mu
            yn = (y - mu) * lax.rsqrt(var + GN_EPS) * gnw_ref[:, sl]
            g = qkvg_ref[rows, 3 * D_RET + p * LANES:3 * D_RET + (p + 1) * LANES].astype(F32)
            out_ref[rows, sl] = (yn * g * _sigmoid(g)).astype(BF16)


def _retention_tables(c):
    h = np.arange(RET_HEADS, dtype=np.float64)
    log_gamma = np.log1p(-np.power(2.0, -5.0 - h))
    idx = np.arange(c, dtype=np.float64)
    rel = idx[:, None] - idx[None, :]
    dec = np.where(rel >= 0, np.exp(np.where(rel >= 0, rel, 0.0)[None] * log_gamma[:, None, None]), 0.0)
    lane_head = np.arange(LANES) // RET_HEAD_DIM
    xi = np.zeros((HEAD_PAIRS, c, LANES))
    zeta = np.zeros((HEAD_PAIRS, c, LANES))
    cd = np.zeros((HEAD_PAIRS, LANES, LANES))
    same = lane_head[:, None] == lane_head[None, :]
    for p in range(HEAD_PAIRS):
        lg = log_gamma[2 * p + lane_head]
        xi[p] = np.exp((idx + 1.0)[:, None] * lg[None, :])
        zeta[p] = np.exp((c - 1 - idx)[:, None] * lg[None, :])
        cd[p] = np.where(same, np.exp(c * lg)[:, None], 0.0)
    bd = same.astype(np.float32)
    m64 = np.concatenate([same, same], axis=0).astype(np.float32) / RET_HEAD_DIM
    f = lambda a: jnp.asarray(a, dtype=F32)
    return (f(dec), f(xi), f(zeta), f(cd), f(bd), jnp.asarray(m64, dtype=BF16),
            jnp.asarray(np.eye(LANES, dtype=np.float32), dtype=BF16))


def _retention(x2, pos2, norm_w, w_in_bf, gn_w, block0, batch, seq):
    c = RET_CHUNK
    rows = RET_STEP_CHUNKS * c
    n = seq // rows
    t = batch * seq
    dec, xi, zeta, cd, bd, m64, eye = _retention_tables(c)
    const3 = lambda shape: pl.BlockSpec(shape, lambda b, i: (0, 0, 0))
    const2 = lambda shape: pl.BlockSpec(shape, lambda b, i: (0, 0))
    out_tile = pl.BlockSpec((rows, D_RET), lambda b, i: (b * n + i, 0))
    return pl.pallas_call(
        _retention_kernel,
        grid=(batch, n),
        in_specs=[pl.BlockSpec((rows, D_MODEL), lambda b, i: (block0 + b * n + i, 0)),
                  pl.BlockSpec((rows, 1), lambda b, i: (block0 + b * n + i, 0)),
                  const2((1, D_MODEL)), const2((D_MODEL, D_IN_PROJ)), const2((8, LANES)),
                  const3((RET_HEADS, c, c)), const3((HEAD_PAIRS, c, LANES)),
                  const3((HEAD_PAIRS, c, LANES)), const3((HEAD_PAIRS, LANES, LANES)),
                  const2((LANES, LANES)), const2((2 * LANES, LANES)), const2((LANES, LANES)),
                  const2((1, D_RET))],
        out_specs=[out_tile, out_tile],
        out_shape=[jax.ShapeDtypeStruct((t, D_RET), BF16), jax.ShapeDtypeStruct((t, D_POOL), BF16)],
        scratch_shapes=[pltpu.VMEM((HEAD_PAIRS, LANES, LANES), F32), pltpu.VMEM((rows, 4 * D_RET), BF16)],
        compiler_params=_params(("arbitrary", "arbitrary")),
        name="retention",
    )(x2, pos2, norm_w, w_in_bf, _rope_table(), dec, xi, zeta, cd, bd, m64, eye, gn_w)


def _mix_router_kernel(seq, n_tiles, x_ref, ret_ref, u_ref, uprev_ref, band_ref, pw_ref, ps_ref, wout_ref,
                       nw_ref, rw_ref, rb_ref,
                       h1_ref, hnp_ref, code_ref, gates_ref, cnt_ref, uext_ref, logits_ref):
    i = pl.program_id(0)
    tm = x_ref.shape[0]
    tile = jnp.minimum(i, n_tiles - 1)
    t0 = lax.rem(tile * tm, seq)

    @pl.when(i == 0)
    def _():
        cnt_ref[...] = jnp.zeros_like(cnt_ref)
        logits_ref[...] = jnp.zeros_like(logits_ref)

    lane = lax.broadcasted_iota(jnp.int32, (tm, LANES), 1).astype(F32)
    code = jnp.zeros((tm, LANES), F32)
    vals = []
    work = logits_ref[...]
    for k in range(TOP_K):
        m = jnp.max(work, axis=-1, keepdims=True)
        idx = jnp.min(jnp.where(work == m, lane, float(LANES)), axis=-1, keepdims=True)
        chosen = lane == idx
        code = jnp.where(chosen, float(k + 1), code)
        work = jnp.where(chosen, -jnp.inf, work)
        vals.append(m)
    exps = [jnp.exp(v - vals[0]) for v in vals]
    denom = exps[0] + exps[1] + exps[2] + exps[3]
    gates = jnp.zeros((tm, LANES), F32)
    for k in range(TOP_K):
        gates = jnp.where(lane == float(k), exps[k] / denom, gates)
    code_ref[...] = code
    gates_ref[...] = gates
    routed = jnp.where(i > 0, 1.0, 0.0)
    cnt_ref[...] += routed * jnp.sum((code > 0).astype(F32), axis=0, keepdims=True)

    prev = uprev_ref[...]
    uext_ref[0:POOL_HISTORY, :] = jnp.where(t0 == 0, jnp.zeros_like(prev), prev)
    uext_ref[POOL_HISTORY:, :] = u_ref[...]

    row = lax.broadcasted_iota(jnp.int32, (tm, 1), 0)
    t_seq = (t0 + row + 1).astype(F32)
    mixed = []
    for gi, w in enumerate(POOL_WINDOWS):
        sl = slice(gi * POOL_GROUP_DIM, (gi + 1) * POOL_GROUP_DIM)
        wsum = jnp.concatenate(
            [_dot(band_ref[gi], uext_ref[r0:r0 + POOL_BLOCK + POOL_HISTORY, sl])
             for r0 in range(0, tm, POOL_BLOCK)], axis=0)
        count = jnp.minimum(t_seq, float(w))
        pooled = wsum / count - u_ref[:, sl].astype(F32)
        mixed.append(_dot(pooled.astype(BF16), pw_ref[gi]))
    pool = (jnp.concatenate(mixed, axis=1) * ps_ref[...]).astype(BF16)

    h1 = (x_ref[...] + _dot(ret_ref[...], wout_ref[0:D_RET, :])
          + _dot(pool, wout_ref[D_RET:, :]))
    h1_ref[...] = h1
    hn = _rms(h1, nw_ref[...])
    hn_hi = hn.astype(BF16)
    half = D_MODEL // 2
    hnp_ref[...] = _pack_bf16_pair(hn[:, :half], hn[:, half:])

    hn_lo = (hn - hn_hi.astype(F32)).astype(BF16)
    both = _dot(hn_hi, rw_ref[...])
    logits_ref[...] = (both[:, :LANES] + both[:, LANES:] + _dot(hn_lo, rw_ref[:, :LANES])
                       + rb_ref[...])


def _pool_bands():
    r = np.arange(POOL_BLOCK)[:, None]
    s = np.arange(POOL_BLOCK + POOL_HISTORY)[None, :] - POOL_HISTORY
    bands = [((s <= r) & (s > r - w)) for w in POOL_WINDOWS]
    return jnp.asarray(np.stack(bands).astype(np.float32), dtype=BF16)


def _mix_router(x2, ret, u, pool_w_bf, pool_scale, w_out_bf, norm_w, rw_split, rb, seq, tile0):
    t = ret.shape[0]
    tm = TOKEN_TILE
    hist_blocks = tm // POOL_HISTORY
    n = t // tm
    mixed = lambda i: jnp.minimum(i, n - 1)
    routed = lambda i: jnp.maximum(i - 1, 0)
    tile = lambda width: pl.BlockSpec((tm, width), lambda i: (mixed(i), 0))
    late = lambda width: pl.BlockSpec((tm, width), lambda i: (routed(i), 0))
    const2 = lambda shape: pl.BlockSpec(shape, lambda i: (0, 0))
    const3 = lambda shape: pl.BlockSpec(shape, lambda i: (0, 0, 0))
    return pl.pallas_call(
        functools.partial(_mix_router_kernel, seq, n),
        grid=(n + 1,),
        in_specs=[
            pl.BlockSpec((tm, D_MODEL), lambda i: (mixed(i) + tile0, 0)), tile(D_RET), tile(D_POOL),
            pl.BlockSpec((POOL_HISTORY, D_POOL), lambda i: (jnp.maximum(mixed(i) * hist_blocks - 1, 0), 0)),
            const3((len(POOL_WINDOWS), POOL_BLOCK, POOL_BLOCK + POOL_HISTORY)),
            const3((len(POOL_WINDOWS), POOL_GROUP_DIM, POOL_GROUP_DIM)),
            const2((1, D_POOL)), const2((D_MODEL, D_MODEL)), const2((1, D_MODEL)),
            const2((D_MODEL, 2 * LANES)), const2((1, LANES)),
        ],
        out_specs=[tile(D_MODEL), tile(D_MODEL // 2), late(LANES), late(LANES), const2((1, LANES))],
        out_shape=[
            jax.ShapeDtypeStruct((t, D_MODEL), F32),
            jax.ShapeDtypeStruct((t, D_MODEL // 2), U32),
            jax.ShapeDtypeStruct((t, LANES), F32),
            jax.ShapeDtypeStruct((t, LANES), F32),
            jax.ShapeDtypeStruct((1, LANES), F32),
        ],
        scratch_shapes=[pltpu.VMEM((tm + POOL_HISTORY, D_POOL), BF16), pltpu.VMEM((tm, LANES), F32)],
        compiler_params=_params(("arbitrary",)),
        name="mix_router",
    )(x2, ret, u, u, _pool_bands(), pool_w_bf, pool_scale, w_out_bf, norm_w, rw_split, rb)


def _positions_kernel(code_ref, off_ref, tri_ref, pos_ref, carry_ref):
    @pl.when(pl.program_id(0) == 0)
    def _():
        carry_ref[...] = jnp.zeros_like(carry_ref)

    code = code_ref[...]
    tm = code.shape[0]
    sel = (code > 0).astype(BF16)
    carry = carry_ref[...]
    rank = _dot(tri_ref[...], sel) + (carry + off_ref[...])
    carry_ref[...] = carry + jnp.sum(sel.astype(F32), axis=0, keepdims=True)
    lane = lax.broadcasted_iota(jnp.int32, (tm, LANES), 1)
    pos = jnp.zeros((tm, LANES), F32)
    for k in range(TOP_K):
        pk = jnp.sum(jnp.where(code == float(k + 1), rank, 0.0), axis=-1, keepdims=True)
        pos = jnp.where(lane == k, pk, pos)
    pos_ref[...] = pos.T[0:POS_ROWS, :].astype(jnp.int32)


def _positions(code, offsets):
    t = code.shape[0]
    tm = POSITIONS_TILE
    tri = jnp.asarray(np.tril(np.ones((tm, tm), np.float32), -1), dtype=BF16)
    return pl.pallas_call(
        _positions_kernel,
        grid=(t // tm,),
        in_specs=[pl.BlockSpec((tm, LANES), lambda i: (i, 0)),
                  pl.BlockSpec((1, LANES), lambda i: (0, 0)),
                  pl.BlockSpec((tm, tm), lambda i: (0, 0))],
        out_specs=pl.BlockSpec((POS_ROWS, tm), lambda i: (0, i)),
        out_shape=jax.ShapeDtypeStruct((POS_ROWS, t), jnp.int32),
        scratch_shapes=[pltpu.VMEM((1, LANES), F32)],
        compiler_params=_params(("arbitrary",)),
        name="positions",
    )(code, offsets, tri)


def _sc_mesh():
    return plsc.VectorSubcoreMesh(core_axis_name="core", subcore_axis_name="subcore")


def _sc_workers():
    info = plsc.get_sparse_core_info()
    return info.num_cores, info.num_cores * info.num_subcores


def _sc_gather(src, idx):
    n = idx.shape[0]
    d = src.shape[1]
    w = SC_WINDOW // 2
    num_cores, workers = _sc_workers()
    per = n // workers
    pairs = per // (2 * w)
    assert per * workers == n and pairs * 2 * w == per

    @functools.partial(
        pl.kernel, out_type=jax.ShapeDtypeStruct((n, d), src.dtype), mesh=_sc_mesh(),
        scratch_types=[pltpu.VMEM((w,), jnp.int32), pltpu.VMEM((w,), jnp.int32),
                       pltpu.VMEM((w, d), src.dtype), pltpu.VMEM((w, d), src.dtype),
                       pltpu.SemaphoreType.DMA, pltpu.SemaphoreType.DMA,
                       pltpu.SemaphoreType.DMA, pltpu.SemaphoreType.DMA],
        name="sc_gather")
    def gather(src_hbm, idx_hbm, out_hbm, idx0, idx1, rows0, rows1, g0, g1, w0, w1):
        first = (lax.axis_index("subcore") * num_cores + lax.axis_index("core")) * per

        def start_gather(win, idx_v, rows_v, sem):
            pltpu.sync_copy(idx_hbm.at[pl.ds(first + win * w, w)], idx_v)
            pltpu.async_copy(src_hbm.at[idx_v], rows_v, sem)

        def wait_gather(idx_v, rows_v, sem):
            pltpu.make_async_copy(src_hbm.at[idx_v], rows_v, sem).wait()

        def start_write(win, rows_v, sem):
            pltpu.async_copy(rows_v, out_hbm.at[pl.ds(first + win * w, w)], sem)

        def wait_write(rows_v, sem):
            pltpu.make_async_copy(rows_v, out_hbm.at[pl.ds(first, w)], sem).wait()

        start_gather(0, idx0, rows0, g0)

        @pl.loop(0, pairs)
        def _(j):
            even = 2 * j

            @pl.when(j > 0)
            def _():
                wait_write(rows1, w1)

            start_gather(even + 1, idx1, rows1, g1)
            wait_gather(idx0, rows0, g0)
            start_write(even, rows0, w0)
            wait_write(rows0, w0)

            @pl.when(j + 1 < pairs)
            def _():
                start_gather(even + 2, idx0, rows0, g0)

            wait_gather(idx1, rows1, g1)
            start_write(even + 1, rows1, w1)

        wait_write(rows1, w1)

    return gather(src, idx)


def _sc_scatter(src, idx):
    t, d = src.shape
    w = SC_WINDOW
    num_cores, workers = _sc_workers()
    per = t // workers
    assert idx.shape[0] >= TOP_K and idx.shape[1] == t and per * workers == t and per % w == 0

    @functools.partial(
        pl.kernel, out_type=jax.ShapeDtypeStruct((TOP_K * t, d), src.dtype), mesh=_sc_mesh(),
        scratch_types=[pltpu.VMEM((TOP_K, w), jnp.int32), pltpu.VMEM((w, d), src.dtype),
                       pltpu.SemaphoreType.DMA],
        name="sc_scatter")
    def scatter(src_hbm, idx_hbm, out_hbm, idx_v, rows_v, sem):
        first = (lax.axis_index("subcore") * num_cores + lax.axis_index("core")) * per

        @pl.loop(0, per // w)
        def _(j):
            base = first + j * w
            for k in range(TOP_K):
                pltpu.sync_copy(idx_hbm.at[k, pl.ds(base, w)], idx_v.at[k])
            pltpu.sync_copy(src_hbm.at[pl.ds(base, w)], rows_v)
            copies = [pltpu.async_copy(rows_v, out_hbm.at[idx_v.at[k]], sem) for k in range(TOP_K)]
            for c in copies:
                c.wait()

    return scatter(src, idx)


def _experts_kernel(items_ref, xs_ref, wgu_hbm, bg_ref, bu_ref, wdn_hbm, bd_ref, perm_ref, y_ref,
                    wgu_ref, wdn_ref, wg_ref, wu_ref, wd_ref, sems):
    i = pl.program_id(0)
    tm = xs_ref.shape[0]

    def weight_copies():
        e = items_ref[I_FETCH, i]
        return (pltpu.make_async_copy(wgu_hbm.at[e], wgu_ref, sems.at[0]),
                pltpu.make_async_copy(wdn_hbm.at[e], wdn_ref, sems.at[1]))

    @pl.when(items_ref[I_START, i] == 1)
    def _():
        for copy in weight_copies():
            copy.start()

    @pl.when(items_ref[I_PREP, i] == 1)
    def _():
        for copy in weight_copies():
            copy.wait()
        ps = items_ref[I_PREP_SLOT, i]
        perm = perm_ref[...]
        pair = 2 * LANES
        for c in range(2 * D_MODEL // pair):
            sel = _dot(wgu_ref[:, c * pair:(c + 1) * pair].astype(BF16), perm).astype(BF16)
            wg_ref[ps, :, c * LANES:(c + 1) * LANES] = sel[:, :LANES]
            wu_ref[ps, :, c * LANES:(c + 1) * LANES] = sel[:, LANES:]
        wd_ref[ps] = wdn_ref[...].astype(BF16)

    def expert_mlp(r0, r1):
        n = r1 - r0
        s = items_ref[I_SLOT, i]
        x_lo, x_hi = _unpack_bf16_pair(xs_ref[r0:r1, :])
        x = jnp.concatenate([x_lo.astype(BF16), x_hi.astype(BF16)], axis=1)
        acts = []
        chunks = [slice(c * EXPERT_COL_CHUNK, (c + 1) * EXPERT_COL_CHUNK)
                  for c in range(D_MODEL // EXPERT_COL_CHUNK)]
        for cs in chunks:
            gate = jnp.minimum(_dot(x, wg_ref[s, :, cs]) + bg_ref[:, cs], SWIGLU_LIMIT)
            up = jnp.clip(_dot(x, wu_ref[s, :, cs]) + bu_ref[:, cs], -SWIGLU_LIMIT, SWIGLU_LIMIT)
            acts.append(((up + 1.0) * (gate * _sigmoid(SWIGLU_ALPHA * gate))).astype(BF16))
        rows = items_ref[I_TILE, i] * tm + r0 + lax.broadcasted_iota(jnp.int32, (n, 1), 0)
        mine = (rows >= items_ref[I_LO, i]) & (rows < items_ref[I_HI, i])

        half, quarter = D_MODEL // 2, D_MODEL // 4
        for h in range(2):
            hs = slice(h * half, (h + 1) * half)
            yh = bd_ref[:, hs]
            for cs, act in zip(chunks, acts):
                yh = yh + _dot(act, wd_ref[s, cs, hs])
            cols = slice(h * quarter, (h + 1) * quarter)
            packed = _pack_bf16_pair(yh[:, :quarter], yh[:, quarter:])
            y_ref[r0:r1, cols] = jnp.where(mine, packed, y_ref[r0:r1, cols])

    @pl.when(items_ref[I_FIRST, i] == 1)
    def _():
        y_ref[...] = jnp.zeros(y_ref.shape, U32)

    block = tm // ROW_BLOCKS
    for m in range(1, ROW_BLOCKS + 1):
        pl.when(items_ref[I_MODE, i] == m)(functools.partial(expert_mlp, 0, m * block))
    for m in range(1, ROW_BLOCKS):
        pl.when(items_ref[I_MODE, i] == ROW_BLOCKS + m)(functools.partial(expert_mlp, m * block, tm))


(I_TILE, I_EXPERT, I_LO, I_HI, I_FIRST, I_MODE, I_FETCH, I_START, I_PREP, I_PREP_SLOT, I_SLOT) = range(11)
ITEM_FIELDS = 16
ROW_BLOCKS = 4


def _plan_kernel(tm, n_tiles, cnt_ref, off_ref, items_ref):
    n = items_ref.shape[1]
    cnt_r = cnt_ref[...]
    sub = lax.broadcasted_iota(jnp.int32, (LANES, LANES), 0).astype(F32)
    lan = lax.broadcasted_iota(jnp.int32, (LANES, LANES), 1).astype(F32)
    big = float(4 * LANES)

    def col(row):
        return jnp.sum(jnp.where(lan == sub, row, 0.0), axis=1, keepdims=True)

    def row(column):
        return jnp.sum(jnp.where(lan == sub, column, 0.0), axis=0, keepdims=True)

    def prefix(r):
        return jnp.sum(jnp.where(lan <= sub, r, 0.0), axis=1, keepdims=True)

    cnt = col(cnt_r)
    ends = prefix(cnt_r)
    starts = ends - cnt
    off_ref[...] = row(starts)
    used = cnt > 0.0
    inv_tm = 1.0 / tm
    first_tile = jnp.floor(starts * inv_tm)
    last_tile = jnp.floor(jnp.maximum(ends - 1.0, 0.0) * inv_tm)
    per = jnp.where(used, last_tile - first_tile + 1.0, 0.0)
    item_end = prefix(row(per))
    item_start = item_end - per
    used_r = row(used.astype(F32))
    ordinal = prefix(used_r) - 1.0
    nxt = jnp.min(jnp.where((lan > sub) & (used_r > 0.0), lan, big), axis=1, keepdims=True)
    has_next = nxt < big
    e_col = sub[:, 0:1]
    nxt = jnp.where(has_next, nxt, e_col)
    e_first = jnp.min(jnp.where(used_r > 0.0, lan[0:1, :], big), axis=1, keepdims=True)
    e_last = jnp.max(jnp.where(used_r > 0.0, lan[0:1, :], -1.0), axis=1, keepdims=True)
    total = jnp.max(item_end, axis=0, keepdims=True)

    item = lax.broadcasted_iota(jnp.int32, (LANES, n), 1).astype(F32) - 1.0
    mine = ((item >= item_start) & (item < item_end)).astype(F32)
    pick = lambda column: jnp.sum(mine * column, axis=0, keepdims=True)
    item_r = item[0:1, :]
    valid = pick(jnp.ones_like(cnt))
    lead = item_r < 0.0
    past = item_r >= total
    new_tile = 1.0 - pick(((item == item_start) & (starts - first_tile * tm > 0.0)).astype(F32))

    def put(field, value):
        items_ref[field:field + 1, :] = value.astype(jnp.int32)

    tile = pick(first_tile - item_start) + item_r
    lo = pick(starts)
    hi = pick(ends)
    inv_block = ROW_BLOCKS * inv_tm
    first_block = jnp.floor(jnp.maximum(lo - tile * tm, 0.0) * inv_block)
    end_block = jnp.floor((jnp.minimum(hi - tile * tm, tm) + (tm / ROW_BLOCKS - 1.0)) * inv_block)
    mode = jnp.where(first_block == 0.0, end_block,
                     jnp.where(end_block == float(ROW_BLOCKS), ROW_BLOCKS + first_block, float(ROW_BLOCKS)))
    put(I_TILE, jnp.where(past, float(n_tiles - 1), jnp.where(lead, 0.0, tile)))
    put(I_EXPERT, jnp.where(past, e_last, jnp.where(lead, e_first, pick(e_col))))
    put(I_LO, lo)
    put(I_HI, hi)
    put(I_FIRST, valid * new_tile)
    put(I_MODE, valid * mode)
    put(I_FETCH, jnp.where(past, e_last, jnp.where(lead, e_first, pick(nxt))))
    put(I_START, jnp.where(lead, 1.0, pick(((item == item_start) & has_next).astype(F32))))
    put(I_PREP, jnp.where(lead, 1.0, pick(((item == item_end - 1.0) & has_next).astype(F32))))
    put(I_PREP_SLOT, pick(ordinal + 1.0 - 2.0 * jnp.floor((ordinal + 1.0) * 0.5)))
    put(I_SLOT, pick(ordinal - 2.0 * jnp.floor(ordinal * 0.5)))
    for field in range(I_SLOT + 1, ITEM_FIELDS):
        put(field, jnp.zeros_like(valid))


def _plan(counts, n_rows, tm):
    n_tiles = n_rows // tm
    n_items = n_tiles + N_EXPERTS
    width = -(-n_items // LANES) * LANES
    offsets, items = pl.pallas_call(
        functools.partial(_plan_kernel, float(tm), n_tiles),
        out_shape=[jax.ShapeDtypeStruct((1, LANES), F32),
                   jax.ShapeDtypeStruct((ITEM_FIELDS, width), jnp.int32)],
        name="plan",
    )(counts)
    return offsets, items, n_items


def _experts(xs, items, n_items, w_gate_up, bg, bu, w_down, bd):
    n_rows = xs.shape[0]
    tm = EXPERT_TILE
    half = D_MODEL // 2
    j = np.arange(2 * LANES)
    perm = np.zeros((2 * LANES, 2 * LANES), np.float32)
    perm[j, np.where(j % 2 == 0, j // 2, LANES + j // 2)] = 1.0
    row_tile = pl.BlockSpec((tm, half), lambda i, items: (items[I_TILE, i], 0))
    by_expert = lambda *shape: pl.BlockSpec((None,) + shape, lambda i, items: (items[I_EXPERT, i], 0, 0))
    in_hbm = pl.BlockSpec(memory_space=pl.ANY)
    return pl.pallas_call(
        _experts_kernel,
        grid_spec=pltpu.PrefetchScalarGridSpec(
            num_scalar_prefetch=1,
            grid=(n_items,),
            in_specs=[row_tile, in_hbm, by_expert(1, D_MODEL), by_expert(1, D_MODEL), in_hbm,
                      by_expert(1, D_MODEL),
                      pl.BlockSpec((2 * LANES, 2 * LANES), lambda i, items: (0, 0))],
            out_specs=row_tile,
            scratch_shapes=[pltpu.VMEM((D_MODEL, 2 * D_MODEL), F32), pltpu.VMEM((D_MODEL, D_MODEL), F32)]
            + [pltpu.VMEM((2, D_MODEL, D_MODEL), BF16)] * 3 + [pltpu.SemaphoreType.DMA((2,))],
        ),
        out_shape=jax.ShapeDtypeStruct((n_rows, half), U32),
        compiler_params=_params(("arbitrary",)),
        name="experts",
    )(items, xs, w_gate_up, bg, bu, w_down, bd, jnp.asarray(perm, dtype=BF16))


def _tail_kernel(h1_ref, yu_ref, gates_ref, p_ref, nple_ref, wg_ref, wp_ref, nfin_ref, *rest):
    out_ref = rest[-1]
    quarter = D_MODEL // 4
    for r in range(0, h1_ref.shape[0], TAIL_ROWS):
        rows = slice(r, r + TAIL_ROWS)
        gates = gates_ref[rows, :]
        lo = None
        hi = None
        for k in range(TOP_K):
            gk = gates[:, k:k + 1]
            yl, yh = _unpack_bf16_pair(yu_ref[k, rows, :])
            lo = gk * yl if lo is None else lo + gk * yl
            hi = gk * yh if hi is None else hi + gk * yh
        moe = jnp.concatenate([lo[:, :quarter], hi[:, :quarter], lo[:, quarter:], hi[:, quarter:]], axis=1)
        h2 = h1_ref[rows, :] + moe
        hn = _rms(h2, nple_ref[...]).astype(BF16)
        gate = _sigmoid(_dot(hn, wg_ref[...]))
        h3 = h2 + gate * _dot(p_ref[rows, :].astype(BF16), wp_ref[...])
        out_ref[rows, :] = _rms(h3, nfin_ref[...])


def _tail(h1, yu, gates, p2, norm_ple_w, ple_gate_bf, ple_proj_bf, final_norm_w, local0, tile0, result):
    t = yu.shape[1]
    tm = TAIL_TILE
    assert t % tm == 0
    half = D_MODEL // 2
    tile = lambda width: pl.BlockSpec((tm, width), lambda i: (i + local0, 0))
    shifted = lambda width: pl.BlockSpec((tm, width), lambda i: (i + tile0 + local0, 0))
    const2 = lambda shape: pl.BlockSpec(shape, lambda i: (0, 0))
    in_specs = [tile(D_MODEL), pl.BlockSpec((TOP_K, tm, half), lambda i: (0, i, 0)),
                tile(LANES), shifted(PLE_DIM), const2((1, D_MODEL)),
                const2((D_MODEL, D_MODEL)), const2((PLE_DIM, D_MODEL)), const2((1, D_MODEL))]
    args = [h1, yu, gates, p2, norm_ple_w, ple_gate_bf, ple_proj_bf, final_norm_w]
    aliases = {}
    if result is not None:
        in_specs.append(pl.BlockSpec(memory_space=pl.ANY))
        args.append(result)
        aliases = {len(args) - 1: 0}
    return pl.pallas_call(
        _tail_kernel,
        grid=(t // tm,),
        in_specs=in_specs,
        out_specs=shifted(D_MODEL),
        out_shape=jax.ShapeDtypeStruct((p2.shape[0], D_MODEL), F32),
        input_output_aliases=aliases,
        compiler_params=_params(("parallel",)),
        name="tail",
    )(*args)


def kernel(x, p, positions, w_in, w_out, ret_gn_w, pool_w, pool_scale, norm_mix_w, norm_moe_w, router_w, router_b, expert_w_gate_up, expert_b_gate_up, expert_w_down, expert_b_down, norm_ple_w, ple_gate_w, ple_proj_w, final_norm_w):
    batch, seq, d = x.shape
    depth = w_in.shape[0]
    assert depth == 1 and d == D_MODEL and seq % TOKEN_TILE == 0
    assert seq % (RET_CHUNK * RET_STEP_CHUNKS) == 0
    group_batches = GROUP_BATCHES if sum(GROUP_BATCHES) == batch else (batch,)
    t = batch * seq
    row = lambda a: a.reshape(1, -1).astype(F32)
    l = 0

    x2 = x.reshape(t, d)
    pos2 = positions.reshape(t, 1)
    p2 = p[l].reshape(t, PLE_DIM)
    w_in_bf = w_in[l].astype(BF16)
    rw = jnp.pad(router_w[l].astype(F32), ((0, 0), (0, LANES - N_EXPERTS)))
    rw_hi = rw.astype(BF16)
    rw_lo = (rw - rw_hi.astype(F32)).astype(BF16)
    rw_split = jnp.concatenate([rw_hi, rw_lo], axis=1)
    rb = jnp.pad(router_b[l].astype(F32), (0, LANES - N_EXPERTS), constant_values=NEG_BIG).reshape(1, LANES)
    bgu = expert_b_gate_up[l].reshape(N_EXPERTS, 1, D_MODEL, 2).astype(F32)
    bdn = expert_b_down[l].reshape(N_EXPERTS, 1, D_MODEL).astype(F32)
    pool_w_bf, w_out_bf = pool_w[l].astype(BF16), w_out[l].astype(BF16)
    ple_gate_bf, ple_proj_bf = ple_gate_w[l].astype(BF16), ple_proj_w[l].astype(BF16)

    out = None
    first_batch = 0
    for gi, nb in enumerate(group_batches):
        tg = nb * seq
        n_rows = tg * TOP_K
        assert n_rows % EXPERT_TILE == 0
        tile0 = first_batch * seq // TOKEN_TILE
        ret, u = _retention(x2, pos2, row(norm_mix_w[l]), w_in_bf, row(ret_gn_w[l]),
                            first_batch * seq // (RET_CHUNK * RET_STEP_CHUNKS), nb, seq)
        first_batch += nb
        h1, hn_packed, code, gates, counts = _mix_router(
            x2, ret, u, pool_w_bf, row(pool_scale[l]), w_out_bf, row(norm_moe_w[l]), rw_split, rb,
            seq, tile0)

        offsets, items, n_items = _plan(counts, n_rows, EXPERT_TILE)
        pos = _positions(code, offsets)
        xs = _sc_scatter(hn_packed, pos)
        y = _experts(xs, items, n_items, expert_w_gate_up[l], bgu[..., 0], bgu[..., 1],
                     expert_w_down[l], bdn)

        spans = LAST_GROUP_SPANS if gi == len(group_batches) - 1 else 1
        span = tg // spans
        for si in range(spans):
            idx = pos[:TOP_K, si * span:(si + 1) * span].reshape(TOP_K * span)
            yu = _sc_gather(y, idx).reshape(TOP_K, span, d // 2)
            out = _tail(h1, yu, gates, p2, row(norm_ple_w[l]), ple_gate_bf, ple_proj_bf,
                        row(final_norm_w), si * (span // TAIL_TILE), tile0 * TOKEN_TILE // TAIL_TILE, out)
    return out.reshape(batch, seq, d)
```
